```python
import math
import jax
import jax.numpy as jnp
from jax import lax
import numpy as np

D_MODEL = 2048
BATCH = 4
SEQ = 2048
DEPTH = 2

HEAD_DIM = 128
GDN_HEADS = 8
GDN_CONV = 4
GDN_CHUNK = 64
GDN_WIDTH = GDN_HEADS * HEAD_DIM
HGRN_HEADS = 8
HGRN_CHUNK = 16
HGRN_WIDTH = HGRN_HEADS * HEAD_DIM
DIL_GROUPS = ((128, 1), (512, 4), (2048, 16))
DIL_HEADS_PER_GROUP = 4
DIL_HEADS = len(DIL_GROUPS) * DIL_HEADS_PER_GROUP
DIL_WIDTH = DIL_HEADS * HEAD_DIM
MOBA_HEADS = 4
MOBA_BLOCK = 256
MOBA_TOPK = 3
MOBA_QUERY_CHUNK = 32
MOBA_WIDTH = MOBA_HEADS * HEAD_DIM
ROPE_THETA = 10000.0
N_EXPERTS = 16
N_EXPERT_GROUPS = 4
EXPERTS_PER_GROUP = N_EXPERTS // N_EXPERT_GROUPS
TOP_K = 2
D_EXPERT = 512
DEEPNORM_ALPHA = (2.0 * DEPTH) ** 0.25
DEEPNORM_BETA = (8.0 * DEPTH) ** -0.25
LN_EPS = 1e-5
RMS_EPS = 1e-6
NEG_INF = -1e30

_EVEN_PARTS = (3 * GDN_WIDTH, GDN_WIDTH, GDN_HEADS, GDN_HEADS, HGRN_WIDTH, HGRN_WIDTH, HGRN_WIDTH, HGRN_WIDTH)
EVEN_IN = sum(_EVEN_PARTS)
EVEN_SPLITS = tuple(int(s) for s in np.cumsum(_EVEN_PARTS)[:-1])
EVEN_OUT = GDN_WIDTH + HGRN_WIDTH
_ODD_PARTS = (DIL_WIDTH, DIL_WIDTH, DIL_WIDTH, MOBA_WIDTH, MOBA_WIDTH, MOBA_WIDTH)
ODD_IN = sum(_ODD_PARTS)
ODD_SPLITS = tuple(int(s) for s in np.cumsum(_ODD_PARTS)[:-1])
ODD_OUT = (DIL_HEADS_PER_GROUP + MOBA_HEADS) * HEAD_DIM
N_EVEN = (DEPTH + 1) // 2
N_ODD = DEPTH // 2

kernel_name = 'hybrid_gdn_hgrn2_dilated_moba_grouped_moe'


def layer_norm(x, gain, bias):
    xf = x.astype(jnp.float32)
    mu = jnp.mean(xf, axis=-1, keepdims=True)
    var = jnp.mean(jnp.square(xf - mu), axis=-1, keepdims=True)
    y = (xf - mu) * lax.rsqrt(var + LN_EPS) * gain.astype(jnp.float32) + bias.astype(jnp.float32)
    return y.astype(x.dtype)


def rms_norm(x, gain):
    return x * lax.rsqrt(jnp.mean(jnp.square(x), axis=-1, keepdims=True) + RMS_EPS) * gain.astype(jnp.float32)


def l2_normalize(x):
    return x * lax.rsqrt(jnp.sum(jnp.square(x), axis=-1, keepdims=True) + RMS_EPS)


def split_heads(x, n_heads):
    b, s, _ = x.shape
    return x.reshape(b, s, n_heads, -1).transpose(0, 2, 1, 3)


def merge_heads(x):
    b, n, s, dh = x.shape
    return x.transpose(0, 2, 1, 3).reshape(b, s, n * dh)


def to_chunks(x, c):
    b, h, s = x.shape[:3]
    return jnp.moveaxis(x.reshape(b, h, s // c, c, *x.shape[3:]), 2, 0)


def from_chunks(x):
    n, b, h, c, d = x.shape
    return jnp.moveaxis(x, 0, 2).reshape(b, h, n * c, d)


def rope_tables(seq_len, dim):
    inv_freq = ROPE_THETA ** (-jnp.arange(0, dim, 2, dtype=jnp.float32) / dim)
    ang = jnp.arange(seq_len, dtype=jnp.float32)[:, None] * inv_freq[None, :]
    return jnp.cos(ang), jnp.sin(ang)


def apply_rope(x, cos, sin):
    x1, x2 = jnp.split(x, 2, axis=-1)
    return jnp.concatenate([x1 * cos - x2 * sin, x2 * cos + x1 * sin], axis=-1)


def causal_short_conv(x, w):
    k = w.shape[0]
    s = x.shape[1]
    xp = jnp.pad(x, ((0, 0), (k - 1, 0), (0, 0)))
    return sum(xp[:, j:j + s, :] * w[j] for j in range(k))


def gated_delta_rule_chunked(q, k, v, g, beta):
    c = GDN_CHUNK
    b, h, s, dk = q.shape
    dv = v.shape[-1]
    qc, kc, vc = to_chunks(q, c), to_chunks(k, c), to_chunks(v, c)
    gcum = jnp.cumsum(to_chunks(g, c), axis=-1)
    bc = to_chunks(beta, c)
    strict = jnp.tril(jnp.ones((c, c), bool), -1)
    incl = jnp.tril(jnp.ones((c, c), bool))
    diff = gcum[..., :, None] - gcum[..., None, :]
    dec_strict = jnp.where(strict, jnp.exp(jnp.where(strict, diff, 0.0)), 0.0)
    dec_incl = jnp.where(incl, jnp.exp(jnp.where(incl, diff, 0.0)), 0.0)
    kk = jnp.einsum('nbhid,nbhjd->nbhij', kc, kc)
    a_mat = jnp.eye(c, dtype=q.dtype) + bc[..., :, None] * dec_strict * kk
    rhs = jnp.concatenate([bc[..., None] * vc, (bc * jnp.exp(gcum))[..., None] * kc], axis=-1)
    sol = lax.linalg.triangular_solve(a_mat, rhs, left_side=True, lower=True, unit_diagonal=True)
    u_base, w_mat = sol[..., :dv], sol[..., dv:]
    qk = jnp.einsum('nbhid,nbhjd->nbhij', qc, kc) * dec_incl
    q_dec = qc * jnp.exp(gcum)[..., None]
    k_dec = kc * jnp.exp(gcum[..., -1:] - gcum)[..., None]
    chunk_dec = jnp.exp(gcum[..., -1])

    def step(state, inp):
        u_b, w_c, q_d, qk_c, k_d, d_c = inp
        u = u_b - jnp.einsum('bhck,bhkv->bhcv', w_c, state)
        o = jnp.einsum('bhck,bhkv->bhcv', q_d, state) + jnp.einsum('bhij,bhjv->bhiv', qk_c, u)
        state = d_c[..., None, None] * state + jnp.einsum('bhck,bhcv->bhkv', k_d, u)
        return state, o

    s0 = jnp.zeros((b, h, dk, dv), q.dtype)
    _, o = lax.scan(step, s0, (u_base, w_mat, q_dec, qk, k_dec, chunk_dec))
    return from_chunks(o)


def hgrn2_chunked(q, f, i):
    c = HGRN_CHUNK
    b, h, s, fd = q.shape
    vd = i.shape[-1]
    qc, ic, fc = to_chunks(q, c), to_chunks(i, c), to_chunks(f, c)
    kc = 1.0 - fc
    bcum = jnp.cumsum(jnp.log(fc), axis=-2)
    q_dec = qc * jnp.exp(bcum)
    k_inv = kc * jnp.exp(-bcum)
    k_dec = kc * jnp.exp(bcum[..., -1:, :] - bcum)
    incl = jnp.tril(jnp.ones((c, c), bool))
    p = jnp.where(incl, jnp.einsum('nbhif,nbhjf->nbhij', q_dec, k_inv), 0.0)
    o_intra = jnp.einsum('nbhij,nbhjv->nbhiv', p, ic)
    chunk_dec = jnp.exp(bcum[..., -1, :])

    def step(state, inp):
        q_d, k_d, i_c, d_c = inp
        o = jnp.einsum('bhcf,bhfv->bhcv', q_d, state)
        state = d_c[..., None] * state + jnp.einsum('bhcf,bhcv->bhfv', k_d, i_c)
        return state, o

    s0 = jnp.zeros((b, h, fd, vd), q.dtype)
    _, o_inter = lax.scan(step, s0, (q_dec, k_dec, ic, chunk_dec))
    return from_chunks(o_intra + o_inter)


def even_mixer(x, w_in, conv_w, a_log, dt_bias, gdn_norm, hgrn_norm, lower_bound, w_out):
    f32 = jnp.float32
    h = jnp.einsum('bsd,de->bse', x, w_in).astype(f32)
    qkv_a, z_a, b_a, a_a, q_b, f_b, i_b, g_b = jnp.split(h, EVEN_SPLITS, axis=-1)
    qkv_a = jax.nn.silu(causal_short_conv(qkv_a, conv_w.astype(f32)))
    q_a, k_a, v_a = jnp.split(qkv_a, 3, axis=-1)
    q_a = l2_normalize(split_heads(q_a, GDN_HEADS)) * HEAD_DIM ** -0.5
    k_a = l2_normalize(split_heads(k_a, GDN_HEADS))
    v_a = split_heads(v_a, GDN_HEADS)
    beta = jax.nn.sigmoid(b_a).transpose(0, 2, 1)
    g = (-jnp.exp(a_log.astype(f32)) * jax.nn.softplus(a_a + dt_bias.astype(f32))).transpose(0, 2, 1)
    o_a = gated_delta_rule_chunked(q_a, k_a, v_a, g, beta)
    o_a = rms_norm(o_a, gdn_norm) * jax.nn.silu(split_heads(z_a, GDN_HEADS))
    lb = lower_bound.astype(f32).reshape(HGRN_HEADS, 1, HEAD_DIM)
    f = lb + (1.0 - lb) * jax.nn.sigmoid(split_heads(f_b, HGRN_HEADS))
    o_b = hgrn2_chunked(split_heads(q_b, HGRN_HEADS), f, split_heads(i_b, HGRN_HEADS))
    o_b = rms_norm(o_b, hgrn_norm) * jax.nn.silu(split_heads(g_b, HGRN_HEADS))
    o = merge_heads(jnp.concatenate([o_a, o_b], axis=1))
    return jnp.einsum('bse,ed->bsd', o.astype(x.dtype), w_out)


def banded_causal_attention(q, k, v, window):
    *lead, length, dh = q.shape
    w = window
    nb = -(-length // w)
    lp = nb * w
    lead_pad = [(0, 0)] * len(lead)
    qb = jnp.pad(q, lead_pad + [(0, lp - length), (0, 0)]).reshape(*lead, nb, w, dh)
    kp = jnp.pad(k, lead_pad + [(w, lp - length), (0, 0)]).reshape(*lead, nb + 1, w, dh)
    vp = jnp.pad(v, lead_pad + [(w, lp - length), (0, 0)]).reshape(*lead, nb + 1, w, dh)
    kcat = jnp.concatenate([kp[..., :-1, :, :], kp[..., 1:, :, :]], axis=-2)
    vcat = jnp.concatenate([vp[..., :-1, :, :], vp[..., 1:, :, :]], axis=-2)
    s = jnp.einsum('...nqd,...nkd->...nqk', qb, kcat) * dh ** -0.5
    r = jnp.arange(w)[:, None]
    c = jnp.arange(2 * w)[None, :]
    dist = r + w - c
    kpos = (jnp.arange(nb) * w)[:, None, None] + c[None] - w
    mask = (dist >= 0)[None] & (dist <= w)[None] & (kpos >= 0)
    s = jnp.where(mask, s, NEG_INF)
    m = jnp.max(s, axis=-1, keepdims=True)
    p = jnp.exp(s - m)
    den = jnp.sum(p, axis=-1)
    o = jnp.einsum('...nqk,...nkd->...nqd', p, vcat) / den[..., None]
    lse = m[..., 0] + jnp.log(den)
    o = o.reshape(*lead, lp, dh)[..., :length, :]
    lse = lse.reshape(*lead, lp)[..., :length]
    return o, lse


def dilated_window_attention(q, k, v, window, dilation):
    b, h, s, dh = q.shape
    length = s // dilation

    def to_sub(t):
        return t.reshape(b, h, length, dilation, dh).transpose(0, 1, 3, 2, 4)

    o, lse = banded_causal_attention(to_sub(q), to_sub(k), to_sub(v), window // dilation)
    o = o.transpose(0, 1, 3, 2, 4).reshape(b, h, s, dh)
    lse = lse.transpose(0, 1, 3, 2).reshape(b, h, s)
    return o, lse


def moba_attention(q, k, v):
    b, h, s, dh = q.shape
    blk = MOBA_BLOCK
    sp = -(-s // blk) * blk
    pad = ((0, 0), (0, 0), (0, sp - s), (0, 0))
    q, k, v = (jnp.pad(t, pad) for t in (q, k, v))
    n_blk = sp // blk
    k_blocks = k.reshape(b, h, n_blk, blk, dh)
    v_blocks = v.reshape(b, h, n_blk, blk, dh)
    k_mean = jnp.mean(k_blocks, axis=3)
    q_block_id = jnp.arange(sp) // blk
    fully_past = jnp.arange(n_blk)[None, :] < q_block_id[:, None]
    gate = jnp.where(fully_past, jnp.einsum('bhsd,bhnd->bhsn', q, k_mean), NEG_INF)
    top_k = min(MOBA_TOPK, n_blk)
    _, sel = lax.top_k(gate, top_k)
    sel_valid = jnp.arange(top_k)[None, :] < q_block_id[:, None]
    scale = dh ** -0.5
    qn = MOBA_QUERY_CHUNK
    b_idx = jnp.arange(b)[:, None, None, None]
    h_idx = jnp.arange(h)[None, :, None, None]

    def attend_chunk(ci):
        start = ci * qn
        q_c = lax.dynamic_slice_in_dim(q, start, qn, axis=2)
        sel_c = lax.dynamic_slice_in_dim(sel, start, qn, axis=2)
        valid_c = lax.dynamic_slice_in_dim(sel_valid, start, qn, axis=0)
        k_sel = k_blocks[b_idx, h_idx, sel_c]
        v_sel = v_blocks[b_idx, h_idx, sel_c]
        s_sel = jnp.einsum('bhqd,bhqnjd->bhqnj', q_c, k_sel) * scale
        s_sel = jnp.where(valid_c[None, None, :, :, None], s_sel, NEG_INF).reshape(b, h, qn, top_k * blk)
        own = start // blk
        k_own = lax.dynamic_index_in_dim(k_blocks, own, axis=2, keepdims=False)
        v_own = lax.dynamic_index_in_dim(v_blocks, own, axis=2, keepdims=False)
        q_pos = start + jnp.arange(qn)
        k_pos = own * blk + jnp.arange(blk)
        s_own = jnp.where(k_pos[None, :] <= q_pos[:, None],
                          jnp.einsum('bhqd,bhjd->bhqj', q_c, k_own) * scale, NEG_INF)
        p = jax.nn.softmax(jnp.concatenate([s_sel, s_own], axis=-1), axis=-1)
        p_sel = p[..., :top_k * blk].reshape(b, h, qn, top_k, blk)
        p_own = p[..., top_k * blk:]
        return jnp.einsum('bhqnj,bhqnjd->bhqd', p_sel, v_sel) + jnp.einsum('bhqj,bhjd->bhqd', p_own, v_own)

    out = lax.map(attend_chunk, jnp.arange(sp // qn))
    out = jnp.moveaxis(out, 0, 2).reshape(b, h, sp, dh)
    return out[:, :, :s]


def odd_mixer(x, w_in, w_out, cos, sin):
    h = jnp.einsum('bsd,de->bse', x, w_in).astype(jnp.float32)
    cq, ck, cv, dq, dk, dv = jnp.split(h, ODD_SPLITS, axis=-1)
    cq = apply_rope(split_heads(cq, DIL_HEADS), cos, sin)
    ck = apply_rope(split_heads(ck, DIL_HEADS), cos, sin)
    cv = split_heads(cv, DIL_HEADS)
    outs, lses = [], []
    for gi, (window, dilation) in enumerate(DIL_GROUPS):
        hs = slice(gi * DIL_HEADS_PER_GROUP, (gi + 1) * DIL_HEADS_PER_GROUP)
        o_g, l_g = dilated_window_attention(cq[:, hs], ck[:, hs], cv[:, hs], window, dilation)
        outs.append(o_g)
        lses.append(l_g)
    wts = jax.nn.softmax(jnp.stack(lses, axis=0), axis=0)
    o_c = jnp.sum(wts[..., None] * jnp.stack(outs, axis=0), axis=0)
    o_d = moba_attention(apply_rope(split_heads(dq, MOBA_HEADS), cos, sin),
                         apply_rope(split_heads(dk, MOBA_HEADS), cos, sin),
                         split_heads(dv, MOBA_HEADS))
    o = merge_heads(jnp.concatenate([o_c, o_d], axis=1))
    return jnp.einsum('bse,ed->bsd', o.astype(x.dtype), w_out)


def moe_ffn(x, router_w, router_bias, w_gate, w_up, w_down):
    b, s, d = x.shape
    t = x.reshape(b * s, d)
    scores = jax.nn.sigmoid(jnp.einsum('td,de->te', t, router_w).astype(jnp.float32))
    biased = scores + router_bias.astype(jnp.float32)
    grouped = biased.reshape(-1, N_EXPERT_GROUPS, EXPERTS_PER_GROUP)
    group_score = jnp.sum(lax.top_k(grouped, 2)[0], axis=-1)
    best_group = jnp.argmax(group_score, axis=-1)
    in_group = (jnp.arange(N_EXPERTS) // EXPERTS_PER_GROUP)[None, :] == best_group[:, None]
    _, top_idx = lax.top_k(jnp.where(in_group, biased, NEG_INF), TOP_K)
    top_scores = jnp.take_along_axis(scores, top_idx, axis=-1)
    weights = top_scores / jnp.sum(top_scores, axis=-1, keepdims=True)
    gates = jnp.sum(jax.nn.one_hot(top_idx, N_EXPERTS, dtype=jnp.float32) * weights[..., None], axis=1)
    hid = jax.nn.silu(jnp.einsum('td,edf->tef', t, w_gate)) * jnp.einsum('td,edf->tef', t, w_up)
    y = jnp.einsum('tef,efd->td', hid * gates[..., None].astype(hid.dtype), w_down)
    return y.reshape(b, s, d).astype(x.dtype)


def setup_inputs(seed: int = 0) -> dict:
    key = jax.random.key(seed)
    ks = jax.random.split(key, 20)
    f32 = jnp.float32

    def nrm(k, shape, scale):
        return jax.random.normal(k, shape, f32) * scale

    x = nrm(ks[0], (BATCH, SEQ, D_MODEL), 1.0)
    ev_w_in = nrm(ks[1], (N_EVEN, D_MODEL, EVEN_IN), D_MODEL ** -0.5)
    ev_conv_w = nrm(ks[2], (N_EVEN, GDN_CONV, 3 * GDN_WIDTH), GDN_CONV ** -0.5)
    ev_a_log = jnp.log(jax.random.uniform(ks[3], (N_EVEN, GDN_HEADS), f32, 1.0, 16.0))
    dt = jnp.exp(jax.random.uniform(ks[4], (N_EVEN, GDN_HEADS), f32, math.log(1e-3), math.log(1e-1)))
    ev_dt_bias = dt + jnp.log(-jnp.expm1(-dt))
    ev_gdn_norm = 1.0 + nrm(ks[5], (N_EVEN, HEAD_DIM), 0.02)
    ev_hgrn_norm = 1.0 + nrm(ks[6], (N_EVEN, HEAD_DIM), 0.02)
    hgrn_lb_logits = nrm(ks[7], (DEPTH + 1, HGRN_WIDTH), 0.1)
    ev_w_out = nrm(ks[8], (N_EVEN, EVEN_OUT, D_MODEL), EVEN_OUT ** -0.5 * DEEPNORM_BETA)
    od_w_in = nrm(ks[9], (N_ODD, D_MODEL, ODD_IN), D_MODEL ** -0.5)
    od_w_out = nrm(ks[10], (N_ODD, ODD_OUT, D_MODEL), ODD_OUT ** -0.5 * DEEPNORM_BETA)
    router_w = nrm(ks[11], (D_MODEL, N_EXPERTS), D_MODEL ** -0.5)
    router_bias = nrm(ks[12], (N_EXPERTS,), 0.01)
    moe_w_gate = nrm(ks[13], (DEPTH, N_EXPERTS, D_MODEL, D_EXPERT), D_MODEL ** -0.5)
    moe_w_up = nrm(ks[14], (DEPTH, N_EXPERTS, D_MODEL, D_EXPERT), D_MODEL ** -0.5)
    moe_w_down = nrm(ks[15], (DEPTH, N_EXPERTS, D_EXPERT, D_MODEL), D_EXPERT ** -0.5 * DEEPNORM_BETA)
    ln_gain = 1.0 + nrm(ks[16], (DEPTH, 2, D_MODEL), 0.02)
    ln_bias = nrm(ks[17], (DEPTH, 2, D_MODEL), 0.01)
    return {'x': x, 'ev_w_in': ev_w_in, 'ev_conv_w': ev_conv_w, 'ev_a_log': ev_a_log,
            'ev_dt_bias': ev_dt_bias, 'ev_gdn_norm': ev_gdn_norm, 'ev_hgrn_norm': ev_hgrn_norm,
            'hgrn_lb_logits': hgrn_lb_logits, 'ev_w_out': ev_w_out, 'od_w_in': od_w_in,
            'od_w_out': od_w_out, 'router_w': router_w, 'router_bias': router_bias,
            'moe_w_gate': moe_w_gate, 'moe_w_up': moe_w_up, 'moe_w_down': moe_w_down,
            'ln_gain': ln_gain, 'ln_bias': ln_bias}


def reference(x, ev_w_in, ev_conv_w, ev_a_log, ev_dt_bias, ev_gdn_norm, ev_hgrn_norm, hgrn_lb_logits,
              ev_w_out, od_w_in, od_w_out, router_w, router_bias, moe_w_gate, moe_w_up, moe_w_down,
              ln_gain, ln_bias):
    cos, sin = rope_tables(x.shape[1], HEAD_DIM)
    lower_bounds = jnp.cumsum(jax.nn.softmax(hgrn_lb_logits.astype(jnp.float32), axis=0), axis=0)
    h = x
    for layer in range(DEPTH):
        if layer % 2 == 0:
            e = layer // 2
            mix = even_mixer(h, ev_w_in[e], ev_conv_w[e], ev_a_log[e], ev_dt_bias[e], ev_gdn_norm[e],
                             ev_hgrn_norm[e], lower_bounds[layer], ev_w_out[e])
        else:
            o = layer // 2
            mix = odd_mixer(h, od_w_in[o], od_w_out[o], cos, sin)
        h = layer_norm(DEEPNORM_ALPHA * h + mix, ln_gain[layer, 0], ln_bias[layer, 0])
        ffn = moe_ffn(h, router_w, router_bias, moe_w_gate[layer], moe_w_up[layer], moe_w_down[layer])
        h = layer_norm(DEEPNORM_ALPHA * h + ffn, ln_gain[layer, 1], ln_bias[layer, 1])
    return h
```

```python
import functools
import math

import jax
import jax.numpy as jnp
import numpy as np
from jax import lax
from jax.experimental import pallas as pl
from jax.experimental.pallas import tpu as pltpu

F32 = jnp.float32
BF16 = jnp.bfloat16

D_MODEL = 2048
DEPTH = 2
HEAD_DIM = 128
GDN_HEADS = 8
GDN_CONV = 4
GDN_CHUNK = 64
GDN_WIDTH = GDN_HEADS * HEAD_DIM
HGRN_HEADS = 8
HGRN_CHUNK = 16
HGRN_WIDTH = HGRN_HEADS * HEAD_DIM
DIL_GROUPS = ((128, 1), (512, 4), (2048, 16))
DIL_HEADS_PER_GROUP = 4
DIL_HEADS = len(DIL_GROUPS) * DIL_HEADS_PER_GROUP
MOBA_HEADS = 4
MOBA_BLOCK = 256
MOBA_TOPK = 3
ROPE_THETA = 10000.0
N_EXPERTS = 16
N_EXPERT_GROUPS = 4
EXPERTS_PER_GROUP = N_EXPERTS // N_EXPERT_GROUPS
D_EXPERT = 512
DEEPNORM_ALPHA = (2.0 * DEPTH) ** 0.25
LN_EPS = 1e-5
RMS_EPS = 1e-6
NEG_INF = -1e30

LANES = 128
VMEM_LIMIT = 56 * 1024 * 1024
ATT_BLOCK = 256

EVEN_MAIN = 3 * GDN_WIDTH + GDN_WIDTH + 4 * HGRN_WIDTH
EVEN_COLS = EVEN_MAIN + LANES
ODD_COLS = 3 * DIL_HEADS * HEAD_DIM + 3 * MOBA_HEADS * HEAD_DIM


def _dot(a, b):
    return jnp.dot(a.astype(BF16), b.astype(BF16), preferred_element_type=F32)


def _dot_nt(a, b):
    return lax.dot_general(a.astype(BF16), b.astype(BF16), (((1,), (1,)), ((), ())),
                           preferred_element_type=F32)


def _dot_tn(a, b):
    return lax.dot_general(a.astype(BF16), b.astype(BF16), (((0,), (0,)), ((), ())),
                           preferred_element_type=F32)


def _dot_hi(a, b):
    return jnp.dot(a, b, preferred_element_type=F32, precision=lax.Precision.HIGHEST)


def _dot_nt_hi(a, b):
    return lax.dot_general(a, b, (((1,), (1,)), ((), ())), preferred_element_type=F32,
                           precision=lax.Precision.HIGHEST)


def _silu(x):
    return x * jax.nn.sigmoid(x)


def _iota(shape, dim):
    return lax.broadcasted_iota(jnp.int32, shape, dim)


def _params(*sem):
    return pltpu.CompilerParams(dimension_semantics=sem, vmem_limit_bytes=VMEM_LIMIT)


def _mm_kernel(x_ref, w_ref, o_ref):
    o_ref[...] = jnp.dot(x_ref[...], w_ref[...], preferred_element_type=F32).astype(o_ref.dtype)


def _matmul(x, w, tm, tn):
    m, k = x.shape
    n = w.shape[1]
    assert m % tm == 0 and n % tn == 0
    return pl.pallas_call(
        _mm_kernel,
        grid=(m // tm, n // tn),
        in_specs=[pl.BlockSpec((tm, k), lambda i, j: (i, 0)),
                  pl.BlockSpec((k, tn), lambda i, j: (0, j))],
        out_specs=pl.BlockSpec((tm, tn), lambda i, j: (i, j)),
        out_shape=jax.ShapeDtypeStruct((m, n), F32),
        compiler_params=_params("parallel", "parallel"),
        name="in_proj",
    )(x, w)


def _gdn_kernel(q_ref, k_ref, v_ref, z_ref, cwq_ref, cwk_ref, cwv_ref, a_ref, b_ref, alog_ref, dt_ref,
                gn_ref, o_ref, pad_s, q_s, k_s, v_s, gcum_s, beta_s):
    seq = q_ref.shape[0]
    c = GDN_CHUNK
    n_chunks = seq // c
    rows = 256

    pad_s[pl.ds(0, 8), :] = jnp.zeros((8, HEAD_DIM), F32)
    for x_ref, cw_ref, dst, mode in ((q_ref, cwq_ref, q_s, "q"), (k_ref, cwk_ref, k_s, "k"),
                                     (v_ref, cwv_ref, v_s, "v")):
        pad_s[pl.ds(8, seq), :] = x_ref[...]
        cw = cw_ref[...]
        for r in range(seq // rows):
            acc = None
            for j in range(GDN_CONV):
                tap = pad_s[pl.ds(8 + r * rows - (GDN_CONV - 1) + j, rows), :] * cw[j:j + 1, :]
                acc = tap if acc is None else acc + tap
            y = _silu(acc)
            if mode != "v":
                y = y * lax.rsqrt(jnp.sum(y * y, axis=-1, keepdims=True) + RMS_EPS)
            if mode == "q":
                y = y * HEAD_DIM ** -0.5
            dst[pl.ds(r * rows, rows), :] = y

    upper = (_iota((c, c), 0) <= _iota((c, c), 1)).astype(F32)
    g = -jnp.exp(alog_ref[...]) * jax.nn.softplus(a_ref[...] + dt_ref[...])
    gcum_s[...] = _dot_hi(g, upper)
    beta_s[...] = jax.nn.sigmoid(b_ref[...])

    ri = _iota((c, c), 0)
    ci = _iota((c, c), 1)
    eye = ri == ci
    strict = ri > ci
    incl = ri >= ci
    eye_f = eye.astype(F32)
    gn = gn_ref[...]

    def to_col(row):
        return jnp.sum(jnp.where(eye, jnp.broadcast_to(row, (c, c)), 0.0), axis=1, keepdims=True)

    def chunk(n, state):
        r0 = pl.multiple_of(n * c, c)
        qc = q_s[pl.ds(r0, c), :]
        kc = k_s[pl.ds(r0, c), :]
        vc = v_s[pl.ds(r0, c), :]
        g_row = gcum_s[pl.ds(n, 1), :]
        b_row = beta_s[pl.ds(n, 1), :]
        g_col = to_col(g_row)
        b_col = to_col(b_row)
        g_last = g_row[:, c - 1:c]
        decay = jnp.exp(jnp.where(incl, g_col - g_row, 0.0))
        dec_strict = jnp.where(strict, decay, 0.0)
        dec_incl = jnp.where(incl, decay, 0.0)
        kk = _dot_nt(kc, kc)
        n_mat = b_col * dec_strict * kk
        inv = eye_f - jnp.where((ri // 2 == ci // 2), n_mat, 0.0)
        s = 2
        while s < c:
            blk = (ri // (2 * s) == ci // (2 * s)) & ((ri // s) % 2 == 1) & ((ci // s) % 2 == 0)
            low = jnp.where(blk, n_mat, 0.0)
            inv = inv - _dot_hi(_dot_hi(inv, low), inv)
            s *= 2
        e_col = jnp.exp(g_col)
        rhs = jnp.concatenate([b_col * vc, (b_col * e_col) * kc], axis=1)
        sol = _dot_hi(inv, rhs)
        u = sol[:, :HEAD_DIM] - _dot(sol[:, HEAD_DIM:], state)
        qk = _dot_nt(qc, kc) * dec_incl
        o = _dot(qc * e_col, state) + _dot(qk, u)
        k_dec = kc * jnp.exp(g_last - g_col)
        state = jnp.exp(g_last) * state + _dot_tn(k_dec, u)
        o = o * lax.rsqrt(jnp.mean(o * o, axis=-1, keepdims=True) + RMS_EPS) * gn
        o_ref[pl.ds(r0, c), :] = (o * _silu(z_ref[pl.ds(r0, c), :])).astype(o_ref.dtype)
        return state

    lax.fori_loop(0, n_chunks, chunk, jnp.zeros((HEAD_DIM, HEAD_DIM), F32))


def _gdn(h3, conv_w, a_rows, b_rows, alog, dt, gn):
    b, s, _ = h3.shape
    nh = GDN_HEADS
    n_chunks = s // GDN_CHUNK
    col = lambda off: pl.BlockSpec((None, s, HEAD_DIM), lambda bi, hi: (bi, 0, off + hi))
    cw = lambda off: pl.BlockSpec((GDN_CONV, HEAD_DIM), lambda bi, hi: (0, off + hi))
    rowspec = pl.BlockSpec((None, None, n_chunks, GDN_CHUNK), lambda bi, hi: (bi, hi, 0, 0))
    headvec = pl.BlockSpec((None, 1, GDN_CHUNK), lambda bi, hi: (hi, 0, 0))
    return pl.pallas_call(
        _gdn_kernel,
        grid=(b, nh),
        in_specs=[col(0), col(nh), col(2 * nh), col(3 * nh), cw(0), cw(nh), cw(2 * nh),
                  rowspec, rowspec, headvec, headvec,
                  pl.BlockSpec((1, HEAD_DIM), lambda bi, hi: (0, 0))],
        out_specs=pl.BlockSpec((None, s, HEAD_DIM), lambda bi, hi: (bi, 0, hi)),
        out_shape=jax.ShapeDtypeStruct((b, s, GDN_WIDTH), BF16),
        scratch_shapes=[pltpu.VMEM((s + 8, HEAD_DIM), F32), pltpu.VMEM((s, HEAD_DIM), F32),
                        pltpu.VMEM((s, HEAD_DIM), F32), pltpu.VMEM((s, HEAD_DIM), F32),
                        pltpu.VMEM((n_chunks, GDN_CHUNK), F32), pltpu.VMEM((n_chunks, GDN_CHUNK), F32)],
        compiler_params=_params("parallel", "parallel"),
        name="gdn",
    )(h3, h3, h3, h3, conv_w, conv_w, conv_w, a_rows, b_rows, alog, dt, gn)


def _hgrn_kernel(q_ref, f_ref, i_ref, g_ref, lb_ref, hn_ref, o_ref):
    seq = q_ref.shape[0]
    c = HGRN_CHUNK
    lower = (_iota((c, c), 1) <= _iota((c, c), 0))
    lower_f = lower.astype(F32)
    lb = lb_ref[...]
    hn = hn_ref[...]

    def chunk(n, state_t):
        r0 = pl.multiple_of(n * c, c)
        qc = q_ref[pl.ds(r0, c), :]
        ic = i_ref[pl.ds(r0, c), :]
        f = lb + (1.0 - lb) * jax.nn.sigmoid(f_ref[pl.ds(r0, c), :])
        kc = 1.0 - f
        bcum = _dot_hi(lower_f, jnp.log(f))
        b_last = bcum[c - 1:c, :]
        q_dec = qc * jnp.exp(bcum)
        k_inv = kc * jnp.exp(-bcum)
        k_dec = kc * jnp.exp(b_last - bcum)
        p = jnp.where(lower, _dot_nt(q_dec, k_inv), 0.0)
        o = _dot(p, ic) + _dot_nt(q_dec, state_t)
        state_t = state_t * jnp.exp(b_last) + _dot_tn(ic, k_dec)
        o = o * lax.rsqrt(jnp.mean(o * o, axis=-1, keepdims=True) + RMS_EPS) * hn
        o_ref[pl.ds(r0, c), :] = (o * _silu(g_ref[pl.ds(r0, c), :])).astype(o_ref.dtype)
        return state_t

    lax.fori_loop(0, seq // c, chunk, jnp.zeros((HEAD_DIM, HEAD_DIM), F32))


def _hgrn(h3, lb, hn):
    b, s, _ = h3.shape
    nh = HGRN_HEADS
    base = 4 * GDN_HEADS
    col = lambda off: pl.BlockSpec((None, s, HEAD_DIM), lambda bi, hi: (bi, 0, base + off + hi))
    return pl.pallas_call(
        _hgrn_kernel,
        grid=(b, nh),
        in_specs=[col(0), col(nh), col(2 * nh), col(3 * nh),
                  pl.BlockSpec((None, 1, HEAD_DIM), lambda bi, hi: (hi, 0, 0)),
                  pl.BlockSpec((1, HEAD_DIM), lambda bi, hi: (0, 0))],
        out_specs=pl.BlockSpec((None, s, HEAD_DIM), lambda bi, hi: (bi, 0, hi)),
        out_shape=jax.ShapeDtypeStruct((b, s, HGRN_WIDTH), BF16),
        compiler_params=_params("parallel", "parallel"),
        name="hgrn2",
    )(h3, h3, h3, h3, lb, hn)


def _rope(x, cos2, sin2):
    return x * cos2 + pltpu.roll(x, HEAD_DIM // 2, axis=1) * sin2


def _flash_step(q_blk, k_blk, v_blk, mask, carry):
    m, l, acc = carry
    s = jnp.where(mask, _dot_nt(q_blk, k_blk) * HEAD_DIM ** -0.5, NEG_INF)
    m_new = jnp.maximum(m, jnp.max(s, axis=-1, keepdims=True))
    alpha = jnp.exp(m - m_new)
    p = jnp.exp(s - m_new)
    l = alpha * l + jnp.sum(p, axis=-1, keepdims=True)
    acc = alpha * acc + _dot(p, v_blk)
    return m_new, l, acc


def _flash_init():
    blk = ATT_BLOCK
    return (jnp.full((blk, 1), NEG_INF, F32), jnp.zeros((blk, 1), F32), jnp.zeros((blk, HEAD_DIM), F32))


def _dilated_kernel(*refs):
    n_g = len(DIL_GROUPS)
    q_refs, k_refs, v_refs = refs[0:n_g], refs[n_g:2 * n_g], refs[2 * n_g:3 * n_g]
    cos_ref, sin_ref, o_ref = refs[3 * n_g:3 * n_g + 3]
    q_s, k_s, v_s = refs[3 * n_g + 3:]
    seq = o_ref.shape[0]
    blk = ATT_BLOCK
    n_blk = seq // blk

    cos2 = cos_ref[...]
    sin2 = sin_ref[...]
    for gi in range(n_g):
        q_s[gi] = _rope(q_refs[gi][...], cos2, sin2).astype(BF16)
        k_s[gi] = _rope(k_refs[gi][...], cos2, sin2).astype(BF16)
        v_s[gi] = v_refs[gi][...].astype(BF16)

    rel0 = _iota((blk, blk), 0) - _iota((blk, blk), 1)

    def q_block(qb, _):
        q0 = pl.multiple_of(qb * blk, blk)
        outs, lses = [], []
        for gi, (window, dilation) in enumerate(DIL_GROUPS):
            q_blk = q_s[gi, pl.ds(q0, blk), :]
            n_prev = -(-window // blk)

            def k_block(kb, carry, gi=gi, window=window, dilation=dilation, q_blk=q_blk):
                k0 = pl.multiple_of(kb * blk, blk)
                rel = rel0 + (qb - kb) * blk
                mask = (rel >= 0) & (rel <= window) & ((rel & (dilation - 1)) == 0)
                return _flash_step(q_blk, k_s[gi, pl.ds(k0, blk), :], v_s[gi, pl.ds(k0, blk), :], mask, carry)

            m, l, acc = lax.fori_loop(jnp.maximum(qb - n_prev, 0), qb + 1, k_block, _flash_init())
            outs.append(acc / l)
            lses.append(m + jnp.log(l))
        top = functools.reduce(jnp.maximum, lses)
        wts = [jnp.exp(x - top) for x in lses]
        den = functools.reduce(lambda a, b: a + b, wts)
        o = functools.reduce(lambda a, b: a + b, [w * x for w, x in zip(wts, outs)]) / den
        o_ref[pl.ds(q0, blk), :] = o.astype(o_ref.dtype)
        return 0

    lax.fori_loop(0, n_blk, q_block, 0)


def _dilated(h3, cos2, sin2):
    b, s, _ = h3.shape
    hpg = DIL_HEADS_PER_GROUP
    n_g = len(DIL_GROUPS)
    col = lambda off: pl.BlockSpec((None, s, HEAD_DIM), lambda bi, hi: (bi, 0, off + hi))
    tab = pl.BlockSpec((s, HEAD_DIM), lambda bi, hi: (0, 0))
    specs = [col(part * DIL_HEADS + gi * hpg) for part in range(3) for gi in range(n_g)]
    return pl.pallas_call(
        _dilated_kernel,
        grid=(b, hpg),
        in_specs=specs + [tab, tab],
        out_specs=pl.BlockSpec((None, s, HEAD_DIM), lambda bi, hi: (bi, 0, hi)),
        out_shape=jax.ShapeDtypeStruct((b, s, hpg * HEAD_DIM), BF16),
        scratch_shapes=[pltpu.VMEM((n_g, s, HEAD_DIM), BF16)] * 3,
        compiler_params=_params("parallel", "parallel"),
        name="dilated_attention",
    )(*([h3] * (3 * n_g)), cos2, sin2)


def _moba_kernel(q_ref, k_ref, v_ref, cos_ref, sin_ref, o_ref, qf_s, q_s, k_s, v_s, km_s):
    seq = o_ref.shape[0]
    blk = MOBA_BLOCK
    n_blk = seq // blk
    cos2 = cos_ref[...]
    sin2 = sin_ref[...]
    q = _rope(q_ref[...], cos2, sin2)
    qf_s[...] = q
    q_s[...] = q.astype(BF16)
    km_s[...] = jnp.zeros(km_s.shape, F32)
    for nb in range(n_blk):
        kb = _rope(k_ref[pl.ds(nb * blk, blk), :], cos2[nb * blk:(nb + 1) * blk], sin2[nb * blk:(nb + 1) * blk])
        k_s[pl.ds(nb * blk, blk), :] = kb.astype(BF16)
        km_s[pl.ds(nb, 1), :] = jnp.mean(kb, axis=0, keepdims=True)
    v_s[...] = v_ref[...].astype(BF16)

    lane = _iota((blk, LANES), 1)
    causal = _iota((blk, blk), 0) >= _iota((blk, blk), 1)
    all_true = _iota((blk, blk), 0) >= 0

    def q_block(qb, _):
        q0 = pl.multiple_of(qb * blk, blk)
        gate = _dot_nt_hi(qf_s[pl.ds(q0, blk), :], km_s[...])
        gate = jnp.where(lane < qb, gate, -jnp.inf)
        sel = jnp.zeros((blk, LANES), F32)
        for _k in range(MOBA_TOPK):
            best = jnp.max(gate, axis=-1, keepdims=True)
            first = jnp.min(jnp.where(gate == best, lane, LANES), axis=-1, keepdims=True)
            pick = (lane == first) & (best > -jnp.inf)
            sel = jnp.where(pick, 1.0, sel)
            gate = jnp.where(pick, -jnp.inf, gate)
        q_blk = q_s[pl.ds(q0, blk), :]

        def k_block(kb, carry):
            k0 = pl.multiple_of(kb * blk, blk)
            chosen = jnp.sum(jnp.where(lane == kb, sel, 0.0), axis=-1, keepdims=True) > 0.0
            return _flash_step(q_blk, k_s[pl.ds(k0, blk), :], v_s[pl.ds(k0, blk), :], chosen & all_true, carry)

        carry = lax.fori_loop(0, qb, k_block, _flash_init())
        m, l, acc = _flash_step(q_blk, k_s[pl.ds(q0, blk), :], v_s[pl.ds(q0, blk), :], causal, carry)
        o_ref[pl.ds(q0, blk), :] = (acc / l).astype(o_ref.dtype)
        return 0

    lax.fori_loop(0, n_blk, q_block, 0)


def _moba(h3, cos2, sin2):
    b, s, _ = h3.shape
    base = 3 * DIL_HEADS
    col = lambda off: pl.BlockSpec((None, s, HEAD_DIM), lambda bi, hi: (bi, 0, base + off + hi))
    tab = pl.BlockSpec((s, HEAD_DIM), lambda bi, hi: (0, 0))
    return pl.pallas_call(
        _moba_kernel,
        grid=(b, MOBA_HEADS),
        in_specs=[col(0), col(MOBA_HEADS), col(2 * MOBA_HEADS), tab, tab],
        out_specs=pl.BlockSpec((None, s, HEAD_DIM), lambda bi, hi: (bi, 0, hi)),
        out_shape=jax.ShapeDtypeStruct((b, s, MOBA_HEADS * HEAD_DIM), BF16),
        scratch_shapes=[pltpu.VMEM((s, HEAD_DIM), F32), pltpu.VMEM((s, HEAD_DIM), BF16),
                        pltpu.VMEM((s, HEAD_DIM), BF16), pltpu.VMEM((s, HEAD_DIM), BF16),
                        pltpu.VMEM((LANES, HEAD_DIM), F32)],
        compiler_params=_params("parallel", "parallel"),
        name="moba_attention",
    )(h3, h3, h3, cos2, sin2)


def _layer_norm(x, gain, bias):
    mu = jnp.mean(x, axis=-1, keepdims=True)
    xc = x - mu
    var = jnp.mean(xc * xc, axis=-1, keepdims=True)
    return xc * lax.rsqrt(var + LN_EPS) * gain + bias


def _out_ln_kernel(*refs, n_parts):
    o_refs = refs[0:n_parts]
    w_refs = refs[n_parts:2 * n_parts]
    h_ref, gain_ref, bias_ref, rw_ref, y_ref, yb_ref, logit_ref = refs[2 * n_parts:]
    mix = None
    for o_r, w_r in zip(o_refs, w_refs):
        part = jnp.dot(o_r[...], w_r[...], preferred_element_type=F32)
        mix = part if mix is None else mix + part
    y = _layer_norm(DEEPNORM_ALPHA * h_ref[...] + mix, gain_ref[...], bias_ref[...])
    y_ref[...] = y
    yb_ref[...] = y.astype(BF16)
    logit_ref[...] = _dot_hi(y, rw_ref[...])


def _out_ln(parts, weights, h, gain, bias, router_w, tm):
    t, d = h.shape
    n_parts = len(parts)
    row = lambda width: pl.BlockSpec((tm, width), lambda i: (i, 0))
    full = lambda a: pl.BlockSpec(a.shape, lambda i: (0, 0))
    return pl.pallas_call(
        functools.partial(_out_ln_kernel, n_parts=n_parts),
        grid=(t // tm,),
        in_specs=[row(p.shape[1]) for p in parts] + [full(w) for w in weights]
                 + [row(d), full(gain), full(bias), full(router_w)],
        out_specs=[row(d), row(d), row(LANES)],
        out_shape=[jax.ShapeDtypeStruct((t, d), F32), jax.ShapeDtypeStruct((t, d), BF16),
                   jax.ShapeDtypeStruct((t, LANES), F32)],
        compiler_params=_params("parallel"),
        name="out_proj_ln",
    )(*parts, *weights, h, gain, bias, router_w)


def _route(logits, rbias):
    rows = logits.shape[0]
    lane = _iota((rows, LANES), 1)
    real = lane < N_EXPERTS
    scores = jax.nn.sigmoid(logits)
    biased = jnp.where(real, scores + rbias, -jnp.inf)

    def first_argmax(vals):
        best = jnp.max(vals, axis=-1, keepdims=True)
        return best, jnp.min(jnp.where(vals == best, lane, LANES), axis=-1, keepdims=True)

    best_score = None
    best_group = None
    for g in range(N_EXPERT_GROUPS):
        vals = jnp.where(lane // EXPERTS_PER_GROUP == g, biased, -jnp.inf)
        top1, idx1 = first_argmax(vals)
        top2, _ = first_argmax(jnp.where(lane == idx1, -jnp.inf, vals))
        score = top1 + top2
        if g == 0:
            best_score, best_group = score, jnp.zeros_like(idx1)
        else:
            better = score > best_score
            best_group = jnp.where(better, g, best_group)
            best_score = jnp.where(better, score, best_score)
    masked = jnp.where(real, jnp.where(lane // EXPERTS_PER_GROUP == best_group, biased, NEG_INF), -jnp.inf)
    _, i1 = first_argmax(masked)
    _, i2 = first_argmax(jnp.where(lane == i1, -jnp.inf, masked))
    s1 = jnp.sum(jnp.where(lane == i1, scores, 0.0), axis=-1, keepdims=True)
    s2 = jnp.sum(jnp.where(lane == i2, scores, 0.0), axis=-1, keepdims=True)
    tot = s1 + s2
    return jnp.where(lane == i1, s1 / tot, 0.0) + jnp.where(lane == i2, s2 / tot, 0.0)


def _moe_kernel(x_ref, logit_ref, rbias_ref, wg_ref, wu_ref, wd_ref, h_ref, gain_ref, bias_ref,
                y_ref, yb_ref, acc_s, gates_s):
    e = pl.program_id(1)

    @pl.when(e == 0)
    def _():
        gates_s[...] = _route(logit_ref[...], rbias_ref[...])
        acc_s[...] = jnp.zeros(acc_s.shape, F32)

    gates = gates_s[...]
    gate_e = jnp.sum(jnp.where(_iota(gates.shape, 1) == e, gates, 0.0), axis=-1, keepdims=True)
    x = x_ref[...]
    hid = _silu(jnp.dot(x, wg_ref[...], preferred_element_type=F32)) * jnp.dot(
        x, wu_ref[...], preferred_element_type=F32)
    acc_s[...] += _dot(hid * gate_e, wd_ref[...])

    @pl.when(e == pl.num_programs(1) - 1)
    def _():
        y = _layer_norm(DEEPNORM_ALPHA * h_ref[...] + acc_s[...], gain_ref[...], bias_ref[...])
        y_ref[...] = y
        yb_ref[...] = y.astype(BF16)


def _moe_ln(xb, logits, rbias, wg, wu, wd, h, gain, bias, tm):
    t, d = h.shape
    n_e, _, f = wg.shape
    row = lambda width: pl.BlockSpec((tm, width), lambda i, e: (i, 0))
    full = lambda a: pl.BlockSpec(a.shape, lambda i, e: (0, 0))
    return pl.pallas_call(
        _moe_kernel,
        grid=(t // tm, n_e),
        in_specs=[row(d), row(LANES), full(rbias),
                  pl.BlockSpec((None, d, f), lambda i, e: (e, 0, 0)),
                  pl.BlockSpec((None, d, f), lambda i, e: (e, 0, 0)),
                  pl.BlockSpec((None, f, d), lambda i, e: (e, 0, 0)),
                  row(d), full(gain), full(bias)],
        out_specs=[row(d), row(d)],
        out_shape=[jax.ShapeDtypeStruct((t, d), F32), jax.ShapeDtypeStruct((t, d), BF16)],
        scratch_shapes=[pltpu.VMEM((tm, d), F32), pltpu.VMEM((tm, LANES), F32)],
        compiler_params=_params("parallel", "arbitrary"),
        name="moe_ln",
    )(xb, logits, rbias, wg, wu, wd, h, gain, bias)


def _rope_tables(seq):
    inv_freq = ROPE_THETA ** (-jnp.arange(0, HEAD_DIM, 2, dtype=F32) / HEAD_DIM)
    ang = jnp.arange(seq, dtype=F32)[:, None] * inv_freq[None, :]
    cos, sin = jnp.cos(ang), jnp.sin(ang)
    return jnp.concatenate([cos, cos], axis=-1), jnp.concatenate([-sin, sin], axis=-1)


def _even_mixer(hb, b, s, w_in, conv_w, a_log, dt_bias, gdn_norm, hgrn_norm, lower_bound):
    gw = GDN_WIDTH
    n_small = 2 * GDN_HEADS
    tail0 = 4 * gw
    w_perm = jnp.concatenate([w_in[:, :tail0], w_in[:, tail0 + n_small:], w_in[:, tail0:tail0 + n_small],
                              jnp.zeros((w_in.shape[0], LANES - n_small), w_in.dtype)], axis=1).astype(BF16)
    h = _matmul(hb, w_perm, 1024, 640)
    h3 = h.reshape(b, s, EVEN_COLS)
    n_chunks = s // GDN_CHUNK
    small = h3[:, :, EVEN_MAIN:EVEN_MAIN + n_small]
    to_rows = lambda a: a.transpose(0, 2, 1).reshape(b, GDN_HEADS, n_chunks, GDN_CHUNK)
    b_rows, a_rows = to_rows(small[..., :GDN_HEADS]), to_rows(small[..., GDN_HEADS:])
    headvec = lambda v: jnp.broadcast_to(v.astype(F32)[:, None, None], (GDN_HEADS, 1, GDN_CHUNK))
    o_a = _gdn(h3, conv_w.astype(F32), a_rows, b_rows, headvec(a_log), headvec(dt_bias),
               gdn_norm.astype(F32).reshape(1, HEAD_DIM))
    o_b = _hgrn(h3, lower_bound.astype(F32).reshape(HGRN_HEADS, 1, HEAD_DIM),
                hgrn_norm.astype(F32).reshape(1, HEAD_DIM))
    return [o_a.reshape(b * s, GDN_WIDTH), o_b.reshape(b * s, HGRN_WIDTH)]


def _odd_mixer(hb, b, s, w_in, cos2, sin2):
    h = _matmul(hb, w_in.astype(BF16), 1024, 768)
    h3 = h.reshape(b, s, ODD_COLS)
    o_c = _dilated(h3, cos2, sin2)
    o_d = _moba(h3, cos2, sin2)
    return [o_c.reshape(b * s, -1), o_d.reshape(b * s, -1)]


def kernel(x, ev_w_in, ev_conv_w, ev_a_log, ev_dt_bias, ev_gdn_norm, ev_hgrn_norm, hgrn_lb_logits, ev_w_out,
           od_w_in, od_w_out, router_w, router_bias, moe_w_gate, moe_w_up, moe_w_down, ln_gain, ln_bias):
    b, s, d = x.shape
    t = b * s
    cos2, sin2 = _rope_tables(s)
    lower_bounds = jnp.cumsum(jax.nn.softmax(hgrn_lb_logits.astype(F32), axis=0), axis=0)
    rw = jnp.pad(router_w.astype(F32), ((0, 0), (0, LANES - N_EXPERTS)))
    rbias = jnp.pad(router_bias.astype(F32), (0, LANES - N_EXPERTS)).reshape(1, LANES)
    vec = lambda v: v.astype(F32).reshape(1, d)

    h = x.reshape(t, d)
    hb = h.astype(BF16)
    for layer in range(DEPTH):
        if layer % 2 == 0:
            e = layer // 2
            parts = _even_mixer(hb, b, s, ev_w_in[e], ev_conv_w[e], ev_a_log[e], ev_dt_bias[e], ev_gdn_norm[e],
                                ev_hgrn_norm[e], lower_bounds[layer])
            w_out = ev_w_out[e].astype(BF16)
        else:
            o = layer // 2
            parts = _odd_mixer(hb, b, s, od_w_in[o], cos2, sin2)
            w_out = od_w_out[o].astype(BF16)
        splits = np.cumsum([p.shape[1] for p in parts])[:-1]
        weights = jnp.split(w_out, splits, axis=0)
        h, hb, logits = _out_ln(parts, weights, h, vec(ln_gain[layer, 0]), vec(ln_bias[layer, 0]), rw, 256)
        h, hb = _moe_ln(hb, logits, rbias, moe_w_gate[layer].astype(BF16), moe_w_up[layer].astype(BF16),
                        moe_w_down[layer].astype(BF16), h, vec(ln_gain[layer, 1]), vec(ln_bias[layer, 1]), 512)
    return h.reshape(b, s, d)
```

```python
import functools
import math

import jax
import jax.numpy as jnp
import numpy as np
from jax import lax
from jax.experimental import pallas as pl
from jax.experimental.pallas import tpu as pltpu

F32 = jnp.float32
BF16 = jnp.bfloat16

D_MODEL = 2048
DEPTH = 2
HEAD_DIM = 128
GDN_HEADS = 8
GDN_CONV = 4
GDN_CHUNK = 64
GDN_WIDTH = GDN_HEADS * HEAD_DIM
HGRN_HEADS = 8
HGRN_CHUNK = 16
HGRN_WIDTH = HGRN_HEADS * HEAD_DIM
DIL_GROUPS = ((128, 1), (512, 4), (2048, 16))
DIL_HEADS_PER_GROUP = 4
DIL_HEADS = len(DIL_GROUPS) * DIL_HEADS_PER_GROUP
MOBA_HEADS = 4
MOBA_BLOCK = 256
MOBA_TOPK = 3
ROPE_THETA = 10000.0
N_EXPERTS = 16
N_EXPERT_GROUPS = 4
EXPERTS_PER_GROUP = N_EXPERTS // N_EXPERT_GROUPS
D_EXPERT = 512
DEEPNORM_ALPHA = (2.0 * DEPTH) ** 0.25
LN_EPS = 1e-5
RMS_EPS = 1e-6
NEG_INF = -1e30

LANES = 128
VMEM_LIMIT = 56 * 1024 * 1024
ATT_BLOCK = 256
GDN_GROUP = 8
HGRN_ROWS = 256

EVEN_MAIN = 3 * GDN_WIDTH + GDN_WIDTH + 4 * HGRN_WIDTH
EVEN_COLS = EVEN_MAIN + LANES
ODD_COLS = 3 * DIL_HEADS * HEAD_DIM + 3 * MOBA_HEADS * HEAD_DIM


def _dot(a, b):
    return jnp.dot(a.astype(BF16), b.astype(BF16), preferred_element_type=F32)


def _dot_nt(a, b):
    return lax.dot_general(a.astype(BF16), b.astype(BF16), (((1,), (1,)), ((), ())),
                           preferred_element_type=F32)


def _dot_tn(a, b):
    return lax.dot_general(a.astype(BF16), b.astype(BF16), (((0,), (0,)), ((), ())),
                           preferred_element_type=F32)


def _dot_hi(a, b):
    return jnp.dot(a, b, preferred_element_type=F32, precision=lax.Precision.HIGHEST)


def _dot_nt_hi(a, b):
    return lax.dot_general(a, b, (((1,), (1,)), ((), ())), preferred_element_type=F32,
                           precision=lax.Precision.HIGHEST)


def _dot3(a, b):
    a_hi = a.astype(BF16)
    b_hi = b.astype(BF16)
    a_lo = (a - a_hi.astype(F32)).astype(BF16)
    b_lo = (b - b_hi.astype(F32)).astype(BF16)
    dot = functools.partial(jnp.dot, preferred_element_type=F32)
    return dot(a_hi, b_hi) + (dot(a_hi, b_lo) + dot(a_lo, b_hi))


def _dot_sel(sel, x):
    dot = functools.partial(jnp.dot, preferred_element_type=F32)
    x_hi = x.astype(BF16)
    r1 = x - x_hi.astype(F32)
    x_mid = r1.astype(BF16)
    x_lo = (r1 - x_mid.astype(F32)).astype(BF16)
    return dot(sel, x_hi) + (dot(sel, x_mid) + dot(sel, x_lo))


_dot_inv = _dot3


def _silu(x):
    return x * jax.nn.sigmoid(x)


def _iota(shape, dim):
    return lax.broadcasted_iota(jnp.int32, shape, dim)


def _params(*sem):
    return pltpu.CompilerParams(dimension_semantics=sem, vmem_limit_bytes=VMEM_LIMIT)


def _mm_kernel(x_ref, w_ref, o_ref):
    o_ref[...] = jnp.dot(x_ref[...], w_ref[...], preferred_element_type=F32).astype(o_ref.dtype)


def _matmul(x, w, tm, tn):
    m, k = x.shape
    n = w.shape[1]
    assert m % tm == 0 and n % tn == 0
    return pl.pallas_call(
        _mm_kernel,
        grid=(m // tm, n // tn),
        in_specs=[pl.BlockSpec((tm, k), lambda i, j: (i, 0)),
                  pl.BlockSpec((k, tn), lambda i, j: (0, j))],
        out_specs=pl.BlockSpec((tm, tn), lambda i, j: (i, j)),
        out_shape=jax.ShapeDtypeStruct((m, n), F32),
        compiler_params=_params("parallel", "parallel"),
        name="in_proj",
    )(x, w)


def _gdn_kernel(q_ref, k_ref, v_ref, z_ref, cwq_ref, cwk_ref, cwv_ref, a_ref, b_ref, alog_ref, dt_ref,
                gn_ref, o_ref, pad_s, q_s, k_s, v_s, gcum_s, beta_s, qe_s, ob_s, sm_s, sa_s):
    seq = q_ref.shape[0]
    c = GDN_CHUNK
    n_chunks = seq // c
    rows = 256

    pad_s[pl.ds(0, 8), :] = jnp.zeros((8, HEAD_DIM), F32)
    for x_ref, cw_ref, dst, mode in ((q_ref, cwq_ref, q_s, "q"), (k_ref, cwk_ref, k_s, "k"),
                                     (v_ref, cwv_ref, v_s, "v")):
        pad_s[pl.ds(8, seq), :] = x_ref[...]
        cw = cw_ref[...]
        for r in range(seq // rows):
            acc = None
            for j in range(GDN_CONV):
                tap = pad_s[pl.ds(8 + r * rows - (GDN_CONV - 1) + j, rows), :] * cw[j:j + 1, :]
                acc = tap if acc is None else acc + tap
            y = _silu(acc)
            if mode != "v":
                y = y * lax.rsqrt(jnp.sum(y * y, axis=-1, keepdims=True) + RMS_EPS)
            if mode == "q":
                y = y * HEAD_DIM ** -0.5
            dst[pl.ds(r * rows, rows), :] = y

    upper = (_iota((c, c), 0) <= _iota((c, c), 1)).astype(F32)
    g = -jnp.exp(alog_ref[...]) * jax.nn.softplus(a_ref[...] + dt_ref[...])
    gcum_s[...] = _dot_hi(g, upper)
    beta_s[...] = jax.nn.sigmoid(b_ref[...])

    ri = _iota((c, c), 0)
    ci = _iota((c, c), 1)
    eye = ri == ci
    strict = ri > ci
    incl = ri >= ci
    eye_f = eye.astype(F32)
    level1 = ri // 2 == ci // 2
    levels = []
    s = 2
    while s < c:
        levels.append((ri // (2 * s) == ci // (2 * s)) & ((ri // s) % 2 == 1) & ((ci // s) % 2 == 0))
        s *= 2

    dot = functools.partial(jnp.dot, preferred_element_type=F32)

    def to_col(row):
        return jnp.sum(jnp.where(eye, jnp.broadcast_to(row, (c, c)), 0.0), axis=1, keepdims=True)

    def prepare(i, _):
        n0 = i * GDN_GROUP
        grp = range(GDN_GROUP)
        starts = [pl.multiple_of((n0 + j) * c, c) for j in grp]
        qc = [q_s[pl.ds(r0, c), :] for r0 in starts]
        kc = [k_s[pl.ds(r0, c), :] for r0 in starts]
        vc = [v_s[pl.ds(r0, c), :] for r0 in starts]
        g_row = [gcum_s[pl.ds(n0 + j, 1), :] for j in grp]
        g_col = [to_col(g_row[j]) for j in grp]
        b_col = [to_col(beta_s[pl.ds(n0 + j, 1), :]) for j in grp]
        decay = [jnp.exp(jnp.where(incl, g_col[j] - g_row[j], 0.0)) for j in grp]
        n_mat = [b_col[j] * jnp.where(strict, decay[j], 0.0) * _dot_nt(kc[j], kc[j]) for j in grp]
        inv = [eye_f - jnp.where(level1, n_mat[j], 0.0) for j in grp]
        for blk in levels:
            tmp = [_dot_inv(inv[j], jnp.where(blk, n_mat[j], 0.0)) for j in grp]
            inv = [inv[j] - _dot_inv(tmp[j], inv[j]) for j in grp]
        e_col = [jnp.exp(g_col[j]) for j in grp]
        sol = [_dot_inv(inv[j], jnp.concatenate([b_col[j] * vc[j], (b_col[j] * e_col[j]) * kc[j]], axis=1))
               for j in grp]
        qk = [(_dot_nt(qc[j], kc[j]) * jnp.where(incl, decay[j], 0.0)).astype(BF16) for j in grp]
        ub = [sol[j][:, :HEAD_DIM].astype(BF16) for j in grp]
        w = [sol[j][:, HEAD_DIM:].astype(BF16) for j in grp]
        kd = [(kc[j] * jnp.exp(g_row[j][:, c - 1:c] - g_col[j])).astype(BF16) for j in grp]
        q_eff = [(qc[j] * e_col[j] - dot(qk[j], w[j])).astype(BF16) for j in grp]
        o_base = [dot(qk[j], ub[j]) for j in grp]
        s_mat = [_dot_tn(kd[j], w[j]).astype(BF16) for j in grp]
        s_add = [_dot_tn(kd[j], ub[j]) for j in grp]
        for j, r0 in enumerate(starts):
            m0 = pl.multiple_of((n0 + j) * HEAD_DIM, HEAD_DIM)
            qe_s[pl.ds(r0, c), :] = q_eff[j]
            ob_s[pl.ds(r0, c), :] = o_base[j]
            sm_s[pl.ds(m0, HEAD_DIM), :] = s_mat[j]
            sa_s[pl.ds(m0, HEAD_DIM), :] = s_add[j]
        return 0

    lax.fori_loop(0, n_chunks // GDN_GROUP, prepare, 0)

    gn = gn_ref[...]

    def chunk(n, state):
        r0 = pl.multiple_of(n * c, c)
        m0 = pl.multiple_of(n * HEAD_DIM, HEAD_DIM)
        g_last = gcum_s[pl.ds(n, 1), :][:, c - 1:c]
        lhs = jnp.concatenate([qe_s[pl.ds(r0, c), :], sm_s[pl.ds(m0, HEAD_DIM), :]], axis=0)
        prod = dot(lhs, state.astype(BF16))
        ob_s[pl.ds(r0, c), :] = prod[:c] + ob_s[pl.ds(r0, c), :]
        return jnp.exp(g_last) * state - prod[c:] + sa_s[pl.ds(m0, HEAD_DIM), :]

    lax.fori_loop(0, n_chunks, chunk, jnp.zeros((HEAD_DIM, HEAD_DIM), F32))

    for r in range(seq // rows):
        sl = pl.ds(r * rows, rows)
        o = ob_s[sl, :]
        o = o * lax.rsqrt(jnp.mean(o * o, axis=-1, keepdims=True) + RMS_EPS) * gn
        o_ref[sl, :] = (o * _silu(z_ref[sl, :])).astype(o_ref.dtype)


def _gdn(h3, conv_w, a_rows, b_rows, alog, dt, gn):
    b, s, _ = h3.shape
    nh = GDN_HEADS
    n_chunks = s // GDN_CHUNK
    col = lambda off: pl.BlockSpec((None, s, HEAD_DIM), lambda bi, hi: (bi, 0, off + hi))
    cw = lambda off: pl.BlockSpec((GDN_CONV, HEAD_DIM), lambda bi, hi: (0, off + hi))
    rowspec = pl.BlockSpec((None, None, n_chunks, GDN_CHUNK), lambda bi, hi: (bi, hi, 0, 0))
    headvec = pl.BlockSpec((None, 1, GDN_CHUNK), lambda bi, hi: (hi, 0, 0))
    return pl.pallas_call(
        _gdn_kernel,
        grid=(b, nh),
        in_specs=[col(0), col(nh), col(2 * nh), col(3 * nh), cw(0), cw(nh), cw(2 * nh),
                  rowspec, rowspec, headvec, headvec,
                  pl.BlockSpec((1, HEAD_DIM), lambda bi, hi: (0, 0))],
        out_specs=pl.BlockSpec((None, s, HEAD_DIM), lambda bi, hi: (bi, 0, hi)),
        out_shape=jax.ShapeDtypeStruct((b, s, GDN_WIDTH), BF16),
        scratch_shapes=[pltpu.VMEM((s + 8, HEAD_DIM), F32), pltpu.VMEM((s, HEAD_DIM), F32),
                        pltpu.VMEM((s, HEAD_DIM), F32), pltpu.VMEM((s, HEAD_DIM), F32),
                        pltpu.VMEM((n_chunks, GDN_CHUNK), F32), pltpu.VMEM((n_chunks, GDN_CHUNK), F32),
                        pltpu.VMEM((s, HEAD_DIM), BF16), pltpu.VMEM((s, HEAD_DIM), F32),
                        pltpu.VMEM((n_chunks * HEAD_DIM, HEAD_DIM), BF16),
                        pltpu.VMEM((n_chunks * HEAD_DIM, HEAD_DIM), F32)],
        compiler_params=_params("parallel", "parallel"),
        name="gdn",
    )(h3, h3, h3, h3, conv_w, conv_w, conv_w, a_rows, b_rows, alog, dt, gn)


def _hgrn_kernel(q_ref, f_ref, i_ref, g_ref, lb_ref, hn_ref, o_ref):
    seq = q_ref.shape[0]
    c = HGRN_CHUNK
    rows = HGRN_ROWS
    ri = _iota((rows, rows), 0)
    ci = _iota((rows, rows), 1)
    same = ri // c == ci // c
    causal = same & (ci <= ri)
    sum_mat = jnp.concatenate([causal.astype(BF16), same.astype(BF16)], axis=0)
    lb = lb_ref[...]
    hn = hn_ref[...]

    def block(n, state_t):
        r0 = pl.multiple_of(n * rows, rows)
        qc = q_ref[pl.ds(r0, rows), :]
        ic = i_ref[pl.ds(r0, rows), :].astype(BF16)
        f = lb + (1.0 - lb) * jax.nn.sigmoid(f_ref[pl.ds(r0, rows), :])
        kc = 1.0 - f
        sums = _dot_sel(sum_mat, jnp.log(f))
        bcum, b_last = sums[:rows], sums[rows:]
        q_dec = (qc * jnp.exp(bcum)).astype(BF16)
        k_inv = kc * jnp.exp(-bcum)
        k_dec = (kc * jnp.exp(b_last - bcum)).astype(BF16)
        chunk_dec = jnp.exp(b_last)
        p = jnp.where(causal, _dot_nt(q_dec, k_inv), 0.0)
        o_intra = _dot(p, ic)
        chunks = [slice(j * c, (j + 1) * c) for j in range(rows // c)]
        updates = [_dot_tn(ic[sl], k_dec[sl]) for sl in chunks]
        outs = []
        for sl, upd in zip(chunks, updates):
            outs.append(o_intra[sl] + _dot_nt(q_dec[sl], state_t))
            state_t = state_t * chunk_dec[sl.start:sl.start + 1] + upd
        o = jnp.concatenate(outs, axis=0)
        o = o * lax.rsqrt(jnp.mean(o * o, axis=-1, keepdims=True) + RMS_EPS) * hn
        o_ref[pl.ds(r0, rows), :] = (o * _silu(g_ref[pl.ds(r0, rows), :])).astype(o_ref.dtype)
        return state_t

    lax.fori_loop(0, seq // rows, block, jnp.zeros((HEAD_DIM, HEAD_DIM), F32))


def _hgrn(h3, lb, hn):
    b, s, _ = h3.shape
    nh = HGRN_HEADS
    base = 4 * GDN_HEADS
    col = lambda off: pl.BlockSpec((None, s, HEAD_DIM), lambda bi, hi: (bi, 0, base + off + hi))
    return pl.pallas_call(
        _hgrn_kernel,
        grid=(b, nh),
        in_specs=[col(0), col(nh), col(2 * nh), col(3 * nh),
                  pl.BlockSpec((None, 1, HEAD_DIM), lambda bi, hi: (hi, 0, 0)),
                  pl.BlockSpec((1, HEAD_DIM), lambda bi, hi: (0, 0))],
        out_specs=pl.BlockSpec((None, s, HEAD_DIM), lambda bi, hi: (bi, 0, hi)),
        out_shape=jax.ShapeDtypeStruct((b, s, HGRN_WIDTH), BF16),
        compiler_params=_params("parallel", "parallel"),
        name="hgrn2",
    )(h3, h3, h3, h3, lb, hn)


def _rope(x, cos2, sin2):
    return x * cos2 + pltpu.roll(x, HEAD_DIM // 2, axis=1) * sin2


def _flash_step(q_blk, k_blk, v_blk, mask, carry):
    m, l, acc = carry
    s = jnp.where(mask, _dot_nt(q_blk, k_blk) * HEAD_DIM ** -0.5, NEG_INF)
    m_new = jnp.maximum(m, jnp.max(s, axis=-1, keepdims=True))
    alpha = jnp.exp(m - m_new)
    p = jnp.exp(s - m_new)
    l = alpha * l + jnp.sum(p, axis=-1, keepdims=True)
    acc = alpha * acc + _dot(p, v_blk)
    return m_new, l, acc


def _flash_init():
    blk = ATT_BLOCK
    return (jnp.full((blk, 1), NEG_INF, F32), jnp.zeros((blk, 1), F32), jnp.zeros((blk, HEAD_DIM), F32))


def _dilated_kernel(*refs):
    n_g = len(DIL_GROUPS)
    q_refs, k_refs, v_refs = refs[0:n_g], refs[n_g:2 * n_g], refs[2 * n_g:3 * n_g]
    cos_ref, sin_ref, o_ref = refs[3 * n_g:3 * n_g + 3]
    q_s, k_s, v_s = refs[3 * n_g + 3:]
    seq = o_ref.shape[0]
    blk = ATT_BLOCK
    n_blk = seq // blk

    cos2 = cos_ref[...]
    sin2 = sin_ref[...]
    for gi in range(n_g):
        q_s[gi] = _rope(q_refs[gi][...], cos2, sin2).astype(BF16)
        k_s[gi] = _rope(k_refs[gi][...], cos2, sin2).astype(BF16)
        v_s[gi] = v_refs[gi][...].astype(BF16)

    rel0 = _iota((blk, blk), 0) - _iota((blk, blk), 1)

    def q_block(qb, _):
        q0 = pl.multiple_of(qb * blk, blk)
        outs, lses = [], []
        for gi, (window, dilation) in enumerate(DIL_GROUPS):
            q_blk = q_s[gi, pl.ds(q0, blk), :]
            n_prev = -(-window // blk)

            def k_block(kb, carry, gi=gi, window=window, dilation=dilation, q_blk=q_blk):
                k0 = pl.multiple_of(kb * blk, blk)
                rel = rel0 + (qb - kb) * blk
                mask = (rel >= 0) & (rel <= window) & ((rel & (dilation - 1)) == 0)
                return _flash_step(q_blk, k_s[gi, pl.ds(k0, blk), :], v_s[gi, pl.ds(k0, blk), :], mask, carry)

            m, l, acc = lax.fori_loop(jnp.maximum(qb - n_prev, 0), qb + 1, k_block, _flash_init())
            outs.append(acc / l)
            lses.append(m + jnp.log(l))
        top = functools.reduce(jnp.maximum, lses)
        wts = [jnp.exp(x - top) for x in lses]
        den = functools.reduce(lambda a, b: a + b, wts)
        o = functools.reduce(lambda a, b: a + b, [w * x for w, x in zip(wts, outs)]) / den
        o_ref[pl.ds(q0, blk), :] = o.astype(o_ref.dtype)
        return 0

    lax.fori_loop(0, n_blk, q_block, 0)


def _dilated(h3, cos2, sin2):
    b, s, _ = h3.shape
    hpg = DIL_HEADS_PER_GROUP
    n_g = len(DIL_GROUPS)
    col = lambda off: pl.BlockSpec((None, s, HEAD_DIM), lambda bi, hi: (bi, 0, off + hi))
    tab = pl.BlockSpec((s, HEAD_DIM), lambda bi, hi: (0, 0))
    specs = [col(part * DIL_HEADS + gi * hpg) for part in range(3) for gi in range(n_g)]
    return pl.pallas_call(
        _dilated_kernel,
        grid=(b, hpg),
        in_specs=specs + [tab, tab],
        out_specs=pl.BlockSpec((None, s, HEAD_DIM), lambda bi, hi: (bi, 0, hi)),
        out_shape=jax.ShapeDtypeStruct((b, s, hpg * HEAD_DIM), BF16),
        scratch_shapes=[pltpu.VMEM((n_g, s, HEAD_DIM), BF16)] * 3,
        compiler_params=_params("parallel", "parallel"),
        name="dilated_attention",
    )(*([h3] * (3 * n_g)), cos2, sin2)


def _moba_kernel(q_ref, k_ref, v_ref, cos_ref, sin_ref, o_ref, qf_s, q_s, k_s, v_s, km_s):
    seq = o_ref.shape[0]
    blk = MOBA_BLOCK
    n_blk = seq // blk
    cos2 = cos_ref[...]
    sin2 = sin_ref[...]
    q = _rope(q_ref[...], cos2, sin2)
    qf_s[...] = q
    q_s[...] = q.astype(BF16)
    km_s[...] = jnp.zeros(km_s.shape, F32)
    for nb in range(n_blk):
        kb = _rope(k_ref[pl.ds(nb * blk, blk), :], cos2[nb * blk:(nb + 1) * blk], sin2[nb * blk:(nb + 1) * blk])
        k_s[pl.ds(nb * blk, blk), :] = kb.astype(BF16)
        km_s[pl.ds(nb, 1), :] = jnp.mean(kb, axis=0, keepdims=True)
    v_s[...] = v_ref[...].astype(BF16)

    lane = _iota((blk, LANES), 1)
    causal = _iota((blk, blk), 0) >= _iota((blk, blk), 1)
    all_true = _iota((blk, blk), 0) >= 0

    def q_block(qb, _):
        q0 = pl.multiple_of(qb * blk, blk)
        gate = _dot_nt_hi(qf_s[pl.ds(q0, blk), :], km_s[...])
        gate = jnp.where(lane < qb, gate, -jnp.inf)
        sel = jnp.zeros((blk, LANES), F32)
        for _k in range(MOBA_TOPK):
            best = jnp.max(gate, axis=-1, keepdims=True)
            first = jnp.min(jnp.where(gate == best, lane, LANES), axis=-1, keepdims=True)
            pick = (lane == first) & (best > -jnp.inf)
            sel = jnp.where(pick, 1.0, sel)
            gate = jnp.where(pick, -jnp.inf, gate)
        q_blk = q_s[pl.ds(q0, blk), :]

        def k_block(kb, carry):
            k0 = pl.multiple_of(kb * blk, blk)
            chosen = jnp.sum(jnp.where(lane == kb, sel, 0.0), axis=-1, keepdims=True) > 0.0
            return _flash_step(q_blk, k_s[pl.ds(k0, blk), :], v_s[pl.ds(k0, blk), :], chosen & all_true, carry)

        carry = lax.fori_loop(0, qb, k_block, _flash_init())
        m, l, acc = _flash_step(q_blk, k_s[pl.ds(q0, blk), :], v_s[pl.ds(q0, blk), :], causal, carry)
        o_ref[pl.ds(q0, blk), :] = (acc / l).astype(o_ref.dtype)
        return 0

    lax.fori_loop(0, n_blk, q_block, 0)


def _moba(h3, cos2, sin2):
    b, s, _ = h3.shape
    base = 3 * DIL_HEADS
    col = lambda off: pl.BlockSpec((None, s, HEAD_DIM), lambda bi, hi: (bi, 0, base + off + hi))
    tab = pl.BlockSpec((s, HEAD_DIM), lambda bi, hi: (0, 0))
    return pl.pallas_call(
        _moba_kernel,
        grid=(b, MOBA_HEADS),
        in_specs=[col(0), col(MOBA_HEADS), col(2 * MOBA_HEADS), tab, tab],
        out_specs=pl.BlockSpec((None, s, HEAD_DIM), lambda bi, hi: (bi, 0, hi)),
        out_shape=jax.ShapeDtypeStruct((b, s, MOBA_HEADS * HEAD_DIM), BF16),
        scratch_shapes=[pltpu.VMEM((s, HEAD_DIM), F32), pltpu.VMEM((s, HEAD_DIM), BF16),
                        pltpu.VMEM((s, HEAD_DIM), BF16), pltpu.VMEM((s, HEAD_DIM), BF16),
                        pltpu.VMEM((LANES, HEAD_DIM), F32)],
        compiler_params=_params("parallel", "parallel"),
        name="moba_attention",
    )(h3, h3, h3, cos2, sin2)


def _layer_norm(x, gain, bias):
    mu = jnp.mean(x, axis=-1, keepdims=True)
    xc = x - mu
    var = jnp.mean(xc * xc, axis=-1, keepdims=True)
    return xc * lax.rsqrt(var + LN_EPS) * gain + bias


def _out_ln_kernel(*refs, n_parts):
    o_refs = refs[0:n_parts]
    w_refs = refs[n_parts:2 * n_parts]
    h_ref, gain_ref, bias_ref, rw_ref, y_ref, yb_ref, logit_ref = refs[2 * n_parts:]
    mix = None
    for o_r, w_r in zip(o_refs, w_refs):
        part = jnp.dot(o_r[...], w_r[...], preferred_element_type=F32)
        mix = part if mix is None else mix + part
    y = _layer_norm(DEEPNORM_ALPHA * h_ref[...] + mix, gain_ref[...], bias_ref[...])
    y_ref[...] = y
    yb_ref[...] = y.astype(BF16)
    logit_ref[...] = _dot_hi(y, rw_ref[...])


def _out_ln(parts, weights, h, gain, bias, router_w, tm):
    t, d = h.shape
    n_parts = len(parts)
    row = lambda width: pl.BlockSpec((tm, width), lambda i: (i, 0))
    full = lambda a: pl.BlockSpec(a.shape, lambda i: (0, 0))
    return pl.pallas_call(
        functools.partial(_out_ln_kernel, n_parts=n_parts),
        grid=(t // tm,),
        in_specs=[row(p.shape[1]) for p in parts] + [full(w) for w in weights]
                 + [row(d), full(gain), full(bias), full(router_w)],
        out_specs=[row(d), row(d), row(LANES)],
        out_shape=[jax.ShapeDtypeStruct((t, d), F32), jax.ShapeDtypeStruct((t, d), BF16),
                   jax.ShapeDtypeStruct((t, LANES), F32)],
        compiler_params=_params("parallel"),
        name="out_proj_ln",
    )(*parts, *weights, h, gain, bias, router_w)


def _route(logits, rbias):
    rows = logits.shape[0]
    lane = _iota((rows, LANES), 1)
    real = lane < N_EXPERTS
    scores = jax.nn.sigmoid(logits)
    biased = jnp.where(real, scores + rbias, -jnp.inf)

    def first_argmax(vals):
        best = jnp.max(vals, axis=-1, keepdims=True)
        return best, jnp.min(jnp.where(vals == best, lane, LANES), axis=-1, keepdims=True)

    best_score = None
    best_group = None
    for g in range(N_EXPERT_GROUPS):
        vals = jnp.where(lane // EXPERTS_PER_GROUP == g, biased, -jnp.inf)
        top1, idx1 = first_argmax(vals)
        top2, _ = first_argmax(jnp.where(lane == idx1, -jnp.inf, vals))
        score = top1 + top2
        if g == 0:
            best_score, best_group = score, jnp.zeros_like(idx1)
        else:
            better = score > best_score
            best_group = jnp.where(better, g, best_group)
            best_score = jnp.where(better, score, best_score)
    masked = jnp.where(real, jnp.where(lane // EXPERTS_PER_GROUP == best_group, biased, NEG_INF), -jnp.inf)
    _, i1 = first_argmax(masked)
    _, i2 = first_argmax(jnp.where(lane == i1, -jnp.inf, masked))
    s1 = jnp.sum(jnp.where(lane == i1, scores, 0.0), axis=-1, keepdims=True)
    s2 = jnp.sum(jnp.where(lane == i2, scores, 0.0), axis=-1, keepdims=True)
    tot = s1 + s2
    return jnp.where(lane == i1, s1 / tot, 0.0) + jnp.where(lane == i2, s2 / tot, 0.0)


def _moe_kernel(x_ref, logit_ref, rbias_ref, wg_ref, wu_ref, wd_ref, h_ref, gain_ref, bias_ref,
                y_ref, yb_ref, acc_s, gates_s):
    e = pl.program_id(1)

    @pl.when(e == 0)
    def _():
        gates_s[...] = _route(logit_ref[...], rbias_ref[...])
        acc_s[...] = jnp.zeros(acc_s.shape, F32)

    gates = gates_s[...]
    gate_e = jnp.sum(jnp.where(_iota(gates.shape, 1) == e, gates, 0.0), axis=-1, keepdims=True)
    x = x_ref[...]
    hid = _silu(jnp.dot(x, wg_ref[...], preferred_element_type=F32)) * jnp.dot(
        x, wu_ref[...], preferred_element_type=F32)
    acc_s[...] += _dot(hid * gate_e, wd_ref[...])

    @pl.when(e == pl.num_programs(1) - 1)
    def _():
        y = _layer_norm(DEEPNORM_ALPHA * h_ref[...] + acc_s[...], gain_ref[...], bias_ref[...])
        y_ref[...] = y
        yb_ref[...] = y.astype(BF16)


def _moe_ln(xb, logits, rbias, wg, wu, wd, h, gain, bias, tm):
    t, d = h.shape
    n_e, _, f = wg.shape
    row = lambda width: pl.BlockSpec((tm, width), lambda i, e: (i, 0))
    full = lambda a: pl.BlockSpec(a.shape, lambda i, e: (0, 0))
    return pl.pallas_call(
        _moe_kernel,
        grid=(t // tm, n_e),
        in_specs=[row(d), row(LANES), full(rbias),
                  pl.BlockSpec((None, d, f), lambda i, e: (e, 0, 0)),
                  pl.BlockSpec((None, d, f), lambda i, e: (e, 0, 0)),
                  pl.BlockSpec((None, f, d), lambda i, e: (e, 0, 0)),
                  row(d), full(gain), full(bias)],
        out_specs=[row(d), row(d)],
        out_shape=[jax.ShapeDtypeStruct((t, d), F32), jax.ShapeDtypeStruct((t, d), BF16)],
        scratch_shapes=[pltpu.VMEM((tm, d), F32), pltpu.VMEM((tm, LANES), F32)],
        compiler_params=_params("parallel", "arbitrary"),
        name="moe_ln",
    )(xb, logits, rbias, wg, wu, wd, h, gain, bias)


def _rope_tables(seq):
    inv_freq = ROPE_THETA ** (-jnp.arange(0, HEAD_DIM, 2, dtype=F32) / HEAD_DIM)
    ang = jnp.arange(seq, dtype=F32)[:, None] * inv_freq[None, :]
    cos, sin = jnp.cos(ang), jnp.sin(ang)
    return jnp.concatenate([cos, cos], axis=-1), jnp.concatenate([-sin, sin], axis=-1)


def _even_mixer(hb, b, s, w_in, conv_w, a_log, dt_bias, gdn_norm, hgrn_norm, lower_bound):
    gw = GDN_WIDTH
    n_small = 2 * GDN_HEADS
    tail0 = 4 * gw
    w_perm = jnp.concatenate([w_in[:, :tail0], w_in[:, tail0 + n_small:], w_in[:, tail0:tail0 + n_small],
                              jnp.zeros((w_in.shape[0], LANES - n_small), w_in.dtype)], axis=1).astype(BF16)
    h = _matmul(hb, w_perm, 1024, 640)
    h3 = h.reshape(b, s, EVEN_COLS)
    n_chunks = s // GDN_CHUNK
    small = h3[:, :, EVEN_MAIN:EVEN_MAIN + n_small]
    to_rows = lambda a: a.transpose(0, 2, 1).reshape(b, GDN_HEADS, n_chunks, GDN_CHUNK)
    b_rows, a_rows = to_rows(small[..., :GDN_HEADS]), to_rows(small[..., GDN_HEADS:])
    headvec = lambda v: jnp.broadcast_to(v.astype(F32)[:, None, None], (GDN_HEADS, 1, GDN_CHUNK))
    o_a = _gdn(h3, conv_w.astype(F32), a_rows, b_rows, headvec(a_log), headvec(dt_bias),
               gdn_norm.astype(F32).reshape(1, HEAD_DIM))
    o_b = _hgrn(h3, lower_bound.astype(F32).reshape(HGRN_HEADS, 1, HEAD_DIM),
                hgrn_norm.astype(F32).reshape(1, HEAD_DIM))
    return [o_a.reshape(b * s, GDN_WIDTH), o_b.reshape(b * s, HGRN_WIDTH)]


def _odd_mixer(hb, b, s, w_in, cos2, sin2):
    h = _matmul(hb, w_in.astype(BF16), 1024, 768)
    h3 = h.reshape(b, s, ODD_COLS)
    o_c = _dilated(h3, cos2, sin2)
    o_d = _moba(h3, cos2, sin2)
    return [o_c.reshape(b * s, -1), o_d.reshape(b * s, -1)]


def kernel(x, ev_w_in, ev_conv_w, ev_a_log, ev_dt_bias, ev_gdn_norm, ev_hgrn_norm, hgrn_lb_logits, ev_w_out,
           od_w_in, od_w_out, router_w, router_bias, moe_w_gate, moe_w_up, moe_w_down, ln_gain, ln_bias):
    b, s, d = x.shape
    t = b * s
    cos2, sin2 = _rope_tables(s)
    lower_bounds = jnp.cumsum(jax.nn.softmax(hgrn_lb_logits.astype(F32), axis=0), axis=0)
    rw = jnp.pad(router_w.astype(F32), ((0, 0), (0, LANES - N_EXPERTS)))
    rbias = jnp.pad(router_bias.astype(F32), (0, LANES - N_EXPERTS)).reshape(1, LANES)
    vec = lambda v: v.astype(F32).reshape(1, d)

    h = x.reshape(t, d)
    hb = h.astype(BF16)
    for layer in range(DEPTH):
        if layer % 2 == 0:
            e = layer // 2
            parts = _even_mixer(hb, b, s, ev_w_in[e], ev_conv_w[e], ev_a_log[e], ev_dt_bias[e], ev_gdn_norm[e],
                                ev_hgrn_norm[e], lower_bounds[layer])
            w_out = ev_w_out[e].astype(BF16)
        else:
            o = layer // 2
            parts = _odd_mixer(hb, b, s, od_w_in[o], cos2, sin2)
            w_out = od_w_out[o].astype(BF16)
        splits = np.cumsum([p.shape[1] for p in parts])[:-1]
        weights = jnp.split(w_out, splits, axis=0)
        h, hb, logits = _out_ln(parts, weights, h, vec(ln_gain[layer, 0]), vec(ln_bias[layer, 0]), rw, 256)
        h, hb = _moe_ln(hb, logits, rbias, moe_w_gate[layer].astype(BF16), moe_w_up[layer].astype(BF16),
                        moe_w_down[layer].astype(BF16), h, vec(ln_gain[layer, 1]), vec(ln_bias[layer, 1]), 512)
    return h.reshape(b, s, d)
```

```python
import functools
import math

import jax
import jax.numpy as jnp
import numpy as np
from jax import lax
from jax.experimental import pallas as pl
from jax.experimental.pallas import tpu as pltpu

F32 = jnp.float32
BF16 = jnp.bfloat16

D_MODEL = 2048
DEPTH = 2
HEAD_DIM = 128
GDN_HEADS = 8
GDN_CONV = 4
GDN_CHUNK = 64
GDN_WIDTH = GDN_HEADS * HEAD_DIM
HGRN_HEADS = 8
HGRN_CHUNK = 16
HGRN_WIDTH = HGRN_HEADS * HEAD_DIM
DIL_GROUPS = ((128, 1), (512, 4), (2048, 16))
DIL_HEADS_PER_GROUP = 4
DIL_HEADS = len(DIL_GROUPS) * DIL_HEADS_PER_GROUP
MOBA_HEADS = 4
MOBA_BLOCK = 256
MOBA_TOPK = 3
ROPE_THETA = 10000.0
N_EXPERTS = 16
N_EXPERT_GROUPS = 4
EXPERTS_PER_GROUP = N_EXPERTS // N_EXPERT_GROUPS
D_EXPERT = 512
MOE_TOPK = 2
MOE_BLOCK_ROWS = 256
DEEPNORM_ALPHA = (2.0 * DEPTH) ** 0.25
LN_EPS = 1e-5
RMS_EPS = 1e-6
NEG_INF = -1e30

LANES = 128
VMEM_LIMIT = 56 * 1024 * 1024
ATT_BLOCK = 256
GDN_GROUP = 8
HGRN_ROWS = 256

EVEN_MAIN = 3 * GDN_WIDTH + GDN_WIDTH + 4 * HGRN_WIDTH
EVEN_COLS = EVEN_MAIN + LANES
ODD_COLS = 3 * DIL_HEADS * HEAD_DIM + 3 * MOBA_HEADS * HEAD_DIM


def _dot(a, b):
    return jnp.dot(a.astype(BF16), b.astype(BF16), preferred_element_type=F32)


def _dot_nt(a, b):
    return lax.dot_general(a.astype(BF16), b.astype(BF16), (((1,), (1,)), ((), ())),
                           preferred_element_type=F32)


def _dot_tn(a, b):
    return lax.dot_general(a.astype(BF16), b.astype(BF16), (((0,), (0,)), ((), ())),
                           preferred_element_type=F32)


def _dot_hi(a, b):
    return jnp.dot(a, b, preferred_element_type=F32, precision=lax.Precision.HIGHEST)


def _dot_nt_hi(a, b):
    return lax.dot_general(a, b, (((1,), (1,)), ((), ())), preferred_element_type=F32,
                           precision=lax.Precision.HIGHEST)


def _dot3(a, b):
    a_hi = a.astype(BF16)
    b_hi = b.astype(BF16)
    a_lo = (a - a_hi.astype(F32)).astype(BF16)
    b_lo = (b - b_hi.astype(F32)).astype(BF16)
    dot = functools.partial(jnp.dot, preferred_element_type=F32)
    return dot(a_hi, b_hi) + (dot(a_hi, b_lo) + dot(a_lo, b_hi))


def _dot_sel(sel, x):
    dot = functools.partial(jnp.dot, preferred_element_type=F32)
    x_hi = x.astype(BF16)
    r1 = x - x_hi.astype(F32)
    x_mid = r1.astype(BF16)
    x_lo = (r1 - x_mid.astype(F32)).astype(BF16)
    return dot(sel, x_hi) + (dot(sel, x_mid) + dot(sel, x_lo))


_dot_inv = _dot3


def _silu(x):
    return x * jax.nn.sigmoid(x)


def _iota(shape, dim):
    return lax.broadcasted_iota(jnp.int32, shape, dim)


def _params(*sem):
    return pltpu.CompilerParams(dimension_semantics=sem, vmem_limit_bytes=VMEM_LIMIT)


def _mm_kernel(x_ref, w_ref, o_ref):
    o_ref[...] = jnp.dot(x_ref[...], w_ref[...], preferred_element_type=F32).astype(o_ref.dtype)


def _matmul(x, w, tm, tn):
    m, k = x.shape
    n = w.shape[1]
    assert m % tm == 0 and n % tn == 0
    return pl.pallas_call(
        _mm_kernel,
        grid=(m // tm, n // tn),
        in_specs=[pl.BlockSpec((tm, k), lambda i, j: (i, 0)),
                  pl.BlockSpec((k, tn), lambda i, j: (0, j))],
        out_specs=pl.BlockSpec((tm, tn), lambda i, j: (i, j)),
        out_shape=jax.ShapeDtypeStruct((m, n), F32),
        compiler_params=_params("parallel", "parallel"),
        name="in_proj",
    )(x, w)


def _gdn_kernel(q_ref, k_ref, v_ref, z_ref, cwq_ref, cwk_ref, cwv_ref, a_ref, b_ref, alog_ref, dt_ref,
                gn_ref, o_ref, pad_s, q_s, k_s, v_s, gcum_s, beta_s, qe_s, ob_s, sm_s, sa_s):
    seq = q_ref.shape[0]
    c = GDN_CHUNK
    n_chunks = seq // c
    rows = 256

    pad_s[pl.ds(0, 8), :] = jnp.zeros((8, HEAD_DIM), F32)
    for x_ref, cw_ref, dst, mode in ((q_ref, cwq_ref, q_s, "q"), (k_ref, cwk_ref, k_s, "k"),
                                     (v_ref, cwv_ref, v_s, "v")):
        pad_s[pl.ds(8, seq), :] = x_ref[...]
        cw = cw_ref[...]
        for r in range(seq // rows):
            acc = None
            for j in range(GDN_CONV):
                tap = pad_s[pl.ds(8 + r * rows - (GDN_CONV - 1) + j, rows), :] * cw[j:j + 1, :]
                acc = tap if acc is None else acc + tap
            y = _silu(acc)
            if mode != "v":
                y = y * lax.rsqrt(jnp.sum(y * y, axis=-1, keepdims=True) + RMS_EPS)
            if mode == "q":
                y = y * HEAD_DIM ** -0.5
            dst[pl.ds(r * rows, rows), :] = y

    upper = (_iota((c, c), 0) <= _iota((c, c), 1)).astype(F32)
    g = -jnp.exp(alog_ref[...]) * jax.nn.softplus(a_ref[...] + dt_ref[...])
    gcum_s[...] = _dot_hi(g, upper)
    beta_s[...] = jax.nn.sigmoid(b_ref[...])

    ri = _iota((c, c), 0)
    ci = _iota((c, c), 1)
    eye = ri == ci
    strict = ri > ci
    incl = ri >= ci
    eye_f = eye.astype(F32)
    level1 = ri // 2 == ci // 2
    levels = []
    s = 2
    while s < c:
        levels.append((ri // (2 * s) == ci // (2 * s)) & ((ri // s) % 2 == 1) & ((ci // s) % 2 == 0))
        s *= 2

    dot = functools.partial(jnp.dot, preferred_element_type=F32)

    def to_col(row):
        return jnp.sum(jnp.where(eye, jnp.broadcast_to(row, (c, c)), 0.0), axis=1, keepdims=True)

    def prepare(i, _):
        n0 = i * GDN_GROUP
        grp = range(GDN_GROUP)
        starts = [pl.multiple_of((n0 + j) * c, c) for j in grp]
        qc = [q_s[pl.ds(r0, c), :] for r0 in starts]
        kc = [k_s[pl.ds(r0, c), :] for r0 in starts]
        vc = [v_s[pl.ds(r0, c), :] for r0 in starts]
        g_row = [gcum_s[pl.ds(n0 + j, 1), :] for j in grp]
        g_col = [to_col(g_row[j]) for j in grp]
        b_col = [to_col(beta_s[pl.ds(n0 + j, 1), :]) for j in grp]
        decay = [jnp.exp(jnp.where(incl, g_col[j] - g_row[j], 0.0)) for j in grp]
        n_mat = [b_col[j] * jnp.where(strict, decay[j], 0.0) * _dot_nt(kc[j], kc[j]) for j in grp]
        inv = [eye_f - jnp.where(level1, n_mat[j], 0.0) for j in grp]
        for blk in levels:
            tmp = [_dot_inv(inv[j], jnp.where(blk, n_mat[j], 0.0)) for j in grp]
            inv = [inv[j] - _dot_inv(tmp[j], inv[j]) for j in grp]
        e_col = [jnp.exp(g_col[j]) for j in grp]
        sol = [_dot_inv(inv[j], jnp.concatenate([b_col[j] * vc[j], (b_col[j] * e_col[j]) * kc[j]], axis=1))
               for j in grp]
        qk = [(_dot_nt(qc[j], kc[j]) * jnp.where(incl, decay[j], 0.0)).astype(BF16) for j in grp]
        ub = [sol[j][:, :HEAD_DIM].astype(BF16) for j in grp]
        w = [sol[j][:, HEAD_DIM:].astype(BF16) for j in grp]
        kd = [(kc[j] * jnp.exp(g_row[j][:, c - 1:c] - g_col[j])).astype(BF16) for j in grp]
        q_eff = [(qc[j] * e_col[j] - dot(qk[j], w[j])).astype(BF16) for j in grp]
        o_base = [dot(qk[j], ub[j]) for j in grp]
        s_mat = [_dot_tn(kd[j], w[j]).astype(BF16) for j in grp]
        s_add = [_dot_tn(kd[j], ub[j]) for j in grp]
        for j, r0 in enumerate(starts):
            m0 = pl.multiple_of((n0 + j) * HEAD_DIM, HEAD_DIM)
            qe_s[pl.ds(r0, c), :] = q_eff[j]
            ob_s[pl.ds(r0, c), :] = o_base[j]
            sm_s[pl.ds(m0, HEAD_DIM), :] = s_mat[j]
            sa_s[pl.ds(m0, HEAD_DIM), :] = s_add[j]
        return 0

    lax.fori_loop(0, n_chunks // GDN_GROUP, prepare, 0)

    gn = gn_ref[...]

    def chunk(n, state):
        r0 = pl.multiple_of(n * c, c)
        m0 = pl.multiple_of(n * HEAD_DIM, HEAD_DIM)
        g_last = gcum_s[pl.ds(n, 1), :][:, c - 1:c]
        lhs = jnp.concatenate([qe_s[pl.ds(r0, c), :], sm_s[pl.ds(m0, HEAD_DIM), :]], axis=0)
        prod = dot(lhs, state.astype(BF16))
        ob_s[pl.ds(r0, c), :] = prod[:c] + ob_s[pl.ds(r0, c), :]
        return jnp.exp(g_last) * state - prod[c:] + sa_s[pl.ds(m0, HEAD_DIM), :]

    lax.fori_loop(0, n_chunks, chunk, jnp.zeros((HEAD_DIM, HEAD_DIM), F32))

    for r in range(seq // rows):
        sl = pl.ds(r * rows, rows)
        o = ob_s[sl, :]
        o = o * lax.rsqrt(jnp.mean(o * o, axis=-1, keepdims=True) + RMS_EPS) * gn
        o_ref[sl, :] = (o * _silu(z_ref[sl, :])).astype(o_ref.dtype)


def _gdn(h3, conv_w, a_rows, b_rows, alog, dt, gn):
    b, s, _ = h3.shape
    nh = GDN_HEADS
    n_chunks = s // GDN_CHUNK
    col = lambda off: pl.BlockSpec((None, s, HEAD_DIM), lambda bi, hi: (bi, 0, off + hi))
    cw = lambda off: pl.BlockSpec((GDN_CONV, HEAD_DIM), lambda bi, hi: (0, off + hi))
    rowspec = pl.BlockSpec((None, None, n_chunks, GDN_CHUNK), lambda bi, hi: (bi, hi, 0, 0))
    headvec = pl.BlockSpec((None, 1, GDN_CHUNK), lambda bi, hi: (hi, 0, 0))
    return pl.pallas_call(
        _gdn_kernel,
        grid=(b, nh),
        in_specs=[col(0), col(nh), col(2 * nh), col(3 * nh), cw(0), cw(nh), cw(2 * nh),
                  rowspec, rowspec, headvec, headvec,
                  pl.BlockSpec((1, HEAD_DIM), lambda bi, hi: (0, 0))],
        out_specs=pl.BlockSpec((None, s, HEAD_DIM), lambda bi, hi: (bi, 0, hi)),
        out_shape=jax.ShapeDtypeStruct((b, s, GDN_WIDTH), BF16),
        scratch_shapes=[pltpu.VMEM((s + 8, HEAD_DIM), F32), pltpu.VMEM((s, HEAD_DIM), F32),
                        pltpu.VMEM((s, HEAD_DIM), F32), pltpu.VMEM((s, HEAD_DIM), F32),
                        pltpu.VMEM((n_chunks, GDN_CHUNK), F32), pltpu.VMEM((n_chunks, GDN_CHUNK), F32),
                        pltpu.VMEM((s, HEAD_DIM), BF16), pltpu.VMEM((s, HEAD_DIM), F32),
                        pltpu.VMEM((n_chunks * HEAD_DIM, HEAD_DIM), BF16),
                        pltpu.VMEM((n_chunks * HEAD_DIM, HEAD_DIM), F32)],
        compiler_params=_params("parallel", "parallel"),
        name="gdn",
    )(h3, h3, h3, h3, conv_w, conv_w, conv_w, a_rows, b_rows, alog, dt, gn)


def _hgrn_kernel(q_ref, f_ref, i_ref, g_ref, lb_ref, hn_ref, o_ref):
    seq = q_ref.shape[0]
    c = HGRN_CHUNK
    rows = HGRN_ROWS
    ri = _iota((rows, rows), 0)
    ci = _iota((rows, rows), 1)
    same = ri // c == ci // c
    causal = same & (ci <= ri)
    sum_mat = jnp.concatenate([causal.astype(BF16), same.astype(BF16)], axis=0)
    lb = lb_ref[...]
    hn = hn_ref[...]

    def block(n, state_t):
        r0 = pl.multiple_of(n * rows, rows)
        qc = q_ref[pl.ds(r0, rows), :]
        ic = i_ref[pl.ds(r0, rows), :].astype(BF16)
        f = lb + (1.0 - lb) * jax.nn.sigmoid(f_ref[pl.ds(r0, rows), :])
        kc = 1.0 - f
        sums = _dot_sel(sum_mat, jnp.log(f))
        bcum, b_last = sums[:rows], sums[rows:]
        q_dec = (qc * jnp.exp(bcum)).astype(BF16)
        k_inv = kc * jnp.exp(-bcum)
        k_dec = (kc * jnp.exp(b_last - bcum)).astype(BF16)
        chunk_dec = jnp.exp(b_last)
        p = jnp.where(causal, _dot_nt(q_dec, k_inv), 0.0)
        o_intra = _dot(p, ic)
        chunks = [slice(j * c, (j + 1) * c) for j in range(rows // c)]
        updates = [_dot_tn(ic[sl], k_dec[sl]) for sl in chunks]
        outs = []
        for sl, upd in zip(chunks, updates):
            outs.append(o_intra[sl] + _dot_nt(q_dec[sl], state_t))
            state_t = state_t * chunk_dec[sl.start:sl.start + 1] + upd
        o = jnp.concatenate(outs, axis=0)
        o = o * lax.rsqrt(jnp.mean(o * o, axis=-1, keepdims=True) + RMS_EPS) * hn
        o_ref[pl.ds(r0, rows), :] = (o * _silu(g_ref[pl.ds(r0, rows), :])).astype(o_ref.dtype)
        return state_t

    lax.fori_loop(0, seq // rows, block, jnp.zeros((HEAD_DIM, HEAD_DIM), F32))


def _hgrn(h3, lb, hn):
    b, s, _ = h3.shape
    nh = HGRN_HEADS
    base = 4 * GDN_HEADS
    col = lambda off: pl.BlockSpec((None, s, HEAD_DIM), lambda bi, hi: (bi, 0, base + off + hi))
    return pl.pallas_call(
        _hgrn_kernel,
        grid=(b, nh),
        in_specs=[col(0), col(nh), col(2 * nh), col(3 * nh),
                  pl.BlockSpec((None, 1, HEAD_DIM), lambda bi, hi: (hi, 0, 0)),
                  pl.BlockSpec((1, HEAD_DIM), lambda bi, hi: (0, 0))],
        out_specs=pl.BlockSpec((None, s, HEAD_DIM), lambda bi, hi: (bi, 0, hi)),
        out_shape=jax.ShapeDtypeStruct((b, s, HGRN_WIDTH), BF16),
        compiler_params=_params("parallel", "parallel"),
        name="hgrn2",
    )(h3, h3, h3, h3, lb, hn)


def _rope(x, cos2, sin2):
    return x * cos2 + pltpu.roll(x, HEAD_DIM // 2, axis=1) * sin2


def _flash_step(q_blk, k_blk, v_blk, mask, carry):
    m, l, acc = carry
    s = jnp.where(mask, _dot_nt(q_blk, k_blk) * HEAD_DIM ** -0.5, NEG_INF)
    m_new = jnp.maximum(m, jnp.max(s, axis=-1, keepdims=True))
    alpha = jnp.exp(m - m_new)
    p = jnp.exp(s - m_new)
    l = alpha * l + jnp.sum(p, axis=-1, keepdims=True)
    acc = alpha * acc + _dot(p, v_blk)
    return m_new, l, acc


def _flash_init():
    blk = ATT_BLOCK
    return (jnp.full((blk, 1), NEG_INF, F32), jnp.zeros((blk, 1), F32), jnp.zeros((blk, HEAD_DIM), F32))


def _dilated_kernel(*refs):
    n_g = len(DIL_GROUPS)
    q_refs, k_refs, v_refs = refs[0:n_g], refs[n_g:2 * n_g], refs[2 * n_g:3 * n_g]
    cos_ref, sin_ref, o_ref = refs[3 * n_g:3 * n_g + 3]
    q_s, k_s, v_s = refs[3 * n_g + 3:]
    seq = o_ref.shape[0]
    blk = ATT_BLOCK
    n_blk = seq // blk

    cos2 = cos_ref[...]
    sin2 = sin_ref[...]
    for gi in range(n_g):
        q_s[gi] = _rope(q_refs[gi][...], cos2, sin2).astype(BF16)
        k_s[gi] = _rope(k_refs[gi][...], cos2, sin2).astype(BF16)
        v_s[gi] = v_refs[gi][...].astype(BF16)

    rel0 = _iota((blk, blk), 0) - _iota((blk, blk), 1)

    def q_block(qb, _):
        q0 = pl.multiple_of(qb * blk, blk)
        outs, lses = [], []
        for gi, (window, dilation) in enumerate(DIL_GROUPS):
            q_blk = q_s[gi, pl.ds(q0, blk), :]
            n_prev = -(-window // blk)

            def k_block(kb, carry, gi=gi, window=window, dilation=dilation, q_blk=q_blk):
                k0 = pl.multiple_of(kb * blk, blk)
                rel = rel0 + (qb - kb) * blk
                mask = (rel >= 0) & (rel <= window) & ((rel & (dilation - 1)) == 0)
                return _flash_step(q_blk, k_s[gi, pl.ds(k0, blk), :], v_s[gi, pl.ds(k0, blk), :], mask, carry)

            m, l, acc = lax.fori_loop(jnp.maximum(qb - n_prev, 0), qb + 1, k_block, _flash_init())
            outs.append(acc / l)
            lses.append(m + jnp.log(l))
        top = functools.reduce(jnp.maximum, lses)
        wts = [jnp.exp(x - top) for x in lses]
        den = functools.reduce(lambda a, b: a + b, wts)
        o = functools.reduce(lambda a, b: a + b, [w * x for w, x in zip(wts, outs)]) / den
        o_ref[pl.ds(q0, blk), :] = o.astype(o_ref.dtype)
        return 0

    lax.fori_loop(0, n_blk, q_block, 0)


def _dilated(h3, cos2, sin2):
    b, s, _ = h3.shape
    hpg = DIL_HEADS_PER_GROUP
    n_g = len(DIL_GROUPS)
    col = lambda off: pl.BlockSpec((None, s, HEAD_DIM), lambda bi, hi: (bi, 0, off + hi))
    tab = pl.BlockSpec((s, HEAD_DIM), lambda bi, hi: (0, 0))
    specs = [col(part * DIL_HEADS + gi * hpg) for part in range(3) for gi in range(n_g)]
    return pl.pallas_call(
        _dilated_kernel,
        grid=(b, hpg),
        in_specs=specs + [tab, tab],
        out_specs=pl.BlockSpec((None, s, HEAD_DIM), lambda bi, hi: (bi, 0, hi)),
        out_shape=jax.ShapeDtypeStruct((b, s, hpg * HEAD_DIM), BF16),
        scratch_shapes=[pltpu.VMEM((n_g, s, HEAD_DIM), BF16)] * 3,
        compiler_params=_params("parallel", "parallel"),
        name="dilated_attention",
    )(*([h3] * (3 * n_g)), cos2, sin2)


def _moba_kernel(q_ref, k_ref, v_ref, cos_ref, sin_ref, o_ref, qf_s, q_s, k_s, v_s, km_s):
    seq = o_ref.shape[0]
    blk = MOBA_BLOCK
    n_blk = seq // blk
    cos2 = cos_ref[...]
    sin2 = sin_ref[...]
    q = _rope(q_ref[...], cos2, sin2)
    qf_s[...] = q
    q_s[...] = q.astype(BF16)
    km_s[...] = jnp.zeros(km_s.shape, F32)
    for nb in range(n_blk):
        kb = _rope(k_ref[pl.ds(nb * blk, blk), :], cos2[nb * blk:(nb + 1) * blk], sin2[nb * blk:(nb + 1) * blk])
        k_s[pl.ds(nb * blk, blk), :] = kb.astype(BF16)
        km_s[pl.ds(nb, 1), :] = jnp.mean(kb, axis=0, keepdims=True)
    v_s[...] = v_ref[...].astype(BF16)

    lane = _iota((blk, LANES), 1)
    causal = _iota((blk, blk), 0) >= _iota((blk, blk), 1)
    all_true = _iota((blk, blk), 0) >= 0

    def q_block(qb, _):
        q0 = pl.multiple_of(qb * blk, blk)
        gate = _dot_nt_hi(qf_s[pl.ds(q0, blk), :], km_s[...])
        gate = jnp.where(lane < qb, gate, -jnp.inf)
        sel = jnp.zeros((blk, LANES), F32)
        for _k in range(MOBA_TOPK):
            best = jnp.max(gate, axis=-1, keepdims=True)
            first = jnp.min(jnp.where(gate == best, lane, LANES), axis=-1, keepdims=True)
            pick = (lane == first) & (best > -jnp.inf)
            sel = jnp.where(pick, 1.0, sel)
            gate = jnp.where(pick, -jnp.inf, gate)
        q_blk = q_s[pl.ds(q0, blk), :]

        def k_block(kb, carry):
            k0 = pl.multiple_of(kb * blk, blk)
            chosen = jnp.sum(jnp.where(lane == kb, sel, 0.0), axis=-1, keepdims=True) > 0.0
            return _flash_step(q_blk, k_s[pl.ds(k0, blk), :], v_s[pl.ds(k0, blk), :], chosen & all_true, carry)

        carry = lax.fori_loop(0, qb, k_block, _flash_init())
        m, l, acc = _flash_step(q_blk, k_s[pl.ds(q0, blk), :], v_s[pl.ds(q0, blk), :], causal, carry)
        o_ref[pl.ds(q0, blk), :] = (acc / l).astype(o_ref.dtype)
        return 0

    lax.fori_loop(0, n_blk, q_block, 0)


def _moba(h3, cos2, sin2):
    b, s, _ = h3.shape
    base = 3 * DIL_HEADS
    col = lambda off: pl.BlockSpec((None, s, HEAD_DIM), lambda bi, hi: (bi, 0, base + off + hi))
    tab = pl.BlockSpec((s, HEAD_DIM), lambda bi, hi: (0, 0))
    return pl.pallas_call(
        _moba_kernel,
        grid=(b, MOBA_HEADS),
        in_specs=[col(0), col(MOBA_HEADS), col(2 * MOBA_HEADS), tab, tab],
        out_specs=pl.BlockSpec((None, s, HEAD_DIM), lambda bi, hi: (bi, 0, hi)),
        out_shape=jax.ShapeDtypeStruct((b, s, MOBA_HEADS * HEAD_DIM), BF16),
        scratch_shapes=[pltpu.VMEM((s, HEAD_DIM), F32), pltpu.VMEM((s, HEAD_DIM), BF16),
                        pltpu.VMEM((s, HEAD_DIM), BF16), pltpu.VMEM((s, HEAD_DIM), BF16),
                        pltpu.VMEM((LANES, HEAD_DIM), F32)],
        compiler_params=_params("parallel", "parallel"),
        name="moba_attention",
    )(h3, h3, h3, cos2, sin2)


def _layer_norm(x, gain, bias):
    mu = jnp.mean(x, axis=-1, keepdims=True)
    xc = x - mu
    var = jnp.mean(xc * xc, axis=-1, keepdims=True)
    return xc * lax.rsqrt(var + LN_EPS) * gain + bias


def _out_ln_kernel(*refs, n_parts):
    o_refs = refs[0:n_parts]
    w_refs = refs[n_parts:2 * n_parts]
    h_ref, gain_ref, bias_ref, rw_ref, y_ref, yb_ref, logit_ref = refs[2 * n_parts:]
    mix = None
    for o_r, w_r in zip(o_refs, w_refs):
        part = jnp.dot(o_r[...], w_r[...], preferred_element_type=F32)
        mix = part if mix is None else mix + part
    y = _layer_norm(DEEPNORM_ALPHA * h_ref[...] + mix, gain_ref[...], bias_ref[...])
    y_ref[...] = y
    yb_ref[...] = y.astype(BF16)
    logit_ref[...] = _dot_hi(y, rw_ref[...])


def _out_ln(parts, weights, h, gain, bias, router_w, tm):
    t, d = h.shape
    n_parts = len(parts)
    row = lambda width: pl.BlockSpec((tm, width), lambda i: (i, 0))
    full = lambda a: pl.BlockSpec(a.shape, lambda i: (0, 0))
    return pl.pallas_call(
        functools.partial(_out_ln_kernel, n_parts=n_parts),
        grid=(t // tm,),
        in_specs=[row(p.shape[1]) for p in parts] + [full(w) for w in weights]
                 + [row(d), full(gain), full(bias), full(router_w)],
        out_specs=[row(d), row(d), row(LANES)],
        out_shape=[jax.ShapeDtypeStruct((t, d), F32), jax.ShapeDtypeStruct((t, d), BF16),
                   jax.ShapeDtypeStruct((t, LANES), F32)],
        compiler_params=_params("parallel"),
        name="out_proj_ln",
    )(*parts, *weights, h, gain, bias, router_w)


def _route(logits, rbias):
    rows = logits.shape[0]
    lane = _iota((rows, LANES), 1)
    real = lane < N_EXPERTS
    scores = jax.nn.sigmoid(logits)
    biased = jnp.where(real, scores + rbias, -jnp.inf)

    def first_argmax(vals):
        best = jnp.max(vals, axis=-1, keepdims=True)
        return best, jnp.min(jnp.where(vals == best, lane, LANES), axis=-1, keepdims=True)

    best_score = None
    best_group = None
    for g in range(N_EXPERT_GROUPS):
        vals = jnp.where(lane // EXPERTS_PER_GROUP == g, biased, -jnp.inf)
        top1, idx1 = first_argmax(vals)
        top2, _ = first_argmax(jnp.where(lane == idx1, -jnp.inf, vals))
        score = top1 + top2
        if g == 0:
            best_score, best_group = score, jnp.zeros_like(idx1)
        else:
            better = score > best_score
            best_group = jnp.where(better, g, best_group)
            best_score = jnp.where(better, score, best_score)
    masked = jnp.where(real, jnp.where(lane // EXPERTS_PER_GROUP == best_group, biased, NEG_INF), -jnp.inf)
    _, i1 = first_argmax(masked)
    _, i2 = first_argmax(jnp.where(lane == i1, -jnp.inf, masked))
    s1 = jnp.sum(jnp.where(lane == i1, scores, 0.0), axis=-1, keepdims=True)
    s2 = jnp.sum(jnp.where(lane == i2, scores, 0.0), axis=-1, keepdims=True)
    tot = s1 + s2
    return i1, i2, s1 / tot, s2 / tot


def _route_kernel(logit_ref, rbias_ref, w_ref, e_ref, r_ref, cnt_ref, carry_s):
    i = pl.program_id(0)
    rows = logit_ref.shape[0]

    @pl.when(i == 0)
    def _():
        carry_s[...] = jnp.zeros(carry_s.shape, F32)

    i1, i2, w1, w2 = _route(logit_ref[...], rbias_ref[...])
    lane = _iota((rows, LANES), 1)
    chosen = (lane == i1) | (lane == i2)
    earlier = (_iota((rows, rows), 0) > _iota((rows, rows), 1)).astype(BF16)
    before = jnp.dot(earlier, jnp.where(chosen, 1.0, 0.0).astype(BF16), preferred_element_type=F32) + carry_s[0:1, :]
    r1 = jnp.sum(jnp.where(lane == i1, before, 0.0), axis=-1, keepdims=True)
    r2 = jnp.sum(jnp.where(lane == i2, before, 0.0), axis=-1, keepdims=True)
    w_ref[...] = jnp.where(lane == 0, w1, jnp.where(lane == 1, w2, 0.0))
    e_ref[...] = jnp.where(lane == 0, i1, jnp.where(lane == 1, i2, 0))
    r_ref[...] = jnp.where(lane == 0, r1, jnp.where(lane == 1, r2, 0.0)).astype(jnp.int32)
    carry_s[...] = carry_s[...] + jnp.sum(jnp.where(chosen, 1.0, 0.0), axis=0, keepdims=True)
    cnt_ref[...] = carry_s[...].astype(jnp.int32)


def _route_tokens(logits, rbias, tm):
    t = logits.shape[0]
    row = pl.BlockSpec((tm, LANES), lambda i: (i, 0))
    return pl.pallas_call(
        _route_kernel,
        grid=(t // tm,),
        in_specs=[row, pl.BlockSpec((1, LANES), lambda i: (0, 0))],
        out_specs=[row, row, row, pl.BlockSpec((8, LANES), lambda i: (0, 0))],
        out_shape=[jax.ShapeDtypeStruct((t, LANES), F32), jax.ShapeDtypeStruct((t, LANES), jnp.int32),
                   jax.ShapeDtypeStruct((t, LANES), jnp.int32), jax.ShapeDtypeStruct((8, LANES), jnp.int32)],
        scratch_shapes=[pltpu.VMEM((8, LANES), F32)],
        compiler_params=_params("arbitrary"),
        name="moe_route",
    )(logits, rbias)


def _dispatch_kernel(pos_ref, x_ref, init_ref, xs_ref, sem):
    del init_ref
    i = pl.program_id(0)
    rows = x_ref.shape[0]

    def issue(r, _):
        for k in range(MOE_TOPK):
            p = pos_ref[MOE_TOPK * (i * rows + r) + k]
            pltpu.make_async_copy(x_ref.at[pl.ds(r, 1)], xs_ref.at[pl.ds(p, 1)], sem).start()
        return 0

    lax.fori_loop(0, rows, issue, 0)
    for _k in range(MOE_TOPK):
        pltpu.make_async_copy(x_ref, xs_ref.at[pl.ds(0, rows)], sem).wait()


def _dispatch(pos, x, n_rows, tm):
    t, d = x.shape
    grid_spec = pltpu.PrefetchScalarGridSpec(
        num_scalar_prefetch=1,
        grid=(t // tm,),
        in_specs=[pl.BlockSpec((tm, d), lambda i, pos: (i, 0)), pl.BlockSpec(memory_space=pl.ANY)],
        out_specs=pl.BlockSpec(memory_space=pl.ANY),
        scratch_shapes=[pltpu.SemaphoreType.DMA(())],
    )
    return pl.pallas_call(
        _dispatch_kernel,
        grid_spec=grid_spec,
        out_shape=jax.ShapeDtypeStruct((n_rows, d), x.dtype),
        input_output_aliases={2: 0},
        compiler_params=_params("arbitrary"),
        name="moe_dispatch",
    )(pos, x, jnp.zeros((n_rows, d), x.dtype))


def _expert_kernel(blk_expert_ref, n_valid_ref, x_ref, wg_ref, wu_ref, wd_ref, y_ref):
    del blk_expert_ref

    @pl.when(pl.program_id(0) < n_valid_ref[0])
    def _():
        x = x_ref[...].astype(BF16)
        hid = _silu(jnp.dot(x, wg_ref[...], preferred_element_type=F32)) * jnp.dot(
            x, wu_ref[...], preferred_element_type=F32)
        y_ref[...] = _dot(hid, wd_ref[...])

    @pl.when(pl.program_id(0) >= n_valid_ref[0])
    def _():
        y_ref[...] = jnp.zeros(y_ref.shape, y_ref.dtype)


def _experts(blk_expert, n_valid, xs, wg, wu, wd):
    n_rows, d = xs.shape
    _, _, f = wg.shape
    rows = MOE_BLOCK_ROWS
    grid_spec = pltpu.PrefetchScalarGridSpec(
        num_scalar_prefetch=2,
        grid=(n_rows // rows,),
        in_specs=[pl.BlockSpec((rows, d), lambda i, be, nv: (i, 0)),
                  pl.BlockSpec((None, d, f), lambda i, be, nv: (be[i], 0, 0)),
                  pl.BlockSpec((None, d, f), lambda i, be, nv: (be[i], 0, 0)),
                  pl.BlockSpec((None, f, d), lambda i, be, nv: (be[i], 0, 0))],
        out_specs=pl.BlockSpec((rows, d), lambda i, be, nv: (i, 0)),
    )
    return pl.pallas_call(
        _expert_kernel,
        grid_spec=grid_spec,
        out_shape=jax.ShapeDtypeStruct((n_rows, d), F32),
        compiler_params=_params("arbitrary"),
        name="moe_experts",
    )(blk_expert, n_valid, xs, wg, wu, wd)


def _combine_kernel(pos_ref, ys_ref, w_ref, h_ref, gain_ref, bias_ref, y_ref, yb_ref, buf, sem):
    i = pl.program_id(0)
    n = pl.num_programs(0)
    rows = h_ref.shape[0]

    def issue(tile, slot):
        def body(r, _):
            for k in range(MOE_TOPK):
                p = pos_ref[MOE_TOPK * (tile * rows + r) + k]
                pltpu.make_async_copy(ys_ref.at[pl.ds(p, 1)], buf.at[slot, k, pl.ds(r, 1)], sem.at[slot]).start()
            return 0

        lax.fori_loop(0, rows, body, 0)

    slot = i % 2

    @pl.when(i == 0)
    def _():
        issue(0, 0)

    @pl.when(i + 1 < n)
    def _():
        issue(i + 1, 1 - slot)

    for k in range(MOE_TOPK):
        pltpu.make_async_copy(ys_ref.at[pl.ds(0, rows)], buf.at[slot, k], sem.at[slot]).wait()
    w = w_ref[...]
    ffn = w[:, 0:1] * buf[slot, 0] + w[:, 1:2] * buf[slot, 1]
    y = _layer_norm(DEEPNORM_ALPHA * h_ref[...] + ffn, gain_ref[...], bias_ref[...])
    y_ref[...] = y
    yb_ref[...] = y.astype(BF16)


def _combine_ln(pos, ys, w12, h, gain, bias, tm):
    t, d = h.shape
    grid_spec = pltpu.PrefetchScalarGridSpec(
        num_scalar_prefetch=1,
        grid=(t // tm,),
        in_specs=[pl.BlockSpec(memory_space=pl.ANY),
                  pl.BlockSpec((tm, LANES), lambda i, pos: (i, 0)),
                  pl.BlockSpec((tm, d), lambda i, pos: (i, 0)),
                  pl.BlockSpec((1, d), lambda i, pos: (0, 0)),
                  pl.BlockSpec((1, d), lambda i, pos: (0, 0))],
        out_specs=[pl.BlockSpec((tm, d), lambda i, pos: (i, 0)), pl.BlockSpec((tm, d), lambda i, pos: (i, 0))],
        scratch_shapes=[pltpu.VMEM((2, MOE_TOPK, tm, d), F32), pltpu.SemaphoreType.DMA((2,))],
    )
    return pl.pallas_call(
        _combine_kernel,
        grid_spec=grid_spec,
        out_shape=[jax.ShapeDtypeStruct((t, d), F32), jax.ShapeDtypeStruct((t, d), BF16)],
        compiler_params=_params("arbitrary"),
        name="moe_combine_ln",
    )(pos, ys, w12, h, gain, bias)


def _moe_ln(h, logits, rbias, wg, wu, wd, gain, bias):
    t, d = h.shape
    rows = MOE_BLOCK_ROWS
    n_blocks = (MOE_TOPK * t) // rows + N_EXPERTS
    w12, e12, r12, counts = _route_tokens(logits, rbias, 512)
    counts = counts[0, :N_EXPERTS]
    blocks_per_expert = (counts + rows - 1) // rows
    block_end = jnp.cumsum(blocks_per_expert)
    row_start = (block_end - blocks_per_expert) * rows
    pos = (row_start[e12[:, :MOE_TOPK]] + r12[:, :MOE_TOPK]).reshape(-1).astype(jnp.int32)
    n_valid = block_end[-1:].astype(jnp.int32)
    blk = jnp.arange(n_blocks, dtype=jnp.int32)
    blk_expert = jnp.sum(jnp.minimum(blk, n_valid - 1)[:, None] >= block_end[None, :], axis=1).astype(jnp.int32)
    xs = _dispatch(pos, h, n_blocks * rows, 512)
    ys = _experts(blk_expert, n_valid, xs, wg, wu, wd)
    return _combine_ln(pos, ys, w12, h, gain, bias, 256)


def _rope_tables(seq):
    inv_freq = ROPE_THETA ** (-jnp.arange(0, HEAD_DIM, 2, dtype=F32) / HEAD_DIM)
    ang = jnp.arange(seq, dtype=F32)[:, None] * inv_freq[None, :]
    cos, sin = jnp.cos(ang), jnp.sin(ang)
    return jnp.concatenate([cos, cos], axis=-1), jnp.concatenate([-sin, sin], axis=-1)


def _even_mixer(hb, b, s, w_in, conv_w, a_log, dt_bias, gdn_norm, hgrn_norm, lower_bound):
    gw = GDN_WIDTH
    n_small = 2 * GDN_HEADS
    tail0 = 4 * gw
    w_perm = jnp.concatenate([w_in[:, :tail0], w_in[:, tail0 + n_small:], w_in[:, tail0:tail0 + n_small],
                              jnp.zeros((w_in.shape[0], LANES - n_small), w_in.dtype)], axis=1).astype(BF16)
    h = _matmul(hb, w_perm, 1024, 640)
    h3 = h.reshape(b, s, EVEN_COLS)
    n_chunks = s // GDN_CHUNK
    small = h3[:, :, EVEN_MAIN:EVEN_MAIN + n_small]
    to_rows = lambda a: a.transpose(0, 2, 1).reshape(b, GDN_HEADS, n_chunks, GDN_CHUNK)
    b_rows, a_rows = to_rows(small[..., :GDN_HEADS]), to_rows(small[..., GDN_HEADS:])
    headvec = lambda v: jnp.broadcast_to(v.astype(F32)[:, None, None], (GDN_HEADS, 1, GDN_CHUNK))
    o_a = _gdn(h3, conv_w.astype(F32), a_rows, b_rows, headvec(a_log), headvec(dt_bias),
               gdn_norm.astype(F32).reshape(1, HEAD_DIM))
    o_b = _hgrn(h3, lower_bound.astype(F32).reshape(HGRN_HEADS, 1, HEAD_DIM),
                hgrn_norm.astype(F32).reshape(1, HEAD_DIM))
    return [o_a.reshape(b * s, GDN_WIDTH), o_b.reshape(b * s, HGRN_WIDTH)]


def _odd_mixer(hb, b, s, w_in, cos2, sin2):
    h = _matmul(hb, w_in.astype(BF16), 1024, 768)
    h3 = h.reshape(b, s, ODD_COLS)
    o_c = _dilated(h3, cos2, sin2)
    o_d = _moba(h3, cos2, sin2)
    return [o_c.reshape(b * s, -1), o_d.reshape(b * s, -1)]


def kernel(x, ev_w_in, ev_conv_w, ev_a_log, ev_dt_bias, ev_gdn_norm, ev_hgrn_norm, hgrn_lb_logits, ev_w_out,
           od_w_in, od_w_out, router_w, router_bias, moe_w_gate, moe_w_up, moe_w_down, ln_gain, ln_bias):
    b, s, d = x.shape
    t = b * s
    cos2, sin2 = _rope_tables(s)
    lower_bounds = jnp.cumsum(jax.nn.softmax(hgrn_lb_logits.astype(F32), axis=0), axis=0)
    rw = jnp.pad(router_w.astype(F32), ((0, 0), (0, LANES - N_EXPERTS)))
    rbias = jnp.pad(router_bias.astype(F32), (0, LANES - N_EXPERTS)).reshape(1, LANES)
    vec = lambda v: v.astype(F32).reshape(1, d)

    h = x.reshape(t, d)
    hb = h.astype(BF16)
    for layer in range(DEPTH):
        if layer % 2 == 0:
            e = layer // 2
            parts = _even_mixer(hb, b, s, ev_w_in[e], ev_conv_w[e], ev_a_log[e], ev_dt_bias[e], ev_gdn_norm[e],
                                ev_hgrn_norm[e], lower_bounds[layer])
            w_out = ev_w_out[e].astype(BF16)
        else:
            o = layer // 2
            parts = _odd_mixer(hb, b, s, od_w_in[o], cos2, sin2)
            w_out = od_w_out[o].astype(BF16)
        splits = np.cumsum([p.shape[1] for p in parts])[:-1]
        weights = jnp.split(w_out, splits, axis=0)
        h, hb, logits = _out_ln(parts, weights, h, vec(ln_gain[layer, 0]), vec(ln_bias[layer, 0]), rw, 256)
        h, hb = _moe_ln(h, logits, rbias, moe_w_gate[layer].astype(BF16), moe_w_up[layer].astype(BF16),
                        moe_w_down[layer].astype(BF16), vec(ln_gain[layer, 1]), vec(ln_bias[layer, 1]))
    return h.reshape(b, s, d)
```

```python
import functools
import math

import jax
import jax.numpy as jnp
import numpy as np
from jax import lax
from jax.experimental import pallas as pl
from jax.experimental.pallas import tpu as pltpu

F32 = jnp.float32
BF16 = jnp.bfloat16

D_MODEL = 2048
DEPTH = 2
HEAD_DIM = 128
GDN_HEADS = 8
GDN_CONV = 4
GDN_CHUNK = 64
GDN_WIDTH = GDN_HEADS * HEAD_DIM
HGRN_HEADS = 8
HGRN_CHUNK = 16
HGRN_WIDTH = HGRN_HEADS * HEAD_DIM
DIL_GROUPS = ((128, 1), (512, 4), (2048, 16))
DIL_HEADS_PER_GROUP = 4
DIL_HEADS = len(DIL_GROUPS) * DIL_HEADS_PER_GROUP
MOBA_HEADS = 4
MOBA_BLOCK = 256
MOBA_TOPK = 3
ROPE_THETA = 10000.0
N_EXPERTS = 16
N_EXPERT_GROUPS = 4
EXPERTS_PER_GROUP = N_EXPERTS // N_EXPERT_GROUPS
D_EXPERT = 512
MOE_TOPK = 2
MOE_BLOCK_ROWS = 256
DEEPNORM_ALPHA = (2.0 * DEPTH) ** 0.25
LN_EPS = 1e-5
RMS_EPS = 1e-6
NEG_INF = -1e30

LANES = 128
VMEM_LIMIT = 56 * 1024 * 1024
ATT_BLOCK = 256
MOBA_Q_GROUPS = ((7, 0, 6, 1), (5, 2, 4, 3))
DIL_BLOCK = 128
GDN_GROUP = 8
HGRN_ROWS = 256

EVEN_MAIN = 3 * GDN_WIDTH + GDN_WIDTH + 4 * HGRN_WIDTH
ODD_COLS = 3 * DIL_HEADS * HEAD_DIM + 3 * MOBA_HEADS * HEAD_DIM


def _dot(a, b):
    return jnp.dot(a.astype(BF16), b.astype(BF16), preferred_element_type=F32)


def _dot_nt(a, b):
    return lax.dot_general(a.astype(BF16), b.astype(BF16), (((1,), (1,)), ((), ())),
                           preferred_element_type=F32)


def _dot_tn(a, b):
    return lax.dot_general(a.astype(BF16), b.astype(BF16), (((0,), (0,)), ((), ())),
                           preferred_element_type=F32)


def _dot_hi(a, b):
    return jnp.dot(a, b, preferred_element_type=F32, precision=lax.Precision.HIGHEST)


def _dot_nt_hi(a, b):
    return lax.dot_general(a, b, (((1,), (1,)), ((), ())), preferred_element_type=F32,
                           precision=lax.Precision.HIGHEST)


def _dot3(a, b):
    a_hi = a.astype(BF16)
    b_hi = b.astype(BF16)
    a_lo = (a - a_hi.astype(F32)).astype(BF16)
    b_lo = (b - b_hi.astype(F32)).astype(BF16)
    dot = functools.partial(jnp.dot, preferred_element_type=F32)
    return dot(a_hi, b_hi) + (dot(a_hi, b_lo) + dot(a_lo, b_hi))


def _dot_sel(sel, x):
    dot = functools.partial(jnp.dot, preferred_element_type=F32)
    x_hi = x.astype(BF16)
    r1 = x - x_hi.astype(F32)
    x_mid = r1.astype(BF16)
    x_lo = (r1 - x_mid.astype(F32)).astype(BF16)
    return dot(sel, x_hi) + (dot(sel, x_mid) + dot(sel, x_lo))


_dot_inv = _dot3


def _silu(x):
    return x * jax.nn.sigmoid(x)


def _iota(shape, dim):
    return lax.broadcasted_iota(jnp.int32, shape, dim)


def _params(*sem):
    return pltpu.CompilerParams(dimension_semantics=sem, vmem_limit_bytes=VMEM_LIMIT)


def _mm_kernel(x_ref, w_ref, o_ref):
    o_ref[...] = jnp.dot(x_ref[...], w_ref[...], preferred_element_type=F32).astype(o_ref.dtype)


def _matmul(x, w, tm, tn):
    m, k = x.shape
    n = w.shape[1]
    assert m % tm == 0 and n % tn == 0
    return pl.pallas_call(
        _mm_kernel,
        grid=(m // tm, n // tn),
        in_specs=[pl.BlockSpec((tm, k), lambda i, j: (i, 0)),
                  pl.BlockSpec((k, tn), lambda i, j: (0, j))],
        out_specs=pl.BlockSpec((tm, tn), lambda i, j: (i, j)),
        out_shape=jax.ShapeDtypeStruct((m, n), F32),
        compiler_params=_params("parallel", "parallel"),
        name="in_proj",
    )(x, w)


def _gdn_kernel(q_ref, k_ref, v_ref, z_ref, cwq_ref, cwk_ref, cwv_ref, a_ref, b_ref, alog_ref, dt_ref,
                gn_ref, o_ref, pad_s, q_s, k_s, v_s, gcum_s, beta_s, qe_s, ob_s, sm_s, sa_s):
    seq = q_ref.shape[0]
    c = GDN_CHUNK
    n_chunks = seq // c
    rows = 256

    pad_s[pl.ds(0, 8), :] = jnp.zeros((8, HEAD_DIM), F32)
    for x_ref, cw_ref, dst, mode in ((q_ref, cwq_ref, q_s, "q"), (k_ref, cwk_ref, k_s, "k"),
                                     (v_ref, cwv_ref, v_s, "v")):
        pad_s[pl.ds(8, seq), :] = x_ref[...]
        cw = cw_ref[...]
        for r in range(seq // rows):
            acc = None
            for j in range(GDN_CONV):
                tap = pad_s[pl.ds(8 + r * rows - (GDN_CONV - 1) + j, rows), :] * cw[j:j + 1, :]
                acc = tap if acc is None else acc + tap
            y = _silu(acc)
            if mode != "v":
                y = y * lax.rsqrt(jnp.sum(y * y, axis=-1, keepdims=True) + RMS_EPS)
            if mode == "q":
                y = y * HEAD_DIM ** -0.5
            dst[pl.ds(r * rows, rows), :] = y

    upper = (_iota((c, c), 0) <= _iota((c, c), 1)).astype(F32)
    g = -jnp.exp(alog_ref[...]) * jax.nn.softplus(a_ref[...] + dt_ref[...])
    gcum_s[...] = _dot_hi(g, upper)
    beta_s[...] = jax.nn.sigmoid(b_ref[...])

    ri = _iota((c, c), 0)
    ci = _iota((c, c), 1)
    eye = ri == ci
    strict = ri > ci
    incl = ri >= ci
    eye_f = eye.astype(F32)
    level1 = ri // 2 == ci // 2
    levels = []
    s = 2
    while s < c:
        levels.append((ri // (2 * s) == ci // (2 * s)) & ((ri // s) % 2 == 1) & ((ci // s) % 2 == 0))
        s *= 2

    dot = functools.partial(jnp.dot, preferred_element_type=F32)

    def to_col(row):
        return jnp.sum(jnp.where(eye, jnp.broadcast_to(row, (c, c)), 0.0), axis=1, keepdims=True)

    def prepare(i, _):
        n0 = i * GDN_GROUP
        grp = range(GDN_GROUP)
        starts = [pl.multiple_of((n0 + j) * c, c) for j in grp]
        qc = [q_s[pl.ds(r0, c), :] for r0 in starts]
        kc = [k_s[pl.ds(r0, c), :] for r0 in starts]
        vc = [v_s[pl.ds(r0, c), :] for r0 in starts]
        g_row = [gcum_s[pl.ds(n0 + j, 1), :] for j in grp]
        g_col = [to_col(g_row[j]) for j in grp]
        b_col = [to_col(beta_s[pl.ds(n0 + j, 1), :]) for j in grp]
        decay = [jnp.exp(jnp.where(incl, g_col[j] - g_row[j], 0.0)) for j in grp]
        n_mat = [b_col[j] * jnp.where(strict, decay[j], 0.0) * _dot_nt(kc[j], kc[j]) for j in grp]
        inv = [eye_f - jnp.where(level1, n_mat[j], 0.0) for j in grp]
        for blk in levels:
            tmp = [_dot_inv(inv[j], jnp.where(blk, n_mat[j], 0.0)) for j in grp]
            inv = [inv[j] - _dot_inv(tmp[j], inv[j]) for j in grp]
        e_col = [jnp.exp(g_col[j]) for j in grp]
        sol = [_dot_inv(inv[j], jnp.concatenate([b_col[j] * vc[j], (b_col[j] * e_col[j]) * kc[j]], axis=1))
               for j in grp]
        qk = [(_dot_nt(qc[j], kc[j]) * jnp.where(incl, decay[j], 0.0)).astype(BF16) for j in grp]
        ub = [sol[j][:, :HEAD_DIM].astype(BF16) for j in grp]
        w = [sol[j][:, HEAD_DIM:].astype(BF16) for j in grp]
        kd = [(kc[j] * jnp.exp(g_row[j][:, c - 1:c] - g_col[j])).astype(BF16) for j in grp]
        q_eff = [(qc[j] * e_col[j] - dot(qk[j], w[j])).astype(BF16) for j in grp]
        o_base = [dot(qk[j], ub[j]) for j in grp]
        s_mat = [_dot_tn(kd[j], w[j]).astype(BF16) for j in grp]
        s_add = [_dot_tn(kd[j], ub[j]) for j in grp]
        for j, r0 in enumerate(starts):
            m0 = pl.multiple_of((n0 + j) * HEAD_DIM, HEAD_DIM)
            qe_s[pl.ds(r0, c), :] = q_eff[j]
            ob_s[pl.ds(r0, c), :] = o_base[j]
            sm_s[pl.ds(m0, HEAD_DIM), :] = s_mat[j]
            sa_s[pl.ds(m0, HEAD_DIM), :] = s_add[j]
        return 0

    lax.fori_loop(0, n_chunks // GDN_GROUP, prepare, 0)

    gn = gn_ref[...]

    def chunk(n, state):
        r0 = pl.multiple_of(n * c, c)
        m0 = pl.multiple_of(n * HEAD_DIM, HEAD_DIM)
        g_last = gcum_s[pl.ds(n, 1), :][:, c - 1:c]
        lhs = jnp.concatenate([qe_s[pl.ds(r0, c), :], sm_s[pl.ds(m0, HEAD_DIM), :]], axis=0)
        prod = dot(lhs, state.astype(BF16))
        ob_s[pl.ds(r0, c), :] = prod[:c] + ob_s[pl.ds(r0, c), :]
        return jnp.exp(g_last) * state - prod[c:] + sa_s[pl.ds(m0, HEAD_DIM), :]

    lax.fori_loop(0, n_chunks, chunk, jnp.zeros((HEAD_DIM, HEAD_DIM), F32))

    for r in range(seq // rows):
        sl = pl.ds(r * rows, rows)
        o = ob_s[sl, :]
        o = o * lax.rsqrt(jnp.mean(o * o, axis=-1, keepdims=True) + RMS_EPS) * gn
        o_ref[sl, :] = (o * _silu(z_ref[sl, :])).astype(o_ref.dtype)


def _gdn(h3, conv_w, a_rows, b_rows, alog, dt, gn):
    b, s, _ = h3.shape
    nh = GDN_HEADS
    n_chunks = s // GDN_CHUNK
    col = lambda off: pl.BlockSpec((None, s, HEAD_DIM), lambda bi, hi: (bi, 0, off + hi))
    cw = lambda off: pl.BlockSpec((GDN_CONV, HEAD_DIM), lambda bi, hi: (0, off + hi))
    rowspec = pl.BlockSpec((None, None, n_chunks, GDN_CHUNK), lambda bi, hi: (bi, hi, 0, 0))
    headvec = pl.BlockSpec((None, 1, GDN_CHUNK), lambda bi, hi: (hi, 0, 0))
    return pl.pallas_call(
        _gdn_kernel,
        grid=(b, nh),
        in_specs=[col(0), col(nh), col(2 * nh), col(3 * nh), cw(0), cw(nh), cw(2 * nh),
                  rowspec, rowspec, headvec, headvec,
                  pl.BlockSpec((1, HEAD_DIM), lambda bi, hi: (0, 0))],
        out_specs=pl.BlockSpec((None, s, HEAD_DIM), lambda bi, hi: (bi, 0, hi)),
        out_shape=jax.ShapeDtypeStruct((b, s, GDN_WIDTH), BF16),
        scratch_shapes=[pltpu.VMEM((s + 8, HEAD_DIM), F32), pltpu.VMEM((s, HEAD_DIM), F32),
                        pltpu.VMEM((s, HEAD_DIM), F32), pltpu.VMEM((s, HEAD_DIM), F32),
                        pltpu.VMEM((n_chunks, GDN_CHUNK), F32), pltpu.VMEM((n_chunks, GDN_CHUNK), F32),
                        pltpu.VMEM((s, HEAD_DIM), BF16), pltpu.VMEM((s, HEAD_DIM), F32),
                        pltpu.VMEM((n_chunks * HEAD_DIM, HEAD_DIM), BF16),
                        pltpu.VMEM((n_chunks * HEAD_DIM, HEAD_DIM), F32)],
        compiler_params=_params("parallel", "parallel"),
        name="gdn",
    )(h3, h3, h3, h3, conv_w, conv_w, conv_w, a_rows, b_rows, alog, dt, gn)


def _hgrn_kernel(q_ref, f_ref, i_ref, g_ref, lb_ref, hn_ref, o_ref):
    seq = q_ref.shape[0]
    c = HGRN_CHUNK
    rows = HGRN_ROWS
    ri = _iota((rows, rows), 0)
    ci = _iota((rows, rows), 1)
    same = ri // c == ci // c
    causal = same & (ci <= ri)
    sum_mat = jnp.concatenate([causal.astype(BF16), same.astype(BF16)], axis=0)
    lb = lb_ref[...]
    hn = hn_ref[...]

    def block(n, state_t):
        r0 = pl.multiple_of(n * rows, rows)
        qc = q_ref[pl.ds(r0, rows), :]
        ic = i_ref[pl.ds(r0, rows), :].astype(BF16)
        f = lb + (1.0 - lb) * jax.nn.sigmoid(f_ref[pl.ds(r0, rows), :])
        kc = 1.0 - f
        sums = _dot_sel(sum_mat, jnp.log(f))
        bcum, b_last = sums[:rows], sums[rows:]
        q_dec = (qc * jnp.exp(bcum)).astype(BF16)
        k_inv = kc * jnp.exp(-bcum)
        k_dec = (kc * jnp.exp(b_last - bcum)).astype(BF16)
        chunk_dec = jnp.exp(b_last)
        p = jnp.where(causal, _dot_nt(q_dec, k_inv), 0.0)
        o_intra = _dot(p, ic)
        chunks = [slice(j * c, (j + 1) * c) for j in range(rows // c)]
        updates = [_dot_tn(ic[sl], k_dec[sl]) for sl in chunks]
        outs = []
        for sl, upd in zip(chunks, updates):
            outs.append(o_intra[sl] + _dot_nt(q_dec[sl], state_t))
            state_t = state_t * chunk_dec[sl.start:sl.start + 1] + upd
        o = jnp.concatenate(outs, axis=0)
        o = o * lax.rsqrt(jnp.mean(o * o, axis=-1, keepdims=True) + RMS_EPS) * hn
        o_ref[pl.ds(r0, rows), :] = (o * _silu(g_ref[pl.ds(r0, rows), :])).astype(o_ref.dtype)
        return state_t

    lax.fori_loop(0, seq // rows, block, jnp.zeros((HEAD_DIM, HEAD_DIM), F32))


def _hgrn(h3, lb, hn):
    b, s, _ = h3.shape
    nh = HGRN_HEADS
    base = 4 * GDN_HEADS
    col = lambda off: pl.BlockSpec((None, s, HEAD_DIM), lambda bi, hi: (bi, 0, base + off + hi))
    return pl.pallas_call(
        _hgrn_kernel,
        grid=(b, nh),
        in_specs=[col(0), col(nh), col(2 * nh), col(3 * nh),
                  pl.BlockSpec((None, 1, HEAD_DIM), lambda bi, hi: (hi, 0, 0)),
                  pl.BlockSpec((1, HEAD_DIM), lambda bi, hi: (0, 0))],
        out_specs=pl.BlockSpec((None, s, HEAD_DIM), lambda bi, hi: (bi, 0, hi)),
        out_shape=jax.ShapeDtypeStruct((b, s, HGRN_WIDTH), BF16),
        compiler_params=_params("parallel", "parallel"),
        name="hgrn2",
    )(h3, h3, h3, h3, lb, hn)


def _rope(x, cos2, sin2):
    return x * cos2 + pltpu.roll(x, HEAD_DIM // 2, axis=1) * sin2


def _flash_step(q_blk, k_blk, v_blk, mask, carry):
    m, l, acc = carry
    s = jnp.where(mask, _dot_nt(q_blk, k_blk) * HEAD_DIM ** -0.5, NEG_INF)
    m_new = jnp.maximum(m, jnp.max(s, axis=-1, keepdims=True))
    alpha = jnp.exp(m - m_new)
    p = jnp.exp(s - m_new)
    l = alpha * l + jnp.sum(p, axis=-1, keepdims=True)
    acc = alpha * acc + _dot(p, v_blk)
    return m_new, l, acc


def _flash_init():
    blk = ATT_BLOCK
    return (jnp.full((blk, 1), NEG_INF, F32), jnp.zeros((blk, 1), F32), jnp.zeros((blk, HEAD_DIM), F32))


def _dilated_kernel(*refs):
    n_g = len(DIL_GROUPS)
    q_refs, k_refs, v_refs = refs[0:n_g], refs[n_g:2 * n_g], refs[2 * n_g:3 * n_g]
    cos_ref, sin_ref, o_ref = refs[3 * n_g:3 * n_g + 3]
    q_s, k_s, v_s, og_s, lse_s = refs[3 * n_g + 3:]
    seq = o_ref.shape[0]
    blk = DIL_BLOCK
    piece = 256
    grp = range(n_g)

    for gi, (window, d) in enumerate(DIL_GROUPS):
        assert window // d == blk
        seg = seq // d
        k_s[gi, pl.ds(0, blk), :] = jnp.zeros((blk, HEAD_DIM), BF16)
        v_s[gi, pl.ds(0, blk), :] = jnp.zeros((blk, HEAD_DIM), BF16)
        for r in range(d):
            for c0 in range(0, seg, piece):
                n = min(piece, seg)
                rows = pl.ds(r + c0 * d, n, stride=d) if d > 1 else pl.ds(c0, n)
                cos2 = cos_ref[rows, :]
                sin2 = sin_ref[rows, :]
                q_s[gi, pl.ds(r * seg + c0, n), :] = _rope(q_refs[gi][rows, :], cos2, sin2).astype(BF16)
                k_s[gi, pl.ds(blk + r * seg + c0, n), :] = _rope(k_refs[gi][rows, :], cos2, sin2).astype(BF16)
                v_s[gi, pl.ds(blk + r * seg + c0, n), :] = v_refs[gi][rows, :].astype(BF16)

    ri = _iota((blk, 2 * blk), 0)
    ci = _iota((blk, 2 * blk), 1)
    rel = ri + blk - ci
    in_window = (rel >= 0) & (rel <= blk)
    dot = functools.partial(jnp.dot, preferred_element_type=F32)

    def q_block(m, _):
        j0 = pl.multiple_of(m * blk, blk)
        segs = [seq // d for _, d in DIL_GROUPS]
        has_prev = [jnp.where(j0 % seg != 0, blk, 0) for seg in segs]
        mask = [in_window & (ci + has_prev[g] >= blk) for g in grp]
        q = [q_s[g, pl.ds(j0, blk), :] for g in grp]
        kw = [k_s[g, pl.ds(j0, 2 * blk), :] for g in grp]
        vw = [v_s[g, pl.ds(j0, 2 * blk), :] for g in grp]
        s = [jnp.where(mask[g], _dot_nt(q[g], kw[g]) * HEAD_DIM ** -0.5, NEG_INF) for g in grp]
        top = [jnp.max(s[g], axis=-1, keepdims=True) for g in grp]
        p = [jnp.exp(s[g] - top[g]) for g in grp]
        den = [jnp.sum(p[g], axis=-1, keepdims=True) for g in grp]
        o = [dot(p[g].astype(BF16), vw[g]) / den[g] for g in grp]
        lse = [top[g] + jnp.log(den[g]) for g in grp]
        for g, (_, d) in enumerate(DIL_GROUPS):
            seg = segs[g]
            dst = pl.ds((j0 % seg) * d + j0 // seg, blk, stride=d) if d > 1 else pl.ds(j0, blk)
            og_s[g, dst, :] = o[g]
            lse_s[g, dst, :] = jnp.broadcast_to(lse[g], (blk, HEAD_DIM))
        return 0

    lax.fori_loop(0, seq // blk, q_block, 0)

    for c0 in range(0, seq, piece):
        rows = pl.ds(c0, piece)
        lses = [lse_s[g, rows, :] for g in grp]
        top = functools.reduce(jnp.maximum, lses)
        wts = [jnp.exp(x - top) for x in lses]
        den = functools.reduce(lambda a, b: a + b, wts)
        o = functools.reduce(lambda a, b: a + b, [wts[g] * og_s[g, rows, :] for g in grp]) / den
        o_ref[rows, :] = o.astype(o_ref.dtype)


def _dilated(h3, cos2, sin2):
    b, s, _ = h3.shape
    hpg = DIL_HEADS_PER_GROUP
    n_g = len(DIL_GROUPS)
    col = lambda off: pl.BlockSpec((None, s, HEAD_DIM), lambda bi, hi: (bi, 0, off + hi))
    tab = pl.BlockSpec((s, HEAD_DIM), lambda bi, hi: (0, 0))
    specs = [col(part * DIL_HEADS + gi * hpg) for part in range(3) for gi in range(n_g)]
    return pl.pallas_call(
        _dilated_kernel,
        grid=(b, hpg),
        in_specs=specs + [tab, tab],
        out_specs=pl.BlockSpec((None, s, HEAD_DIM), lambda bi, hi: (bi, 0, hi)),
        out_shape=jax.ShapeDtypeStruct((b, s, hpg * HEAD_DIM), BF16),
        scratch_shapes=[pltpu.VMEM((n_g, s, HEAD_DIM), BF16), pltpu.VMEM((n_g, s + DIL_BLOCK, HEAD_DIM), BF16),
                        pltpu.VMEM((n_g, s + DIL_BLOCK, HEAD_DIM), BF16), pltpu.VMEM((n_g, s, HEAD_DIM), F32),
                        pltpu.VMEM((n_g, s, HEAD_DIM), F32)],
        compiler_params=_params("parallel", "parallel"),
        name="dilated_attention",
    )(*([h3] * (3 * n_g)), cos2, sin2)


def _moba_kernel(q_ref, k_ref, v_ref, cos_ref, sin_ref, o_ref, qf_s, q_s, k_s, v_s, km_s, sel_s):
    seq = o_ref.shape[0]
    blk = MOBA_BLOCK
    n_blk = seq // blk
    cos2 = cos_ref[...]
    sin2 = sin_ref[...]
    q = _rope(q_ref[...], cos2, sin2)
    qf_s[...] = q
    q_s[...] = q.astype(BF16)
    km_s[...] = jnp.zeros(km_s.shape, F32)
    for nb in range(n_blk):
        kb = _rope(k_ref[pl.ds(nb * blk, blk), :], cos2[nb * blk:(nb + 1) * blk], sin2[nb * blk:(nb + 1) * blk])
        k_s[pl.ds(nb * blk, blk), :] = kb.astype(BF16)
        km_s[pl.ds(nb, 1), :] = jnp.mean(kb, axis=0, keepdims=True)
    v_s[...] = v_ref[...].astype(BF16)

    lane = _iota((blk, LANES), 1)
    causal = _iota((blk, blk), 0) >= _iota((blk, blk), 1)
    all_true = _iota((blk, blk), 0) >= 0
    rows = lambda nb: pl.ds(nb * blk, blk)

    past = range(1, n_blk)
    km = km_s[...]
    gate = {qb: jnp.where(lane < qb, _dot_nt_hi(qf_s[rows(qb), :], km), -jnp.inf) for qb in past}
    sel = {qb: jnp.zeros((blk, LANES), F32) for qb in past}
    for _k in range(MOBA_TOPK):
        best = {qb: jnp.max(gate[qb], axis=-1, keepdims=True) for qb in past}
        first = {qb: jnp.min(jnp.where(gate[qb] == best[qb], lane, LANES), axis=-1, keepdims=True) for qb in past}
        pick = {qb: (lane == first[qb]) & (best[qb] > -jnp.inf) for qb in past}
        sel = {qb: jnp.where(pick[qb], 1.0, sel[qb]) for qb in past}
        gate = {qb: jnp.where(pick[qb], -jnp.inf, gate[qb]) for qb in past}
    for qb in past:
        sel_s[rows(qb), :] = sel[qb]

    assert sorted(qb for group in MOBA_Q_GROUPS for qb in group) == list(range(n_blk))
    for group in MOBA_Q_GROUPS:
        carry = {qb: _flash_init() for qb in group}
        for j in range(max(group) + 1):
            for qb in group:
                if j < qb:
                    mask = (sel_s[rows(qb), :][:, j:j + 1] > 0.0) & all_true
                elif j == qb:
                    mask = causal
                else:
                    continue
                carry[qb] = _flash_step(q_s[rows(qb), :], k_s[rows(j), :], v_s[rows(j), :], mask, carry[qb])
        for qb in group:
            m, l, acc = carry[qb]
            o_ref[rows(qb), :] = (acc / l).astype(o_ref.dtype)


def _moba(h3, cos2, sin2):
    b, s, _ = h3.shape
    base = 3 * DIL_HEADS
    col = lambda off: pl.BlockSpec((None, s, HEAD_DIM), lambda bi, hi: (bi, 0, base + off + hi))
    tab = pl.BlockSpec((s, HEAD_DIM), lambda bi, hi: (0, 0))
    return pl.pallas_call(
        _moba_kernel,
        grid=(b, MOBA_HEADS),
        in_specs=[col(0), col(MOBA_HEADS), col(2 * MOBA_HEADS), tab, tab],
        out_specs=pl.BlockSpec((None, s, HEAD_DIM), lambda bi, hi: (bi, 0, hi)),
        out_shape=jax.ShapeDtypeStruct((b, s, MOBA_HEADS * HEAD_DIM), BF16),
        scratch_shapes=[pltpu.VMEM((s, HEAD_DIM), F32), pltpu.VMEM((s, HEAD_DIM), BF16),
                        pltpu.VMEM((s, HEAD_DIM), BF16), pltpu.VMEM((s, HEAD_DIM), BF16),
                        pltpu.VMEM((LANES, HEAD_DIM), F32), pltpu.VMEM((s, LANES), F32)],
        compiler_params=_params("parallel", "parallel"),
        name="moba_attention",
    )(h3, h3, h3, cos2, sin2)


def _layer_norm(x, gain, bias):
    mu = jnp.mean(x, axis=-1, keepdims=True)
    xc = x - mu
    var = jnp.mean(xc * xc, axis=-1, keepdims=True)
    return xc * lax.rsqrt(var + LN_EPS) * gain + bias


def _out_ln_kernel(*refs, n_parts):
    o_refs = refs[0:n_parts]
    w_refs = refs[n_parts:2 * n_parts]
    h_ref, gain_ref, bias_ref, rw_ref, y_ref, yb_ref, logit_ref = refs[2 * n_parts:]
    mix = None
    for o_r, w_r in zip(o_refs, w_refs):
        part = jnp.dot(o_r[...], w_r[...], preferred_element_type=F32)
        mix = part if mix is None else mix + part
    y = _layer_norm(DEEPNORM_ALPHA * h_ref[...] + mix, gain_ref[...], bias_ref[...])
    y_ref[...] = y
    yb_ref[...] = y.astype(BF16)
    logit_ref[...] = _dot3(y, rw_ref[...])


def _out_ln(parts, weights, h, gain, bias, router_w, tm):
    t, d = h.shape
    n_parts = len(parts)
    row = lambda width: pl.BlockSpec((tm, width), lambda i: (i, 0))
    full = lambda a: pl.BlockSpec(a.shape, lambda i: (0, 0))
    return pl.pallas_call(
        functools.partial(_out_ln_kernel, n_parts=n_parts),
        grid=(t // tm,),
        in_specs=[row(p.shape[1]) for p in parts] + [full(w) for w in weights]
                 + [row(d), full(gain), full(bias), full(router_w)],
        out_specs=[row(d), row(d), row(LANES)],
        out_shape=[jax.ShapeDtypeStruct((t, d), F32), jax.ShapeDtypeStruct((t, d), BF16),
                   jax.ShapeDtypeStruct((t, LANES), F32)],
        compiler_params=_params("parallel"),
        name="out_proj_ln",
    )(*parts, *weights, h, gain, bias, router_w)


def _route(logits, rbias):
    rows = logits.shape[0]
    lane = _iota((rows, LANES), 1)
    real = lane < N_EXPERTS
    scores = jax.nn.sigmoid(logits)
    biased = jnp.where(real, scores + rbias, -jnp.inf)

    def first_argmax(vals):
        best = jnp.max(vals, axis=-1, keepdims=True)
        return best, jnp.min(jnp.where(vals == best, lane, LANES), axis=-1, keepdims=True)

    best_score = None
    best_group = None
    for g in range(N_EXPERT_GROUPS):
        vals = jnp.where(lane // EXPERTS_PER_GROUP == g, biased, -jnp.inf)
        top1, idx1 = first_argmax(vals)
        top2, _ = first_argmax(jnp.where(lane == idx1, -jnp.inf, vals))
        score = top1 + top2
        if g == 0:
            best_score, best_group = score, jnp.zeros_like(idx1)
        else:
            better = score > best_score
            best_group = jnp.where(better, g, best_group)
            best_score = jnp.where(better, score, best_score)
    masked = jnp.where(real, jnp.where(lane // EXPERTS_PER_GROUP == best_group, biased, NEG_INF), -jnp.inf)
    _, i1 = first_argmax(masked)
    _, i2 = first_argmax(jnp.where(lane == i1, -jnp.inf, masked))
    s1 = jnp.sum(jnp.where(lane == i1, scores, 0.0), axis=-1, keepdims=True)
    s2 = jnp.sum(jnp.where(lane == i2, scores, 0.0), axis=-1, keepdims=True)
    tot = s1 + s2
    return i1, i2, s1 / tot, s2 / tot


def _route_kernel(logit_ref, rbias_ref, w_ref, e_ref, r_ref, cnt_ref, carry_s):
    i = pl.program_id(0)
    rows = logit_ref.shape[0]

    @pl.when(i == 0)
    def _():
        carry_s[...] = jnp.zeros(carry_s.shape, F32)

    i1, i2, w1, w2 = _route(logit_ref[...], rbias_ref[...])
    lane = _iota((rows, LANES), 1)
    chosen = (lane == i1) | (lane == i2)
    earlier = (_iota((rows, rows), 0) > _iota((rows, rows), 1)).astype(BF16)
    before = jnp.dot(earlier, jnp.where(chosen, 1.0, 0.0).astype(BF16), preferred_element_type=F32) + carry_s[0:1, :]
    r1 = jnp.sum(jnp.where(lane == i1, before, 0.0), axis=-1, keepdims=True)
    r2 = jnp.sum(jnp.where(lane == i2, before, 0.0), axis=-1, keepdims=True)
    w_ref[...] = jnp.where(lane == 0, w1, jnp.where(lane == 1, w2, 0.0))
    e_ref[...] = jnp.where(lane == 0, i1, jnp.where(lane == 1, i2, 0))
    r_ref[...] = jnp.where(lane == 0, r1, jnp.where(lane == 1, r2, 0.0)).astype(jnp.int32)
    carry_s[...] = carry_s[...] + jnp.sum(jnp.where(chosen, 1.0, 0.0), axis=0, keepdims=True)
    cnt_ref[...] = carry_s[...].astype(jnp.int32)


def _route_tokens(logits, rbias, tm):
    t = logits.shape[0]
    row = pl.BlockSpec((tm, LANES), lambda i: (i, 0))
    return pl.pallas_call(
        _route_kernel,
        grid=(t // tm,),
        in_specs=[row, pl.BlockSpec((1, LANES), lambda i: (0, 0))],
        out_specs=[row, row, row, pl.BlockSpec((8, LANES), lambda i: (0, 0))],
        out_shape=[jax.ShapeDtypeStruct((t, LANES), F32), jax.ShapeDtypeStruct((t, LANES), jnp.int32),
                   jax.ShapeDtypeStruct((t, LANES), jnp.int32), jax.ShapeDtypeStruct((8, LANES), jnp.int32)],
        scratch_shapes=[pltpu.VMEM((8, LANES), F32)],
        compiler_params=_params("arbitrary"),
        name="moe_route",
    )(logits, rbias)


def _dispatch_kernel(pos_ref, x_ref, init_ref, xs_ref, sem):
    del init_ref
    i = pl.program_id(0)
    rows = x_ref.shape[0]

    def issue(r, _):
        for k in range(MOE_TOPK):
            p = pos_ref[MOE_TOPK * (i * rows + r) + k]
            pltpu.make_async_copy(x_ref.at[pl.ds(r, 1)], xs_ref.at[pl.ds(p, 1)], sem).start()
        return 0

    lax.fori_loop(0, rows, issue, 0, unroll=8)
    for _k in range(MOE_TOPK):
        pltpu.make_async_copy(x_ref, xs_ref.at[pl.ds(0, rows)], sem).wait()


def _dispatch(pos, x, n_rows, tm):
    t, d = x.shape
    grid_spec = pltpu.PrefetchScalarGridSpec(
        num_scalar_prefetch=1,
        grid=(t // tm,),
        in_specs=[pl.BlockSpec((tm, d), lambda i, pos: (i, 0)), pl.BlockSpec(memory_space=pl.ANY)],
        out_specs=pl.BlockSpec(memory_space=pl.ANY),
        scratch_shapes=[pltpu.SemaphoreType.DMA(())],
    )
    return pl.pallas_call(
        _dispatch_kernel,
        grid_spec=grid_spec,
        out_shape=jax.ShapeDtypeStruct((n_rows, d), x.dtype),
        input_output_aliases={2: 0},
        compiler_params=_params("arbitrary"),
        name="moe_dispatch",
    )(pos, x, jnp.zeros((n_rows, d), x.dtype))


def _expert_kernel(blk_expert_ref, n_valid_ref, x_ref, wg_ref, wu_ref, wd_ref, y_ref, wg_s, wu_s, wd_s):
    i = pl.program_id(0)

    @pl.when((i == 0) | (blk_expert_ref[i] != blk_expert_ref[jnp.maximum(i - 1, 0)]))
    def _():
        wg_s[...] = wg_ref[...].astype(BF16)
        wu_s[...] = wu_ref[...].astype(BF16)
        wd_s[...] = wd_ref[...].astype(BF16)

    @pl.when(i < n_valid_ref[0])
    def _():
        x = x_ref[...].astype(BF16)
        hid = _silu(jnp.dot(x, wg_s[...], preferred_element_type=F32)) * jnp.dot(
            x, wu_s[...], preferred_element_type=F32)
        y_ref[...] = _dot(hid, wd_s[...])

    @pl.when(i >= n_valid_ref[0])
    def _():
        y_ref[...] = jnp.zeros(y_ref.shape, y_ref.dtype)


def _experts(blk_expert, n_valid, xs, wg, wu, wd, layer):
    n_rows, d = xs.shape
    f = wg.shape[-1]
    rows = MOE_BLOCK_ROWS
    grid_spec = pltpu.PrefetchScalarGridSpec(
        num_scalar_prefetch=2,
        grid=(n_rows // rows,),
        in_specs=[pl.BlockSpec((rows, d), lambda i, be, nv: (i, 0)),
                  pl.BlockSpec((None, None, d, f), lambda i, be, nv: (layer, be[i], 0, 0)),
                  pl.BlockSpec((None, None, d, f), lambda i, be, nv: (layer, be[i], 0, 0)),
                  pl.BlockSpec((None, None, f, d), lambda i, be, nv: (layer, be[i], 0, 0))],
        out_specs=pl.BlockSpec((rows, d), lambda i, be, nv: (i, 0)),
        scratch_shapes=[pltpu.VMEM((d, f), BF16), pltpu.VMEM((d, f), BF16), pltpu.VMEM((f, d), BF16)],
    )
    return pl.pallas_call(
        _expert_kernel,
        grid_spec=grid_spec,
        out_shape=jax.ShapeDtypeStruct((n_rows, d), F32),
        compiler_params=_params("arbitrary"),
        name="moe_experts",
    )(blk_expert, n_valid, xs, wg, wu, wd)


def _combine_kernel(pos_ref, ys_ref, w_ref, h_ref, gain_ref, bias_ref, y_ref, yb_ref, buf, sem):
    i = pl.program_id(0)
    n = pl.num_programs(0)
    rows = h_ref.shape[0]

    def issue(tile, slot):
        def body(r, _):
            for k in range(MOE_TOPK):
                p = pos_ref[MOE_TOPK * (tile * rows + r) + k]
                pltpu.make_async_copy(ys_ref.at[pl.ds(p, 1)], buf.at[slot, k, pl.ds(r, 1)], sem.at[slot]).start()
            return 0

        lax.fori_loop(0, rows, body, 0, unroll=8)

    slot = i % 2

    @pl.when(i == 0)
    def _():
        issue(0, 0)

    @pl.when(i + 1 < n)
    def _():
        issue(i + 1, 1 - slot)

    for k in range(MOE_TOPK):
        pltpu.make_async_copy(ys_ref.at[pl.ds(0, rows)], buf.at[slot, k], sem.at[slot]).wait()
    w = w_ref[...]
    ffn = w[:, 0:1] * buf[slot, 0] + w[:, 1:2] * buf[slot, 1]
    y = _layer_norm(DEEPNORM_ALPHA * h_ref[...] + ffn, gain_ref[...], bias_ref[...])
    y_ref[...] = y
    yb_ref[...] = y.astype(BF16)


def _combine_ln(pos, ys, w12, h, gain, bias, tm):
    t, d = h.shape
    grid_spec = pltpu.PrefetchScalarGridSpec(
        num_scalar_prefetch=1,
        grid=(t // tm,),
        in_specs=[pl.BlockSpec(memory_space=pl.ANY),
                  pl.BlockSpec((tm, LANES), lambda i, pos: (i, 0)),
                  pl.BlockSpec((tm, d), lambda i, pos: (i, 0)),
                  pl.BlockSpec((1, d), lambda i, pos: (0, 0)),
                  pl.BlockSpec((1, d), lambda i, pos: (0, 0))],
        out_specs=[pl.BlockSpec((tm, d), lambda i, pos: (i, 0)), pl.BlockSpec((tm, d), lambda i, pos: (i, 0))],
        scratch_shapes=[pltpu.VMEM((2, MOE_TOPK, tm, d), F32), pltpu.SemaphoreType.DMA((2,))],
    )
    return pl.pallas_call(
        _combine_kernel,
        grid_spec=grid_spec,
        out_shape=[jax.ShapeDtypeStruct((t, d), F32), jax.ShapeDtypeStruct((t, d), BF16)],
        compiler_params=_params("arbitrary"),
        name="moe_combine_ln",
    )(pos, ys, w12, h, gain, bias)


def _moe_ln(h, logits, rbias, wg, wu, wd, layer, gain, bias):
    t, d = h.shape
    rows = MOE_BLOCK_ROWS
    n_blocks = (MOE_TOPK * t) // rows + N_EXPERTS
    w12, e12, r12, counts = _route_tokens(logits, rbias, 512)
    counts = counts[0, :N_EXPERTS]
    blocks_per_expert = (counts + rows - 1) // rows
    block_end = jnp.cumsum(blocks_per_expert)
    row_start = (block_end - blocks_per_expert) * rows
    pos = (row_start[e12[:, :MOE_TOPK]] + r12[:, :MOE_TOPK]).reshape(-1).astype(jnp.int32)
    n_valid = block_end[-1:].astype(jnp.int32)
    blk = jnp.arange(n_blocks, dtype=jnp.int32)
    blk_expert = jnp.sum(jnp.minimum(blk, n_valid - 1)[:, None] >= block_end[None, :], axis=1).astype(jnp.int32)
    xs = _dispatch(pos, h, n_blocks * rows, 512)
    ys = _experts(blk_expert, n_valid, xs, wg, wu, wd, layer)
    return _combine_ln(pos, ys, w12, h, gain, bias, 256)


def _rope_tables(seq):
    inv_freq = ROPE_THETA ** (-jnp.arange(0, HEAD_DIM, 2, dtype=F32) / HEAD_DIM)
    ang = jnp.arange(seq, dtype=F32)[:, None] * inv_freq[None, :]
    cos, sin = jnp.cos(ang), jnp.sin(ang)
    return jnp.concatenate([cos, cos], axis=-1), jnp.concatenate([-sin, sin], axis=-1)


def _even_mixer(hb, b, s, w_in, conv_w, a_log, dt_bias, gdn_norm, hgrn_norm, lower_bound):
    gw = GDN_WIDTH
    n_small = 2 * GDN_HEADS
    tail0 = 4 * gw
    w_main = jnp.concatenate([w_in[:, :tail0], w_in[:, tail0 + n_small:]], axis=1).astype(BF16)
    w_small = jnp.pad(w_in[:, tail0:tail0 + n_small], ((0, 0), (0, LANES - n_small))).astype(BF16)
    h3 = _matmul(hb, w_main, 1024, 1024).reshape(b, s, EVEN_MAIN)
    n_chunks = s // GDN_CHUNK
    small = _matmul(hb, w_small, 1024, LANES).reshape(b, s, LANES)
    to_rows = lambda a: a.transpose(0, 2, 1).reshape(b, GDN_HEADS, n_chunks, GDN_CHUNK)
    b_rows, a_rows = to_rows(small[..., :GDN_HEADS]), to_rows(small[..., GDN_HEADS:n_small])
    headvec = lambda v: jnp.broadcast_to(v.astype(F32)[:, None, None], (GDN_HEADS, 1, GDN_CHUNK))
    o_a = _gdn(h3, conv_w.astype(F32), a_rows, b_rows, headvec(a_log), headvec(dt_bias),
               gdn_norm.astype(F32).reshape(1, HEAD_DIM))
    o_b = _hgrn(h3, lower_bound.astype(F32).reshape(HGRN_HEADS, 1, HEAD_DIM),
                hgrn_norm.astype(F32).reshape(1, HEAD_DIM))
    return [o_a.reshape(b * s, GDN_WIDTH), o_b.reshape(b * s, HGRN_WIDTH)]


def _odd_mixer(hb, b, s, w_in, cos2, sin2):
    h = _matmul(hb, w_in.astype(BF16), 1024, 768)
    h3 = h.reshape(b, s, ODD_COLS)
    o_c = _dilated(h3, cos2, sin2)
    o_d = _moba(h3, cos2, sin2)
    return [o_c.reshape(b * s, -1), o_d.reshape(b * s, -1)]


def kernel(x, ev_w_in, ev_conv_w, ev_a_log, ev_dt_bias, ev_gdn_norm, ev_hgrn_norm, hgrn_lb_logits, ev_w_out,
           od_w_in, od_w_out, router_w, router_bias, moe_w_gate, moe_w_up, moe_w_down, ln_gain, ln_bias):
    b, s, d = x.shape
    t = b * s
    cos2, sin2 = _rope_tables(s)
    lower_bounds = jnp.cumsum(jax.nn.softmax(hgrn_lb_logits.astype(F32), axis=0), axis=0)
    rw = jnp.pad(router_w.astype(F32), ((0, 0), (0, LANES - N_EXPERTS)))
    rbias = jnp.pad(router_bias.astype(F32), (0, LANES - N_EXPERTS)).reshape(1, LANES)
    vec = lambda v: v.astype(F32).reshape(1, d)

    h = x.reshape(t, d)
    hb = h.astype(BF16)
    for layer in range(DEPTH):
        if layer % 2 == 0:
            e = layer // 2
            parts = _even_mixer(hb, b, s, ev_w_in[e], ev_conv_w[e], ev_a_log[e], ev_dt_bias[e], ev_gdn_norm[e],
                                ev_hgrn_norm[e], lower_bounds[layer])
            w_out = ev_w_out[e].astype(BF16)
        else:
            o = layer // 2
            parts = _odd_mixer(hb, b, s, od_w_in[o], cos2, sin2)
            w_out = od_w_out[o].astype(BF16)
        splits = np.cumsum([p.shape[1] for p in parts])[:-1]
        weights = jnp.split(w_out, splits, axis=0)
        h, hb, logits = _out_ln(parts, weights, h, vec(ln_gain[layer, 0]), vec(ln_bias[layer, 0]), rw, 256)
        h, hb = _moe_ln(h, logits, rbias, moe_w_gate, moe_w_up, moe_w_down, layer,
                        vec(ln_gain[layer, 1]), vec(ln_bias[layer, 1]))
    return h.reshape(b, s, d)
```

```python
import functools
import math

import jax
import jax.numpy as jnp
import numpy as np
from jax import lax
from jax.experimental import pallas as pl
from jax.experimental.pallas import tpu as pltpu

F32 = jnp.float32
BF16 = jnp.bfloat16

D_MODEL = 2048
DEPTH = 2
HEAD_DIM = 128
GDN_HEADS = 8
GDN_CONV = 4
GDN_CHUNK = 64
GDN_WIDTH = GDN_HEADS * HEAD_DIM
HGRN_HEADS = 8
HGRN_CHUNK = 16
HGRN_WIDTH = HGRN_HEADS * HEAD_DIM
DIL_GROUPS = ((128, 1), (512, 4), (2048, 16))
DIL_HEADS_PER_GROUP = 4
DIL_HEADS = len(DIL_GROUPS) * DIL_HEADS_PER_GROUP
MOBA_HEADS = 4
MOBA_BLOCK = 256
MOBA_TOPK = 3
ROPE_THETA = 10000.0
N_EXPERTS = 16
N_EXPERT_GROUPS = 4
EXPERTS_PER_GROUP = N_EXPERTS // N_EXPERT_GROUPS
D_EXPERT = 512
MOE_TOPK = 2
MOE_BLOCK_ROWS = 512
DEEPNORM_ALPHA = (2.0 * DEPTH) ** 0.25
LN_EPS = 1e-5
RMS_EPS = 1e-6
NEG_INF = -1e30

LANES = 128
VMEM_LIMIT = 56 * 1024 * 1024
ATT_BLOCK = 256
MOBA_Q_GROUPS = ((7, 0, 6, 1), (5, 2, 4, 3))
DIL_BLOCK = 128
GDN_GROUP = 16
HGRN_GROUP = 2
HGRN_ROWS = 256

EVEN_MAIN = 3 * GDN_WIDTH + GDN_WIDTH + 4 * HGRN_WIDTH
ODD_COLS = 3 * DIL_HEADS * HEAD_DIM + 3 * MOBA_HEADS * HEAD_DIM


def _dot(a, b):
    return jnp.dot(a.astype(BF16), b.astype(BF16), preferred_element_type=F32)


def _dot_nt(a, b):
    return lax.dot_general(a.astype(BF16), b.astype(BF16), (((1,), (1,)), ((), ())),
                           preferred_element_type=F32)


def _dot_tn(a, b):
    return lax.dot_general(a.astype(BF16), b.astype(BF16), (((0,), (0,)), ((), ())),
                           preferred_element_type=F32)


def _dot_hi(a, b):
    return jnp.dot(a, b, preferred_element_type=F32, precision=lax.Precision.HIGHEST)


def _dot_nt_hi(a, b):
    return lax.dot_general(a, b, (((1,), (1,)), ((), ())), preferred_element_type=F32,
                           precision=lax.Precision.HIGHEST)


def _dot3(a, b):
    a_hi = a.astype(BF16)
    b_hi = b.astype(BF16)
    a_lo = (a - a_hi.astype(F32)).astype(BF16)
    b_lo = (b - b_hi.astype(F32)).astype(BF16)
    dot = functools.partial(jnp.dot, preferred_element_type=F32)
    return dot(a_hi, b_hi) + (dot(a_hi, b_lo) + dot(a_lo, b_hi))


def _dot_sel(sel, x):
    dot = functools.partial(jnp.dot, preferred_element_type=F32)
    x_hi = x.astype(BF16)
    r1 = x - x_hi.astype(F32)
    x_mid = r1.astype(BF16)
    x_lo = (r1 - x_mid.astype(F32)).astype(BF16)
    return dot(sel, x_hi) + (dot(sel, x_mid) + dot(sel, x_lo))


_dot_inv = _dot


def _silu(x):
    return x * jax.nn.sigmoid(x)


def _iota(shape, dim):
    return lax.broadcasted_iota(jnp.int32, shape, dim)


def _params(*sem):
    return pltpu.CompilerParams(dimension_semantics=sem, vmem_limit_bytes=VMEM_LIMIT)


def _mm_kernel(x_ref, w_ref, o_ref):
    o_ref[...] = jnp.dot(x_ref[...], w_ref[...], preferred_element_type=F32).astype(o_ref.dtype)


def _matmul(x, w, tm, tn):
    m, k = x.shape
    n = w.shape[1]
    assert m % tm == 0 and n % tn == 0
    return pl.pallas_call(
        _mm_kernel,
        grid=(m // tm, n // tn),
        in_specs=[pl.BlockSpec((tm, k), lambda i, j: (i, 0)),
                  pl.BlockSpec((k, tn), lambda i, j: (0, j))],
        out_specs=pl.BlockSpec((tm, tn), lambda i, j: (i, j)),
        out_shape=jax.ShapeDtypeStruct((m, n), F32),
        compiler_params=_params("parallel", "parallel"),
        name="in_proj",
    )(x, w)


def _gdn_kernel(q_ref, k_ref, v_ref, z_ref, cwq_ref, cwk_ref, cwv_ref, a_ref, b_ref, alog_ref, dt_ref,
                gn_ref, o_ref, pad_s, q_s, k_s, v_s, gcum_s, beta_s, qe_s, ob_s, sm_s, sa_s):
    seq = q_ref.shape[0]
    c = GDN_CHUNK
    n_chunks = seq // c
    rows = 256

    pad_s[pl.ds(0, 8), :] = jnp.zeros((8, HEAD_DIM), F32)
    for x_ref, cw_ref, dst, mode in ((q_ref, cwq_ref, q_s, "q"), (k_ref, cwk_ref, k_s, "k"),
                                     (v_ref, cwv_ref, v_s, "v")):
        pad_s[pl.ds(8, seq), :] = x_ref[...]
        cw = cw_ref[...]
        for r in range(seq // rows):
            acc = None
            for j in range(GDN_CONV):
                tap = pad_s[pl.ds(8 + r * rows - (GDN_CONV - 1) + j, rows), :] * cw[j:j + 1, :]
                acc = tap if acc is None else acc + tap
            y = _silu(acc)
            if mode != "v":
                y = y * lax.rsqrt(jnp.sum(y * y, axis=-1, keepdims=True) + RMS_EPS)
            if mode == "q":
                y = y * HEAD_DIM ** -0.5
            dst[pl.ds(r * rows, rows), :] = y

    upper = (_iota((c, c), 0) <= _iota((c, c), 1)).astype(F32)
    g = -jnp.exp(alog_ref[...]) * jax.nn.softplus(a_ref[...] + dt_ref[...])
    gcum_s[...] = _dot_hi(g, upper)
    beta_s[...] = jax.nn.sigmoid(b_ref[...])

    ri = _iota((c, c), 0)
    ci = _iota((c, c), 1)
    eye = ri == ci
    strict = ri > ci
    incl = ri >= ci
    eye_f = eye.astype(F32)
    level1 = ri // 2 == ci // 2
    levels = []
    s = 2
    while s < c:
        levels.append((ri // (2 * s) == ci // (2 * s)) & ((ri // s) % 2 == 1) & ((ci // s) % 2 == 0))
        s *= 2

    dot = functools.partial(jnp.dot, preferred_element_type=F32)

    def to_col(row):
        return jnp.sum(jnp.where(eye, jnp.broadcast_to(row, (c, c)), 0.0), axis=1, keepdims=True)

    def prepare(i, _):
        n0 = i * GDN_GROUP
        grp = range(GDN_GROUP)
        starts = [pl.multiple_of((n0 + j) * c, c) for j in grp]
        qc = [q_s[pl.ds(r0, c), :] for r0 in starts]
        kc = [k_s[pl.ds(r0, c), :] for r0 in starts]
        vc = [v_s[pl.ds(r0, c), :] for r0 in starts]
        g_row = [gcum_s[pl.ds(n0 + j, 1), :] for j in grp]
        g_col = [to_col(g_row[j]) for j in grp]
        b_col = [to_col(beta_s[pl.ds(n0 + j, 1), :]) for j in grp]
        decay = [jnp.exp(jnp.where(incl, g_col[j] - g_row[j], 0.0)) for j in grp]
        n_mat = [b_col[j] * jnp.where(strict, decay[j], 0.0) * _dot_nt(kc[j], kc[j]) for j in grp]
        inv = [eye_f - jnp.where(level1, n_mat[j], 0.0) for j in grp]
        for blk in levels:
            tmp = [_dot_inv(inv[j], jnp.where(blk, n_mat[j], 0.0)) for j in grp]
            inv = [inv[j] - _dot_inv(tmp[j], inv[j]) for j in grp]
        e_col = [jnp.exp(g_col[j]) for j in grp]
        sol = [_dot_inv(inv[j], jnp.concatenate([b_col[j] * vc[j], (b_col[j] * e_col[j]) * kc[j]], axis=1))
               for j in grp]
        qk = [(_dot_nt(qc[j], kc[j]) * jnp.where(incl, decay[j], 0.0)).astype(BF16) for j in grp]
        ub = [sol[j][:, :HEAD_DIM].astype(BF16) for j in grp]
        w = [sol[j][:, HEAD_DIM:].astype(BF16) for j in grp]
        kd = [(kc[j] * jnp.exp(g_row[j][:, c - 1:c] - g_col[j])).astype(BF16) for j in grp]
        q_eff = [(qc[j] * e_col[j] - dot(qk[j], w[j])).astype(BF16) for j in grp]
        o_base = [dot(qk[j], ub[j]) for j in grp]
        s_mat = [_dot_tn(kd[j], w[j]).astype(BF16) for j in grp]
        s_add = [_dot_tn(kd[j], ub[j]) for j in grp]
        for j, r0 in enumerate(starts):
            m0 = pl.multiple_of((n0 + j) * HEAD_DIM, HEAD_DIM)
            qe_s[pl.ds(r0, c), :] = q_eff[j]
            ob_s[pl.ds(r0, c), :] = o_base[j]
            sm_s[pl.ds(m0, HEAD_DIM), :] = s_mat[j]
            sa_s[pl.ds(m0, HEAD_DIM), :] = s_add[j]
        return 0

    lax.fori_loop(0, n_chunks // GDN_GROUP, prepare, 0)

    gn = gn_ref[...]

    def chunk(n, state):
        r0 = pl.multiple_of(n * c, c)
        m0 = pl.multiple_of(n * HEAD_DIM, HEAD_DIM)
        g_last = gcum_s[pl.ds(n, 1), :][:, c - 1:c]
        lhs = jnp.concatenate([qe_s[pl.ds(r0, c), :], sm_s[pl.ds(m0, HEAD_DIM), :]], axis=0)
        prod = dot(lhs, state.astype(BF16))
        ob_s[pl.ds(r0, c), :] = prod[:c] + ob_s[pl.ds(r0, c), :]
        return jnp.exp(g_last) * state - prod[c:] + sa_s[pl.ds(m0, HEAD_DIM), :]

    lax.fori_loop(0, n_chunks, chunk, jnp.zeros((HEAD_DIM, HEAD_DIM), F32))

    for r in range(seq // rows):
        sl = pl.ds(r * rows, rows)
        o = ob_s[sl, :]
        o = o * lax.rsqrt(jnp.mean(o * o, axis=-1, keepdims=True) + RMS_EPS) * gn
        o_ref[sl, :] = (o * _silu(z_ref[sl, :])).astype(o_ref.dtype)


def _gdn(h3, conv_w, a_rows, b_rows, alog, dt, gn):
    b, s, _ = h3.shape
    nh = GDN_HEADS
    n_chunks = s // GDN_CHUNK
    col = lambda off: pl.BlockSpec((None, s, HEAD_DIM), lambda bi, hi: (bi, 0, off + hi))
    cw = lambda off: pl.BlockSpec((GDN_CONV, HEAD_DIM), lambda bi, hi: (0, off + hi))
    rowspec = pl.BlockSpec((None, None, n_chunks, GDN_CHUNK), lambda bi, hi: (bi, hi, 0, 0))
    headvec = pl.BlockSpec((None, 1, GDN_CHUNK), lambda bi, hi: (hi, 0, 0))
    return pl.pallas_call(
        _gdn_kernel,
        grid=(b, nh),
        in_specs=[col(0), col(nh), col(2 * nh), col(3 * nh), cw(0), cw(nh), cw(2 * nh),
                  rowspec, rowspec, headvec, headvec,
                  pl.BlockSpec((1, HEAD_DIM), lambda bi, hi: (0, 0))],
        out_specs=pl.BlockSpec((None, s, HEAD_DIM), lambda bi, hi: (bi, 0, hi)),
        out_shape=jax.ShapeDtypeStruct((b, s, GDN_WIDTH), BF16),
        scratch_shapes=[pltpu.VMEM((s + 8, HEAD_DIM), F32), pltpu.VMEM((s, HEAD_DIM), F32),
                        pltpu.VMEM((s, HEAD_DIM), F32), pltpu.VMEM((s, HEAD_DIM), F32),
                        pltpu.VMEM((n_chunks, GDN_CHUNK), F32), pltpu.VMEM((n_chunks, GDN_CHUNK), F32),
                        pltpu.VMEM((s, HEAD_DIM), BF16), pltpu.VMEM((s, HEAD_DIM), F32),
                        pltpu.VMEM((n_chunks * HEAD_DIM, HEAD_DIM), BF16),
                        pltpu.VMEM((n_chunks * HEAD_DIM, HEAD_DIM), F32)],
        compiler_params=_params("parallel", "parallel"),
        name="gdn",
    )(h3, h3, h3, h3, conv_w, conv_w, conv_w, a_rows, b_rows, alog, dt, gn)


def _hgrn_kernel(q_ref, f_ref, i_ref, g_ref, lb_ref, hn_ref, o_ref):
    seq = q_ref.shape[0]
    c = HGRN_CHUNK
    rows = HGRN_ROWS
    ri = _iota((rows, rows), 0)
    ci = _iota((rows, rows), 1)
    same = ri // c == ci // c
    causal = same & (ci <= ri)
    sum_mat = jnp.concatenate([causal.astype(BF16), same.astype(BF16)], axis=0)
    lb = lb_ref[...]
    hn = hn_ref[...]

    chunks = [slice(j * c, (j + 1) * c) for j in range(rows // c)]
    grp = range(HGRN_GROUP)

    def group(n, state_t):
        starts = [pl.multiple_of((n * HGRN_GROUP + j) * rows, rows) for j in grp]
        qc = [q_ref[pl.ds(r0, rows), :] for r0 in starts]
        ic = [i_ref[pl.ds(r0, rows), :].astype(BF16) for r0 in starts]
        f = [lb + (1.0 - lb) * jax.nn.sigmoid(f_ref[pl.ds(r0, rows), :]) for r0 in starts]
        sums = [_dot_sel(sum_mat, jnp.log(f[j])) for j in grp]
        bcum = [sums[j][:rows] for j in grp]
        b_last = [sums[j][rows:] for j in grp]
        q_dec = [(qc[j] * jnp.exp(bcum[j])).astype(BF16) for j in grp]
        k_inv = [(1.0 - f[j]) * jnp.exp(-bcum[j]) for j in grp]
        k_dec = [((1.0 - f[j]) * jnp.exp(b_last[j] - bcum[j])).astype(BF16) for j in grp]
        chunk_dec = [jnp.exp(b_last[j]) for j in grp]
        p = [jnp.where(causal, _dot_nt(q_dec[j], k_inv[j]), 0.0) for j in grp]
        o_intra = [_dot(p[j], ic[j]) for j in grp]
        updates = [[_dot_tn(ic[j][sl], k_dec[j][sl]) for sl in chunks] for j in grp]
        for j, r0 in enumerate(starts):
            outs = []
            for sl, upd in zip(chunks, updates[j]):
                outs.append(o_intra[j][sl] + _dot_nt(q_dec[j][sl], state_t))
                state_t = state_t * chunk_dec[j][sl.start:sl.start + 1] + upd
            o = jnp.concatenate(outs, axis=0)
            o = o * lax.rsqrt(jnp.mean(o * o, axis=-1, keepdims=True) + RMS_EPS) * hn
            o_ref[pl.ds(r0, rows), :] = (o * _silu(g_ref[pl.ds(r0, rows), :])).astype(o_ref.dtype)
        return state_t

    lax.fori_loop(0, seq // (rows * HGRN_GROUP), group, jnp.zeros((HEAD_DIM, HEAD_DIM), F32))


def _hgrn(h3, lb, hn):
    b, s, _ = h3.shape
    nh = HGRN_HEADS
    base = 4 * GDN_HEADS
    col = lambda off: pl.BlockSpec((None, s, HEAD_DIM), lambda bi, hi: (bi, 0, base + off + hi))
    return pl.pallas_call(
        _hgrn_kernel,
        grid=(b, nh),
        in_specs=[col(0), col(nh), col(2 * nh), col(3 * nh),
                  pl.BlockSpec((None, 1, HEAD_DIM), lambda bi, hi: (hi, 0, 0)),
                  pl.BlockSpec((1, HEAD_DIM), lambda bi, hi: (0, 0))],
        out_specs=pl.BlockSpec((None, s, HEAD_DIM), lambda bi, hi: (bi, 0, hi)),
        out_shape=jax.ShapeDtypeStruct((b, s, HGRN_WIDTH), BF16),
        compiler_params=_params("parallel", "parallel"),
        name="hgrn2",
    )(h3, h3, h3, h3, lb, hn)


def _rope(x, cos2, sin2):
    return x * cos2 + pltpu.roll(x, HEAD_DIM // 2, axis=1) * sin2


def _flash_step(q_blk, k_blk, v_blk, mask, carry):
    m, l, acc = carry
    s = jnp.where(mask, _dot_nt(q_blk, k_blk) * HEAD_DIM ** -0.5, NEG_INF)
    m_new = jnp.maximum(m, jnp.max(s, axis=-1, keepdims=True))
    alpha = jnp.exp(m - m_new)
    p = jnp.exp(s - m_new)
    l = alpha * l + jnp.sum(p, axis=-1, keepdims=True)
    acc = alpha * acc + _dot(p, v_blk)
    return m_new, l, acc


def _flash_init():
    blk = ATT_BLOCK
    return (jnp.full((blk, 1), NEG_INF, F32), jnp.zeros((blk, 1), F32), jnp.zeros((blk, HEAD_DIM), F32))


def _dilated_kernel(*refs):
    n_g = len(DIL_GROUPS)
    q_refs, k_refs, v_refs = refs[0:n_g], refs[n_g:2 * n_g], refs[2 * n_g:3 * n_g]
    cos_ref, sin_ref, o_ref = refs[3 * n_g:3 * n_g + 3]
    q_s, k_s, v_s, og_s, lse_s = refs[3 * n_g + 3:]
    seq = o_ref.shape[0]
    blk = DIL_BLOCK
    piece = 256
    grp = range(n_g)

    for gi, (window, d) in enumerate(DIL_GROUPS):
        assert window // d == blk
        seg = seq // d
        k_s[gi, pl.ds(0, blk), :] = jnp.zeros((blk, HEAD_DIM), BF16)
        v_s[gi, pl.ds(0, blk), :] = jnp.zeros((blk, HEAD_DIM), BF16)
        for r in range(d):
            for c0 in range(0, seg, piece):
                n = min(piece, seg)
                rows = pl.ds(r + c0 * d, n, stride=d) if d > 1 else pl.ds(c0, n)
                cos2 = cos_ref[rows, :]
                sin2 = sin_ref[rows, :]
                q_s[gi, pl.ds(r * seg + c0, n), :] = _rope(q_refs[gi][rows, :], cos2, sin2).astype(BF16)
                k_s[gi, pl.ds(blk + r * seg + c0, n), :] = _rope(k_refs[gi][rows, :], cos2, sin2).astype(BF16)
                v_s[gi, pl.ds(blk + r * seg + c0, n), :] = v_refs[gi][rows, :].astype(BF16)

    ri = _iota((blk, 2 * blk), 0)
    ci = _iota((blk, 2 * blk), 1)
    rel = ri + blk - ci
    in_window = (rel >= 0) & (rel <= blk)
    dot = functools.partial(jnp.dot, preferred_element_type=F32)

    def q_block(m, _):
        j0 = pl.multiple_of(m * blk, blk)
        segs = [seq // d for _, d in DIL_GROUPS]
        has_prev = [jnp.where(j0 % seg != 0, blk, 0) for seg in segs]
        mask = [in_window & (ci + has_prev[g] >= blk) for g in grp]
        q = [q_s[g, pl.ds(j0, blk), :] for g in grp]
        kw = [k_s[g, pl.ds(j0, 2 * blk), :] for g in grp]
        vw = [v_s[g, pl.ds(j0, 2 * blk), :] for g in grp]
        s = [jnp.where(mask[g], _dot_nt(q[g], kw[g]) * HEAD_DIM ** -0.5, NEG_INF) for g in grp]
        top = [jnp.max(s[g], axis=-1, keepdims=True) for g in grp]
        p = [jnp.exp(s[g] - top[g]) for g in grp]
        den = [jnp.sum(p[g], axis=-1, keepdims=True) for g in grp]
        o = [dot(p[g].astype(BF16), vw[g]) / den[g] for g in grp]
        lse = [top[g] + jnp.log(den[g]) for g in grp]
        for g, (_, d) in enumerate(DIL_GROUPS):
            seg = segs[g]
            dst = pl.ds((j0 % seg) * d + j0 // seg, blk, stride=d) if d > 1 else pl.ds(j0, blk)
            og_s[g, dst, :] = o[g]
            lse_s[g, dst, :] = jnp.broadcast_to(lse[g], (blk, HEAD_DIM))
        return 0

    lax.fori_loop(0, seq // blk, q_block, 0)

    for c0 in range(0, seq, piece):
        rows = pl.ds(c0, piece)
        lses = [lse_s[g, rows, :] for g in grp]
        top = functools.reduce(jnp.maximum, lses)
        wts = [jnp.exp(x - top) for x in lses]
        den = functools.reduce(lambda a, b: a + b, wts)
        o = functools.reduce(lambda a, b: a + b, [wts[g] * og_s[g, rows, :] for g in grp]) / den
        o_ref[rows, :] = o.astype(o_ref.dtype)


def _dilated(h3, cos2, sin2):
    b, s, _ = h3.shape
    hpg = DIL_HEADS_PER_GROUP
    n_g = len(DIL_GROUPS)
    col = lambda off: pl.BlockSpec((None, s, HEAD_DIM), lambda bi, hi: (bi, 0, off + hi))
    tab = pl.BlockSpec((s, HEAD_DIM), lambda bi, hi: (0, 0))
    specs = [col(part * DIL_HEADS + gi * hpg) for part in range(3) for gi in range(n_g)]
    return pl.pallas_call(
        _dilated_kernel,
        grid=(b, hpg),
        in_specs=specs + [tab, tab],
        out_specs=pl.BlockSpec((None, s, HEAD_DIM), lambda bi, hi: (bi, 0, hi)),
        out_shape=jax.ShapeDtypeStruct((b, s, hpg * HEAD_DIM), BF16),
        scratch_shapes=[pltpu.VMEM((n_g, s, HEAD_DIM), BF16), pltpu.VMEM((n_g, s + DIL_BLOCK, HEAD_DIM), BF16),
                        pltpu.VMEM((n_g, s + DIL_BLOCK, HEAD_DIM), BF16), pltpu.VMEM((n_g, s, HEAD_DIM), F32),
                        pltpu.VMEM((n_g, s, HEAD_DIM), F32)],
        compiler_params=_params("parallel", "parallel"),
        name="dilated_attention",
    )(*([h3] * (3 * n_g)), cos2, sin2)


def _moba_kernel(q_ref, k_ref, v_ref, cos_ref, sin_ref, o_ref, qf_s, q_s, k_s, v_s, km_s, sel_s):
    seq = o_ref.shape[0]
    blk = MOBA_BLOCK
    n_blk = seq // blk
    cos2 = cos_ref[...]
    sin2 = sin_ref[...]
    q = _rope(q_ref[...], cos2, sin2)
    qf_s[...] = q
    q_s[...] = q.astype(BF16)
    km_s[...] = jnp.zeros(km_s.shape, F32)
    for nb in range(n_blk):
        kb = _rope(k_ref[pl.ds(nb * blk, blk), :], cos2[nb * blk:(nb + 1) * blk], sin2[nb * blk:(nb + 1) * blk])
        k_s[pl.ds(nb * blk, blk), :] = kb.astype(BF16)
        km_s[pl.ds(nb, 1), :] = jnp.mean(kb, axis=0, keepdims=True)
    v_s[...] = v_ref[...].astype(BF16)

    lane = _iota((blk, LANES), 1)
    causal = _iota((blk, blk), 0) >= _iota((blk, blk), 1)
    all_true = _iota((blk, blk), 0) >= 0
    rows = lambda nb: pl.ds(nb * blk, blk)

    past = range(1, n_blk)
    km = km_s[...]
    gate = {qb: jnp.where(lane < qb, _dot_nt_hi(qf_s[rows(qb), :], km), -jnp.inf) for qb in past}
    sel = {qb: jnp.zeros((blk, LANES), F32) for qb in past}
    for _k in range(MOBA_TOPK):
        best = {qb: jnp.max(gate[qb], axis=-1, keepdims=True) for qb in past}
        first = {qb: jnp.min(jnp.where(gate[qb] == best[qb], lane, LANES), axis=-1, keepdims=True) for qb in past}
        pick = {qb: (lane == first[qb]) & (best[qb] > -jnp.inf) for qb in past}
        sel = {qb: jnp.where(pick[qb], 1.0, sel[qb]) for qb in past}
        gate = {qb: jnp.where(pick[qb], -jnp.inf, gate[qb]) for qb in past}
    for qb in past:
        sel_s[rows(qb), :] = sel[qb]

    assert sorted(qb for group in MOBA_Q_GROUPS for qb in group) == list(range(n_blk))
    for group in MOBA_Q_GROUPS:
        carry = {qb: _flash_init() for qb in group}
        for j in range(max(group) + 1):
            for qb in group:
                if j < qb:
                    mask = (sel_s[rows(qb), :][:, j:j + 1] > 0.0) & all_true
                elif j == qb:
                    mask = causal
                else:
                    continue
                carry[qb] = _flash_step(q_s[rows(qb), :], k_s[rows(j), :], v_s[rows(j), :], mask, carry[qb])
        for qb in group:
            m, l, acc = carry[qb]
            o_ref[rows(qb), :] = (acc / l).astype(o_ref.dtype)


def _moba(h3, cos2, sin2):
    b, s, _ = h3.shape
    base = 3 * DIL_HEADS
    col = lambda off: pl.BlockSpec((None, s, HEAD_DIM), lambda bi, hi: (bi, 0, base + off + hi))
    tab = pl.BlockSpec((s, HEAD_DIM), lambda bi, hi: (0, 0))
    return pl.pallas_call(
        _moba_kernel,
        grid=(b, MOBA_HEADS),
        in_specs=[col(0), col(MOBA_HEADS), col(2 * MOBA_HEADS), tab, tab],
        out_specs=pl.BlockSpec((None, s, HEAD_DIM), lambda bi, hi: (bi, 0, hi)),
        out_shape=jax.ShapeDtypeStruct((b, s, MOBA_HEADS * HEAD_DIM), BF16),
        scratch_shapes=[pltpu.VMEM((s, HEAD_DIM), F32), pltpu.VMEM((s, HEAD_DIM), BF16),
                        pltpu.VMEM((s, HEAD_DIM), BF16), pltpu.VMEM((s, HEAD_DIM), BF16),
                        pltpu.VMEM((LANES, HEAD_DIM), F32), pltpu.VMEM((s, LANES), F32)],
        compiler_params=_params("parallel", "parallel"),
        name="moba_attention",
    )(h3, h3, h3, cos2, sin2)


def _layer_norm(x, gain, bias):
    mu = jnp.mean(x, axis=-1, keepdims=True)
    xc = x - mu
    var = jnp.mean(xc * xc, axis=-1, keepdims=True)
    return xc * lax.rsqrt(var + LN_EPS) * gain + bias


def _out_ln_kernel(*refs, n_parts):
    o_refs = refs[0:n_parts]
    w_refs = refs[n_parts:2 * n_parts]
    h_ref, gain_ref, bias_ref, rw_ref, y_ref, yb_ref, logit_ref = refs[2 * n_parts:]
    mix = None
    for o_r, w_r in zip(o_refs, w_refs):
        part = jnp.dot(o_r[...], w_r[...], preferred_element_type=F32)
        mix = part if mix is None else mix + part
    y = _layer_norm(DEEPNORM_ALPHA * h_ref[...] + mix, gain_ref[...], bias_ref[...])
    y_ref[...] = y
    yb_ref[...] = y.astype(BF16)
    logit_ref[...] = _dot3(y, rw_ref[...])


def _out_ln(parts, weights, h, gain, bias, router_w, tm):
    t, d = h.shape
    n_parts = len(parts)
    row = lambda width: pl.BlockSpec((tm, width), lambda i: (i, 0))
    full = lambda a: pl.BlockSpec(a.shape, lambda i: (0, 0))
    return pl.pallas_call(
        functools.partial(_out_ln_kernel, n_parts=n_parts),
        grid=(t // tm,),
        in_specs=[row(p.shape[1]) for p in parts] + [full(w) for w in weights]
                 + [row(d), full(gain), full(bias), full(router_w)],
        out_specs=[row(d), row(d), row(LANES)],
        out_shape=[jax.ShapeDtypeStruct((t, d), F32), jax.ShapeDtypeStruct((t, d), BF16),
                   jax.ShapeDtypeStruct((t, LANES), F32)],
        compiler_params=_params("parallel"),
        name="out_proj_ln",
    )(*parts, *weights, h, gain, bias, router_w)


def _route(logits, rbias):
    rows = logits.shape[0]
    lane = _iota((rows, LANES), 1)
    real = lane < N_EXPERTS
    scores = jax.nn.sigmoid(logits)
    biased = jnp.where(real, scores + rbias, -jnp.inf)

    def first_argmax(vals):
        best = jnp.max(vals, axis=-1, keepdims=True)
        return best, jnp.min(jnp.where(vals == best, lane, LANES), axis=-1, keepdims=True)

    best_score = None
    best_group = None
    for g in range(N_EXPERT_GROUPS):
        vals = jnp.where(lane // EXPERTS_PER_GROUP == g, biased, -jnp.inf)
        top1, idx1 = first_argmax(vals)
        top2, _ = first_argmax(jnp.where(lane == idx1, -jnp.inf, vals))
        score = top1 + top2
        if g == 0:
            best_score, best_group = score, jnp.zeros_like(idx1)
        else:
            better = score > best_score
            best_group = jnp.where(better, g, best_group)
            best_score = jnp.where(better, score, best_score)
    masked = jnp.where(real, jnp.where(lane // EXPERTS_PER_GROUP == best_group, biased, NEG_INF), -jnp.inf)
    _, i1 = first_argmax(masked)
    _, i2 = first_argmax(jnp.where(lane == i1, -jnp.inf, masked))
    s1 = jnp.sum(jnp.where(lane == i1, scores, 0.0), axis=-1, keepdims=True)
    s2 = jnp.sum(jnp.where(lane == i2, scores, 0.0), axis=-1, keepdims=True)
    tot = s1 + s2
    return i1, i2, s1 / tot, s2 / tot


def _route_kernel(logit_ref, rbias_ref, w_ref, e_ref, r_ref, cnt_ref, carry_s):
    i = pl.program_id(0)
    rows = logit_ref.shape[0]

    @pl.when(i == 0)
    def _():
        carry_s[...] = jnp.zeros(carry_s.shape, F32)

    i1, i2, w1, w2 = _route(logit_ref[...], rbias_ref[...])
    lane = _iota((rows, LANES), 1)
    chosen = (lane == i1) | (lane == i2)
    earlier = (_iota((rows, rows), 0) > _iota((rows, rows), 1)).astype(BF16)
    before = jnp.dot(earlier, jnp.where(chosen, 1.0, 0.0).astype(BF16), preferred_element_type=F32) + carry_s[0:1, :]
    r1 = jnp.sum(jnp.where(lane == i1, before, 0.0), axis=-1, keepdims=True)
    r2 = jnp.sum(jnp.where(lane == i2, before, 0.0), axis=-1, keepdims=True)
    w_ref[...] = jnp.where(lane == 0, w1, jnp.where(lane == 1, w2, 0.0))
    e_ref[...] = jnp.where(lane == 0, i1, jnp.where(lane == 1, i2, 0))
    r_ref[...] = jnp.where(lane == 0, r1, jnp.where(lane == 1, r2, 0.0)).astype(jnp.int32)
    carry_s[...] = carry_s[...] + jnp.sum(jnp.where(chosen, 1.0, 0.0), axis=0, keepdims=True)
    cnt_ref[...] = carry_s[...].astype(jnp.int32)


def _route_tokens(logits, rbias, tm):
    t = logits.shape[0]
    row = pl.BlockSpec((tm, LANES), lambda i: (i, 0))
    return pl.pallas_call(
        _route_kernel,
        grid=(t // tm,),
        in_specs=[row, pl.BlockSpec((1, LANES), lambda i: (0, 0))],
        out_specs=[row, row, row, pl.BlockSpec((8, LANES), lambda i: (0, 0))],
        out_shape=[jax.ShapeDtypeStruct((t, LANES), F32), jax.ShapeDtypeStruct((t, LANES), jnp.int32),
                   jax.ShapeDtypeStruct((t, LANES), jnp.int32), jax.ShapeDtypeStruct((8, LANES), jnp.int32)],
        scratch_shapes=[pltpu.VMEM((8, LANES), F32)],
        compiler_params=_params("arbitrary"),
        name="moe_route",
    )(logits, rbias)


def _dispatch_kernel(pos_ref, pad_start_ref, pad_len_ref, n_valid_ref, x_ref, xs_ref, zero_s, sem, zero_sem):
    i = pl.program_id(0)
    rows = x_ref.shape[0]
    n_blocks = xs_ref.shape[0] // MOE_BLOCK_ROWS

    @pl.when(i == 0)
    def _():
        zero_s[...] = jnp.zeros(zero_s.shape, zero_s.dtype)
        sizes = [MOE_BLOCK_ROWS >> s for s in range(MOE_BLOCK_ROWS.bit_length() - 3)]
        pieces = [(e, size) for e in range(N_EXPERTS) for size in sizes]

        def piece_copy(e, size):
            start = pad_start_ref[e] + (pad_len_ref[e] & ~(2 * size - 1))
            dst = xs_ref.at[pl.ds(pl.multiple_of(start, 8), size)]
            return pltpu.make_async_copy(zero_s.at[pl.ds(0, size)], dst, zero_sem)

        def block_copy(j):
            return pltpu.make_async_copy(zero_s, xs_ref.at[pl.ds(j * MOE_BLOCK_ROWS, MOE_BLOCK_ROWS)], zero_sem)

        for action in ("start", "wait"):
            for e, size in pieces:
                @pl.when((pad_len_ref[e] & size) != 0)
                def _():
                    getattr(piece_copy(e, size), action)()
            for j in range(n_blocks - N_EXPERTS, n_blocks):
                @pl.when(j >= n_valid_ref[0])
                def _():
                    getattr(block_copy(j), action)()

    def issue(r, _):
        for k in range(MOE_TOPK):
            p = pos_ref[MOE_TOPK * (i * rows + r) + k]
            pltpu.make_async_copy(x_ref.at[pl.ds(r, 1)], xs_ref.at[pl.ds(p, 1)], sem).start()
        return 0

    lax.fori_loop(0, rows, issue, 0, unroll=8)
    for _k in range(MOE_TOPK):
        pltpu.make_async_copy(x_ref, xs_ref.at[pl.ds(0, rows)], sem).wait()


def _dispatch(pos, pad_start, pad_len, n_valid, x, n_rows, tm):
    t, d = x.shape
    grid_spec = pltpu.PrefetchScalarGridSpec(
        num_scalar_prefetch=4,
        grid=(t // tm,),
        in_specs=[pl.BlockSpec((tm, d), lambda i, *_: (i, 0))],
        out_specs=pl.BlockSpec(memory_space=pl.ANY),
        scratch_shapes=[pltpu.VMEM((MOE_BLOCK_ROWS, d), x.dtype), pltpu.SemaphoreType.DMA(()),
                        pltpu.SemaphoreType.DMA(())],
    )
    return pl.pallas_call(
        _dispatch_kernel,
        grid_spec=grid_spec,
        out_shape=jax.ShapeDtypeStruct((n_rows, d), x.dtype),
        compiler_params=_params("arbitrary"),
        name="moe_dispatch",
    )(pos, pad_start, pad_len, n_valid, x)


def _expert_kernel(blk_expert_ref, n_valid_ref, x_ref, wg_ref, wu_ref, wd_ref, y_ref, wg_s, wu_s, wd_s):
    i = pl.program_id(0)

    @pl.when((i == 0) | (blk_expert_ref[i] != blk_expert_ref[jnp.maximum(i - 1, 0)]))
    def _():
        wg_s[...] = wg_ref[...].astype(BF16)
        wu_s[...] = wu_ref[...].astype(BF16)
        wd_s[...] = wd_ref[...].astype(BF16)

    @pl.when(i < n_valid_ref[0])
    def _():
        x = x_ref[...].astype(BF16)
        hid = _silu(jnp.dot(x, wg_s[...], preferred_element_type=F32)) * jnp.dot(
            x, wu_s[...], preferred_element_type=F32)
        y_ref[...] = _dot(hid, wd_s[...])

    @pl.when(i >= n_valid_ref[0])
    def _():
        y_ref[...] = jnp.zeros(y_ref.shape, y_ref.dtype)


def _experts(blk_expert, n_valid, xs, wg, wu, wd, layer, n_blocks):
    d = xs.shape[1]
    f = wg.shape[-1]
    rows = MOE_BLOCK_ROWS
    n_rows = n_blocks * rows
    grid_spec = pltpu.PrefetchScalarGridSpec(
        num_scalar_prefetch=2,
        grid=(n_blocks,),
        in_specs=[pl.BlockSpec((rows, d), lambda i, be, nv: (jnp.minimum(i, nv[0] - 1), 0)),
                  pl.BlockSpec((None, None, d, f), lambda i, be, nv: (layer, be[i], 0, 0)),
                  pl.BlockSpec((None, None, d, f), lambda i, be, nv: (layer, be[i], 0, 0)),
                  pl.BlockSpec((None, None, f, d), lambda i, be, nv: (layer, be[i], 0, 0))],
        out_specs=pl.BlockSpec((rows, d), lambda i, be, nv: (i, 0)),
        scratch_shapes=[pltpu.VMEM((d, f), BF16), pltpu.VMEM((d, f), BF16), pltpu.VMEM((f, d), BF16)],
    )
    return pl.pallas_call(
        _expert_kernel,
        grid_spec=grid_spec,
        out_shape=jax.ShapeDtypeStruct((n_rows, d), F32),
        compiler_params=_params("arbitrary"),
        name="moe_experts",
    )(blk_expert, n_valid, xs, wg, wu, wd)


def _combine_kernel(pos_ref, ys_ref, w_ref, h_ref, gain_ref, bias_ref, y_ref, yb_ref, buf, sem):
    i = pl.program_id(0)
    n = pl.num_programs(0)
    rows = h_ref.shape[0]

    def issue(tile, slot):
        def body(r, _):
            for k in range(MOE_TOPK):
                p = pos_ref[MOE_TOPK * (tile * rows + r) + k]
                pltpu.make_async_copy(ys_ref.at[pl.ds(p, 1)], buf.at[slot, k, pl.ds(r, 1)], sem.at[slot]).start()
            return 0

        lax.fori_loop(0, rows, body, 0, unroll=8)

    slot = i % 2

    @pl.when(i == 0)
    def _():
        issue(0, 0)

    @pl.when(i + 1 < n)
    def _():
        issue(i + 1, 1 - slot)

    for k in range(MOE_TOPK):
        pltpu.make_async_copy(ys_ref.at[pl.ds(0, rows)], buf.at[slot, k], sem.at[slot]).wait()
    w = w_ref[...]
    ffn = w[:, 0:1] * buf[slot, 0] + w[:, 1:2] * buf[slot, 1]
    y = _layer_norm(DEEPNORM_ALPHA * h_ref[...] + ffn, gain_ref[...], bias_ref[...])
    y_ref[...] = y
    yb_ref[...] = y.astype(BF16)


def _combine_ln(pos, ys, w12, h, gain, bias, tm):
    t, d = h.shape
    grid_spec = pltpu.PrefetchScalarGridSpec(
        num_scalar_prefetch=1,
        grid=(t // tm,),
        in_specs=[pl.BlockSpec(memory_space=pl.ANY),
                  pl.BlockSpec((tm, LANES), lambda i, pos: (i, 0)),
                  pl.BlockSpec((tm, d), lambda i, pos: (i, 0)),
                  pl.BlockSpec((1, d), lambda i, pos: (0, 0)),
                  pl.BlockSpec((1, d), lambda i, pos: (0, 0))],
        out_specs=[pl.BlockSpec((tm, d), lambda i, pos: (i, 0)), pl.BlockSpec((tm, d), lambda i, pos: (i, 0))],
        scratch_shapes=[pltpu.VMEM((2, MOE_TOPK, tm, d), F32), pltpu.SemaphoreType.DMA((2,))],
    )
    return pl.pallas_call(
        _combine_kernel,
        grid_spec=grid_spec,
        out_shape=[jax.ShapeDtypeStruct((t, d), F32), jax.ShapeDtypeStruct((t, d), BF16)],
        compiler_params=_params("arbitrary"),
        name="moe_combine_ln",
    )(pos, ys, w12, h, gain, bias)


def _moe_ln(h, logits, rbias, wg, wu, wd, layer, gain, bias):
    t, d = h.shape
    rows = MOE_BLOCK_ROWS
    n_blocks = (MOE_TOPK * t) // rows + N_EXPERTS
    w12, e12, r12, counts = _route_tokens(logits, rbias, 512)
    counts = counts[0, :N_EXPERTS]
    blocks_per_expert = (counts + rows - 1) // rows
    block_end = jnp.cumsum(blocks_per_expert)
    row_start = (block_end - blocks_per_expert) * rows
    pos = (row_start[e12[:, :MOE_TOPK]] + r12[:, :MOE_TOPK]).reshape(-1).astype(jnp.int32)
    n_valid = block_end[-1:].astype(jnp.int32)
    blk = jnp.arange(n_blocks, dtype=jnp.int32)
    blk_expert = jnp.sum(jnp.minimum(blk, n_valid - 1)[:, None] >= block_end[None, :], axis=1).astype(jnp.int32)
    pad_start = ((row_start + counts) // 8 * 8).astype(jnp.int32)
    pad_len = (row_start + blocks_per_expert * rows - pad_start).astype(jnp.int32)
    xs = _dispatch(pos, pad_start, pad_len, n_valid, h, n_blocks * rows, 512)
    ys = _experts(blk_expert, n_valid, xs, wg, wu, wd, layer, n_blocks)
    return _combine_ln(pos, ys, w12, h, gain, bias, 256)


def _rope_tables(seq):
    inv_freq = ROPE_THETA ** (-jnp.arange(0, HEAD_DIM, 2, dtype=F32) / HEAD_DIM)
    ang = jnp.arange(seq, dtype=F32)[:, None] * inv_freq[None, :]
    cos, sin = jnp.cos(ang), jnp.sin(ang)
    return jnp.concatenate([cos, cos], axis=-1), jnp.concatenate([-sin, sin], axis=-1)


def _even_mixer(hb, b, s, w_in, conv_w, a_log, dt_bias, gdn_norm, hgrn_norm, lower_bound):
    gw = GDN_WIDTH
    n_small = 2 * GDN_HEADS
    tail0 = 4 * gw
    w_main = jnp.concatenate([w_in[:, :tail0], w_in[:, tail0 + n_small:]], axis=1).astype(BF16)
    w_small = jnp.pad(w_in[:, tail0:tail0 + n_small], ((0, 0), (0, LANES - n_small))).astype(BF16)
    h3 = _matmul(hb, w_main, 1024, 1024).reshape(b, s, EVEN_MAIN)
    n_chunks = s // GDN_CHUNK
    small = _matmul(hb, w_small, 1024, LANES).reshape(b, s, LANES)
    to_rows = lambda a: a.transpose(0, 2, 1).reshape(b, GDN_HEADS, n_chunks, GDN_CHUNK)
    b_rows, a_rows = to_rows(small[..., :GDN_HEADS]), to_rows(small[..., GDN_HEADS:n_small])
    headvec = lambda v: jnp.broadcast_to(v.astype(F32)[:, None, None], (GDN_HEADS, 1, GDN_CHUNK))
    o_a = _gdn(h3, conv_w.astype(F32), a_rows, b_rows, headvec(a_log), headvec(dt_bias),
               gdn_norm.astype(F32).reshape(1, HEAD_DIM))
    o_b = _hgrn(h3, lower_bound.astype(F32).reshape(HGRN_HEADS, 1, HEAD_DIM),
                hgrn_norm.astype(F32).reshape(1, HEAD_DIM))
    return [o_a.reshape(b * s, GDN_WIDTH), o_b.reshape(b * s, HGRN_WIDTH)]


def _odd_mixer(hb, b, s, w_in, cos2, sin2):
    h = _matmul(hb, w_in.astype(BF16), 1024, 768)
    h3 = h.reshape(b, s, ODD_COLS)
    o_c = _dilated(h3, cos2, sin2)
    o_d = _moba(h3, cos2, sin2)
    return [o_c.reshape(b * s, -1), o_d.reshape(b * s, -1)]


def kernel(x, ev_w_in, ev_conv_w, ev_a_log, ev_dt_bias, ev_gdn_norm, ev_hgrn_norm, hgrn_lb_logits, ev_w_out,
           od_w_in, od_w_out, router_w, router_bias, moe_w_gate, moe_w_up, moe_w_down, ln_gain, ln_bias):
    b, s, d = x.shape
    t = b * s
    cos2, sin2 = _rope_tables(s)
    lower_bounds = jnp.cumsum(jax.nn.softmax(hgrn_lb_logits.astype(F32), axis=0), axis=0)
    rw = jnp.pad(router_w.astype(F32), ((0, 0), (0, LANES - N_EXPERTS)))
    rbias = jnp.pad(router_bias.astype(F32), (0, LANES - N_EXPERTS)).reshape(1, LANES)
    vec = lambda v: v.astype(F32).reshape(1, d)

    h = x.reshape(t, d)
    hb = h.astype(BF16)
    for layer in range(DEPTH):
        if layer % 2 == 0:
            e = layer // 2
            parts = _even_mixer(hb, b, s, ev_w_in[e], ev_conv_w[e], ev_a_log[e], ev_dt_bias[e], ev_gdn_norm[e],
                                ev_hgrn_norm[e], lower_bounds[layer])
            w_out = ev_w_out[e].astype(BF16)
        else:
            o = layer // 2
            parts = _odd_mixer(hb, b, s, od_w_in[o], cos2, sin2)
            w_out = od_w_out[o].astype(BF16)
        splits = np.cumsum([p.shape[1] for p in parts])[:-1]
        weights = jnp.split(w_out, splits, axis=0)
        h, hb, logits = _out_ln(parts, weights, h, vec(ln_gain[layer, 0]), vec(ln_bias[layer, 0]), rw, 256)
        h, hb = _moe_ln(h, logits, rbias, moe_w_gate, moe_w_up, moe_w_down, layer,
                        vec(ln_gain[layer, 1]), vec(ln_bias[layer, 1]))
    return h.reshape(b, s, d)
```

```python
import functools
import math

import jax
import jax.numpy as jnp
import numpy as np
from jax import lax
from jax.experimental import pallas as pl
from jax.experimental.pallas import tpu as pltpu

F32 = jnp.float32
BF16 = jnp.bfloat16

D_MODEL = 2048
DEPTH = 2
HEAD_DIM = 128
GDN_HEADS = 8
GDN_CONV = 4
GDN_CHUNK = 64
GDN_WIDTH = GDN_HEADS * HEAD_DIM
HGRN_HEADS = 8
HGRN_CHUNK = 16
HGRN_WIDTH = HGRN_HEADS * HEAD_DIM
DIL_GROUPS = ((128, 1), (512, 4), (2048, 16))
DIL_HEADS_PER_GROUP = 4
DIL_HEADS = len(DIL_GROUPS) * DIL_HEADS_PER_GROUP
MOBA_HEADS = 4
MOBA_BLOCK = 256
MOBA_TOPK = 3
ROPE_THETA = 10000.0
N_EXPERTS = 16
N_EXPERT_GROUPS = 4
EXPERTS_PER_GROUP = N_EXPERTS // N_EXPERT_GROUPS
D_EXPERT = 512
MOE_TOPK = 2
MOE_BLOCK_ROWS = 512
DEEPNORM_ALPHA = (2.0 * DEPTH) ** 0.25
LN_EPS = 1e-5
RMS_EPS = 1e-6
NEG_INF = -1e30

LANES = 128
VMEM_LIMIT = 56 * 1024 * 1024
ATT_BLOCK = 256
MOBA_Q_GROUPS = ((7, 0, 6, 1), (5, 2, 4, 3))
DIL_BLOCK = 128
GDN_GROUP = 16
HGRN_GROUP = 4
HGRN_ROWS = 256

EVEN_MAIN = 3 * GDN_WIDTH + GDN_WIDTH + 4 * HGRN_WIDTH
ODD_COLS = 3 * DIL_HEADS * HEAD_DIM + 3 * MOBA_HEADS * HEAD_DIM


def _dot(a, b):
    return jnp.dot(a.astype(BF16), b.astype(BF16), preferred_element_type=F32)


def _dot_nt(a, b):
    return lax.dot_general(a.astype(BF16), b.astype(BF16), (((1,), (1,)), ((), ())),
                           preferred_element_type=F32)


def _dot_tn(a, b):
    return lax.dot_general(a.astype(BF16), b.astype(BF16), (((0,), (0,)), ((), ())),
                           preferred_element_type=F32)


def _dot_hi(a, b):
    return jnp.dot(a, b, preferred_element_type=F32, precision=lax.Precision.HIGHEST)


def _dot_nt_hi(a, b):
    return lax.dot_general(a, b, (((1,), (1,)), ((), ())), preferred_element_type=F32,
                           precision=lax.Precision.HIGHEST)


def _dot3(a, b):
    a_hi = a.astype(BF16)
    b_hi = b.astype(BF16)
    a_lo = (a - a_hi.astype(F32)).astype(BF16)
    b_lo = (b - b_hi.astype(F32)).astype(BF16)
    dot = functools.partial(jnp.dot, preferred_element_type=F32)
    return dot(a_hi, b_hi) + (dot(a_hi, b_lo) + dot(a_lo, b_hi))


def _dot_sel(sel, x):
    dot = functools.partial(jnp.dot, preferred_element_type=F32)
    x_hi = x.astype(BF16)
    r1 = x - x_hi.astype(F32)
    x_mid = r1.astype(BF16)
    x_lo = (r1 - x_mid.astype(F32)).astype(BF16)
    return dot(sel, x_hi) + (dot(sel, x_mid) + dot(sel, x_lo))


_dot_inv = _dot


def _silu(x):
    return x * jax.nn.sigmoid(x)


def _iota(shape, dim):
    return lax.broadcasted_iota(jnp.int32, shape, dim)


def _params(*sem):
    return pltpu.CompilerParams(dimension_semantics=sem, vmem_limit_bytes=VMEM_LIMIT)


def _mm_kernel(x_ref, w_ref, o_ref):
    o_ref[...] = jnp.dot(x_ref[...].astype(BF16), w_ref[...], preferred_element_type=F32).astype(o_ref.dtype)


def _matmul(x, w, tm, tn):
    m, k = x.shape
    n = w.shape[1]
    assert m % tm == 0 and n % tn == 0
    return pl.pallas_call(
        _mm_kernel,
        grid=(m // tm, n // tn),
        in_specs=[pl.BlockSpec((tm, k), lambda i, j: (i, 0)),
                  pl.BlockSpec((k, tn), lambda i, j: (0, j))],
        out_specs=pl.BlockSpec((tm, tn), lambda i, j: (i, j)),
        out_shape=jax.ShapeDtypeStruct((m, n), F32),
        compiler_params=_params("parallel", "parallel"),
        name="in_proj",
    )(x, w)


def _gdn_kernel(q_ref, k_ref, v_ref, z_ref, cwq_ref, cwk_ref, cwv_ref, a_ref, b_ref, alog_ref, dt_ref,
                gn_ref, o_ref, pad_s, q_s, k_s, v_s, gcum_s, beta_s, qe_s, ob_s, sm_s, sa_s):
    seq = q_ref.shape[0]
    c = GDN_CHUNK
    n_chunks = seq // c
    rows = 256

    pad_s[pl.ds(0, 8), :] = jnp.zeros((8, HEAD_DIM), F32)
    for x_ref, cw_ref, dst, mode in ((q_ref, cwq_ref, q_s, "q"), (k_ref, cwk_ref, k_s, "k"),
                                     (v_ref, cwv_ref, v_s, "v")):
        pad_s[pl.ds(8, seq), :] = x_ref[...]
        cw = cw_ref[...]
        for r in range(seq // rows):
            acc = None
            for j in range(GDN_CONV):
                tap = pad_s[pl.ds(8 + r * rows - (GDN_CONV - 1) + j, rows), :] * cw[j:j + 1, :]
                acc = tap if acc is None else acc + tap
            y = _silu(acc)
            if mode != "v":
                y = y * lax.rsqrt(jnp.sum(y * y, axis=-1, keepdims=True) + RMS_EPS)
            if mode == "q":
                y = y * HEAD_DIM ** -0.5
            dst[pl.ds(r * rows, rows), :] = y

    upper = (_iota((c, c), 0) <= _iota((c, c), 1)).astype(F32)
    g = -jnp.exp(alog_ref[...]) * jax.nn.softplus(a_ref[...] + dt_ref[...])
    gcum_s[...] = _dot_hi(g, upper)
    beta_s[...] = jax.nn.sigmoid(b_ref[...])

    ri = _iota((c, c), 0)
    ci = _iota((c, c), 1)
    eye = ri == ci
    strict = ri > ci
    incl = ri >= ci
    eye_f = eye.astype(F32)
    level1 = ri // 2 == ci // 2
    levels = []
    s = 2
    while s < c:
        levels.append((ri // (2 * s) == ci // (2 * s)) & ((ri // s) % 2 == 1) & ((ci // s) % 2 == 0))
        s *= 2

    dot = functools.partial(jnp.dot, preferred_element_type=F32)

    def to_col(row):
        return jnp.sum(jnp.where(eye, jnp.broadcast_to(row, (c, c)), 0.0), axis=1, keepdims=True)

    def prepare(i, _):
        n0 = i * GDN_GROUP
        grp = range(GDN_GROUP)
        starts = [pl.multiple_of((n0 + j) * c, c) for j in grp]
        qc = [q_s[pl.ds(r0, c), :] for r0 in starts]
        kc = [k_s[pl.ds(r0, c), :] for r0 in starts]
        vc = [v_s[pl.ds(r0, c), :] for r0 in starts]
        g_row = [gcum_s[pl.ds(n0 + j, 1), :] for j in grp]
        g_col = [to_col(g_row[j]) for j in grp]
        b_col = [to_col(beta_s[pl.ds(n0 + j, 1), :]) for j in grp]
        decay = [jnp.exp(jnp.where(incl, g_col[j] - g_row[j], 0.0)) for j in grp]
        n_mat = [b_col[j] * jnp.where(strict, decay[j], 0.0) * _dot_nt(kc[j], kc[j]) for j in grp]
        inv = [eye_f - jnp.where(level1, n_mat[j], 0.0) for j in grp]
        for blk in levels:
            tmp = [_dot_inv(inv[j], jnp.where(blk, n_mat[j], 0.0)) for j in grp]
            inv = [inv[j] - _dot_inv(tmp[j], inv[j]) for j in grp]
        e_col = [jnp.exp(g_col[j]) for j in grp]
        sol = [_dot_inv(inv[j], jnp.concatenate([b_col[j] * vc[j], (b_col[j] * e_col[j]) * kc[j]], axis=1))
               for j in grp]
        qk = [(_dot_nt(qc[j], kc[j]) * jnp.where(incl, decay[j], 0.0)).astype(BF16) for j in grp]
        ub = [sol[j][:, :HEAD_DIM].astype(BF16) for j in grp]
        w = [sol[j][:, HEAD_DIM:].astype(BF16) for j in grp]
        kd = [(kc[j] * jnp.exp(g_row[j][:, c - 1:c] - g_col[j])).astype(BF16) for j in grp]
        q_eff = [(qc[j] * e_col[j] - dot(qk[j], w[j])).astype(BF16) for j in grp]
        o_base = [dot(qk[j], ub[j]) for j in grp]
        s_mat = [_dot_tn(kd[j], w[j]).astype(BF16) for j in grp]
        s_add = [_dot_tn(kd[j], ub[j]) for j in grp]
        for j, r0 in enumerate(starts):
            m0 = pl.multiple_of((n0 + j) * HEAD_DIM, HEAD_DIM)
            qe_s[pl.ds(r0, c), :] = q_eff[j]
            ob_s[pl.ds(r0, c), :] = o_base[j]
            sm_s[pl.ds(m0, HEAD_DIM), :] = s_mat[j]
            sa_s[pl.ds(m0, HEAD_DIM), :] = s_add[j]
        return 0

    lax.fori_loop(0, n_chunks // GDN_GROUP, prepare, 0)

    gn = gn_ref[...]

    def chunk(n, state):
        r0 = pl.multiple_of(n * c, c)
        m0 = pl.multiple_of(n * HEAD_DIM, HEAD_DIM)
        g_last = gcum_s[pl.ds(n, 1), :][:, c - 1:c]
        lhs = jnp.concatenate([qe_s[pl.ds(r0, c), :], sm_s[pl.ds(m0, HEAD_DIM), :]], axis=0)
        prod = dot(lhs, state.astype(BF16))
        ob_s[pl.ds(r0, c), :] = prod[:c] + ob_s[pl.ds(r0, c), :]
        return jnp.exp(g_last) * state - prod[c:] + sa_s[pl.ds(m0, HEAD_DIM), :]

    lax.fori_loop(0, n_chunks, chunk, jnp.zeros((HEAD_DIM, HEAD_DIM), F32))

    for r in range(seq // rows):
        sl = pl.ds(r * rows, rows)
        o = ob_s[sl, :]
        o = o * lax.rsqrt(jnp.mean(o * o, axis=-1, keepdims=True) + RMS_EPS) * gn
        o_ref[sl, :] = (o * _silu(z_ref[sl, :])).astype(o_ref.dtype)


def _gdn(h3, conv_w, a_rows, b_rows, alog, dt, gn):
    b, s, _ = h3.shape
    nh = GDN_HEADS
    n_chunks = s // GDN_CHUNK
    col = lambda off: pl.BlockSpec((None, s, HEAD_DIM), lambda bi, hi: (bi, 0, off + hi))
    cw = lambda off: pl.BlockSpec((GDN_CONV, HEAD_DIM), lambda bi, hi: (0, off + hi))
    rowspec = pl.BlockSpec((None, None, n_chunks, GDN_CHUNK), lambda bi, hi: (bi, hi, 0, 0))
    headvec = pl.BlockSpec((None, 1, GDN_CHUNK), lambda bi, hi: (hi, 0, 0))
    return pl.pallas_call(
        _gdn_kernel,
        grid=(b, nh),
        in_specs=[col(0), col(nh), col(2 * nh), col(3 * nh), cw(0), cw(nh), cw(2 * nh),
                  rowspec, rowspec, headvec, headvec,
                  pl.BlockSpec((1, HEAD_DIM), lambda bi, hi: (0, 0))],
        out_specs=pl.BlockSpec((None, s, HEAD_DIM), lambda bi, hi: (bi, 0, hi)),
        out_shape=jax.ShapeDtypeStruct((b, s, GDN_WIDTH), BF16),
        scratch_shapes=[pltpu.VMEM((s + 8, HEAD_DIM), F32), pltpu.VMEM((s, HEAD_DIM), F32),
                        pltpu.VMEM((s, HEAD_DIM), F32), pltpu.VMEM((s, HEAD_DIM), F32),
                        pltpu.VMEM((n_chunks, GDN_CHUNK), F32), pltpu.VMEM((n_chunks, GDN_CHUNK), F32),
                        pltpu.VMEM((s, HEAD_DIM), BF16), pltpu.VMEM((s, HEAD_DIM), F32),
                        pltpu.VMEM((n_chunks * HEAD_DIM, HEAD_DIM), BF16),
                        pltpu.VMEM((n_chunks * HEAD_DIM, HEAD_DIM), F32)],
        compiler_params=_params("parallel", "parallel"),
        name="gdn",
    )(h3, h3, h3, h3, conv_w, conv_w, conv_w, a_rows, b_rows, alog, dt, gn)


def _hgrn_kernel(q_ref, f_ref, i_ref, g_ref, lb_ref, hn_ref, o_ref):
    seq = q_ref.shape[0]
    c = HGRN_CHUNK
    rows = HGRN_ROWS
    ri = _iota((rows, rows), 0)
    ci = _iota((rows, rows), 1)
    causal = (ri // c == ci // c) & (ci <= ri)
    row_in_chunk = _iota((rows, HEAD_DIM), 0) % c
    lb = lb_ref[...]
    hn = hn_ref[...]

    chunks = [slice(j * c, (j + 1) * c) for j in range(rows // c)]
    grp = range(HGRN_GROUP)

    def chunk_scan(x, suffix):
        step = 1
        while step < c:
            if suffix:
                x = x + jnp.where(row_in_chunk < c - step, pltpu.roll(x, rows - step, axis=0), 0.0)
            else:
                x = x + jnp.where(row_in_chunk >= step, pltpu.roll(x, step, axis=0), 0.0)
            step *= 2
        return x

    def group(n, state_t):
        starts = [pl.multiple_of((n * HGRN_GROUP + j) * rows, rows) for j in grp]
        qc = [q_ref[pl.ds(r0, rows), :] for r0 in starts]
        ic = [i_ref[pl.ds(r0, rows), :].astype(BF16) for r0 in starts]
        f = [lb + (1.0 - lb) * jax.nn.sigmoid(f_ref[pl.ds(r0, rows), :]) for r0 in starts]
        log_f = [jnp.log(f[j]) for j in grp]
        bcum = [chunk_scan(log_f[j], False) for j in grp]
        to_end = [chunk_scan(log_f[j], True) - log_f[j] for j in grp]
        chunk_dec = [jnp.exp(bcum[j]) for j in grp]
        q_dec = [(qc[j] * chunk_dec[j]).astype(BF16) for j in grp]
        k_inv = [(1.0 - f[j]) * jnp.exp(-bcum[j]) for j in grp]
        k_dec = [((1.0 - f[j]) * jnp.exp(to_end[j])).astype(BF16) for j in grp]
        p = [jnp.where(causal, _dot_nt(q_dec[j], k_inv[j]), 0.0) for j in grp]
        o_intra = [_dot(p[j], ic[j]) for j in grp]
        updates = [[_dot_tn(ic[j][sl], k_dec[j][sl]) for sl in chunks] for j in grp]
        for j, r0 in enumerate(starts):
            outs = []
            for sl, upd in zip(chunks, updates[j]):
                outs.append(o_intra[j][sl] + _dot_nt(q_dec[j][sl], state_t))
                state_t = state_t * chunk_dec[j][sl.stop - 1:sl.stop] + upd
            o = jnp.concatenate(outs, axis=0)
            o = o * lax.rsqrt(jnp.mean(o * o, axis=-1, keepdims=True) + RMS_EPS) * hn
            o_ref[pl.ds(r0, rows), :] = (o * _silu(g_ref[pl.ds(r0, rows), :])).astype(o_ref.dtype)
        return state_t

    lax.fori_loop(0, seq // (rows * HGRN_GROUP), group, jnp.zeros((HEAD_DIM, HEAD_DIM), F32))


def _hgrn(h3, lb, hn):
    b, s, _ = h3.shape
    nh = HGRN_HEADS
    base = 4 * GDN_HEADS
    col = lambda off: pl.BlockSpec((None, s, HEAD_DIM), lambda bi, hi: (bi, 0, base + off + hi))
    return pl.pallas_call(
        _hgrn_kernel,
        grid=(b, nh),
        in_specs=[col(0), col(nh), col(2 * nh), col(3 * nh),
                  pl.BlockSpec((None, 1, HEAD_DIM), lambda bi, hi: (hi, 0, 0)),
                  pl.BlockSpec((1, HEAD_DIM), lambda bi, hi: (0, 0))],
        out_specs=pl.BlockSpec((None, s, HEAD_DIM), lambda bi, hi: (bi, 0, hi)),
        out_shape=jax.ShapeDtypeStruct((b, s, HGRN_WIDTH), BF16),
        compiler_params=_params("parallel", "parallel"),
        name="hgrn2",
    )(h3, h3, h3, h3, lb, hn)


def _rope(x, cos2, sin2):
    return x * cos2 + pltpu.roll(x, HEAD_DIM // 2, axis=1) * sin2


def _flash_step(q_blk, k_blk, v_blk, mask, carry):
    m, l, acc = carry
    s = jnp.where(mask, _dot_nt(q_blk, k_blk) * HEAD_DIM ** -0.5, NEG_INF)
    m_new = jnp.maximum(m, jnp.max(s, axis=-1, keepdims=True))
    alpha = jnp.exp(m - m_new)
    p = jnp.exp(s - m_new)
    l = alpha * l + jnp.sum(p, axis=-1, keepdims=True)
    acc = alpha * acc + _dot(p, v_blk)
    return m_new, l, acc


def _flash_init():
    blk = ATT_BLOCK
    return (jnp.full((blk, 1), NEG_INF, F32), jnp.zeros((blk, 1), F32), jnp.zeros((blk, HEAD_DIM), F32))


def _dilated_kernel(*refs):
    n_g = len(DIL_GROUPS)
    q_refs, k_refs, v_refs = refs[0:n_g], refs[n_g:2 * n_g], refs[2 * n_g:3 * n_g]
    cos_ref, sin_ref, o_ref = refs[3 * n_g:3 * n_g + 3]
    q_s, k_s, v_s, og_s, lse_s = refs[3 * n_g + 3:]
    seq = o_ref.shape[0]
    blk = DIL_BLOCK
    piece = 256
    grp = range(n_g)

    for gi, (window, d) in enumerate(DIL_GROUPS):
        assert window // d == blk
        seg = seq // d
        k_s[gi, pl.ds(0, blk), :] = jnp.zeros((blk, HEAD_DIM), BF16)
        v_s[gi, pl.ds(0, blk), :] = jnp.zeros((blk, HEAD_DIM), BF16)
        for r in range(d):
            for c0 in range(0, seg, piece):
                n = min(piece, seg)
                rows = pl.ds(r + c0 * d, n, stride=d) if d > 1 else pl.ds(c0, n)
                cos2 = cos_ref[rows, :]
                sin2 = sin_ref[rows, :]
                q_s[gi, pl.ds(r * seg + c0, n), :] = _rope(q_refs[gi][rows, :], cos2, sin2).astype(BF16)
                k_s[gi, pl.ds(blk + r * seg + c0, n), :] = _rope(k_refs[gi][rows, :], cos2, sin2).astype(BF16)
                v_s[gi, pl.ds(blk + r * seg + c0, n), :] = v_refs[gi][rows, :].astype(BF16)

    ri = _iota((blk, 2 * blk), 0)
    ci = _iota((blk, 2 * blk), 1)
    rel = ri + blk - ci
    in_window = (rel >= 0) & (rel <= blk)
    dot = functools.partial(jnp.dot, preferred_element_type=F32)

    def q_block(m, _):
        j0 = pl.multiple_of(m * blk, blk)
        segs = [seq // d for _, d in DIL_GROUPS]
        has_prev = [jnp.where(j0 % seg != 0, blk, 0) for seg in segs]
        mask = [in_window & (ci + has_prev[g] >= blk) for g in grp]
        q = [q_s[g, pl.ds(j0, blk), :] for g in grp]
        kw = [k_s[g, pl.ds(j0, 2 * blk), :] for g in grp]
        vw = [v_s[g, pl.ds(j0, 2 * blk), :] for g in grp]
        s = [jnp.where(mask[g], _dot_nt(q[g], kw[g]) * HEAD_DIM ** -0.5, NEG_INF) for g in grp]
        top = [jnp.max(s[g], axis=-1, keepdims=True) for g in grp]
        p = [jnp.exp(s[g] - top[g]) for g in grp]
        den = [jnp.sum(p[g], axis=-1, keepdims=True) for g in grp]
        o = [dot(p[g].astype(BF16), vw[g]) / den[g] for g in grp]
        lse = [top[g] + jnp.log(den[g]) for g in grp]
        for g, (_, d) in enumerate(DIL_GROUPS):
            seg = segs[g]
            dst = pl.ds((j0 % seg) * d + j0 // seg, blk, stride=d) if d > 1 else pl.ds(j0, blk)
            og_s[g, dst, :] = o[g]
            lse_s[g, dst, :] = jnp.broadcast_to(lse[g], (blk, HEAD_DIM))
        return 0

    lax.fori_loop(0, seq // blk, q_block, 0)

    for c0 in range(0, seq, piece):
        rows = pl.ds(c0, piece)
        lses = [lse_s[g, rows, :] for g in grp]
        top = functools.reduce(jnp.maximum, lses)
        wts = [jnp.exp(x - top) for x in lses]
        den = functools.reduce(lambda a, b: a + b, wts)
        o = functools.reduce(lambda a, b: a + b, [wts[g] * og_s[g, rows, :] for g in grp]) / den
        o_ref[rows, :] = o.astype(o_ref.dtype)


def _dilated(h3, cos2, sin2):
    b, s, _ = h3.shape
    hpg = DIL_HEADS_PER_GROUP
    n_g = len(DIL_GROUPS)
    col = lambda off: pl.BlockSpec((None, s, HEAD_DIM), lambda bi, hi: (bi, 0, off + hi))
    tab = pl.BlockSpec((s, HEAD_DIM), lambda bi, hi: (0, 0))
    specs = [col(part * DIL_HEADS + gi * hpg) for part in range(3) for gi in range(n_g)]
    return pl.pallas_call(
        _dilated_kernel,
        grid=(b, hpg),
        in_specs=specs + [tab, tab],
        out_specs=pl.BlockSpec((None, s, HEAD_DIM), lambda bi, hi: (bi, 0, hi)),
        out_shape=jax.ShapeDtypeStruct((b, s, hpg * HEAD_DIM), BF16),
        scratch_shapes=[pltpu.VMEM((n_g, s, HEAD_DIM), BF16), pltpu.VMEM((n_g, s + DIL_BLOCK, HEAD_DIM), BF16),
                        pltpu.VMEM((n_g, s + DIL_BLOCK, HEAD_DIM), BF16), pltpu.VMEM((n_g, s, HEAD_DIM), F32),
                        pltpu.VMEM((n_g, s, HEAD_DIM), F32)],
        compiler_params=_params("parallel", "parallel"),
        name="dilated_attention",
    )(*([h3] * (3 * n_g)), cos2, sin2)


def _moba_kernel(q_ref, k_ref, v_ref, cos_ref, sin_ref, o_ref, qf_s, q_s, k_s, v_s, km_s, sel_s):
    seq = o_ref.shape[0]
    blk = MOBA_BLOCK
    n_blk = seq // blk
    cos2 = cos_ref[...]
    sin2 = sin_ref[...]
    q = _rope(q_ref[...], cos2, sin2)
    qf_s[...] = q
    q_s[...] = q.astype(BF16)
    km_s[...] = jnp.zeros(km_s.shape, F32)
    for nb in range(n_blk):
        kb = _rope(k_ref[pl.ds(nb * blk, blk), :], cos2[nb * blk:(nb + 1) * blk], sin2[nb * blk:(nb + 1) * blk])
        k_s[pl.ds(nb * blk, blk), :] = kb.astype(BF16)
        km_s[pl.ds(nb, 1), :] = jnp.mean(kb, axis=0, keepdims=True)
    v_s[...] = v_ref[...].astype(BF16)

    lane = _iota((blk, LANES), 1)
    causal = _iota((blk, blk), 0) >= _iota((blk, blk), 1)
    all_true = _iota((blk, blk), 0) >= 0
    rows = lambda nb: pl.ds(nb * blk, blk)

    past = range(1, n_blk)
    km = km_s[...]
    gate = {qb: jnp.where(lane < qb, _dot_nt_hi(qf_s[rows(qb), :], km), -jnp.inf) for qb in past}
    sel = {qb: jnp.zeros((blk, LANES), F32) for qb in past}
    for _k in range(MOBA_TOPK):
        best = {qb: jnp.max(gate[qb], axis=-1, keepdims=True) for qb in past}
        first = {qb: jnp.min(jnp.where(gate[qb] == best[qb], lane, LANES), axis=-1, keepdims=True) for qb in past}
        pick = {qb: (lane == first[qb]) & (best[qb] > -jnp.inf) for qb in past}
        sel = {qb: jnp.where(pick[qb], 1.0, sel[qb]) for qb in past}
        gate = {qb: jnp.where(pick[qb], -jnp.inf, gate[qb]) for qb in past}
    for qb in past:
        sel_s[rows(qb), :] = sel[qb]

    assert sorted(qb for group in MOBA_Q_GROUPS for qb in group) == list(range(n_blk))
    for group in MOBA_Q_GROUPS:
        carry = {qb: _flash_init() for qb in group}
        for j in range(max(group) + 1):
            for qb in group:
                if j < qb:
                    mask = (sel_s[rows(qb), :][:, j:j + 1] > 0.0) & all_true
                elif j == qb:
                    mask = causal
                else:
                    continue
                carry[qb] = _flash_step(q_s[rows(qb), :], k_s[rows(j), :], v_s[rows(j), :], mask, carry[qb])
        for qb in group:
            m, l, acc = carry[qb]
            o_ref[rows(qb), :] = (acc / l).astype(o_ref.dtype)


def _moba(h3, cos2, sin2):
    b, s, _ = h3.shape
    base = 3 * DIL_HEADS
    col = lambda off: pl.BlockSpec((None, s, HEAD_DIM), lambda bi, hi: (bi, 0, base + off + hi))
    tab = pl.BlockSpec((s, HEAD_DIM), lambda bi, hi: (0, 0))
    return pl.pallas_call(
        _moba_kernel,
        grid=(b, MOBA_HEADS),
        in_specs=[col(0), col(MOBA_HEADS), col(2 * MOBA_HEADS), tab, tab],
        out_specs=pl.BlockSpec((None, s, HEAD_DIM), lambda bi, hi: (bi, 0, hi)),
        out_shape=jax.ShapeDtypeStruct((b, s, MOBA_HEADS * HEAD_DIM), BF16),
        scratch_shapes=[pltpu.VMEM((s, HEAD_DIM), F32), pltpu.VMEM((s, HEAD_DIM), BF16),
                        pltpu.VMEM((s, HEAD_DIM), BF16), pltpu.VMEM((s, HEAD_DIM), BF16),
                        pltpu.VMEM((LANES, HEAD_DIM), F32), pltpu.VMEM((s, LANES), F32)],
        compiler_params=_params("parallel", "parallel"),
        name="moba_attention",
    )(h3, h3, h3, cos2, sin2)


def _layer_norm(x, gain, bias):
    mu = jnp.mean(x, axis=-1, keepdims=True)
    xc = x - mu
    var = jnp.mean(xc * xc, axis=-1, keepdims=True)
    return xc * lax.rsqrt(var + LN_EPS) * gain + bias


def _out_ln_kernel(*refs, n_parts):
    o_refs = refs[0:n_parts]
    w_refs = refs[n_parts:2 * n_parts]
    h_ref, gain_ref, bias_ref, rw_ref, y_ref, yb_ref, logit_ref = refs[2 * n_parts:]
    mix = None
    for o_r, w_r in zip(o_refs, w_refs):
        part = jnp.dot(o_r[...], w_r[...], preferred_element_type=F32)
        mix = part if mix is None else mix + part
    y = _layer_norm(DEEPNORM_ALPHA * h_ref[...] + mix, gain_ref[...], bias_ref[...])
    y_ref[...] = y
    yb_ref[...] = y.astype(BF16)
    logit_ref[...] = _dot3(y, rw_ref[...])


def _out_ln(parts, weights, h, gain, bias, router_w, tm):
    t, d = h.shape
    n_parts = len(parts)
    row = lambda width: pl.BlockSpec((tm, width), lambda i: (i, 0))
    full = lambda a: pl.BlockSpec(a.shape, lambda i: (0, 0))
    return pl.pallas_call(
        functools.partial(_out_ln_kernel, n_parts=n_parts),
        grid=(t // tm,),
        in_specs=[row(p.shape[1]) for p in parts] + [full(w) for w in weights]
                 + [row(d), full(gain), full(bias), full(router_w)],
        out_specs=[row(d), row(d), row(LANES)],
        out_shape=[jax.ShapeDtypeStruct((t, d), F32), jax.ShapeDtypeStruct((t, d), BF16),
                   jax.ShapeDtypeStruct((t, LANES), F32)],
        compiler_params=_params("parallel"),
        name="out_proj_ln",
    )(*parts, *weights, h, gain, bias, router_w)


def _route(logits_t, rbias_col):
    row = _iota(logits_t.shape, 0)
    scores = jax.nn.sigmoid(logits_t)
    biased = scores + rbias_col

    def first_argmax(vals):
        best = jnp.max(vals, axis=0, keepdims=True)
        return best, jnp.min(jnp.where(vals == best, row, N_EXPERTS), axis=0, keepdims=True)

    best_score = None
    best_group = None
    for g in range(N_EXPERT_GROUPS):
        vals = jnp.where(row // EXPERTS_PER_GROUP == g, biased, -jnp.inf)
        top1, idx1 = first_argmax(vals)
        top2, _ = first_argmax(jnp.where(row == idx1, -jnp.inf, vals))
        score = top1 + top2
        if g == 0:
            best_score, best_group = score, jnp.zeros_like(idx1)
        else:
            better = score > best_score
            best_group = jnp.where(better, g, best_group)
            best_score = jnp.where(better, score, best_score)
    masked = jnp.where(row // EXPERTS_PER_GROUP == best_group, biased, NEG_INF)
    _, i1 = first_argmax(masked)
    _, i2 = first_argmax(jnp.where(row == i1, -jnp.inf, masked))
    s1 = jnp.sum(jnp.where(row == i1, scores, 0.0), axis=0, keepdims=True)
    s2 = jnp.sum(jnp.where(row == i2, scores, 0.0), axis=0, keepdims=True)
    tot = s1 + s2
    return i1, i2, s1 / tot, s2 / tot


def _route_kernel(logit_ref, rbias_ref, pos_ref, w_ref, cnt_ref, carry_s, total_s):
    sweep = pl.program_id(0)
    i = pl.program_id(1)
    tm = logit_ref.shape[0]
    block = float(MOE_BLOCK_ROWS)

    @pl.when((i == 0) & (sweep == 1))
    def _():
        total_s[...] = carry_s[...]

    @pl.when(i == 0)
    def _():
        carry_s[...] = jnp.zeros(carry_s.shape, F32)

    logits_t = logit_ref[...].T[:N_EXPERTS]
    i1, i2, w1, w2 = _route(logits_t, rbias_ref[...][:N_EXPERTS])
    row = _iota((N_EXPERTS, tm), 0)
    chosen = (row == i1) | (row == i2)
    earlier = (_iota((tm, tm), 0) < _iota((tm, tm), 1)).astype(BF16)
    before = jnp.dot(jnp.where(chosen, 1.0, 0.0).astype(BF16), earlier, preferred_element_type=F32) + carry_s[...]
    carry_s[...] = carry_s[...] + jnp.sum(jnp.where(chosen, 1.0, 0.0), axis=1, keepdims=True)

    @pl.when(sweep == 0)
    def _():
        pos_ref[...] = jnp.zeros(pos_ref.shape, pos_ref.dtype)
        w_ref[...] = jnp.zeros(w_ref.shape, w_ref.dtype)
        cnt_ref[...] = jnp.zeros(cnt_ref.shape, cnt_ref.dtype)

    @pl.when(sweep == 1)
    def _():
        total = total_s[...]
        blocks = jnp.floor((total + (block - 1.0)) * (1.0 / block))
        inclusive = (_iota((N_EXPERTS, N_EXPERTS), 1) <= _iota((N_EXPERTS, N_EXPERTS), 0)).astype(BF16)
        block_end = jnp.dot(inclusive, jnp.broadcast_to(blocks, (N_EXPERTS, LANES)).astype(BF16),
                            preferred_element_type=F32)[:, 0:1]
        slot = (block_end - blocks) * block + before
        p1 = jnp.sum(jnp.where(row == i1, slot, 0.0), axis=0, keepdims=True)
        p2 = jnp.sum(jnp.where(row == i2, slot, 0.0), axis=0, keepdims=True)
        out_row = _iota((8, tm), 0)
        pos_ref[...] = jnp.where(out_row == 0, p1, jnp.where(out_row == 1, p2, 0.0)).astype(jnp.int32)
        w_ref[...] = jnp.where(out_row == 0, w1, jnp.where(out_row == 1, w2, 0.0))
        cnt_ref[...] = jnp.broadcast_to(total, cnt_ref.shape).astype(jnp.int32)


def _route_tokens(logits, rbias_col, tm):
    t = logits.shape[0]
    tok = pl.BlockSpec((None, 8, tm), lambda s, i: (s, 0, i))
    pos, w, counts = pl.pallas_call(
        _route_kernel,
        grid=(2, t // tm),
        in_specs=[pl.BlockSpec((tm, LANES), lambda s, i: (i, 0)), pl.BlockSpec((LANES, 1), lambda s, i: (0, 0))],
        out_specs=[tok, tok, pl.BlockSpec((N_EXPERTS, LANES), lambda s, i: (0, 0))],
        out_shape=[jax.ShapeDtypeStruct((2, 8, t), jnp.int32), jax.ShapeDtypeStruct((2, 8, t), F32),
                   jax.ShapeDtypeStruct((N_EXPERTS, LANES), jnp.int32)],
        scratch_shapes=[pltpu.VMEM((N_EXPERTS, 1), F32), pltpu.VMEM((N_EXPERTS, 1), F32)],
        compiler_params=_params("arbitrary", "arbitrary"),
        name="moe_route",
    )(logits, rbias_col)
    return pos[1], w[1], counts


def _dispatch_kernel(pos_ref, pad_start_ref, pad_len_ref, n_valid_ref, x_ref, xs_ref, zero_s, sem, zero_sem):
    i = pl.program_id(0)
    rows = x_ref.shape[0]
    n_blocks = xs_ref.shape[0] // MOE_BLOCK_ROWS
    n_tokens = pos_ref.shape[0] // MOE_TOPK

    @pl.when(i == 0)
    def _():
        zero_s[...] = jnp.zeros(zero_s.shape, zero_s.dtype)
        sizes = [MOE_BLOCK_ROWS >> s for s in range(MOE_BLOCK_ROWS.bit_length() - 3)]
        pieces = [(e, size) for e in range(N_EXPERTS) for size in sizes]

        def piece_copy(e, size):
            start = pad_start_ref[e] + (pad_len_ref[e] & ~(2 * size - 1))
            dst = xs_ref.at[pl.ds(pl.multiple_of(start, 8), size)]
            return pltpu.make_async_copy(zero_s.at[pl.ds(0, size)], dst, zero_sem)

        def block_copy(j):
            return pltpu.make_async_copy(zero_s, xs_ref.at[pl.ds(j * MOE_BLOCK_ROWS, MOE_BLOCK_ROWS)], zero_sem)

        for action in ("start", "wait"):
            for e, size in pieces:
                @pl.when((pad_len_ref[e] & size) != 0)
                def _():
                    getattr(piece_copy(e, size), action)()
            for j in range(n_blocks - N_EXPERTS, n_blocks):
                @pl.when(j >= n_valid_ref[0])
                def _():
                    getattr(block_copy(j), action)()

    def issue(r, _):
        for k in range(MOE_TOPK):
            p = pos_ref[k * n_tokens + i * rows + r]
            pltpu.make_async_copy(x_ref.at[pl.ds(r, 1)], xs_ref.at[pl.ds(p, 1)], sem).start()
        return 0

    lax.fori_loop(0, rows, issue, 0, unroll=8)
    for _k in range(MOE_TOPK):
        pltpu.make_async_copy(x_ref, xs_ref.at[pl.ds(0, rows)], sem).wait()


def _dispatch(pos, pad_start, pad_len, n_valid, x, n_rows, tm):
    t, d = x.shape
    grid_spec = pltpu.PrefetchScalarGridSpec(
        num_scalar_prefetch=4,
        grid=(t // tm,),
        in_specs=[pl.BlockSpec((tm, d), lambda i, *_: (i, 0))],
        out_specs=pl.BlockSpec(memory_space=pl.ANY),
        scratch_shapes=[pltpu.VMEM((MOE_BLOCK_ROWS, d), x.dtype), pltpu.SemaphoreType.DMA(()),
                        pltpu.SemaphoreType.DMA(())],
    )
    return pl.pallas_call(
        _dispatch_kernel,
        grid_spec=grid_spec,
        out_shape=jax.ShapeDtypeStruct((n_rows, d), x.dtype),
        compiler_params=_params("arbitrary"),
        name="moe_dispatch",
    )(pos, pad_start, pad_len, n_valid, x)


def _expert_kernel(blk_expert_ref, n_valid_ref, x_ref, wg_ref, wu_ref, wd_ref, y_ref, wg_s, wu_s, wd_s):
    i = pl.program_id(0)

    @pl.when((i == 0) | (blk_expert_ref[i] != blk_expert_ref[jnp.maximum(i - 1, 0)]))
    def _():
        wg_s[...] = wg_ref[...].astype(BF16)
        wu_s[...] = wu_ref[...].astype(BF16)
        wd_s[...] = wd_ref[...].astype(BF16)

    @pl.when(i < n_valid_ref[0])
    def _():
        x = x_ref[...].astype(BF16)
        hid = _silu(jnp.dot(x, wg_s[...], preferred_element_type=F32)) * jnp.dot(
            x, wu_s[...], preferred_element_type=F32)
        y_ref[...] = _dot(hid, wd_s[...])

    @pl.when(i >= n_valid_ref[0])
    def _():
        y_ref[...] = jnp.zeros(y_ref.shape, y_ref.dtype)


def _experts(blk_expert, n_valid, xs, wg, wu, wd, layer, n_blocks):
    d = xs.shape[1]
    f = wg.shape[-1]
    rows = MOE_BLOCK_ROWS
    n_rows = n_blocks * rows
    grid_spec = pltpu.PrefetchScalarGridSpec(
        num_scalar_prefetch=2,
        grid=(n_blocks,),
        in_specs=[pl.BlockSpec((rows, d), lambda i, be, nv: (jnp.minimum(i, nv[0] - 1), 0)),
                  pl.BlockSpec((None, None, d, f), lambda i, be, nv: (layer, be[i], 0, 0)),
                  pl.BlockSpec((None, None, d, f), lambda i, be, nv: (layer, be[i], 0, 0)),
                  pl.BlockSpec((None, None, f, d), lambda i, be, nv: (layer, be[i], 0, 0))],
        out_specs=pl.BlockSpec((rows, d), lambda i, be, nv: (i, 0)),
        scratch_shapes=[pltpu.VMEM((d, f), BF16), pltpu.VMEM((d, f), BF16), pltpu.VMEM((f, d), BF16)],
    )
    return pl.pallas_call(
        _expert_kernel,
        grid_spec=grid_spec,
        out_shape=jax.ShapeDtypeStruct((n_rows, d), F32),
        compiler_params=_params("arbitrary"),
        name="moe_experts",
    )(blk_expert, n_valid, xs, wg, wu, wd)


def _combine_kernel(pos_ref, ys_ref, w_ref, h_ref, gain_ref, bias_ref, y_ref, yb_ref, buf, sem):
    i = pl.program_id(0)
    n = pl.num_programs(0)
    rows = h_ref.shape[0]
    n_tokens = pos_ref.shape[0] // MOE_TOPK

    def issue(tile, slot):
        def body(r, _):
            for k in range(MOE_TOPK):
                p = pos_ref[k * n_tokens + tile * rows + r]
                pltpu.make_async_copy(ys_ref.at[pl.ds(p, 1)], buf.at[slot, k, pl.ds(r, 1)], sem.at[slot]).start()
            return 0

        lax.fori_loop(0, rows, body, 0, unroll=8)

    slot = i % 2

    @pl.when(i == 0)
    def _():
        issue(0, 0)

    @pl.when(i + 1 < n)
    def _():
        issue(i + 1, 1 - slot)

    for k in range(MOE_TOPK):
        pltpu.make_async_copy(ys_ref.at[pl.ds(0, rows)], buf.at[slot, k], sem.at[slot]).wait()
    eye = _iota((rows, rows), 0) == _iota((rows, rows), 1)
    w_col = [jnp.sum(jnp.where(eye, jnp.broadcast_to(w_ref[k:k + 1, :], (rows, rows)), 0.0), axis=1, keepdims=True)
             for k in range(MOE_TOPK)]
    ffn = w_col[0] * buf[slot, 0] + w_col[1] * buf[slot, 1]
    y = _layer_norm(DEEPNORM_ALPHA * h_ref[...] + ffn, gain_ref[...], bias_ref[...])
    y_ref[...] = y
    yb_ref[...] = y.astype(BF16)


def _combine_ln(pos, ys, w12, h, gain, bias, tm):
    t, d = h.shape
    grid_spec = pltpu.PrefetchScalarGridSpec(
        num_scalar_prefetch=1,
        grid=(t // tm,),
        in_specs=[pl.BlockSpec(memory_space=pl.ANY),
                  pl.BlockSpec((8, tm), lambda i, pos: (0, i)),
                  pl.BlockSpec((tm, d), lambda i, pos: (i, 0)),
                  pl.BlockSpec((1, d), lambda i, pos: (0, 0)),
                  pl.BlockSpec((1, d), lambda i, pos: (0, 0))],
        out_specs=[pl.BlockSpec((tm, d), lambda i, pos: (i, 0)), pl.BlockSpec((tm, d), lambda i, pos: (i, 0))],
        scratch_shapes=[pltpu.VMEM((2, MOE_TOPK, tm, d), F32), pltpu.SemaphoreType.DMA((2,))],
    )
    return pl.pallas_call(
        _combine_kernel,
        grid_spec=grid_spec,
        out_shape=[jax.ShapeDtypeStruct((t, d), F32), jax.ShapeDtypeStruct((t, d), BF16)],
        compiler_params=_params("arbitrary"),
        name="moe_combine_ln",
    )(pos, ys, w12, h, gain, bias)


def _moe_ln(h, logits, rbias, wg, wu, wd, layer, gain, bias):
    t, d = h.shape
    rows = MOE_BLOCK_ROWS
    n_blocks = (MOE_TOPK * t) // rows + N_EXPERTS
    pos_t, w_t, counts = _route_tokens(logits, rbias, 512)
    pos = pos_t[:MOE_TOPK].reshape(-1)
    counts = counts[:, 0]
    blocks_per_expert = (counts + rows - 1) // rows
    block_end = jnp.cumsum(blocks_per_expert)
    row_start = (block_end - blocks_per_expert) * rows
    n_valid = block_end[-1:].astype(jnp.int32)
    blk = jnp.arange(n_blocks, dtype=jnp.int32)
    blk_expert = jnp.sum(jnp.minimum(blk, n_valid - 1)[:, None] >= block_end[None, :], axis=1).astype(jnp.int32)
    pad_start = ((row_start + counts) // 8 * 8).astype(jnp.int32)
    pad_len = (row_start + blocks_per_expert * rows - pad_start).astype(jnp.int32)
    xs = _dispatch(pos, pad_start, pad_len, n_valid, h, n_blocks * rows, 512)
    ys = _experts(blk_expert, n_valid, xs, wg, wu, wd, layer, n_blocks)
    return _combine_ln(pos, ys, w_t, h, gain, bias, 256)


def _rope_tables(seq):
    inv_freq = ROPE_THETA ** (-jnp.arange(0, HEAD_DIM, 2, dtype=F32) / HEAD_DIM)
    ang = jnp.arange(seq, dtype=F32)[:, None] * inv_freq[None, :]
    cos, sin = jnp.cos(ang), jnp.sin(ang)
    return jnp.concatenate([cos, cos], axis=-1), jnp.concatenate([-sin, sin], axis=-1)


def _even_mixer(hb, b, s, w_in, conv_w, a_log, dt_bias, gdn_norm, hgrn_norm, lower_bound):
    gw = GDN_WIDTH
    n_small = 2 * GDN_HEADS
    tail0 = 4 * gw
    w_main = jnp.concatenate([w_in[:, :tail0], w_in[:, tail0 + n_small:]], axis=1).astype(BF16)
    w_small = jnp.pad(w_in[:, tail0:tail0 + n_small], ((0, 0), (0, LANES - n_small))).astype(BF16)
    h3 = _matmul(hb, w_main, 1024, 1024).reshape(b, s, EVEN_MAIN)
    n_chunks = s // GDN_CHUNK
    small = _matmul(hb, w_small, 1024, LANES).reshape(b, s, LANES)
    to_rows = lambda a: a.transpose(0, 2, 1).reshape(b, GDN_HEADS, n_chunks, GDN_CHUNK)
    b_rows, a_rows = to_rows(small[..., :GDN_HEADS]), to_rows(small[..., GDN_HEADS:n_small])
    headvec = lambda v: jnp.broadcast_to(v.astype(F32)[:, None, None], (GDN_HEADS, 1, GDN_CHUNK))
    o_a = _gdn(h3, conv_w.astype(F32), a_rows, b_rows, headvec(a_log), headvec(dt_bias),
               gdn_norm.astype(F32).reshape(1, HEAD_DIM))
    o_b = _hgrn(h3, lower_bound.astype(F32).reshape(HGRN_HEADS, 1, HEAD_DIM),
                hgrn_norm.astype(F32).reshape(1, HEAD_DIM))
    return [o_a.reshape(b * s, GDN_WIDTH), o_b.reshape(b * s, HGRN_WIDTH)]


def _odd_mixer(hb, b, s, w_in, cos2, sin2):
    h = _matmul(hb, w_in.astype(BF16), 1024, 768)
    h3 = h.reshape(b, s, ODD_COLS)
    o_c = _dilated(h3, cos2, sin2)
    o_d = _moba(h3, cos2, sin2)
    return [o_c.reshape(b * s, -1), o_d.reshape(b * s, -1)]


def kernel(x, ev_w_in, ev_conv_w, ev_a_log, ev_dt_bias, ev_gdn_norm, ev_hgrn_norm, hgrn_lb_logits, ev_w_out,
           od_w_in, od_w_out, router_w, router_bias, moe_w_gate, moe_w_up, moe_w_down, ln_gain, ln_bias):
    b, s, d = x.shape
    t = b * s
    cos2, sin2 = _rope_tables(s)
    lower_bounds = jnp.cumsum(jax.nn.softmax(hgrn_lb_logits.astype(F32), axis=0), axis=0)
    rw = jnp.pad(router_w.astype(F32), ((0, 0), (0, LANES - N_EXPERTS)))
    rbias = jnp.pad(router_bias.astype(F32), (0, LANES - N_EXPERTS)).reshape(LANES, 1)
    vec = lambda v: v.astype(F32).reshape(1, d)

    h = x.reshape(t, d)
    hb = h
    for layer in range(DEPTH):
        if layer % 2 == 0:
            e = layer // 2
            parts = _even_mixer(hb, b, s, ev_w_in[e], ev_conv_w[e], ev_a_log[e], ev_dt_bias[e], ev_gdn_norm[e],
                                ev_hgrn_norm[e], lower_bounds[layer])
            w_out = ev_w_out[e].astype(BF16)
        else:
            o = layer // 2
            parts = _odd_mixer(hb, b, s, od_w_in[o], cos2, sin2)
            w_out = od_w_out[o].astype(BF16)
        splits = np.cumsum([p.shape[1] for p in parts])[:-1]
        weights = jnp.split(w_out, splits, axis=0)
        h, hb, logits = _out_ln(parts, weights, h, vec(ln_gain[layer, 0]), vec(ln_bias[layer, 0]), rw, 512)
        h, hb = _moe_ln(h, logits, rbias, moe_w_gate, moe_w_up, moe_w_down, layer,
                        vec(ln_gain[layer, 1]), vec(ln_bias[layer, 1]))
    return h.reshape(b, s, d)
```

```python
import functools
import math

import jax
import jax.numpy as jnp
import numpy as np
from jax import lax
from jax.experimental import pallas as pl
from jax.experimental.pallas import tpu as pltpu

F32 = jnp.float32
BF16 = jnp.bfloat16

D_MODEL = 2048
DEPTH = 2
HEAD_DIM = 128
GDN_HEADS = 8
GDN_CONV = 4
GDN_CHUNK = 64
GDN_WIDTH = GDN_HEADS * HEAD_DIM
HGRN_HEADS = 8
HGRN_CHUNK = 16
HGRN_WIDTH = HGRN_HEADS * HEAD_DIM
DIL_GROUPS = ((128, 1), (512, 4), (2048, 16))
DIL_HEADS_PER_GROUP = 4
DIL_HEADS = len(DIL_GROUPS) * DIL_HEADS_PER_GROUP
MOBA_HEADS = 4
MOBA_BLOCK = 256
MOBA_TOPK = 3
ROPE_THETA = 10000.0
N_EXPERTS = 16
N_EXPERT_GROUPS = 4
EXPERTS_PER_GROUP = N_EXPERTS // N_EXPERT_GROUPS
D_EXPERT = 512
MOE_TOPK = 2
MOE_BLOCK_ROWS = 512
DEEPNORM_ALPHA = (2.0 * DEPTH) ** 0.25
LN_EPS = 1e-5
RMS_EPS = 1e-6
NEG_INF = -1e30

LANES = 128
VMEM_LIMIT = 56 * 1024 * 1024
ATT_BLOCK = 256
MOBA_Q_GROUPS = ((7, 0, 6, 1), (5, 2, 4, 3))
DIL_BLOCK = 128
GDN_GROUP = 16
HGRN_GROUP = 4
HGRN_ROWS = 256

EVEN_MAIN = 3 * GDN_WIDTH + GDN_WIDTH + 4 * HGRN_WIDTH
ODD_COLS = 3 * DIL_HEADS * HEAD_DIM + 3 * MOBA_HEADS * HEAD_DIM


def _dot(a, b):
    return jnp.dot(a.astype(BF16), b.astype(BF16), preferred_element_type=F32)


def _dot_nt(a, b):
    return lax.dot_general(a.astype(BF16), b.astype(BF16), (((1,), (1,)), ((), ())),
                           preferred_element_type=F32)


def _dot_tn(a, b):
    return lax.dot_general(a.astype(BF16), b.astype(BF16), (((0,), (0,)), ((), ())),
                           preferred_element_type=F32)


def _dot_hi(a, b):
    return jnp.dot(a, b, preferred_element_type=F32, precision=lax.Precision.HIGHEST)


def _dot_nt_hi(a, b):
    return lax.dot_general(a, b, (((1,), (1,)), ((), ())), preferred_element_type=F32,
                           precision=lax.Precision.HIGHEST)


def _dot3(a, b):
    a_hi = a.astype(BF16)
    b_hi = b.astype(BF16)
    a_lo = (a - a_hi.astype(F32)).astype(BF16)
    b_lo = (b - b_hi.astype(F32)).astype(BF16)
    dot = functools.partial(jnp.dot, preferred_element_type=F32)
    return dot(a_hi, b_hi) + (dot(a_hi, b_lo) + dot(a_lo, b_hi))


def _dot_sel(sel, x):
    dot = functools.partial(jnp.dot, preferred_element_type=F32)
    x_hi = x.astype(BF16)
    r1 = x - x_hi.astype(F32)
    x_mid = r1.astype(BF16)
    x_lo = (r1 - x_mid.astype(F32)).astype(BF16)
    return dot(sel, x_hi) + (dot(sel, x_mid) + dot(sel, x_lo))


_dot_inv = _dot


def _silu(x):
    return x * jax.nn.sigmoid(x)


def _iota(shape, dim):
    return lax.broadcasted_iota(jnp.int32, shape, dim)


def _params(*sem):
    return pltpu.CompilerParams(dimension_semantics=sem, vmem_limit_bytes=VMEM_LIMIT)


def _mm_kernel(x_ref, w_ref, o_ref, *, transposed):
    x = x_ref[...].astype(BF16)
    prod = _dot_nt(x, w_ref[...]) if transposed else jnp.dot(x, w_ref[...], preferred_element_type=F32)
    o_ref[...] = prod.astype(o_ref.dtype)


def _matmul(x, w, tm, tn, transposed=False):
    m, k = x.shape
    n = w.shape[0] if transposed else w.shape[1]
    assert m % tm == 0 and n % tn == 0
    w_spec = pl.BlockSpec((tn, k), lambda i, j: (j, 0)) if transposed else pl.BlockSpec((k, tn), lambda i, j: (0, j))
    return pl.pallas_call(
        functools.partial(_mm_kernel, transposed=transposed),
        grid=(m // tm, n // tn),
        in_specs=[pl.BlockSpec((tm, k), lambda i, j: (i, 0)), w_spec],
        out_specs=pl.BlockSpec((tm, tn), lambda i, j: (i, j)),
        out_shape=jax.ShapeDtypeStruct((m, n), F32),
        compiler_params=_params("parallel", "parallel"),
        name="in_proj",
    )(x, w)


def _gdn_kernel(q_ref, k_ref, v_ref, z_ref, cwq_ref, cwk_ref, cwv_ref, a_ref, b_ref, alog_ref, dt_ref,
                gn_ref, o_ref, pad_s, q_s, k_s, v_s, gcum_s, beta_s, qe_s, ob_s, sm_s, sa_s):
    seq = q_ref.shape[0]
    c = GDN_CHUNK
    n_chunks = seq // c
    rows = 256

    pad_s[pl.ds(0, 8), :] = jnp.zeros((8, HEAD_DIM), F32)
    for x_ref, cw_ref, dst, mode in ((q_ref, cwq_ref, q_s, "q"), (k_ref, cwk_ref, k_s, "k"),
                                     (v_ref, cwv_ref, v_s, "v")):
        pad_s[pl.ds(8, seq), :] = x_ref[...]
        cw = cw_ref[...]
        for r in range(seq // rows):
            acc = None
            for j in range(GDN_CONV):
                tap = pad_s[pl.ds(8 + r * rows - (GDN_CONV - 1) + j, rows), :] * cw[j:j + 1, :]
                acc = tap if acc is None else acc + tap
            y = _silu(acc)
            if mode != "v":
                y = y * lax.rsqrt(jnp.sum(y * y, axis=-1, keepdims=True) + RMS_EPS)
            if mode == "q":
                y = y * HEAD_DIM ** -0.5
            dst[pl.ds(r * rows, rows), :] = y

    upper = (_iota((c, c), 0) <= _iota((c, c), 1)).astype(F32)
    g = -jnp.exp(alog_ref[...]) * jax.nn.softplus(a_ref[...] + dt_ref[...])
    gcum_s[...] = _dot_hi(g, upper)
    beta_s[...] = jax.nn.sigmoid(b_ref[...])

    ri = _iota((c, c), 0)
    ci = _iota((c, c), 1)
    eye = ri == ci
    strict = ri > ci
    incl = ri >= ci
    eye_f = eye.astype(F32)
    level1 = ri // 2 == ci // 2
    levels = []
    s = 2
    while s < c:
        levels.append((ri // (2 * s) == ci // (2 * s)) & ((ri // s) % 2 == 1) & ((ci // s) % 2 == 0))
        s *= 2

    dot = functools.partial(jnp.dot, preferred_element_type=F32)

    def to_col(row):
        return jnp.sum(jnp.where(eye, jnp.broadcast_to(row, (c, c)), 0.0), axis=1, keepdims=True)

    def prepare(i, _):
        n0 = i * GDN_GROUP
        grp = range(GDN_GROUP)
        starts = [pl.multiple_of((n0 + j) * c, c) for j in grp]
        qc = [q_s[pl.ds(r0, c), :] for r0 in starts]
        kc = [k_s[pl.ds(r0, c), :] for r0 in starts]
        vc = [v_s[pl.ds(r0, c), :] for r0 in starts]
        g_row = [gcum_s[pl.ds(n0 + j, 1), :] for j in grp]
        g_col = [to_col(g_row[j]) for j in grp]
        b_col = [to_col(beta_s[pl.ds(n0 + j, 1), :]) for j in grp]
        decay = [jnp.exp(jnp.where(incl, g_col[j] - g_row[j], 0.0)) for j in grp]
        n_mat = [b_col[j] * jnp.where(strict, decay[j], 0.0) * _dot_nt(kc[j], kc[j]) for j in grp]
        inv = [eye_f - jnp.where(level1, n_mat[j], 0.0) for j in grp]
        for blk in levels:
            tmp = [_dot_inv(inv[j], jnp.where(blk, n_mat[j], 0.0)) for j in grp]
            inv = [inv[j] - _dot_inv(tmp[j], inv[j]) for j in grp]
        e_col = [jnp.exp(g_col[j]) for j in grp]
        sol = [_dot_inv(inv[j], jnp.concatenate([b_col[j] * vc[j], (b_col[j] * e_col[j]) * kc[j]], axis=1))
               for j in grp]
        qk = [(_dot_nt(qc[j], kc[j]) * jnp.where(incl, decay[j], 0.0)).astype(BF16) for j in grp]
        ub = [sol[j][:, :HEAD_DIM].astype(BF16) for j in grp]
        w = [sol[j][:, HEAD_DIM:].astype(BF16) for j in grp]
        kd = [(kc[j] * jnp.exp(g_row[j][:, c - 1:c] - g_col[j])).astype(BF16) for j in grp]
        q_eff = [(qc[j] * e_col[j] - dot(qk[j], w[j])).astype(BF16) for j in grp]
        o_base = [dot(qk[j], ub[j]) for j in grp]
        s_mat = [_dot_tn(kd[j], w[j]).astype(BF16) for j in grp]
        s_add = [_dot_tn(kd[j], ub[j]) for j in grp]
        for j, r0 in enumerate(starts):
            m0 = pl.multiple_of((n0 + j) * HEAD_DIM, HEAD_DIM)
            qe_s[pl.ds(r0, c), :] = q_eff[j]
            ob_s[pl.ds(r0, c), :] = o_base[j]
            sm_s[pl.ds(m0, HEAD_DIM), :] = s_mat[j]
            sa_s[pl.ds(m0, HEAD_DIM), :] = s_add[j]
        return 0

    lax.fori_loop(0, n_chunks // GDN_GROUP, prepare, 0)

    gn = gn_ref[...]

    def chunk(n, state):
        r0 = pl.multiple_of(n * c, c)
        m0 = pl.multiple_of(n * HEAD_DIM, HEAD_DIM)
        g_last = gcum_s[pl.ds(n, 1), :][:, c - 1:c]
        lhs = jnp.concatenate([qe_s[pl.ds(r0, c), :], sm_s[pl.ds(m0, HEAD_DIM), :]], axis=0)
        prod = dot(lhs, state.astype(BF16))
        ob_s[pl.ds(r0, c), :] = prod[:c] + ob_s[pl.ds(r0, c), :]
        return jnp.exp(g_last) * state - prod[c:] + sa_s[pl.ds(m0, HEAD_DIM), :]

    lax.fori_loop(0, n_chunks, chunk, jnp.zeros((HEAD_DIM, HEAD_DIM), F32))

    for r in range(seq // rows):
        sl = pl.ds(r * rows, rows)
        o = ob_s[sl, :]
        o = o * lax.rsqrt(jnp.mean(o * o, axis=-1, keepdims=True) + RMS_EPS) * gn
        o_ref[sl, :] = (o * _silu(z_ref[sl, :])).astype(o_ref.dtype)


def _gdn(h3, conv_w, a_rows, b_rows, alog, dt, gn):
    b, s, _ = h3.shape
    nh = GDN_HEADS
    n_chunks = s // GDN_CHUNK
    col = lambda off: pl.BlockSpec((None, s, HEAD_DIM), lambda bi, hi: (bi, 0, off + hi))
    cw = lambda off: pl.BlockSpec((GDN_CONV, HEAD_DIM), lambda bi, hi: (0, off + hi))
    rowspec = pl.BlockSpec((None, None, n_chunks, GDN_CHUNK), lambda bi, hi: (bi, hi, 0, 0))
    headvec = pl.BlockSpec((None, 1, GDN_CHUNK), lambda bi, hi: (hi, 0, 0))
    return pl.pallas_call(
        _gdn_kernel,
        grid=(b, nh),
        in_specs=[col(0), col(nh), col(2 * nh), col(3 * nh), cw(0), cw(nh), cw(2 * nh),
                  rowspec, rowspec, headvec, headvec,
                  pl.BlockSpec((1, HEAD_DIM), lambda bi, hi: (0, 0))],
        out_specs=pl.BlockSpec((None, s, HEAD_DIM), lambda bi, hi: (bi, 0, hi)),
        out_shape=jax.ShapeDtypeStruct((b, s, GDN_WIDTH), BF16),
        scratch_shapes=[pltpu.VMEM((s + 8, HEAD_DIM), F32), pltpu.VMEM((s, HEAD_DIM), F32),
                        pltpu.VMEM((s, HEAD_DIM), F32), pltpu.VMEM((s, HEAD_DIM), F32),
                        pltpu.VMEM((n_chunks, GDN_CHUNK), F32), pltpu.VMEM((n_chunks, GDN_CHUNK), F32),
                        pltpu.VMEM((s, HEAD_DIM), BF16), pltpu.VMEM((s, HEAD_DIM), F32),
                        pltpu.VMEM((n_chunks * HEAD_DIM, HEAD_DIM), BF16),
                        pltpu.VMEM((n_chunks * HEAD_DIM, HEAD_DIM), F32)],
        compiler_params=_params("parallel", "parallel"),
        name="gdn",
    )(h3, h3, h3, h3, conv_w, conv_w, conv_w, a_rows, b_rows, alog, dt, gn)


def _hgrn_kernel(q_ref, f_ref, i_ref, g_ref, lb_ref, hn_ref, o_ref):
    seq = q_ref.shape[0]
    c = HGRN_CHUNK
    rows = HGRN_ROWS
    ri = _iota((rows, rows), 0)
    ci = _iota((rows, rows), 1)
    causal = (ri // c == ci // c) & (ci <= ri)
    row_in_chunk = _iota((rows, HEAD_DIM), 0) % c
    lb = lb_ref[...]
    hn = hn_ref[...]

    chunks = [slice(j * c, (j + 1) * c) for j in range(rows // c)]
    grp = range(HGRN_GROUP)

    def chunk_scan(x, suffix):
        step = 1
        while step < c:
            if suffix:
                x = x + jnp.where(row_in_chunk < c - step, pltpu.roll(x, rows - step, axis=0), 0.0)
            else:
                x = x + jnp.where(row_in_chunk >= step, pltpu.roll(x, step, axis=0), 0.0)
            step *= 2
        return x

    def group(n, state_t):
        starts = [pl.multiple_of((n * HGRN_GROUP + j) * rows, rows) for j in grp]
        qc = [q_ref[pl.ds(r0, rows), :] for r0 in starts]
        ic = [i_ref[pl.ds(r0, rows), :].astype(BF16) for r0 in starts]
        f = [lb + (1.0 - lb) * jax.nn.sigmoid(f_ref[pl.ds(r0, rows), :]) for r0 in starts]
        log_f = [jnp.log(f[j]) for j in grp]
        bcum = [chunk_scan(log_f[j], False) for j in grp]
        to_end = [chunk_scan(log_f[j], True) - log_f[j] for j in grp]
        chunk_dec = [jnp.exp(bcum[j]) for j in grp]
        q_dec = [(qc[j] * chunk_dec[j]).astype(BF16) for j in grp]
        k_inv = [(1.0 - f[j]) * jnp.exp(-bcum[j]) for j in grp]
        k_dec = [((1.0 - f[j]) * jnp.exp(to_end[j])).astype(BF16) for j in grp]
        p = [jnp.where(causal, _dot_nt(q_dec[j], k_inv[j]), 0.0) for j in grp]
        o_intra = [_dot(p[j], ic[j]) for j in grp]
        updates = [[_dot_tn(ic[j][sl], k_dec[j][sl]) for sl in chunks] for j in grp]
        for j, r0 in enumerate(starts):
            outs = []
            for sl, upd in zip(chunks, updates[j]):
                outs.append(o_intra[j][sl] + _dot_nt(q_dec[j][sl], state_t))
                state_t = state_t * chunk_dec[j][sl.stop - 1:sl.stop] + upd
            o = jnp.concatenate(outs, axis=0)
            o = o * lax.rsqrt(jnp.mean(o * o, axis=-1, keepdims=True) + RMS_EPS) * hn
            o_ref[pl.ds(r0, rows), :] = (o * _silu(g_ref[pl.ds(r0, rows), :])).astype(o_ref.dtype)
        return state_t

    lax.fori_loop(0, seq // (rows * HGRN_GROUP), group, jnp.zeros((HEAD_DIM, HEAD_DIM), F32))


def _hgrn(h3, lb, hn):
    b, s, _ = h3.shape
    nh = HGRN_HEADS
    base = 4 * GDN_HEADS
    col = lambda off: pl.BlockSpec((None, s, HEAD_DIM), lambda bi, hi: (bi, 0, base + off + hi))
    return pl.pallas_call(
        _hgrn_kernel,
        grid=(b, nh),
        in_specs=[col(0), col(nh), col(2 * nh), col(3 * nh),
                  pl.BlockSpec((None, 1, HEAD_DIM), lambda bi, hi: (hi, 0, 0)),
                  pl.BlockSpec((1, HEAD_DIM), lambda bi, hi: (0, 0))],
        out_specs=pl.BlockSpec((None, s, HEAD_DIM), lambda bi, hi: (bi, 0, hi)),
        out_shape=jax.ShapeDtypeStruct((b, s, HGRN_WIDTH), BF16),
        compiler_params=_params("parallel", "parallel"),
        name="hgrn2",
    )(h3, h3, h3, h3, lb, hn)


def _rope(x, cos2, sin2):
    return x * cos2 + pltpu.roll(x, HEAD_DIM // 2, axis=1) * sin2


def _flash_step(q_blk, k_blk, v_blk, mask, carry):
    m, l, acc = carry
    s = jnp.where(mask, _dot_nt(q_blk, k_blk) * HEAD_DIM ** -0.5, NEG_INF)
    m_new = jnp.maximum(m, jnp.max(s, axis=-1, keepdims=True))
    alpha = jnp.exp(m - m_new)
    p = jnp.exp(s - m_new)
    l = alpha * l + jnp.sum(p, axis=-1, keepdims=True)
    acc = alpha * acc + _dot(p, v_blk)
    return m_new, l, acc


def _flash_init():
    blk = ATT_BLOCK
    return (jnp.full((blk, 1), NEG_INF, F32), jnp.zeros((blk, 1), F32), jnp.zeros((blk, HEAD_DIM), F32))


def _dilated_kernel(*refs):
    n_g = len(DIL_GROUPS)
    q_refs, k_refs, v_refs = refs[0:n_g], refs[n_g:2 * n_g], refs[2 * n_g:3 * n_g]
    cos_ref, sin_ref, o_ref = refs[3 * n_g:3 * n_g + 3]
    q_s, k_s, v_s, og_s, lse_s = refs[3 * n_g + 3:]
    seq = o_ref.shape[0]
    blk = DIL_BLOCK
    piece = 256
    grp = range(n_g)

    for gi, (window, d) in enumerate(DIL_GROUPS):
        assert window // d == blk
        seg = seq // d
        k_s[gi, pl.ds(0, blk), :] = jnp.zeros((blk, HEAD_DIM), BF16)
        v_s[gi, pl.ds(0, blk), :] = jnp.zeros((blk, HEAD_DIM), BF16)
        for r in range(d):
            for c0 in range(0, seg, piece):
                n = min(piece, seg)
                rows = pl.ds(r + c0 * d, n, stride=d) if d > 1 else pl.ds(c0, n)
                cos2 = cos_ref[rows, :]
                sin2 = sin_ref[rows, :]
                q_s[gi, pl.ds(r * seg + c0, n), :] = _rope(q_refs[gi][rows, :], cos2, sin2).astype(BF16)
                k_s[gi, pl.ds(blk + r * seg + c0, n), :] = _rope(k_refs[gi][rows, :], cos2, sin2).astype(BF16)
                v_s[gi, pl.ds(blk + r * seg + c0, n), :] = v_refs[gi][rows, :].astype(BF16)

    ri = _iota((blk, 2 * blk), 0)
    ci = _iota((blk, 2 * blk), 1)
    rel = ri + blk - ci
    in_window = (rel >= 0) & (rel <= blk)
    dot = functools.partial(jnp.dot, preferred_element_type=F32)

    def q_block(m, _):
        j0 = pl.multiple_of(m * blk, blk)
        segs = [seq // d for _, d in DIL_GROUPS]
        has_prev = [jnp.where(j0 % seg != 0, blk, 0) for seg in segs]
        mask = [in_window & (ci + has_prev[g] >= blk) for g in grp]
        q = [q_s[g, pl.ds(j0, blk), :] for g in grp]
        kw = [k_s[g, pl.ds(j0, 2 * blk), :] for g in grp]
        vw = [v_s[g, pl.ds(j0, 2 * blk), :] for g in grp]
        s = [jnp.where(mask[g], _dot_nt(q[g], kw[g]) * HEAD_DIM ** -0.5, NEG_INF) for g in grp]
        top = [jnp.max(s[g], axis=-1, keepdims=True) for g in grp]
        p = [jnp.exp(s[g] - top[g]) for g in grp]
        den = [jnp.sum(p[g], axis=-1, keepdims=True) for g in grp]
        o = [dot(p[g].astype(BF16), vw[g]) / den[g] for g in grp]
        lse = [top[g] + jnp.log(den[g]) for g in grp]
        for g, (_, d) in enumerate(DIL_GROUPS):
            seg = segs[g]
            dst = pl.ds((j0 % seg) * d + j0 // seg, blk, stride=d) if d > 1 else pl.ds(j0, blk)
            og_s[g, dst, :] = o[g]
            lse_s[g, dst, :] = jnp.broadcast_to(lse[g], (blk, HEAD_DIM))
        return 0

    lax.fori_loop(0, seq // blk, q_block, 0)

    for c0 in range(0, seq, piece):
        rows = pl.ds(c0, piece)
        lses = [lse_s[g, rows, :] for g in grp]
        top = functools.reduce(jnp.maximum, lses)
        wts = [jnp.exp(x - top) for x in lses]
        den = functools.reduce(lambda a, b: a + b, wts)
        o = functools.reduce(lambda a, b: a + b, [wts[g] * og_s[g, rows, :] for g in grp]) / den
        o_ref[rows, :] = o.astype(o_ref.dtype)


def _dilated(h3, cos2, sin2):
    b, s, _ = h3.shape
    hpg = DIL_HEADS_PER_GROUP
    n_g = len(DIL_GROUPS)
    col = lambda off: pl.BlockSpec((None, s, HEAD_DIM), lambda bi, hi: (bi, 0, off + hi))
    tab = pl.BlockSpec((s, HEAD_DIM), lambda bi, hi: (0, 0))
    specs = [col(part * DIL_HEADS + gi * hpg) for part in range(3) for gi in range(n_g)]
    return pl.pallas_call(
        _dilated_kernel,
        grid=(b, hpg),
        in_specs=specs + [tab, tab],
        out_specs=pl.BlockSpec((None, s, HEAD_DIM), lambda bi, hi: (bi, 0, hi)),
        out_shape=jax.ShapeDtypeStruct((b, s, hpg * HEAD_DIM), BF16),
        scratch_shapes=[pltpu.VMEM((n_g, s, HEAD_DIM), BF16), pltpu.VMEM((n_g, s + DIL_BLOCK, HEAD_DIM), BF16),
                        pltpu.VMEM((n_g, s + DIL_BLOCK, HEAD_DIM), BF16), pltpu.VMEM((n_g, s, HEAD_DIM), F32),
                        pltpu.VMEM((n_g, s, HEAD_DIM), F32)],
        compiler_params=_params("parallel", "parallel"),
        name="dilated_attention",
    )(*([h3] * (3 * n_g)), cos2, sin2)


def _moba_kernel(q_ref, k_ref, v_ref, cos_ref, sin_ref, o_ref, qf_s, q_s, k_s, v_s, km_s, sel_s):
    seq = o_ref.shape[0]
    blk = MOBA_BLOCK
    n_blk = seq // blk
    cos2 = cos_ref[...]
    sin2 = sin_ref[...]
    q = _rope(q_ref[...], cos2, sin2)
    qf_s[...] = q
    q_s[...] = q.astype(BF16)
    km_s[...] = jnp.zeros(km_s.shape, F32)
    for nb in range(n_blk):
        kb = _rope(k_ref[pl.ds(nb * blk, blk), :], cos2[nb * blk:(nb + 1) * blk], sin2[nb * blk:(nb + 1) * blk])
        k_s[pl.ds(nb * blk, blk), :] = kb.astype(BF16)
        km_s[pl.ds(nb, 1), :] = jnp.mean(kb, axis=0, keepdims=True)
    v_s[...] = v_ref[...].astype(BF16)

    lane = _iota((blk, LANES), 1).astype(F32)
    causal = _iota((blk, blk), 0) >= _iota((blk, blk), 1)
    all_true = _iota((blk, blk), 0) >= 0
    rows = lambda nb: pl.ds(nb * blk, blk)

    past = range(1, n_blk)
    km = km_s[...]
    gate = {qb: jnp.where(lane < qb, _dot_nt_hi(qf_s[rows(qb), :], km), -jnp.inf) for qb in past}
    sel = {qb: jnp.zeros((blk, LANES), F32) for qb in past}
    for _k in range(MOBA_TOPK):
        best = {qb: jnp.max(gate[qb], axis=-1, keepdims=True) for qb in past}
        first = {qb: jnp.min(jnp.where(gate[qb] == best[qb], lane, LANES), axis=-1, keepdims=True) for qb in past}
        pick = {qb: (lane == first[qb]) & (best[qb] > -jnp.inf) for qb in past}
        sel = {qb: jnp.where(pick[qb], 1.0, sel[qb]) for qb in past}
        gate = {qb: jnp.where(pick[qb], -jnp.inf, gate[qb]) for qb in past}
    for qb in past:
        sel_s[rows(qb), :] = sel[qb]

    assert sorted(qb for group in MOBA_Q_GROUPS for qb in group) == list(range(n_blk))
    for group in MOBA_Q_GROUPS:
        carry = {qb: _flash_init() for qb in group}
        for j in range(max(group) + 1):
            for qb in group:
                if j < qb:
                    mask = (sel_s[rows(qb), :][:, j:j + 1] > 0.0) & all_true
                elif j == qb:
                    mask = causal
                else:
                    continue
                carry[qb] = _flash_step(q_s[rows(qb), :], k_s[rows(j), :], v_s[rows(j), :], mask, carry[qb])
        for qb in group:
            m, l, acc = carry[qb]
            o_ref[rows(qb), :] = (acc / l).astype(o_ref.dtype)


def _moba(h3, cos2, sin2):
    b, s, _ = h3.shape
    base = 3 * DIL_HEADS
    col = lambda off: pl.BlockSpec((None, s, HEAD_DIM), lambda bi, hi: (bi, 0, base + off + hi))
    tab = pl.BlockSpec((s, HEAD_DIM), lambda bi, hi: (0, 0))
    return pl.pallas_call(
        _moba_kernel,
        grid=(b, MOBA_HEADS),
        in_specs=[col(0), col(MOBA_HEADS), col(2 * MOBA_HEADS), tab, tab],
        out_specs=pl.BlockSpec((None, s, HEAD_DIM), lambda bi, hi: (bi, 0, hi)),
        out_shape=jax.ShapeDtypeStruct((b, s, MOBA_HEADS * HEAD_DIM), BF16),
        scratch_shapes=[pltpu.VMEM((s, HEAD_DIM), F32), pltpu.VMEM((s, HEAD_DIM), BF16),
                        pltpu.VMEM((s, HEAD_DIM), BF16), pltpu.VMEM((s, HEAD_DIM), BF16),
                        pltpu.VMEM((LANES, HEAD_DIM), F32), pltpu.VMEM((s, LANES), F32)],
        compiler_params=_params("parallel", "parallel"),
        name="moba_attention",
    )(h3, h3, h3, cos2, sin2)


def _layer_norm(x, gain, bias):
    mu = jnp.mean(x, axis=-1, keepdims=True)
    xc = x - mu
    var = jnp.mean(xc * xc, axis=-1, keepdims=True)
    return xc * lax.rsqrt(var + LN_EPS) * gain + bias


def _out_ln_kernel(*refs, n_parts):
    o_refs = refs[0:n_parts]
    w_refs = refs[n_parts:2 * n_parts]
    h_ref, gain_ref, bias_ref, rw_ref, y_ref, yb_ref, logit_ref = refs[2 * n_parts:]
    mix = None
    for o_r, w_r in zip(o_refs, w_refs):
        part = jnp.dot(o_r[...], w_r[...], preferred_element_type=F32)
        mix = part if mix is None else mix + part
    y = _layer_norm(DEEPNORM_ALPHA * h_ref[...] + mix, gain_ref[...], bias_ref[...])
    y_ref[...] = y
    yb_ref[...] = y.astype(BF16)
    logit_ref[...] = _dot3(y, rw_ref[...])


def _out_ln(parts, weights, h, gain, bias, router_w, tm):
    t, d = h.shape
    n_parts = len(parts)
    row = lambda width: pl.BlockSpec((tm, width), lambda i: (i, 0))
    full = lambda a: pl.BlockSpec(a.shape, lambda i: (0, 0))
    return pl.pallas_call(
        functools.partial(_out_ln_kernel, n_parts=n_parts),
        grid=(t // tm,),
        in_specs=[row(p.shape[1]) for p in parts] + [full(w) for w in weights]
                 + [row(d), full(gain), full(bias), full(router_w)],
        out_specs=[row(d), row(d), row(LANES)],
        out_shape=[jax.ShapeDtypeStruct((t, d), F32), jax.ShapeDtypeStruct((t, d), BF16),
                   jax.ShapeDtypeStruct((t, LANES), F32)],
        compiler_params=_params("parallel"),
        name="out_proj_ln",
    )(*parts, *weights, h, gain, bias, router_w)


def _route(logits_t, rbias_col):
    row = _iota(logits_t.shape, 0)
    scores = jax.nn.sigmoid(logits_t)
    biased = scores + rbias_col

    def first_argmax(vals):
        best = jnp.max(vals, axis=0, keepdims=True)
        return best, jnp.min(jnp.where(vals == best, row, N_EXPERTS), axis=0, keepdims=True)

    best_score = None
    best_group = None
    for g in range(N_EXPERT_GROUPS):
        vals = jnp.where(row // EXPERTS_PER_GROUP == g, biased, -jnp.inf)
        top1, idx1 = first_argmax(vals)
        top2, _ = first_argmax(jnp.where(row == idx1, -jnp.inf, vals))
        score = top1 + top2
        if g == 0:
            best_score, best_group = score, jnp.zeros_like(idx1)
        else:
            better = score > best_score
            best_group = jnp.where(better, g, best_group)
            best_score = jnp.where(better, score, best_score)
    masked = jnp.where(row // EXPERTS_PER_GROUP == best_group, biased, NEG_INF)
    _, i1 = first_argmax(masked)
    _, i2 = first_argmax(jnp.where(row == i1, -jnp.inf, masked))
    s1 = jnp.sum(jnp.where(row == i1, scores, 0.0), axis=0, keepdims=True)
    s2 = jnp.sum(jnp.where(row == i2, scores, 0.0), axis=0, keepdims=True)
    tot = s1 + s2
    return i1, i2, s1 / tot, s2 / tot


def _route_kernel(logit_ref, rbias_ref, pos_ref, w_ref, cnt_ref, carry_s, total_s):
    sweep = pl.program_id(0)
    i = pl.program_id(1)
    tm = logit_ref.shape[0]
    block = float(MOE_BLOCK_ROWS)

    @pl.when((i == 0) & (sweep == 1))
    def _():
        total_s[...] = carry_s[...]

    @pl.when(i == 0)
    def _():
        carry_s[...] = jnp.zeros(carry_s.shape, F32)

    logits_t = logit_ref[...].T[:N_EXPERTS]
    i1, i2, w1, w2 = _route(logits_t, rbias_ref[...][:N_EXPERTS])
    row = _iota((N_EXPERTS, tm), 0)
    chosen = (row == i1) | (row == i2)
    earlier = (_iota((tm, tm), 0) < _iota((tm, tm), 1)).astype(BF16)
    before = jnp.dot(jnp.where(chosen, 1.0, 0.0).astype(BF16), earlier, preferred_element_type=F32) + carry_s[...]
    carry_s[...] = carry_s[...] + jnp.sum(jnp.where(chosen, 1.0, 0.0), axis=1, keepdims=True)

    @pl.when(sweep == 0)
    def _():
        pos_ref[...] = jnp.zeros(pos_ref.shape, pos_ref.dtype)
        w_ref[...] = jnp.zeros(w_ref.shape, w_ref.dtype)
        cnt_ref[...] = jnp.zeros(cnt_ref.shape, cnt_ref.dtype)

    @pl.when(sweep == 1)
    def _():
        total = total_s[...]
        blocks = jnp.floor((total + (block - 1.0)) * (1.0 / block))
        inclusive = (_iota((N_EXPERTS, N_EXPERTS), 1) <= _iota((N_EXPERTS, N_EXPERTS), 0)).astype(BF16)
        block_end = jnp.dot(inclusive, jnp.broadcast_to(blocks, (N_EXPERTS, LANES)).astype(BF16),
                            preferred_element_type=F32)[:, 0:1]
        slot = (block_end - blocks) * block + before
        p1 = jnp.sum(jnp.where(row == i1, slot, 0.0), axis=0, keepdims=True)
        p2 = jnp.sum(jnp.where(row == i2, slot, 0.0), axis=0, keepdims=True)
        out_row = _iota((8, tm), 0)
        pos_ref[...] = jnp.where(out_row == 0, p1, jnp.where(out_row == 1, p2, 0.0)).astype(jnp.int32)
        w_ref[...] = jnp.where(out_row == 0, w1, jnp.where(out_row == 1, w2, 0.0))
        cnt_ref[...] = jnp.broadcast_to(total, cnt_ref.shape).astype(jnp.int32)


def _route_tokens(logits, rbias_col, tm):
    t = logits.shape[0]
    tok = pl.BlockSpec((None, 8, tm), lambda s, i: (s, 0, i))
    pos, w, counts = pl.pallas_call(
        _route_kernel,
        grid=(2, t // tm),
        in_specs=[pl.BlockSpec((tm, LANES), lambda s, i: (i, 0)), pl.BlockSpec((LANES, 1), lambda s, i: (0, 0))],
        out_specs=[tok, tok, pl.BlockSpec((N_EXPERTS, LANES), lambda s, i: (0, 0))],
        out_shape=[jax.ShapeDtypeStruct((2, 8, t), jnp.int32), jax.ShapeDtypeStruct((2, 8, t), F32),
                   jax.ShapeDtypeStruct((N_EXPERTS, LANES), jnp.int32)],
        scratch_shapes=[pltpu.VMEM((N_EXPERTS, 1), F32), pltpu.VMEM((N_EXPERTS, 1), F32)],
        compiler_params=_params("arbitrary", "arbitrary"),
        name="moe_route",
    )(logits, rbias_col)
    return pos[1], w[1], counts


def _dispatch_kernel(pos_ref, pad_start_ref, pad_len_ref, n_valid_ref, x_ref, xs_ref, zero_s, sem, zero_sem):
    i = pl.program_id(0)
    rows = x_ref.shape[0]
    n_blocks = xs_ref.shape[0] // MOE_BLOCK_ROWS
    n_tokens = pos_ref.shape[0] // MOE_TOPK

    @pl.when(i == 0)
    def _():
        zero_s[...] = jnp.zeros(zero_s.shape, zero_s.dtype)
        sizes = [MOE_BLOCK_ROWS >> s for s in range(MOE_BLOCK_ROWS.bit_length() - 3)]
        pieces = [(e, size) for e in range(N_EXPERTS) for size in sizes]

        def piece_copy(e, size):
            start = pad_start_ref[e] + (pad_len_ref[e] & ~(2 * size - 1))
            dst = xs_ref.at[pl.ds(pl.multiple_of(start, 8), size)]
            return pltpu.make_async_copy(zero_s.at[pl.ds(0, size)], dst, zero_sem)

        def block_copy(j):
            return pltpu.make_async_copy(zero_s, xs_ref.at[pl.ds(j * MOE_BLOCK_ROWS, MOE_BLOCK_ROWS)], zero_sem)

        for action in ("start", "wait"):
            for e, size in pieces:
                @pl.when((pad_len_ref[e] & size) != 0)
                def _():
                    getattr(piece_copy(e, size), action)()
            for j in range(n_blocks - N_EXPERTS, n_blocks):
                @pl.when(j >= n_valid_ref[0])
                def _():
                    getattr(block_copy(j), action)()

    def issue(r, _):
        for k in range(MOE_TOPK):
            p = pos_ref[k * n_tokens + i * rows + r]
            pltpu.make_async_copy(x_ref.at[pl.ds(r, 1)], xs_ref.at[pl.ds(p, 1)], sem).start()
        return 0

    lax.fori_loop(0, rows, issue, 0, unroll=8)
    for _k in range(MOE_TOPK):
        pltpu.make_async_copy(x_ref, xs_ref.at[pl.ds(0, rows)], sem).wait()


def _dispatch(pos, pad_start, pad_len, n_valid, x, n_rows, tm):
    t, d = x.shape
    grid_spec = pltpu.PrefetchScalarGridSpec(
        num_scalar_prefetch=4,
        grid=(t // tm,),
        in_specs=[pl.BlockSpec((tm, d), lambda i, *_: (i, 0))],
        out_specs=pl.BlockSpec(memory_space=pl.ANY),
        scratch_shapes=[pltpu.VMEM((MOE_BLOCK_ROWS, d), x.dtype), pltpu.SemaphoreType.DMA(()),
                        pltpu.SemaphoreType.DMA(())],
    )
    return pl.pallas_call(
        _dispatch_kernel,
        grid_spec=grid_spec,
        out_shape=jax.ShapeDtypeStruct((n_rows, d), x.dtype),
        compiler_params=_params("arbitrary"),
        name="moe_dispatch",
    )(pos, pad_start, pad_len, n_valid, x)


def _expert_kernel(blk_expert_ref, next_expert_ref, n_valid_ref, x_ref, wg_ref, wu_ref, wd_ref, y_ref,
                   wg_s, wu_s, wd_s, wg_buf, wu_buf, wd_buf, slot_s, sem, *, layer):
    i = pl.program_id(0)

    def fetch(expert, slot):
        return [pltpu.make_async_copy(src.at[layer, expert], dst.at[slot], sem.at[slot])
                for src, dst in ((wg_ref, wg_buf), (wu_ref, wu_buf), (wd_ref, wd_buf))]

    @pl.when(i == 0)
    def _():
        slot_s[0] = 1
        for copy in fetch(blk_expert_ref[0], 0):
            copy.start()

    @pl.when((i == 0) | (blk_expert_ref[i] != blk_expert_ref[jnp.maximum(i - 1, 0)]))
    def _():
        slot = 1 - slot_s[0]
        slot_s[0] = slot
        for copy in fetch(blk_expert_ref[i], slot):
            copy.wait()
        wg_s[...] = wg_buf[slot].astype(BF16)
        wu_s[...] = wu_buf[slot].astype(BF16)
        wd_s[...] = wd_buf[slot].astype(BF16)

        @pl.when(next_expert_ref[i] >= 0)
        def _():
            for copy in fetch(next_expert_ref[i], 1 - slot):
                copy.start()

    @pl.when(i < n_valid_ref[0])
    def _():
        x = x_ref[...].astype(BF16)
        hid = _silu(jnp.dot(x, wg_s[...], preferred_element_type=F32)) * jnp.dot(
            x, wu_s[...], preferred_element_type=F32)
        y_ref[...] = _dot(hid, wd_s[...])

    @pl.when(i >= n_valid_ref[0])
    def _():
        y_ref[...] = jnp.zeros(y_ref.shape, y_ref.dtype)


def _experts(blk_expert, next_expert, n_valid, xs, wg, wu, wd, layer, n_blocks):
    d = xs.shape[1]
    f = wg.shape[-1]
    rows = MOE_BLOCK_ROWS
    n_rows = n_blocks * rows
    hbm = pl.BlockSpec(memory_space=pl.ANY)
    grid_spec = pltpu.PrefetchScalarGridSpec(
        num_scalar_prefetch=3,
        grid=(n_blocks,),
        in_specs=[pl.BlockSpec((rows, d), lambda i, be, ne, nv: (jnp.minimum(i, nv[0] - 1), 0)), hbm, hbm, hbm],
        out_specs=pl.BlockSpec((rows, d), lambda i, be, ne, nv: (i, 0)),
        scratch_shapes=[pltpu.VMEM((d, f), BF16), pltpu.VMEM((d, f), BF16), pltpu.VMEM((f, d), BF16),
                        pltpu.VMEM((2, d, f), F32), pltpu.VMEM((2, d, f), F32), pltpu.VMEM((2, f, d), F32),
                        pltpu.SMEM((1,), jnp.int32), pltpu.SemaphoreType.DMA((2,))],
    )
    return pl.pallas_call(
        functools.partial(_expert_kernel, layer=layer),
        grid_spec=grid_spec,
        out_shape=jax.ShapeDtypeStruct((n_rows, d), F32),
        compiler_params=_params("arbitrary"),
        name="moe_experts",
    )(blk_expert, next_expert, n_valid, xs, wg, wu, wd)


def _combine_kernel(pos_ref, ys_ref, w_ref, h_ref, gain_ref, bias_ref, y_ref, yb_ref, buf, sem):
    i = pl.program_id(0)
    n = pl.num_programs(0)
    rows = h_ref.shape[0]
    n_tokens = pos_ref.shape[0] // MOE_TOPK

    def issue(tile, slot):
        def body(r, _):
            for k in range(MOE_TOPK):
                p = pos_ref[k * n_tokens + tile * rows + r]
                pltpu.make_async_copy(ys_ref.at[pl.ds(p, 1)], buf.at[slot, k, pl.ds(r, 1)], sem.at[slot]).start()
            return 0

        lax.fori_loop(0, rows, body, 0, unroll=8)

    slot = i % 2

    @pl.when(i == 0)
    def _():
        issue(0, 0)

    @pl.when(i + 1 < n)
    def _():
        issue(i + 1, 1 - slot)

    for k in range(MOE_TOPK):
        pltpu.make_async_copy(ys_ref.at[pl.ds(0, rows)], buf.at[slot, k], sem.at[slot]).wait()
    eye = _iota((rows, rows), 0) == _iota((rows, rows), 1)
    w_col = [jnp.sum(jnp.where(eye, jnp.broadcast_to(w_ref[k:k + 1, :], (rows, rows)), 0.0), axis=1, keepdims=True)
             for k in range(MOE_TOPK)]
    ffn = w_col[0] * buf[slot, 0] + w_col[1] * buf[slot, 1]
    y = _layer_norm(DEEPNORM_ALPHA * h_ref[...] + ffn, gain_ref[...], bias_ref[...])
    y_ref[...] = y
    yb_ref[...] = y.astype(BF16)


def _combine_ln(pos, ys, w12, h, gain, bias, tm):
    t, d = h.shape
    grid_spec = pltpu.PrefetchScalarGridSpec(
        num_scalar_prefetch=1,
        grid=(t // tm,),
        in_specs=[pl.BlockSpec(memory_space=pl.ANY),
                  pl.BlockSpec((8, tm), lambda i, pos: (0, i)),
                  pl.BlockSpec((tm, d), lambda i, pos: (i, 0)),
                  pl.BlockSpec((1, d), lambda i, pos: (0, 0)),
                  pl.BlockSpec((1, d), lambda i, pos: (0, 0))],
        out_specs=[pl.BlockSpec((tm, d), lambda i, pos: (i, 0)), pl.BlockSpec((tm, d), lambda i, pos: (i, 0))],
        scratch_shapes=[pltpu.VMEM((2, MOE_TOPK, tm, d), F32), pltpu.SemaphoreType.DMA((2,))],
    )
    return pl.pallas_call(
        _combine_kernel,
        grid_spec=grid_spec,
        out_shape=[jax.ShapeDtypeStruct((t, d), F32), jax.ShapeDtypeStruct((t, d), BF16)],
        compiler_params=_params("arbitrary"),
        name="moe_combine_ln",
    )(pos, ys, w12, h, gain, bias)


def _moe_ln(h, logits, rbias, wg, wu, wd, layer, gain, bias):
    t, d = h.shape
    rows = MOE_BLOCK_ROWS
    n_blocks = (MOE_TOPK * t) // rows + N_EXPERTS
    pos_t, w_t, counts = _route_tokens(logits, rbias, 512)
    pos = pos_t[:MOE_TOPK].reshape(-1)
    counts = counts[:, 0]
    blocks_per_expert = (counts + rows - 1) // rows
    block_end = jnp.cumsum(blocks_per_expert)
    row_start = (block_end - blocks_per_expert) * rows
    n_valid = block_end[-1:].astype(jnp.int32)
    blk = jnp.arange(n_blocks, dtype=jnp.int32)
    blk_expert = jnp.sum(jnp.minimum(blk, n_valid - 1)[:, None] >= block_end[None, :], axis=1).astype(jnp.int32)
    pad_start = ((row_start + counts) // 8 * 8).astype(jnp.int32)
    pad_len = (row_start + blocks_per_expert * rows - pad_start).astype(jnp.int32)
    xs = _dispatch(pos, pad_start, pad_len, n_valid, h, n_blocks * rows, 512)
    experts = jnp.arange(N_EXPERTS, dtype=jnp.int32)
    later_used = (experts[None, :] > experts[:, None]) & (blocks_per_expert[None, :] > 0)
    next_used = jnp.min(jnp.where(later_used, experts[None, :], N_EXPERTS), axis=1)
    next_expert = jnp.where(next_used < N_EXPERTS, next_used, -1).astype(jnp.int32)[blk_expert]
    ys = _experts(blk_expert, next_expert, n_valid, xs, wg, wu, wd, layer, n_blocks)
    return _combine_ln(pos, ys, w_t, h, gain, bias, 256)


def _rope_tables(seq):
    inv_freq = ROPE_THETA ** (-jnp.arange(0, HEAD_DIM, 2, dtype=F32) / HEAD_DIM)
    ang = jnp.arange(seq, dtype=F32)[:, None] * inv_freq[None, :]
    cos, sin = jnp.cos(ang), jnp.sin(ang)
    return jnp.concatenate([cos, cos], axis=-1), jnp.concatenate([-sin, sin], axis=-1)


def _even_mixer(hb, b, s, w_in, conv_w, a_log, dt_bias, gdn_norm, hgrn_norm, lower_bound):
    gw = GDN_WIDTH
    n_small = 2 * GDN_HEADS
    tail0 = 4 * gw
    w_t = w_in.T
    w_main = jnp.concatenate([w_t[:tail0], w_t[tail0 + n_small:]], axis=0).astype(BF16)
    w_small = jnp.pad(w_t[tail0:tail0 + n_small], ((0, LANES - n_small), (0, 0))).astype(BF16)
    h3 = _matmul(hb, w_main, 1024, 1024, transposed=True).reshape(b, s, EVEN_MAIN)
    n_chunks = s // GDN_CHUNK
    small = _matmul(hb, w_small, 1024, LANES, transposed=True).reshape(b, s, LANES)
    to_rows = lambda a: a.transpose(0, 2, 1).reshape(b, GDN_HEADS, n_chunks, GDN_CHUNK)
    b_rows, a_rows = to_rows(small[..., :GDN_HEADS]), to_rows(small[..., GDN_HEADS:n_small])
    headvec = lambda v: jnp.broadcast_to(v.astype(F32)[:, None, None], (GDN_HEADS, 1, GDN_CHUNK))
    o_a = _gdn(h3, conv_w.astype(F32), a_rows, b_rows, headvec(a_log), headvec(dt_bias),
               gdn_norm.astype(F32).reshape(1, HEAD_DIM))
    o_b = _hgrn(h3, lower_bound.astype(F32).reshape(HGRN_HEADS, 1, HEAD_DIM),
                hgrn_norm.astype(F32).reshape(1, HEAD_DIM))
    return [o_a.reshape(b * s, GDN_WIDTH), o_b.reshape(b * s, HGRN_WIDTH)]


def _odd_mixer(hb, b, s, w_in, cos2, sin2):
    h = _matmul(hb, w_in.astype(BF16), 1024, 768)
    h3 = h.reshape(b, s, ODD_COLS)
    o_c = _dilated(h3, cos2, sin2)
    o_d = _moba(h3, cos2, sin2)
    return [o_c.reshape(b * s, -1), o_d.reshape(b * s, -1)]


def kernel(x, ev_w_in, ev_conv_w, ev_a_log, ev_dt_bias, ev_gdn_norm, ev_hgrn_norm, hgrn_lb_logits, ev_w_out,
           od_w_in, od_w_out, router_w, router_bias, moe_w_gate, moe_w_up, moe_w_down, ln_gain, ln_bias):
    b, s, d = x.shape
    t = b * s
    cos2, sin2 = _rope_tables(s)
    lower_bounds = jnp.cumsum(jax.nn.softmax(hgrn_lb_logits.astype(F32), axis=0), axis=0)
    rw = jnp.pad(router_w.astype(F32), ((0, 0), (0, LANES - N_EXPERTS)))
    rbias = jnp.pad(router_bias.astype(F32), (0, LANES - N_EXPERTS)).reshape(LANES, 1)
    vec = lambda v: v.astype(F32).reshape(1, d)

    h = x.reshape(t, d)
    hb = h
    for layer in range(DEPTH):
        if layer % 2 == 0:
            e = layer // 2
            parts = _even_mixer(hb, b, s, ev_w_in[e], ev_conv_w[e], ev_a_log[e], ev_dt_bias[e], ev_gdn_norm[e],
                                ev_hgrn_norm[e], lower_bounds[layer])
            w_out = ev_w_out[e].astype(BF16)
        else:
            o = layer // 2
            parts = _odd_mixer(hb, b, s, od_w_in[o], cos2, sin2)
            w_out = od_w_out[o].astype(BF16)
        splits = np.cumsum([p.shape[1] for p in parts])[:-1]
        weights = jnp.split(w_out, splits, axis=0)
        h, hb, logits = _out_ln(parts, weights, h, vec(ln_gain[layer, 0]), vec(ln_bias[layer, 0]), rw, 256)
        h, hb = _moe_ln(h, logits, rbias, moe_w_gate, moe_w_up, moe_w_down, layer,
                        vec(ln_gain[layer, 1]), vec(ln_bias[layer, 1]))
    return h.reshape(b, s, d)
```

```python
import functools
import math

import jax
import jax.numpy as jnp
import numpy as np
from jax import lax
from jax.experimental import pallas as pl
from jax.experimental.pallas import tpu as pltpu

F32 = jnp.float32
BF16 = jnp.bfloat16

D_MODEL = 2048
DEPTH = 2
HEAD_DIM = 128
GDN_HEADS = 8
GDN_CONV = 4
GDN_CHUNK = 64
GDN_WIDTH = GDN_HEADS * HEAD_DIM
HGRN_HEADS = 8
HGRN_CHUNK = 16
HGRN_WIDTH = HGRN_HEADS * HEAD_DIM
DIL_GROUPS = ((128, 1), (512, 4), (2048, 16))
DIL_HEADS_PER_GROUP = 4
DIL_HEADS = len(DIL_GROUPS) * DIL_HEADS_PER_GROUP
MOBA_HEADS = 4
MOBA_BLOCK = 256
MOBA_TOPK = 3
ROPE_THETA = 10000.0
N_EXPERTS = 16
N_EXPERT_GROUPS = 4
EXPERTS_PER_GROUP = N_EXPERTS // N_EXPERT_GROUPS
D_EXPERT = 512
MOE_TOPK = 2
MOE_BLOCK_ROWS = 512
DEEPNORM_ALPHA = (2.0 * DEPTH) ** 0.25
LN_EPS = 1e-5
RMS_EPS = 1e-6
NEG_INF = -1e30

LANES = 128
VMEM_LIMIT = 56 * 1024 * 1024
ATT_BLOCK = 256
MOBA_Q_GROUPS = ((7, 0, 6, 1), (5, 2, 4, 3))
DIL_BLOCK = 128
GDN_HEADS_PER_STEP = 2
GDN_GROUP = 16
HGRN_GROUP = 4
HGRN_ROWS = 256

ODD_COLS = 3 * DIL_HEADS * HEAD_DIM + 3 * MOBA_HEADS * HEAD_DIM


def _dot(a, b):
    return jnp.dot(a.astype(BF16), b.astype(BF16), preferred_element_type=F32)


def _dot_nt(a, b):
    return lax.dot_general(a.astype(BF16), b.astype(BF16), (((1,), (1,)), ((), ())),
                           preferred_element_type=F32)


def _dot_tn(a, b):
    return lax.dot_general(a.astype(BF16), b.astype(BF16), (((0,), (0,)), ((), ())),
                           preferred_element_type=F32)


def _dot_hi(a, b):
    return jnp.dot(a, b, preferred_element_type=F32, precision=lax.Precision.HIGHEST)


def _dot_nt_hi(a, b):
    return lax.dot_general(a, b, (((1,), (1,)), ((), ())), preferred_element_type=F32,
                           precision=lax.Precision.HIGHEST)


def _dot3(a, b):
    a_hi = a.astype(BF16)
    b_hi = b.astype(BF16)
    a_lo = (a - a_hi.astype(F32)).astype(BF16)
    b_lo = (b - b_hi.astype(F32)).astype(BF16)
    dot = functools.partial(jnp.dot, preferred_element_type=F32)
    return dot(a_hi, b_hi) + (dot(a_hi, b_lo) + dot(a_lo, b_hi))


def _dot_sel(sel, x):
    dot = functools.partial(jnp.dot, preferred_element_type=F32)
    x_hi = x.astype(BF16)
    r1 = x - x_hi.astype(F32)
    x_mid = r1.astype(BF16)
    x_lo = (r1 - x_mid.astype(F32)).astype(BF16)
    return dot(sel, x_hi) + (dot(sel, x_mid) + dot(sel, x_lo))


_dot_inv = _dot


def _silu(x):
    return x * jax.nn.sigmoid(x)


def _iota(shape, dim):
    return lax.broadcasted_iota(jnp.int32, shape, dim)


def _params(*sem):
    return pltpu.CompilerParams(dimension_semantics=sem, vmem_limit_bytes=VMEM_LIMIT)


def _mm_kernel(x_ref, w_ref, o_ref, *, transposed):
    x = x_ref[...].astype(BF16)
    prod = _dot_nt(x, w_ref[...]) if transposed else jnp.dot(x, w_ref[...], preferred_element_type=F32)
    o_ref[...] = prod.astype(o_ref.dtype)


def _matmul(x, w, tm, tn, transposed=False):
    m, k = x.shape
    n = w.shape[0] if transposed else w.shape[1]
    assert m % tm == 0 and n % tn == 0
    w_spec = pl.BlockSpec((tn, k), lambda i, j: (j, 0)) if transposed else pl.BlockSpec((k, tn), lambda i, j: (0, j))
    return pl.pallas_call(
        functools.partial(_mm_kernel, transposed=transposed),
        grid=(m // tm, n // tn),
        in_specs=[pl.BlockSpec((tm, k), lambda i, j: (i, 0)), w_spec],
        out_specs=pl.BlockSpec((tm, tn), lambda i, j: (i, j)),
        out_shape=jax.ShapeDtypeStruct((m, n), F32),
        compiler_params=_params("parallel", "parallel"),
        name="in_proj",
    )(x, w)


def _gdn_kernel(q_ref, k_ref, v_ref, z_ref, cwq_ref, cwk_ref, cwv_ref, a_ref, b_ref, alog_ref, dt_ref,
                gn_ref, o_ref, pad_s, q_s, k_s, v_s, gcum_s, beta_s, qe_s, ob_s, sm_s, sa_s):
    seq = q_ref.shape[0]
    c = GDN_CHUNK
    n_chunks = seq // c
    rows = 256
    heads = range(GDN_HEADS_PER_STEP)
    lanes = [slice(hh * HEAD_DIM, (hh + 1) * HEAD_DIM) for hh in heads]

    def conv_norm(hh):
        pad_s[pl.ds(0, 8), :] = jnp.zeros((8, HEAD_DIM), F32)
        for x_ref, cw_ref, dst, mode in ((q_ref, cwq_ref, q_s, "q"), (k_ref, cwk_ref, k_s, "k"),
                                         (v_ref, cwv_ref, v_s, "v")):
            pad_s[pl.ds(8, seq), :] = x_ref[:, lanes[hh]]
            cw = cw_ref[:, lanes[hh]]
            for r in range(seq // rows):
                acc = None
                for j in range(GDN_CONV):
                    tap = pad_s[pl.ds(8 + r * rows - (GDN_CONV - 1) + j, rows), :] * cw[j:j + 1, :]
                    acc = tap if acc is None else acc + tap
                y = _silu(acc)
                if mode != "v":
                    y = y * lax.rsqrt(jnp.sum(y * y, axis=-1, keepdims=True) + RMS_EPS)
                if mode == "q":
                    y = y * HEAD_DIM ** -0.5
                dst[pl.ds(r * rows, rows), :] = y

    upper = (_iota((c, c), 0) <= _iota((c, c), 1)).astype(F32)
    for hh in heads:
        g = -jnp.exp(alog_ref[hh]) * jax.nn.softplus(a_ref[hh] + dt_ref[hh])
        gcum_s[hh] = _dot_hi(g, upper)
        beta_s[hh] = jax.nn.sigmoid(b_ref[hh])

    ri = _iota((c, c), 0)
    ci = _iota((c, c), 1)
    eye = ri == ci
    strict = ri > ci
    incl = ri >= ci
    eye_f = eye.astype(F32)
    level1 = ri // 2 == ci // 2
    levels = []
    s = 2
    while s < c:
        levels.append((ri // (2 * s) == ci // (2 * s)) & ((ri // s) % 2 == 1) & ((ci // s) % 2 == 0))
        s *= 2

    dot = functools.partial(jnp.dot, preferred_element_type=F32)

    def to_col(row):
        return jnp.sum(jnp.where(eye, jnp.broadcast_to(row, (c, c)), 0.0), axis=1, keepdims=True)

    def prepare(i, _, hh):
        n0 = i * GDN_GROUP
        grp = range(GDN_GROUP)
        starts = [pl.multiple_of((n0 + j) * c, c) for j in grp]
        qc = [q_s[pl.ds(r0, c), :] for r0 in starts]
        kc = [k_s[pl.ds(r0, c), :] for r0 in starts]
        vc = [v_s[pl.ds(r0, c), :] for r0 in starts]
        g_row = [gcum_s[hh, pl.ds(n0 + j, 1), :] for j in grp]
        g_col = [to_col(g_row[j]) for j in grp]
        b_col = [to_col(beta_s[hh, pl.ds(n0 + j, 1), :]) for j in grp]
        decay = [jnp.exp(jnp.where(incl, g_col[j] - g_row[j], 0.0)) for j in grp]
        n_mat = [b_col[j] * jnp.where(strict, decay[j], 0.0) * _dot_nt(kc[j], kc[j]) for j in grp]
        inv = [eye_f - jnp.where(level1, n_mat[j], 0.0) for j in grp]
        for blk in levels:
            tmp = [_dot_inv(inv[j], jnp.where(blk, n_mat[j], 0.0)) for j in grp]
            inv = [inv[j] - _dot_inv(tmp[j], inv[j]) for j in grp]
        e_col = [jnp.exp(g_col[j]) for j in grp]
        sol = [_dot_inv(inv[j], jnp.concatenate([b_col[j] * vc[j], (b_col[j] * e_col[j]) * kc[j]], axis=1))
               for j in grp]
        qk = [(_dot_nt(qc[j], kc[j]) * jnp.where(incl, decay[j], 0.0)).astype(BF16) for j in grp]
        ub = [sol[j][:, :HEAD_DIM].astype(BF16) for j in grp]
        w = [sol[j][:, HEAD_DIM:].astype(BF16) for j in grp]
        kd = [(kc[j] * jnp.exp(g_row[j][:, c - 1:c] - g_col[j])).astype(BF16) for j in grp]
        q_eff = [(qc[j] * e_col[j] - dot(qk[j], w[j])).astype(BF16) for j in grp]
        o_base = [dot(qk[j], ub[j]) for j in grp]
        s_mat = [_dot_tn(kd[j], w[j]).astype(BF16) for j in grp]
        s_add = [_dot_tn(kd[j], ub[j]) for j in grp]
        for j, r0 in enumerate(starts):
            m0 = pl.multiple_of((n0 + j) * HEAD_DIM, HEAD_DIM)
            qe_s[hh, pl.ds(r0, c), :] = q_eff[j]
            ob_s[hh, pl.ds(r0, c), :] = o_base[j]
            sm_s[hh, pl.ds(m0, HEAD_DIM), :] = s_mat[j]
            sa_s[hh, pl.ds(m0, HEAD_DIM), :] = s_add[j]
        return 0

    for hh in heads:
        conv_norm(hh)
        lax.fori_loop(0, n_chunks // GDN_GROUP, functools.partial(prepare, hh=hh), 0)

    gn = gn_ref[...]

    def chunk(n, states):
        r0 = pl.multiple_of(n * c, c)
        m0 = pl.multiple_of(n * HEAD_DIM, HEAD_DIM)
        g_last = [gcum_s[hh, pl.ds(n, 1), :][:, c - 1:c] for hh in heads]
        lhs = [jnp.concatenate([qe_s[hh, pl.ds(r0, c), :], sm_s[hh, pl.ds(m0, HEAD_DIM), :]], axis=0)
               for hh in heads]
        prod = [dot(lhs[hh], states[hh].astype(BF16)) for hh in heads]
        for hh in heads:
            ob_s[hh, pl.ds(r0, c), :] = prod[hh][:c] + ob_s[hh, pl.ds(r0, c), :]
        return tuple(jnp.exp(g_last[hh]) * states[hh] - prod[hh][c:] + sa_s[hh, pl.ds(m0, HEAD_DIM), :]
                     for hh in heads)

    lax.fori_loop(0, n_chunks, chunk, tuple(jnp.zeros((HEAD_DIM, HEAD_DIM), F32) for _ in heads))

    for hh in heads:
        for r in range(seq // rows):
            sl = pl.ds(r * rows, rows)
            o = ob_s[hh, sl, :]
            o = o * lax.rsqrt(jnp.mean(o * o, axis=-1, keepdims=True) + RMS_EPS) * gn
            o_ref[sl, lanes[hh]] = (o * _silu(z_ref[sl, lanes[hh]])).astype(o_ref.dtype)


def _gdn(h3, conv_w, a_rows, b_rows, alog, dt, gn):
    b, s, _ = h3.shape
    hp = GDN_HEADS_PER_STEP
    nb = GDN_HEADS // hp
    wide = hp * HEAD_DIM
    n_chunks = s // GDN_CHUNK
    col = lambda off: pl.BlockSpec((None, s, wide), lambda bi, hi: (bi, 0, off + hi))
    cw = lambda off: pl.BlockSpec((GDN_CONV, wide), lambda bi, hi: (0, off + hi))
    rowspec = pl.BlockSpec((None, hp, n_chunks, GDN_CHUNK), lambda bi, hi: (bi, hi, 0, 0))
    headvec = pl.BlockSpec((hp, 1, GDN_CHUNK), lambda bi, hi: (hi, 0, 0))
    return pl.pallas_call(
        _gdn_kernel,
        grid=(b, nb),
        in_specs=[col(0), col(nb), col(2 * nb), col(3 * nb), cw(0), cw(nb), cw(2 * nb),
                  rowspec, rowspec, headvec, headvec,
                  pl.BlockSpec((1, HEAD_DIM), lambda bi, hi: (0, 0))],
        out_specs=pl.BlockSpec((None, s, wide), lambda bi, hi: (bi, 0, hi)),
        out_shape=jax.ShapeDtypeStruct((b, s, GDN_WIDTH), BF16),
        scratch_shapes=[pltpu.VMEM((s + 8, HEAD_DIM), F32), pltpu.VMEM((s, HEAD_DIM), F32),
                        pltpu.VMEM((s, HEAD_DIM), F32), pltpu.VMEM((s, HEAD_DIM), F32),
                        pltpu.VMEM((hp, n_chunks, GDN_CHUNK), F32), pltpu.VMEM((hp, n_chunks, GDN_CHUNK), F32),
                        pltpu.VMEM((hp, s, HEAD_DIM), BF16), pltpu.VMEM((hp, s, HEAD_DIM), F32),
                        pltpu.VMEM((hp, n_chunks * HEAD_DIM, HEAD_DIM), BF16),
                        pltpu.VMEM((hp, n_chunks * HEAD_DIM, HEAD_DIM), F32)],
        compiler_params=_params("parallel", "parallel"),
        name="gdn",
    )(h3, h3, h3, h3, conv_w, conv_w, conv_w, a_rows, b_rows, alog, dt, gn)


def _hgrn_kernel(q_ref, f_ref, i_ref, g_ref, lb_ref, hn_ref, o_ref):
    seq = q_ref.shape[0]
    c = HGRN_CHUNK
    rows = HGRN_ROWS
    ri = _iota((rows, rows), 0)
    ci = _iota((rows, rows), 1)
    causal = (ri // c == ci // c) & (ci <= ri)
    row_in_chunk = _iota((rows, HEAD_DIM), 0) % c
    lb = lb_ref[...]
    hn = hn_ref[...]

    chunks = [slice(j * c, (j + 1) * c) for j in range(rows // c)]
    grp = range(HGRN_GROUP)

    def chunk_scan(x, suffix):
        step = 1
        while step < c:
            if suffix:
                x = x + jnp.where(row_in_chunk < c - step, pltpu.roll(x, rows - step, axis=0), 0.0)
            else:
                x = x + jnp.where(row_in_chunk >= step, pltpu.roll(x, step, axis=0), 0.0)
            step *= 2
        return x

    def group(n, state_t):
        starts = [pl.multiple_of((n * HGRN_GROUP + j) * rows, rows) for j in grp]
        qc = [q_ref[pl.ds(r0, rows), :] for r0 in starts]
        ic = [i_ref[pl.ds(r0, rows), :].astype(BF16) for r0 in starts]
        f = [lb + (1.0 - lb) * jax.nn.sigmoid(f_ref[pl.ds(r0, rows), :]) for r0 in starts]
        log_f = [jnp.log(f[j]) for j in grp]
        bcum = [chunk_scan(log_f[j], False) for j in grp]
        to_end = [chunk_scan(log_f[j], True) - log_f[j] for j in grp]
        chunk_dec = [jnp.exp(bcum[j]) for j in grp]
        q_dec = [(qc[j] * chunk_dec[j]).astype(BF16) for j in grp]
        k_inv = [(1.0 - f[j]) * jnp.exp(-bcum[j]) for j in grp]
        k_dec = [((1.0 - f[j]) * jnp.exp(to_end[j])).astype(BF16) for j in grp]
        p = [jnp.where(causal, _dot_nt(q_dec[j], k_inv[j]), 0.0) for j in grp]
        o_intra = [_dot(p[j], ic[j]) for j in grp]
        updates = [[_dot_tn(ic[j][sl], k_dec[j][sl]) for sl in chunks] for j in grp]
        for j, r0 in enumerate(starts):
            outs = []
            for sl, upd in zip(chunks, updates[j]):
                outs.append(o_intra[j][sl] + _dot_nt(q_dec[j][sl], state_t))
                state_t = state_t * chunk_dec[j][sl.stop - 1:sl.stop] + upd
            o = jnp.concatenate(outs, axis=0)
            o = o * lax.rsqrt(jnp.mean(o * o, axis=-1, keepdims=True) + RMS_EPS) * hn
            o_ref[pl.ds(r0, rows), :] = (o * _silu(g_ref[pl.ds(r0, rows), :])).astype(o_ref.dtype)
        return state_t

    lax.fori_loop(0, seq // (rows * HGRN_GROUP), group, jnp.zeros((HEAD_DIM, HEAD_DIM), F32))


def _hgrn(h3, lb, hn):
    b, s, _ = h3.shape
    nh = HGRN_HEADS
    col = lambda off: pl.BlockSpec((None, s, HEAD_DIM), lambda bi, hi: (bi, 0, off + hi))
    return pl.pallas_call(
        _hgrn_kernel,
        grid=(b, nh),
        in_specs=[col(0), col(nh), col(2 * nh), col(3 * nh),
                  pl.BlockSpec((None, 1, HEAD_DIM), lambda bi, hi: (hi, 0, 0)),
                  pl.BlockSpec((1, HEAD_DIM), lambda bi, hi: (0, 0))],
        out_specs=pl.BlockSpec((None, s, HEAD_DIM), lambda bi, hi: (bi, 0, hi)),
        out_shape=jax.ShapeDtypeStruct((b, s, HGRN_WIDTH), BF16),
        compiler_params=_params("parallel", "parallel"),
        name="hgrn2",
    )(h3, h3, h3, h3, lb, hn)


def _rope(x, cos2, sin2):
    return x * cos2 + pltpu.roll(x, HEAD_DIM // 2, axis=1) * sin2


def _flash_step(q_blk, k_blk, v_blk, mask, carry):
    m, l, acc = carry
    s = jnp.where(mask, _dot_nt(q_blk, k_blk) * HEAD_DIM ** -0.5, NEG_INF)
    m_new = jnp.maximum(m, jnp.max(s, axis=-1, keepdims=True))
    alpha = jnp.exp(m - m_new)
    p = jnp.exp(s - m_new)
    l = alpha * l + jnp.sum(p, axis=-1, keepdims=True)
    acc = alpha * acc + _dot(p, v_blk)
    return m_new, l, acc


def _flash_init():
    blk = ATT_BLOCK
    return (jnp.full((blk, 1), NEG_INF, F32), jnp.zeros((blk, 1), F32), jnp.zeros((blk, HEAD_DIM), F32))


def _dilated_kernel(*refs):
    n_g = len(DIL_GROUPS)
    q_refs, k_refs, v_refs = refs[0:n_g], refs[n_g:2 * n_g], refs[2 * n_g:3 * n_g]
    cos_ref, sin_ref, o_ref = refs[3 * n_g:3 * n_g + 3]
    q_s, k_s, v_s, og_s, lse_s = refs[3 * n_g + 3:]
    seq = o_ref.shape[0]
    blk = DIL_BLOCK
    piece = 256
    grp = range(n_g)

    for gi, (window, d) in enumerate(DIL_GROUPS):
        assert window // d == blk
        seg = seq // d
        k_s[gi, pl.ds(0, blk), :] = jnp.zeros((blk, HEAD_DIM), BF16)
        v_s[gi, pl.ds(0, blk), :] = jnp.zeros((blk, HEAD_DIM), BF16)
        for r in range(d):
            for c0 in range(0, seg, piece):
                n = min(piece, seg)
                rows = pl.ds(r + c0 * d, n, stride=d) if d > 1 else pl.ds(c0, n)
                cos2 = cos_ref[rows, :]
                sin2 = sin_ref[rows, :]
                q_s[gi, pl.ds(r * seg + c0, n), :] = _rope(q_refs[gi][rows, :], cos2, sin2).astype(BF16)
                k_s[gi, pl.ds(blk + r * seg + c0, n), :] = _rope(k_refs[gi][rows, :], cos2, sin2).astype(BF16)
                v_s[gi, pl.ds(blk + r * seg + c0, n), :] = v_refs[gi][rows, :].astype(BF16)

    ri = _iota((blk, 2 * blk), 0)
    ci = _iota((blk, 2 * blk), 1)
    rel = ri + blk - ci
    in_window = (rel >= 0) & (rel <= blk)
    dot = functools.partial(jnp.dot, preferred_element_type=F32)

    def q_block(m, _):
        j0 = pl.multiple_of(m * blk, blk)
        segs = [seq // d for _, d in DIL_GROUPS]
        has_prev = [jnp.where(j0 % seg != 0, blk, 0) for seg in segs]
        mask = [in_window & (ci + has_prev[g] >= blk) for g in grp]
        q = [q_s[g, pl.ds(j0, blk), :] for g in grp]
        kw = [k_s[g, pl.ds(j0, 2 * blk), :] for g in grp]
        vw = [v_s[g, pl.ds(j0, 2 * blk), :] for g in grp]
        s = [jnp.where(mask[g], _dot_nt(q[g], kw[g]) * HEAD_DIM ** -0.5, NEG_INF) for g in grp]
        top = [jnp.max(s[g], axis=-1, keepdims=True) for g in grp]
        p = [jnp.exp(s[g] - top[g]) for g in grp]
        den = [jnp.sum(p[g], axis=-1, keepdims=True) for g in grp]
        o = [dot(p[g].astype(BF16), vw[g]) / den[g] for g in grp]
        lse = [top[g] + jnp.log(den[g]) for g in grp]
        for g, (_, d) in enumerate(DIL_GROUPS):
            seg = segs[g]
            dst = pl.ds((j0 % seg) * d + j0 // seg, blk, stride=d) if d > 1 else pl.ds(j0, blk)
            og_s[g, dst, :] = o[g]
            lse_s[g, dst, :] = jnp.broadcast_to(lse[g], (blk, HEAD_DIM))
        return 0

    lax.fori_loop(0, seq // blk, q_block, 0)

    for c0 in range(0, seq, piece):
        rows = pl.ds(c0, piece)
        lses = [lse_s[g, rows, :] for g in grp]
        top = functools.reduce(jnp.maximum, lses)
        wts = [jnp.exp(x - top) for x in lses]
        den = functools.reduce(lambda a, b: a + b, wts)
        o = functools.reduce(lambda a, b: a + b, [wts[g] * og_s[g, rows, :] for g in grp]) / den
        o_ref[rows, :] = o.astype(o_ref.dtype)


def _dilated(h3, cos2, sin2):
    b, s, _ = h3.shape
    hpg = DIL_HEADS_PER_GROUP
    n_g = len(DIL_GROUPS)
    col = lambda off: pl.BlockSpec((None, s, HEAD_DIM), lambda bi, hi: (bi, 0, off + hi))
    tab = pl.BlockSpec((s, HEAD_DIM), lambda bi, hi: (0, 0))
    specs = [col(part * DIL_HEADS + gi * hpg) for part in range(3) for gi in range(n_g)]
    return pl.pallas_call(
        _dilated_kernel,
        grid=(b, hpg),
        in_specs=specs + [tab, tab],
        out_specs=pl.BlockSpec((None, s, HEAD_DIM), lambda bi, hi: (bi, 0, hi)),
        out_shape=jax.ShapeDtypeStruct((b, s, hpg * HEAD_DIM), BF16),
        scratch_shapes=[pltpu.VMEM((n_g, s, HEAD_DIM), BF16), pltpu.VMEM((n_g, s + DIL_BLOCK, HEAD_DIM), BF16),
                        pltpu.VMEM((n_g, s + DIL_BLOCK, HEAD_DIM), BF16), pltpu.VMEM((n_g, s, HEAD_DIM), F32),
                        pltpu.VMEM((n_g, s, HEAD_DIM), F32)],
        compiler_params=_params("parallel", "parallel"),
        name="dilated_attention",
    )(*([h3] * (3 * n_g)), cos2, sin2)


def _moba_kernel(q_ref, k_ref, v_ref, cos_ref, sin_ref, o_ref, qf_s, q_s, k_s, v_s, km_s, sel_s):
    seq = o_ref.shape[0]
    blk = MOBA_BLOCK
    n_blk = seq // blk
    cos2 = cos_ref[...]
    sin2 = sin_ref[...]
    q = _rope(q_ref[...], cos2, sin2)
    qf_s[...] = q
    q_s[...] = q.astype(BF16)
    km_s[...] = jnp.zeros(km_s.shape, F32)
    for nb in range(n_blk):
        kb = _rope(k_ref[pl.ds(nb * blk, blk), :], cos2[nb * blk:(nb + 1) * blk], sin2[nb * blk:(nb + 1) * blk])
        k_s[pl.ds(nb * blk, blk), :] = kb.astype(BF16)
        km_s[pl.ds(nb, 1), :] = jnp.mean(kb, axis=0, keepdims=True)
    v_s[...] = v_ref[...].astype(BF16)

    lane = _iota((blk, LANES), 1).astype(F32)
    causal = _iota((blk, blk), 0) >= _iota((blk, blk), 1)
    all_true = _iota((blk, blk), 0) >= 0
    rows = lambda nb: pl.ds(nb * blk, blk)

    past = range(1, n_blk)
    km = km_s[...]
    gate = {qb: jnp.where(lane < qb, _dot_nt_hi(qf_s[rows(qb), :], km), -jnp.inf) for qb in past}
    sel = {qb: jnp.zeros((blk, LANES), F32) for qb in past}
    for _k in range(MOBA_TOPK):
        best = {qb: jnp.max(gate[qb], axis=-1, keepdims=True) for qb in past}
        first = {qb: jnp.min(jnp.where(gate[qb] == best[qb], lane, LANES), axis=-1, keepdims=True) for qb in past}
        pick = {qb: (lane == first[qb]) & (best[qb] > -jnp.inf) for qb in past}
        sel = {qb: jnp.where(pick[qb], 1.0, sel[qb]) for qb in past}
        gate = {qb: jnp.where(pick[qb], -jnp.inf, gate[qb]) for qb in past}
    for qb in past:
        sel_s[rows(qb), :] = sel[qb]

    assert sorted(qb for group in MOBA_Q_GROUPS for qb in group) == list(range(n_blk))
    for group in MOBA_Q_GROUPS:
        carry = {qb: _flash_init() for qb in group}
        for j in range(max(group) + 1):
            for qb in group:
                if j < qb:
                    mask = (sel_s[rows(qb), :][:, j:j + 1] > 0.0) & all_true
                elif j == qb:
                    mask = causal
                else:
                    continue
                carry[qb] = _flash_step(q_s[rows(qb), :], k_s[rows(j), :], v_s[rows(j), :], mask, carry[qb])
        for qb in group:
            m, l, acc = carry[qb]
            o_ref[rows(qb), :] = (acc / l).astype(o_ref.dtype)


def _moba(h3, cos2, sin2):
    b, s, _ = h3.shape
    base = 3 * DIL_HEADS
    col = lambda off: pl.BlockSpec((None, s, HEAD_DIM), lambda bi, hi: (bi, 0, base + off + hi))
    tab = pl.BlockSpec((s, HEAD_DIM), lambda bi, hi: (0, 0))
    return pl.pallas_call(
        _moba_kernel,
        grid=(b, MOBA_HEADS),
        in_specs=[col(0), col(MOBA_HEADS), col(2 * MOBA_HEADS), tab, tab],
        out_specs=pl.BlockSpec((None, s, HEAD_DIM), lambda bi, hi: (bi, 0, hi)),
        out_shape=jax.ShapeDtypeStruct((b, s, MOBA_HEADS * HEAD_DIM), BF16),
        scratch_shapes=[pltpu.VMEM((s, HEAD_DIM), F32), pltpu.VMEM((s, HEAD_DIM), BF16),
                        pltpu.VMEM((s, HEAD_DIM), BF16), pltpu.VMEM((s, HEAD_DIM), BF16),
                        pltpu.VMEM((LANES, HEAD_DIM), F32), pltpu.VMEM((s, LANES), F32)],
        compiler_params=_params("parallel", "parallel"),
        name="moba_attention",
    )(h3, h3, h3, cos2, sin2)


def _layer_norm(x, gain, bias):
    mu = jnp.mean(x, axis=-1, keepdims=True)
    xc = x - mu
    var = jnp.mean(xc * xc, axis=-1, keepdims=True)
    return xc * lax.rsqrt(var + LN_EPS) * gain + bias


def _out_ln_kernel(*refs, n_parts):
    o_refs = refs[0:n_parts]
    w_refs = refs[n_parts:2 * n_parts]
    h_ref, gain_ref, bias_ref, rw_ref, y_ref, yb_ref, logit_ref = refs[2 * n_parts:]
    mix = None
    for o_r, w_r in zip(o_refs, w_refs):
        part = jnp.dot(o_r[...], w_r[...], preferred_element_type=F32)
        mix = part if mix is None else mix + part
    y = _layer_norm(DEEPNORM_ALPHA * h_ref[...] + mix, gain_ref[...], bias_ref[...])
    y_ref[...] = y
    yb_ref[...] = y.astype(BF16)
    logit_ref[...] = _dot3(y, rw_ref[...])


def _out_ln(parts, weights, h, gain, bias, router_w, tm):
    t, d = h.shape
    n_parts = len(parts)
    row = lambda width: pl.BlockSpec((tm, width), lambda i: (i, 0))
    full = lambda a: pl.BlockSpec(a.shape, lambda i: (0, 0))
    return pl.pallas_call(
        functools.partial(_out_ln_kernel, n_parts=n_parts),
        grid=(t // tm,),
        in_specs=[row(p.shape[1]) for p in parts] + [full(w) for w in weights]
                 + [row(d), full(gain), full(bias), full(router_w)],
        out_specs=[row(d), row(d), row(LANES)],
        out_shape=[jax.ShapeDtypeStruct((t, d), F32), jax.ShapeDtypeStruct((t, d), BF16),
                   jax.ShapeDtypeStruct((t, LANES), F32)],
        compiler_params=_params("parallel"),
        name="out_proj_ln",
    )(*parts, *weights, h, gain, bias, router_w)


def _route(logits_t, rbias_col):
    row = _iota(logits_t.shape, 0)
    scores = jax.nn.sigmoid(logits_t)
    biased = scores + rbias_col

    def first_argmax(vals):
        best = jnp.max(vals, axis=0, keepdims=True)
        return best, jnp.min(jnp.where(vals == best, row, N_EXPERTS), axis=0, keepdims=True)

    best_score = None
    best_group = None
    for g in range(N_EXPERT_GROUPS):
        vals = jnp.where(row // EXPERTS_PER_GROUP == g, biased, -jnp.inf)
        top1, idx1 = first_argmax(vals)
        top2, _ = first_argmax(jnp.where(row == idx1, -jnp.inf, vals))
        score = top1 + top2
        if g == 0:
            best_score, best_group = score, jnp.zeros_like(idx1)
        else:
            better = score > best_score
            best_group = jnp.where(better, g, best_group)
            best_score = jnp.where(better, score, best_score)
    masked = jnp.where(row // EXPERTS_PER_GROUP == best_group, biased, NEG_INF)
    _, i1 = first_argmax(masked)
    _, i2 = first_argmax(jnp.where(row == i1, -jnp.inf, masked))
    s1 = jnp.sum(jnp.where(row == i1, scores, 0.0), axis=0, keepdims=True)
    s2 = jnp.sum(jnp.where(row == i2, scores, 0.0), axis=0, keepdims=True)
    tot = s1 + s2
    return i1, i2, s1 / tot, s2 / tot


def _route_kernel(logit_ref, rbias_ref, pos_ref, w_ref, cnt_ref, carry_s, total_s):
    sweep = pl.program_id(0)
    i = pl.program_id(1)
    tm = logit_ref.shape[0]
    block = float(MOE_BLOCK_ROWS)

    @pl.when((i == 0) & (sweep == 1))
    def _():
        total_s[...] = carry_s[...]

    @pl.when(i == 0)
    def _():
        carry_s[...] = jnp.zeros(carry_s.shape, F32)

    logits_t = logit_ref[...].T[:N_EXPERTS]
    i1, i2, w1, w2 = _route(logits_t, rbias_ref[...][:N_EXPERTS])
    row = _iota((N_EXPERTS, tm), 0)
    chosen = (row == i1) | (row == i2)
    earlier = (_iota((tm, tm), 0) < _iota((tm, tm), 1)).astype(BF16)
    before = jnp.dot(jnp.where(chosen, 1.0, 0.0).astype(BF16), earlier, preferred_element_type=F32) + carry_s[...]
    carry_s[...] = carry_s[...] + jnp.sum(jnp.where(chosen, 1.0, 0.0), axis=1, keepdims=True)

    @pl.when(sweep == 0)
    def _():
        pos_ref[...] = jnp.zeros(pos_ref.shape, pos_ref.dtype)
        w_ref[...] = jnp.zeros(w_ref.shape, w_ref.dtype)
        cnt_ref[...] = jnp.zeros(cnt_ref.shape, cnt_ref.dtype)

    @pl.when(sweep == 1)
    def _():
        total = total_s[...]
        blocks = jnp.floor((total + (block - 1.0)) * (1.0 / block))
        inclusive = (_iota((N_EXPERTS, N_EXPERTS), 1) <= _iota((N_EXPERTS, N_EXPERTS), 0)).astype(BF16)
        block_end = jnp.dot(inclusive, jnp.broadcast_to(blocks, (N_EXPERTS, LANES)).astype(BF16),
                            preferred_element_type=F32)[:, 0:1]
        slot = (block_end - blocks) * block + before
        p1 = jnp.sum(jnp.where(row == i1, slot, 0.0), axis=0, keepdims=True)
        p2 = jnp.sum(jnp.where(row == i2, slot, 0.0), axis=0, keepdims=True)
        out_row = _iota((8, tm), 0)
        pos_ref[...] = jnp.where(out_row == 0, p1, jnp.where(out_row == 1, p2, 0.0)).astype(jnp.int32)
        w_ref[...] = jnp.where(out_row == 0, w1, jnp.where(out_row == 1, w2, 0.0))
        cnt_ref[...] = jnp.broadcast_to(total, cnt_ref.shape).astype(jnp.int32)


def _route_tokens(logits, rbias_col, tm):
    t = logits.shape[0]
    tok = pl.BlockSpec((None, 8, tm), lambda s, i: (s, 0, i))
    pos, w, counts = pl.pallas_call(
        _route_kernel,
        grid=(2, t // tm),
        in_specs=[pl.BlockSpec((tm, LANES), lambda s, i: (i, 0)), pl.BlockSpec((LANES, 1), lambda s, i: (0, 0))],
        out_specs=[tok, tok, pl.BlockSpec((N_EXPERTS, LANES), lambda s, i: (0, 0))],
        out_shape=[jax.ShapeDtypeStruct((2, 8, t), jnp.int32), jax.ShapeDtypeStruct((2, 8, t), F32),
                   jax.ShapeDtypeStruct((N_EXPERTS, LANES), jnp.int32)],
        scratch_shapes=[pltpu.VMEM((N_EXPERTS, 1), F32), pltpu.VMEM((N_EXPERTS, 1), F32)],
        compiler_params=_params("arbitrary", "arbitrary"),
        name="moe_route",
    )(logits, rbias_col)
    return pos[1], w[1], counts


def _dispatch_kernel(pos_ref, pad_start_ref, pad_len_ref, n_valid_ref, x_ref, xs_ref, zero_s, sem, zero_sem):
    i = pl.program_id(0)
    rows = x_ref.shape[0]
    n_blocks = xs_ref.shape[0] // MOE_BLOCK_ROWS
    n_tokens = pos_ref.shape[0] // MOE_TOPK

    @pl.when(i == 0)
    def _():
        zero_s[...] = jnp.zeros(zero_s.shape, zero_s.dtype)
        sizes = [MOE_BLOCK_ROWS >> s for s in range(MOE_BLOCK_ROWS.bit_length() - 3)]
        pieces = [(e, size) for e in range(N_EXPERTS) for size in sizes]

        def piece_copy(e, size):
            start = pad_start_ref[e] + (pad_len_ref[e] & ~(2 * size - 1))
            dst = xs_ref.at[pl.ds(pl.multiple_of(start, 8), size)]
            return pltpu.make_async_copy(zero_s.at[pl.ds(0, size)], dst, zero_sem)

        def block_copy(j):
            return pltpu.make_async_copy(zero_s, xs_ref.at[pl.ds(j * MOE_BLOCK_ROWS, MOE_BLOCK_ROWS)], zero_sem)

        for action in ("start", "wait"):
            for e, size in pieces:
                @pl.when((pad_len_ref[e] & size) != 0)
                def _():
                    getattr(piece_copy(e, size), action)()
            for j in range(n_blocks - N_EXPERTS, n_blocks):
                @pl.when(j >= n_valid_ref[0])
                def _():
                    getattr(block_copy(j), action)()

    def issue(r, _):
        for k in range(MOE_TOPK):
            p = pos_ref[k * n_tokens + i * rows + r]
            pltpu.make_async_copy(x_ref.at[pl.ds(r, 1)], xs_ref.at[pl.ds(p, 1)], sem).start()
        return 0

    lax.fori_loop(0, rows, issue, 0, unroll=8)
    for _k in range(MOE_TOPK):
        pltpu.make_async_copy(x_ref, xs_ref.at[pl.ds(0, rows)], sem).wait()


def _dispatch(pos, pad_start, pad_len, n_valid, x, n_rows, tm):
    t, d = x.shape
    grid_spec = pltpu.PrefetchScalarGridSpec(
        num_scalar_prefetch=4,
        grid=(t // tm,),
        in_specs=[pl.BlockSpec((tm, d), lambda i, *_: (i, 0))],
        out_specs=pl.BlockSpec(memory_space=pl.ANY),
        scratch_shapes=[pltpu.VMEM((MOE_BLOCK_ROWS, d), x.dtype), pltpu.SemaphoreType.DMA(()),
                        pltpu.SemaphoreType.DMA(())],
    )
    return pl.pallas_call(
        _dispatch_kernel,
        grid_spec=grid_spec,
        out_shape=jax.ShapeDtypeStruct((n_rows, d), x.dtype),
        compiler_params=_params("arbitrary"),
        name="moe_dispatch",
    )(pos, pad_start, pad_len, n_valid, x)


def _expert_kernel(blk_expert_ref, next_expert_ref, n_valid_ref, x_ref, wg_ref, wu_ref, wd_ref, y_ref,
                   wg_s, wu_s, wd_s, wg_buf, wu_buf, wd_buf, slot_s, sem, *, layer):
    i = pl.program_id(0)

    def fetch(expert, slot):
        return [pltpu.make_async_copy(src.at[layer, expert], dst.at[slot], sem.at[slot])
                for src, dst in ((wg_ref, wg_buf), (wu_ref, wu_buf), (wd_ref, wd_buf))]

    @pl.when(i == 0)
    def _():
        slot_s[0] = 1
        for copy in fetch(blk_expert_ref[0], 0):
            copy.start()

    @pl.when((i == 0) | (blk_expert_ref[i] != blk_expert_ref[jnp.maximum(i - 1, 0)]))
    def _():
        slot = 1 - slot_s[0]
        slot_s[0] = slot
        for copy in fetch(blk_expert_ref[i], slot):
            copy.wait()
        wg_s[...] = wg_buf[slot].astype(BF16)
        wu_s[...] = wu_buf[slot].astype(BF16)
        wd_s[...] = wd_buf[slot].astype(BF16)

        @pl.when(next_expert_ref[i] >= 0)
        def _():
            for copy in fetch(next_expert_ref[i], 1 - slot):
                copy.start()

    @pl.when(i < n_valid_ref[0])
    def _():
        x = x_ref[...].astype(BF16)
        hid = _silu(jnp.dot(x, wg_s[...], preferred_element_type=F32)) * jnp.dot(
            x, wu_s[...], preferred_element_type=F32)
        y_ref[...] = _dot(hid, wd_s[...])

    @pl.when(i >= n_valid_ref[0])
    def _():
        y_ref[...] = jnp.zeros(y_ref.shape, y_ref.dtype)


def _experts(blk_expert, next_expert, n_valid, xs, wg, wu, wd, layer, n_blocks):
    d = xs.shape[1]
    f = wg.shape[-1]
    rows = MOE_BLOCK_ROWS
    n_rows = n_blocks * rows
    hbm = pl.BlockSpec(memory_space=pl.ANY)
    grid_spec = pltpu.PrefetchScalarGridSpec(
        num_scalar_prefetch=3,
        grid=(n_blocks,),
        in_specs=[pl.BlockSpec((rows, d), lambda i, be, ne, nv: (jnp.minimum(i, nv[0] - 1), 0)), hbm, hbm, hbm],
        out_specs=pl.BlockSpec((rows, d), lambda i, be, ne, nv: (i, 0)),
        scratch_shapes=[pltpu.VMEM((d, f), BF16), pltpu.VMEM((d, f), BF16), pltpu.VMEM((f, d), BF16),
                        pltpu.VMEM((2, d, f), F32), pltpu.VMEM((2, d, f), F32), pltpu.VMEM((2, f, d), F32),
                        pltpu.SMEM((1,), jnp.int32), pltpu.SemaphoreType.DMA((2,))],
    )
    return pl.pallas_call(
        functools.partial(_expert_kernel, layer=layer),
        grid_spec=grid_spec,
        out_shape=jax.ShapeDtypeStruct((n_rows, d), F32),
        compiler_params=_params("arbitrary"),
        name="moe_experts",
    )(blk_expert, next_expert, n_valid, xs, wg, wu, wd)


def _combine_kernel(pos_ref, ys_ref, w_ref, h_ref, gain_ref, bias_ref, y_ref, yb_ref, buf, sem):
    i = pl.program_id(0)
    n = pl.num_programs(0)
    rows = h_ref.shape[0]
    n_tokens = pos_ref.shape[0] // MOE_TOPK

    def issue(tile, slot):
        def body(r, _):
            for k in range(MOE_TOPK):
                p = pos_ref[k * n_tokens + tile * rows + r]
                pltpu.make_async_copy(ys_ref.at[pl.ds(p, 1)], buf.at[slot, k, pl.ds(r, 1)], sem.at[slot]).start()
            return 0

        lax.fori_loop(0, rows, body, 0, unroll=8)

    slot = i % 2

    @pl.when(i == 0)
    def _():
        issue(0, 0)

    @pl.when(i + 1 < n)
    def _():
        issue(i + 1, 1 - slot)

    for k in range(MOE_TOPK):
        pltpu.make_async_copy(ys_ref.at[pl.ds(0, rows)], buf.at[slot, k], sem.at[slot]).wait()
    eye = _iota((rows, rows), 0) == _iota((rows, rows), 1)
    w_col = [jnp.sum(jnp.where(eye, jnp.broadcast_to(w_ref[k:k + 1, :], (rows, rows)), 0.0), axis=1, keepdims=True)
             for k in range(MOE_TOPK)]
    ffn = w_col[0] * buf[slot, 0] + w_col[1] * buf[slot, 1]
    y = _layer_norm(DEEPNORM_ALPHA * h_ref[...] + ffn, gain_ref[...], bias_ref[...])
    y_ref[...] = y
    yb_ref[...] = y.astype(BF16)


def _combine_ln(pos, ys, w12, h, gain, bias, tm):
    t, d = h.shape
    grid_spec = pltpu.PrefetchScalarGridSpec(
        num_scalar_prefetch=1,
        grid=(t // tm,),
        in_specs=[pl.BlockSpec(memory_space=pl.ANY),
                  pl.BlockSpec((8, tm), lambda i, pos: (0, i)),
                  pl.BlockSpec((tm, d), lambda i, pos: (i, 0)),
                  pl.BlockSpec((1, d), lambda i, pos: (0, 0)),
                  pl.BlockSpec((1, d), lambda i, pos: (0, 0))],
        out_specs=[pl.BlockSpec((tm, d), lambda i, pos: (i, 0)), pl.BlockSpec((tm, d), lambda i, pos: (i, 0))],
        scratch_shapes=[pltpu.VMEM((2, MOE_TOPK, tm, d), F32), pltpu.SemaphoreType.DMA((2,))],
    )
    return pl.pallas_call(
        _combine_kernel,
        grid_spec=grid_spec,
        out_shape=[jax.ShapeDtypeStruct((t, d), F32), jax.ShapeDtypeStruct((t, d), BF16)],
        compiler_params=_params("arbitrary"),
        name="moe_combine_ln",
    )(pos, ys, w12, h, gain, bias)


def _moe_ln(h, logits, rbias, wg, wu, wd, layer, gain, bias):
    t, d = h.shape
    rows = MOE_BLOCK_ROWS
    n_blocks = (MOE_TOPK * t) // rows + N_EXPERTS
    pos_t, w_t, counts = _route_tokens(logits, rbias, 512)
    pos = pos_t[:MOE_TOPK].reshape(-1)
    counts = counts[:, 0]
    blocks_per_expert = (counts + rows - 1) // rows
    block_end = jnp.cumsum(blocks_per_expert)
    row_start = (block_end - blocks_per_expert) * rows
    n_valid = block_end[-1:].astype(jnp.int32)
    blk = jnp.arange(n_blocks, dtype=jnp.int32)
    blk_expert = jnp.sum(jnp.minimum(blk, n_valid - 1)[:, None] >= block_end[None, :], axis=1).astype(jnp.int32)
    pad_start = ((row_start + counts) // 8 * 8).astype(jnp.int32)
    pad_len = (row_start + blocks_per_expert * rows - pad_start).astype(jnp.int32)
    xs = _dispatch(pos, pad_start, pad_len, n_valid, h, n_blocks * rows, 512)
    experts = jnp.arange(N_EXPERTS, dtype=jnp.int32)
    later_used = (experts[None, :] > experts[:, None]) & (blocks_per_expert[None, :] > 0)
    next_used = jnp.min(jnp.where(later_used, experts[None, :], N_EXPERTS), axis=1)
    next_expert = jnp.where(next_used < N_EXPERTS, next_used, -1).astype(jnp.int32)[blk_expert]
    ys = _experts(blk_expert, next_expert, n_valid, xs, wg, wu, wd, layer, n_blocks)
    return _combine_ln(pos, ys, w_t, h, gain, bias, 256)


def _rope_tables(seq):
    inv_freq = ROPE_THETA ** (-jnp.arange(0, HEAD_DIM, 2, dtype=F32) / HEAD_DIM)
    ang = jnp.arange(seq, dtype=F32)[:, None] * inv_freq[None, :]
    cos, sin = jnp.cos(ang), jnp.sin(ang)
    return jnp.concatenate([cos, cos], axis=-1), jnp.concatenate([-sin, sin], axis=-1)


def _even_mixer(hb, b, s, w_in, conv_w, a_log, dt_bias, gdn_norm, hgrn_norm, lower_bound):
    gw = GDN_WIDTH
    n_small = 2 * GDN_HEADS
    tail0 = 4 * gw
    w_t = w_in.T
    w_small = jnp.pad(w_t[tail0:tail0 + n_small], ((0, LANES - n_small), (0, 0))).astype(BF16)
    h_a = _matmul(hb, w_t[:tail0].astype(BF16), 1024, 1024, transposed=True).reshape(b, s, tail0)
    h_b = _matmul(hb, w_t[tail0 + n_small:].astype(BF16), 1024, 1024, transposed=True).reshape(b, s, 4 * HGRN_WIDTH)
    n_chunks = s // GDN_CHUNK
    small = _matmul(hb, w_small, 1024, LANES, transposed=True).reshape(b, s, LANES)
    to_rows = lambda a: a.transpose(0, 2, 1).reshape(b, GDN_HEADS, n_chunks, GDN_CHUNK)
    b_rows, a_rows = to_rows(small[..., :GDN_HEADS]), to_rows(small[..., GDN_HEADS:n_small])
    headvec = lambda v: jnp.broadcast_to(v.astype(F32)[:, None, None], (GDN_HEADS, 1, GDN_CHUNK))
    o_a = _gdn(h_a, conv_w.astype(F32), a_rows, b_rows, headvec(a_log), headvec(dt_bias),
               gdn_norm.astype(F32).reshape(1, HEAD_DIM))
    o_b = _hgrn(h_b, lower_bound.astype(F32).reshape(HGRN_HEADS, 1, HEAD_DIM),
                hgrn_norm.astype(F32).reshape(1, HEAD_DIM))
    return [o_a.reshape(b * s, GDN_WIDTH), o_b.reshape(b * s, HGRN_WIDTH)]


def _odd_mixer(hb, b, s, w_in, cos2, sin2):
    h = _matmul(hb, w_in.astype(BF16), 1024, 768)
    h3 = h.reshape(b, s, ODD_COLS)
    o_c = _dilated(h3, cos2, sin2)
    o_d = _moba(h3, cos2, sin2)
    return [o_c.reshape(b * s, -1), o_d.reshape(b * s, -1)]


def kernel(x, ev_w_in, ev_conv_w, ev_a_log, ev_dt_bias, ev_gdn_norm, ev_hgrn_norm, hgrn_lb_logits, ev_w_out,
           od_w_in, od_w_out, router_w, router_bias, moe_w_gate, moe_w_up, moe_w_down, ln_gain, ln_bias):
    b, s, d = x.shape
    t = b * s
    cos2, sin2 = _rope_tables(s)
    lower_bounds = jnp.cumsum(jax.nn.softmax(hgrn_lb_logits.astype(F32), axis=0), axis=0)
    rw = jnp.pad(router_w.astype(F32), ((0, 0), (0, LANES - N_EXPERTS)))
    rbias = jnp.pad(router_bias.astype(F32), (0, LANES - N_EXPERTS)).reshape(LANES, 1)
    vec = lambda v: v.astype(F32).reshape(1, d)

    h = x.reshape(t, d)
    hb = h
    for layer in range(DEPTH):
        if layer % 2 == 0:
            e = layer // 2
            parts = _even_mixer(hb, b, s, ev_w_in[e], ev_conv_w[e], ev_a_log[e], ev_dt_bias[e], ev_gdn_norm[e],
                                ev_hgrn_norm[e], lower_bounds[layer])
            w_out = ev_w_out[e].astype(BF16)
        else:
            o = layer // 2
            parts = _odd_mixer(hb, b, s, od_w_in[o], cos2, sin2)
            w_out = od_w_out[o].astype(BF16)
        splits = np.cumsum([p.shape[1] for p in parts])[:-1]
        weights = jnp.split(w_out, splits, axis=0)
        h, hb, logits = _out_ln(parts, weights, h, vec(ln_gain[layer, 0]), vec(ln_bias[layer, 0]), rw, 256)
        h, hb = _moe_ln(h, logits, rbias, moe_w_gate, moe_w_up, moe_w_down, layer,
                        vec(ln_gain[layer, 1]), vec(ln_bias[layer, 1]))
    return h.reshape(b, s, d)
```

```python
import functools
import math

import jax
import jax.numpy as jnp
import numpy as np
from jax import lax
from jax.experimental import pallas as pl
from jax.experimental.pallas import tpu as pltpu

F32 = jnp.float32
BF16 = jnp.bfloat16

D_MODEL = 2048
DEPTH = 2
HEAD_DIM = 128
GDN_HEADS = 8
GDN_CONV = 4
GDN_CHUNK = 64
GDN_WIDTH = GDN_HEADS * HEAD_DIM
HGRN_HEADS = 8
HGRN_CHUNK = 16
HGRN_WIDTH = HGRN_HEADS * HEAD_DIM
DIL_GROUPS = ((128, 1), (512, 4), (2048, 16))
DIL_HEADS_PER_GROUP = 4
DIL_HEADS = len(DIL_GROUPS) * DIL_HEADS_PER_GROUP
MOBA_HEADS = 4
MOBA_BLOCK = 256
MOBA_TOPK = 3
ROPE_THETA = 10000.0
N_EXPERTS = 16
N_EXPERT_GROUPS = 4
EXPERTS_PER_GROUP = N_EXPERTS // N_EXPERT_GROUPS
D_EXPERT = 512
MOE_TOPK = 2
MOE_BLOCK_ROWS = 512
MOE_TILE = 512
MOE_ALIGN = 16
DEEPNORM_ALPHA = (2.0 * DEPTH) ** 0.25
LN_EPS = 1e-5
RMS_EPS = 1e-6
NEG_INF = -1e30

LANES = 128
VMEM_LIMIT = 56 * 1024 * 1024
ATT_BLOCK = 256
MOBA_Q_GROUPS = ((7, 0, 6, 1), (5, 2, 4, 3))
DIL_BLOCK = 128
GDN_HEADS_PER_STEP = 2
GDN_GROUP = 16
HGRN_GROUP = 4
HGRN_ROWS = 256

ODD_COLS = 3 * DIL_HEADS * HEAD_DIM + 3 * MOBA_HEADS * HEAD_DIM


def _dot(a, b):
    return jnp.dot(a.astype(BF16), b.astype(BF16), preferred_element_type=F32)


def _dot_nt(a, b):
    return lax.dot_general(a.astype(BF16), b.astype(BF16), (((1,), (1,)), ((), ())),
                           preferred_element_type=F32)


def _dot_tn(a, b):
    return lax.dot_general(a.astype(BF16), b.astype(BF16), (((0,), (0,)), ((), ())),
                           preferred_element_type=F32)


def _dot_hi(a, b):
    return jnp.dot(a, b, preferred_element_type=F32, precision=lax.Precision.HIGHEST)


def _dot_nt_hi(a, b):
    return lax.dot_general(a, b, (((1,), (1,)), ((), ())), preferred_element_type=F32,
                           precision=lax.Precision.HIGHEST)


def _dot3(a, b):
    a_hi = a.astype(BF16)
    b_hi = b.astype(BF16)
    a_lo = (a - a_hi.astype(F32)).astype(BF16)
    b_lo = (b - b_hi.astype(F32)).astype(BF16)
    dot = functools.partial(jnp.dot, preferred_element_type=F32)
    return dot(a_hi, b_hi) + (dot(a_hi, b_lo) + dot(a_lo, b_hi))


def _dot_sel(sel, x):
    dot = functools.partial(jnp.dot, preferred_element_type=F32)
    x_hi = x.astype(BF16)
    r1 = x - x_hi.astype(F32)
    x_mid = r1.astype(BF16)
    x_lo = (r1 - x_mid.astype(F32)).astype(BF16)
    return dot(sel, x_hi) + (dot(sel, x_mid) + dot(sel, x_lo))


_dot_inv = _dot


def _silu(x):
    return x * jax.nn.sigmoid(x)


def _iota(shape, dim):
    return lax.broadcasted_iota(jnp.int32, shape, dim)


def _params(*sem):
    return pltpu.CompilerParams(dimension_semantics=sem, vmem_limit_bytes=VMEM_LIMIT)


def _mm_kernel(x_ref, w_ref, o_ref, *, transposed):
    x = x_ref[...].astype(BF16)
    prod = _dot_nt(x, w_ref[...]) if transposed else jnp.dot(x, w_ref[...], preferred_element_type=F32)
    o_ref[...] = prod.astype(o_ref.dtype)


def _matmul(x, w, tm, tn, transposed=False):
    m, k = x.shape
    n = w.shape[0] if transposed else w.shape[1]
    assert m % tm == 0 and n % tn == 0
    w_spec = pl.BlockSpec((tn, k), lambda i, j: (j, 0)) if transposed else pl.BlockSpec((k, tn), lambda i, j: (0, j))
    return pl.pallas_call(
        functools.partial(_mm_kernel, transposed=transposed),
        grid=(m // tm, n // tn),
        in_specs=[pl.BlockSpec((tm, k), lambda i, j: (i, 0)), w_spec],
        out_specs=pl.BlockSpec((tm, tn), lambda i, j: (i, j)),
        out_shape=jax.ShapeDtypeStruct((m, n), F32),
        compiler_params=_params("parallel", "parallel"),
        name="in_proj",
    )(x, w)


def _gdn_kernel(q_ref, k_ref, v_ref, z_ref, cwq_ref, cwk_ref, cwv_ref, a_ref, b_ref, alog_ref, dt_ref,
                gn_ref, o_ref, pad_s, q_s, k_s, v_s, gcum_s, beta_s, qe_s, ob_s, sm_s, sa_s):
    seq = q_ref.shape[0]
    c = GDN_CHUNK
    n_chunks = seq // c
    rows = 256
    heads = range(GDN_HEADS_PER_STEP)
    lanes = [slice(hh * HEAD_DIM, (hh + 1) * HEAD_DIM) for hh in heads]

    def conv_norm(hh):
        pad_s[pl.ds(0, 8), :] = jnp.zeros((8, HEAD_DIM), F32)
        for x_ref, cw_ref, dst, mode in ((q_ref, cwq_ref, q_s, "q"), (k_ref, cwk_ref, k_s, "k"),
                                         (v_ref, cwv_ref, v_s, "v")):
            pad_s[pl.ds(8, seq), :] = x_ref[:, lanes[hh]]
            cw = cw_ref[:, lanes[hh]]
            for r in range(seq // rows):
                acc = None
                for j in range(GDN_CONV):
                    tap = pad_s[pl.ds(8 + r * rows - (GDN_CONV - 1) + j, rows), :] * cw[j:j + 1, :]
                    acc = tap if acc is None else acc + tap
                y = _silu(acc)
                if mode != "v":
                    y = y * lax.rsqrt(jnp.sum(y * y, axis=-1, keepdims=True) + RMS_EPS)
                if mode == "q":
                    y = y * HEAD_DIM ** -0.5
                dst[pl.ds(r * rows, rows), :] = y

    upper = (_iota((c, c), 0) <= _iota((c, c), 1)).astype(F32)
    for hh in heads:
        g = -jnp.exp(alog_ref[hh]) * jax.nn.softplus(a_ref[hh] + dt_ref[hh])
        gcum_s[hh] = _dot_hi(g, upper)
        beta_s[hh] = jax.nn.sigmoid(b_ref[hh])

    ri = _iota((c, c), 0)
    ci = _iota((c, c), 1)
    eye = ri == ci
    strict = ri > ci
    incl = ri >= ci
    eye_f = eye.astype(F32)
    level1 = ri // 2 == ci // 2
    levels = []
    s = 2
    while s < c:
        levels.append((ri // (2 * s) == ci // (2 * s)) & ((ri // s) % 2 == 1) & ((ci // s) % 2 == 0))
        s *= 2

    dot = functools.partial(jnp.dot, preferred_element_type=F32)

    def to_col(row):
        return jnp.sum(jnp.where(eye, jnp.broadcast_to(row, (c, c)), 0.0), axis=1, keepdims=True)

    def prepare(i, _, hh):
        n0 = i * GDN_GROUP
        grp = range(GDN_GROUP)
        starts = [pl.multiple_of((n0 + j) * c, c) for j in grp]
        qc = [q_s[pl.ds(r0, c), :] for r0 in starts]
        kc = [k_s[pl.ds(r0, c), :] for r0 in starts]
        vc = [v_s[pl.ds(r0, c), :] for r0 in starts]
        g_row = [gcum_s[hh, pl.ds(n0 + j, 1), :] for j in grp]
        g_col = [to_col(g_row[j]) for j in grp]
        b_col = [to_col(beta_s[hh, pl.ds(n0 + j, 1), :]) for j in grp]
        decay = [jnp.exp(jnp.where(incl, g_col[j] - g_row[j], 0.0)) for j in grp]
        n_mat = [b_col[j] * jnp.where(strict, decay[j], 0.0) * _dot_nt(kc[j], kc[j]) for j in grp]
        inv = [eye_f - jnp.where(level1, n_mat[j], 0.0) for j in grp]
        for blk in levels:
            tmp = [_dot_inv(inv[j], jnp.where(blk, n_mat[j], 0.0)) for j in grp]
            inv = [inv[j] - _dot_inv(tmp[j], inv[j]) for j in grp]
        e_col = [jnp.exp(g_col[j]) for j in grp]
        sol = [_dot_inv(inv[j], jnp.concatenate([b_col[j] * vc[j], (b_col[j] * e_col[j]) * kc[j]], axis=1))
               for j in grp]
        qk = [(_dot_nt(qc[j], kc[j]) * jnp.where(incl, decay[j], 0.0)).astype(BF16) for j in grp]
        ub = [sol[j][:, :HEAD_DIM].astype(BF16) for j in grp]
        w = [sol[j][:, HEAD_DIM:].astype(BF16) for j in grp]
        kd = [(kc[j] * jnp.exp(g_row[j][:, c - 1:c] - g_col[j])).astype(BF16) for j in grp]
        q_eff = [(qc[j] * e_col[j] - dot(qk[j], w[j])).astype(BF16) for j in grp]
        o_base = [dot(qk[j], ub[j]) for j in grp]
        s_mat = [_dot_tn(kd[j], w[j]).astype(BF16) for j in grp]
        s_add = [_dot_tn(kd[j], ub[j]) for j in grp]
        for j, r0 in enumerate(starts):
            m0 = pl.multiple_of((n0 + j) * HEAD_DIM, HEAD_DIM)
            qe_s[hh, pl.ds(r0, c), :] = q_eff[j]
            ob_s[hh, pl.ds(r0, c), :] = o_base[j]
            sm_s[hh, pl.ds(m0, HEAD_DIM), :] = s_mat[j]
            sa_s[hh, pl.ds(m0, HEAD_DIM), :] = s_add[j]
        return 0

    for hh in heads:
        conv_norm(hh)
        lax.fori_loop(0, n_chunks // GDN_GROUP, functools.partial(prepare, hh=hh), 0)

    gn = gn_ref[...]

    def chunk(n, states):
        r0 = pl.multiple_of(n * c, c)
        m0 = pl.multiple_of(n * HEAD_DIM, HEAD_DIM)
        g_last = [gcum_s[hh, pl.ds(n, 1), :][:, c - 1:c] for hh in heads]
        lhs = [jnp.concatenate([qe_s[hh, pl.ds(r0, c), :], sm_s[hh, pl.ds(m0, HEAD_DIM), :]], axis=0)
               for hh in heads]
        prod = [dot(lhs[hh], states[hh].astype(BF16)) for hh in heads]
        for hh in heads:
            ob_s[hh, pl.ds(r0, c), :] = prod[hh][:c] + ob_s[hh, pl.ds(r0, c), :]
        return tuple(jnp.exp(g_last[hh]) * states[hh] - prod[hh][c:] + sa_s[hh, pl.ds(m0, HEAD_DIM), :]
                     for hh in heads)

    lax.fori_loop(0, n_chunks, chunk, tuple(jnp.zeros((HEAD_DIM, HEAD_DIM), F32) for _ in heads))

    for hh in heads:
        for r in range(seq // rows):
            sl = pl.ds(r * rows, rows)
            o = ob_s[hh, sl, :]
            o = o * lax.rsqrt(jnp.mean(o * o, axis=-1, keepdims=True) + RMS_EPS) * gn
            o_ref[sl, lanes[hh]] = (o * _silu(z_ref[sl, lanes[hh]])).astype(o_ref.dtype)


def _gdn(h3, conv_w, a_rows, b_rows, alog, dt, gn):
    b, s, _ = h3.shape
    hp = GDN_HEADS_PER_STEP
    nb = GDN_HEADS // hp
    wide = hp * HEAD_DIM
    n_chunks = s // GDN_CHUNK
    col = lambda off: pl.BlockSpec((None, s, wide), lambda bi, hi: (bi, 0, off + hi))
    cw = lambda off: pl.BlockSpec((GDN_CONV, wide), lambda bi, hi: (0, off + hi))
    rowspec = pl.BlockSpec((None, hp, n_chunks, GDN_CHUNK), lambda bi, hi: (bi, hi, 0, 0))
    headvec = pl.BlockSpec((hp, 1, GDN_CHUNK), lambda bi, hi: (hi, 0, 0))
    return pl.pallas_call(
        _gdn_kernel,
        grid=(b, nb),
        in_specs=[col(0), col(nb), col(2 * nb), col(3 * nb), cw(0), cw(nb), cw(2 * nb),
                  rowspec, rowspec, headvec, headvec,
                  pl.BlockSpec((1, HEAD_DIM), lambda bi, hi: (0, 0))],
        out_specs=pl.BlockSpec((None, s, wide), lambda bi, hi: (bi, 0, hi)),
        out_shape=jax.ShapeDtypeStruct((b, s, GDN_WIDTH), BF16),
        scratch_shapes=[pltpu.VMEM((s + 8, HEAD_DIM), F32), pltpu.VMEM((s, HEAD_DIM), F32),
                        pltpu.VMEM((s, HEAD_DIM), F32), pltpu.VMEM((s, HEAD_DIM), F32),
                        pltpu.VMEM((hp, n_chunks, GDN_CHUNK), F32), pltpu.VMEM((hp, n_chunks, GDN_CHUNK), F32),
                        pltpu.VMEM((hp, s, HEAD_DIM), BF16), pltpu.VMEM((hp, s, HEAD_DIM), F32),
                        pltpu.VMEM((hp, n_chunks * HEAD_DIM, HEAD_DIM), BF16),
                        pltpu.VMEM((hp, n_chunks * HEAD_DIM, HEAD_DIM), F32)],
        compiler_params=_params("parallel", "parallel"),
        name="gdn",
    )(h3, h3, h3, h3, conv_w, conv_w, conv_w, a_rows, b_rows, alog, dt, gn)


def _hgrn_kernel(q_ref, f_ref, i_ref, g_ref, lb_ref, hn_ref, o_ref):
    seq = q_ref.shape[0]
    c = HGRN_CHUNK
    rows = HGRN_ROWS
    ri = _iota((rows, rows), 0)
    ci = _iota((rows, rows), 1)
    causal = (ri // c == ci // c) & (ci <= ri)
    row_in_chunk = _iota((rows, HEAD_DIM), 0) % c
    lb = lb_ref[...]
    hn = hn_ref[...]

    chunks = [slice(j * c, (j + 1) * c) for j in range(rows // c)]
    grp = range(HGRN_GROUP)

    def chunk_scan(x, suffix):
        step = 1
        while step < c:
            if suffix:
                x = x + jnp.where(row_in_chunk < c - step, pltpu.roll(x, rows - step, axis=0), 0.0)
            else:
                x = x + jnp.where(row_in_chunk >= step, pltpu.roll(x, step, axis=0), 0.0)
            step *= 2
        return x

    def group(n, state_t):
        starts = [pl.multiple_of((n * HGRN_GROUP + j) * rows, rows) for j in grp]
        qc = [q_ref[pl.ds(r0, rows), :] for r0 in starts]
        ic = [i_ref[pl.ds(r0, rows), :].astype(BF16) for r0 in starts]
        f = [lb + (1.0 - lb) * jax.nn.sigmoid(f_ref[pl.ds(r0, rows), :]) for r0 in starts]
        log_f = [jnp.log(f[j]) for j in grp]
        bcum = [chunk_scan(log_f[j], False) for j in grp]
        to_end = [chunk_scan(log_f[j], True) - log_f[j] for j in grp]
        chunk_dec = [jnp.exp(bcum[j]) for j in grp]
        q_dec = [(qc[j] * chunk_dec[j]).astype(BF16) for j in grp]
        k_inv = [(1.0 - f[j]) * jnp.exp(-bcum[j]) for j in grp]
        k_dec = [((1.0 - f[j]) * jnp.exp(to_end[j])).astype(BF16) for j in grp]
        p = [jnp.where(causal, _dot_nt(q_dec[j], k_inv[j]), 0.0) for j in grp]
        o_intra = [_dot(p[j], ic[j]) for j in grp]
        updates = [[_dot_tn(ic[j][sl], k_dec[j][sl]) for sl in chunks] for j in grp]
        for j, r0 in enumerate(starts):
            outs = []
            for sl, upd in zip(chunks, updates[j]):
                outs.append(o_intra[j][sl] + _dot_nt(q_dec[j][sl], state_t))
                state_t = state_t * chunk_dec[j][sl.stop - 1:sl.stop] + upd
            o = jnp.concatenate(outs, axis=0)
            o = o * lax.rsqrt(jnp.mean(o * o, axis=-1, keepdims=True) + RMS_EPS) * hn
            o_ref[pl.ds(r0, rows), :] = (o * _silu(g_ref[pl.ds(r0, rows), :])).astype(o_ref.dtype)
        return state_t

    lax.fori_loop(0, seq // (rows * HGRN_GROUP), group, jnp.zeros((HEAD_DIM, HEAD_DIM), F32))


def _hgrn(h3, lb, hn):
    b, s, _ = h3.shape
    nh = HGRN_HEADS
    col = lambda off: pl.BlockSpec((None, s, HEAD_DIM), lambda bi, hi: (bi, 0, off + hi))
    return pl.pallas_call(
        _hgrn_kernel,
        grid=(b, nh),
        in_specs=[col(0), col(nh), col(2 * nh), col(3 * nh),
                  pl.BlockSpec((None, 1, HEAD_DIM), lambda bi, hi: (hi, 0, 0)),
                  pl.BlockSpec((1, HEAD_DIM), lambda bi, hi: (0, 0))],
        out_specs=pl.BlockSpec((None, s, HEAD_DIM), lambda bi, hi: (bi, 0, hi)),
        out_shape=jax.ShapeDtypeStruct((b, s, HGRN_WIDTH), BF16),
        compiler_params=_params("parallel", "parallel"),
        name="hgrn2",
    )(h3, h3, h3, h3, lb, hn)


def _rope(x, cos2, sin2):
    return x * cos2 + pltpu.roll(x, HEAD_DIM // 2, axis=1) * sin2


def _flash_step(q_blk, k_blk, v_blk, mask, carry):
    m, l, acc = carry
    s = jnp.where(mask, _dot_nt(q_blk, k_blk) * HEAD_DIM ** -0.5, NEG_INF)
    m_new = jnp.maximum(m, jnp.max(s, axis=-1, keepdims=True))
    alpha = jnp.exp(m - m_new)
    p = jnp.exp(s - m_new)
    l = alpha * l + jnp.sum(p, axis=-1, keepdims=True)
    acc = alpha * acc + _dot(p, v_blk)
    return m_new, l, acc


def _flash_init():
    blk = ATT_BLOCK
    return (jnp.full((blk, 1), NEG_INF, F32), jnp.zeros((blk, 1), F32), jnp.zeros((blk, HEAD_DIM), F32))


def _dilated_kernel(*refs):
    n_g = len(DIL_GROUPS)
    q_refs, k_refs, v_refs = refs[0:n_g], refs[n_g:2 * n_g], refs[2 * n_g:3 * n_g]
    cos_ref, sin_ref, o_ref = refs[3 * n_g:3 * n_g + 3]
    q_s, k_s, v_s, og_s, lse_s = refs[3 * n_g + 3:]
    seq = o_ref.shape[0]
    blk = DIL_BLOCK
    piece = 256
    grp = range(n_g)

    for gi, (window, d) in enumerate(DIL_GROUPS):
        assert window // d == blk
        seg = seq // d
        k_s[gi, pl.ds(0, blk), :] = jnp.zeros((blk, HEAD_DIM), BF16)
        v_s[gi, pl.ds(0, blk), :] = jnp.zeros((blk, HEAD_DIM), BF16)
        for r in range(d):
            for c0 in range(0, seg, piece):
                n = min(piece, seg)
                rows = pl.ds(r + c0 * d, n, stride=d) if d > 1 else pl.ds(c0, n)
                cos2 = cos_ref[rows, :]
                sin2 = sin_ref[rows, :]
                q_s[gi, pl.ds(r * seg + c0, n), :] = _rope(q_refs[gi][rows, :], cos2, sin2).astype(BF16)
                k_s[gi, pl.ds(blk + r * seg + c0, n), :] = _rope(k_refs[gi][rows, :], cos2, sin2).astype(BF16)
                v_s[gi, pl.ds(blk + r * seg + c0, n), :] = v_refs[gi][rows, :].astype(BF16)

    ri = _iota((blk, 2 * blk), 0)
    ci = _iota((blk, 2 * blk), 1)
    rel = ri + blk - ci
    in_window = (rel >= 0) & (rel <= blk)
    dot = functools.partial(jnp.dot, preferred_element_type=F32)

    def q_block(m, _):
        j0 = pl.multiple_of(m * blk, blk)
        segs = [seq // d for _, d in DIL_GROUPS]
        has_prev = [jnp.where(j0 % seg != 0, blk, 0) for seg in segs]
        mask = [in_window & (ci + has_prev[g] >= blk) for g in grp]
        q = [q_s[g, pl.ds(j0, blk), :] for g in grp]
        kw = [k_s[g, pl.ds(j0, 2 * blk), :] for g in grp]
        vw = [v_s[g, pl.ds(j0, 2 * blk), :] for g in grp]
        s = [jnp.where(mask[g], _dot_nt(q[g], kw[g]) * HEAD_DIM ** -0.5, NEG_INF) for g in grp]
        top = [jnp.max(s[g], axis=-1, keepdims=True) for g in grp]
        p = [jnp.exp(s[g] - top[g]) for g in grp]
        den = [jnp.sum(p[g], axis=-1, keepdims=True) for g in grp]
        o = [dot(p[g].astype(BF16), vw[g]) / den[g] for g in grp]
        lse = [top[g] + jnp.log(den[g]) for g in grp]
        for g, (_, d) in enumerate(DIL_GROUPS):
            seg = segs[g]
            dst = pl.ds((j0 % seg) * d + j0 // seg, blk, stride=d) if d > 1 else pl.ds(j0, blk)
            og_s[g, dst, :] = o[g]
            lse_s[g, dst, :] = jnp.broadcast_to(lse[g], (blk, HEAD_DIM))
        return 0

    lax.fori_loop(0, seq // blk, q_block, 0)

    for c0 in range(0, seq, piece):
        rows = pl.ds(c0, piece)
        lses = [lse_s[g, rows, :] for g in grp]
        top = functools.reduce(jnp.maximum, lses)
        wts = [jnp.exp(x - top) for x in lses]
        den = functools.reduce(lambda a, b: a + b, wts)
        o = functools.reduce(lambda a, b: a + b, [wts[g] * og_s[g, rows, :] for g in grp]) / den
        o_ref[rows, :] = o.astype(o_ref.dtype)


def _dilated(h3, cos2, sin2):
    b, s, _ = h3.shape
    hpg = DIL_HEADS_PER_GROUP
    n_g = len(DIL_GROUPS)
    col = lambda off: pl.BlockSpec((None, s, HEAD_DIM), lambda bi, hi: (bi, 0, off + hi))
    tab = pl.BlockSpec((s, HEAD_DIM), lambda bi, hi: (0, 0))
    specs = [col(part * DIL_HEADS + gi * hpg) for part in range(3) for gi in range(n_g)]
    return pl.pallas_call(
        _dilated_kernel,
        grid=(b, hpg),
        in_specs=specs + [tab, tab],
        out_specs=pl.BlockSpec((None, s, HEAD_DIM), lambda bi, hi: (bi, 0, hi)),
        out_shape=jax.ShapeDtypeStruct((b, s, hpg * HEAD_DIM), BF16),
        scratch_shapes=[pltpu.VMEM((n_g, s, HEAD_DIM), BF16), pltpu.VMEM((n_g, s + DIL_BLOCK, HEAD_DIM), BF16),
                        pltpu.VMEM((n_g, s + DIL_BLOCK, HEAD_DIM), BF16), pltpu.VMEM((n_g, s, HEAD_DIM), F32),
                        pltpu.VMEM((n_g, s, HEAD_DIM), F32)],
        compiler_params=_params("parallel", "parallel"),
        name="dilated_attention",
    )(*([h3] * (3 * n_g)), cos2, sin2)


def _moba_kernel(q_ref, k_ref, v_ref, cos_ref, sin_ref, o_ref, qf_s, q_s, k_s, v_s, km_s, sel_s):
    seq = o_ref.shape[0]
    blk = MOBA_BLOCK
    n_blk = seq // blk
    cos2 = cos_ref[...]
    sin2 = sin_ref[...]
    q = _rope(q_ref[...], cos2, sin2)
    qf_s[...] = q
    q_s[...] = q.astype(BF16)
    km_s[...] = jnp.zeros(km_s.shape, F32)
    for nb in range(n_blk):
        kb = _rope(k_ref[pl.ds(nb * blk, blk), :], cos2[nb * blk:(nb + 1) * blk], sin2[nb * blk:(nb + 1) * blk])
        k_s[pl.ds(nb * blk, blk), :] = kb.astype(BF16)
        km_s[pl.ds(nb, 1), :] = jnp.mean(kb, axis=0, keepdims=True)
    v_s[...] = v_ref[...].astype(BF16)

    lane = _iota((blk, LANES), 1).astype(F32)
    causal = _iota((blk, blk), 0) >= _iota((blk, blk), 1)
    all_true = _iota((blk, blk), 0) >= 0
    rows = lambda nb: pl.ds(nb * blk, blk)

    past = range(1, n_blk)
    km = km_s[...]
    gate = {qb: jnp.where(lane < qb, _dot_nt_hi(qf_s[rows(qb), :], km), -jnp.inf) for qb in past}
    sel = {qb: jnp.zeros((blk, LANES), F32) for qb in past}
    for _k in range(MOBA_TOPK):
        best = {qb: jnp.max(gate[qb], axis=-1, keepdims=True) for qb in past}
        first = {qb: jnp.min(jnp.where(gate[qb] == best[qb], lane, LANES), axis=-1, keepdims=True) for qb in past}
        pick = {qb: (lane == first[qb]) & (best[qb] > -jnp.inf) for qb in past}
        sel = {qb: jnp.where(pick[qb], 1.0, sel[qb]) for qb in past}
        gate = {qb: jnp.where(pick[qb], -jnp.inf, gate[qb]) for qb in past}
    for qb in past:
        sel_s[rows(qb), :] = sel[qb]

    assert sorted(qb for group in MOBA_Q_GROUPS for qb in group) == list(range(n_blk))
    for group in MOBA_Q_GROUPS:
        carry = {qb: _flash_init() for qb in group}
        for j in range(max(group) + 1):
            for qb in group:
                if j < qb:
                    mask = (sel_s[rows(qb), :][:, j:j + 1] > 0.0) & all_true
                elif j == qb:
                    mask = causal
                else:
                    continue
                carry[qb] = _flash_step(q_s[rows(qb), :], k_s[rows(j), :], v_s[rows(j), :], mask, carry[qb])
        for qb in group:
            m, l, acc = carry[qb]
            o_ref[rows(qb), :] = (acc / l).astype(o_ref.dtype)


def _moba(h3, cos2, sin2):
    b, s, _ = h3.shape
    base = 3 * DIL_HEADS
    col = lambda off: pl.BlockSpec((None, s, HEAD_DIM), lambda bi, hi: (bi, 0, base + off + hi))
    tab = pl.BlockSpec((s, HEAD_DIM), lambda bi, hi: (0, 0))
    return pl.pallas_call(
        _moba_kernel,
        grid=(b, MOBA_HEADS),
        in_specs=[col(0), col(MOBA_HEADS), col(2 * MOBA_HEADS), tab, tab],
        out_specs=pl.BlockSpec((None, s, HEAD_DIM), lambda bi, hi: (bi, 0, hi)),
        out_shape=jax.ShapeDtypeStruct((b, s, MOBA_HEADS * HEAD_DIM), BF16),
        scratch_shapes=[pltpu.VMEM((s, HEAD_DIM), F32), pltpu.VMEM((s, HEAD_DIM), BF16),
                        pltpu.VMEM((s, HEAD_DIM), BF16), pltpu.VMEM((s, HEAD_DIM), BF16),
                        pltpu.VMEM((LANES, HEAD_DIM), F32), pltpu.VMEM((s, LANES), F32)],
        compiler_params=_params("parallel", "parallel"),
        name="moba_attention",
    )(h3, h3, h3, cos2, sin2)


def _layer_norm(x, gain, bias):
    mu = jnp.mean(x, axis=-1, keepdims=True)
    xc = x - mu
    var = jnp.mean(xc * xc, axis=-1, keepdims=True)
    return xc * lax.rsqrt(var + LN_EPS) * gain + bias


def _out_ln_kernel(*refs, n_parts):
    o_refs = refs[0:n_parts]
    w_refs = refs[n_parts:2 * n_parts]
    h_ref, gain_ref, bias_ref, rw_ref, y_ref, yb_ref, logit_ref = refs[2 * n_parts:]
    mix = None
    for o_r, w_r in zip(o_refs, w_refs):
        part = jnp.dot(o_r[...], w_r[...], preferred_element_type=F32)
        mix = part if mix is None else mix + part
    y = _layer_norm(DEEPNORM_ALPHA * h_ref[...] + mix, gain_ref[...], bias_ref[...])
    y_ref[...] = y
    yb_ref[...] = y.astype(BF16)
    logit_ref[...] = _dot3(y, rw_ref[...])


def _out_ln(parts, weights, h, gain, bias, router_w, tm):
    t, d = h.shape
    n_parts = len(parts)
    row = lambda width: pl.BlockSpec((tm, width), lambda i: (i, 0))
    full = lambda a: pl.BlockSpec(a.shape, lambda i: (0, 0))
    return pl.pallas_call(
        functools.partial(_out_ln_kernel, n_parts=n_parts),
        grid=(t // tm,),
        in_specs=[row(p.shape[1]) for p in parts] + [full(w) for w in weights]
                 + [row(d), full(gain), full(bias), full(router_w)],
        out_specs=[row(d), row(d), row(LANES)],
        out_shape=[jax.ShapeDtypeStruct((t, d), F32), jax.ShapeDtypeStruct((t, d), BF16),
                   jax.ShapeDtypeStruct((t, LANES), F32)],
        compiler_params=_params("parallel"),
        name="out_proj_ln",
    )(*parts, *weights, h, gain, bias, router_w)


def _route(logits_t, rbias_col):
    row = _iota(logits_t.shape, 0)
    scores = jax.nn.sigmoid(logits_t)
    biased = scores + rbias_col

    def first_argmax(vals):
        best = jnp.max(vals, axis=0, keepdims=True)
        return best, jnp.min(jnp.where(vals == best, row, N_EXPERTS), axis=0, keepdims=True)

    best_score = None
    best_group = None
    for g in range(N_EXPERT_GROUPS):
        vals = jnp.where(row // EXPERTS_PER_GROUP == g, biased, -jnp.inf)
        top1, idx1 = first_argmax(vals)
        top2, _ = first_argmax(jnp.where(row == idx1, -jnp.inf, vals))
        score = top1 + top2
        if g == 0:
            best_score, best_group = score, jnp.zeros_like(idx1)
        else:
            better = score > best_score
            best_group = jnp.where(better, g, best_group)
            best_score = jnp.where(better, score, best_score)
    masked = jnp.where(row // EXPERTS_PER_GROUP == best_group, biased, NEG_INF)
    _, i1 = first_argmax(masked)
    _, i2 = first_argmax(jnp.where(row == i1, -jnp.inf, masked))
    s1 = jnp.sum(jnp.where(row == i1, scores, 0.0), axis=0, keepdims=True)
    s2 = jnp.sum(jnp.where(row == i2, scores, 0.0), axis=0, keepdims=True)
    tot = s1 + s2
    return i1, i2, s1 / tot, s2 / tot


def _route_kernel(logit_ref, rbias_ref, loc_ref, w_ref, tab_ref, cnt_ref, carry_s, total_s):
    sweep = pl.program_id(0)
    i = pl.program_id(1)
    tm = logit_ref.shape[0]
    block = float(MOE_BLOCK_ROWS)
    align = float(MOE_ALIGN)

    @pl.when((i == 0) & (sweep == 1))
    def _():
        total_s[...] = carry_s[...]

    @pl.when(i == 0)
    def _():
        carry_s[...] = jnp.zeros(carry_s.shape, F32)

    logits_t = logit_ref[...].T[:N_EXPERTS]
    i1, i2, w1, w2 = _route(logits_t, rbias_ref[...][:N_EXPERTS])
    row = _iota((N_EXPERTS, tm), 0)
    chosen = jnp.where((row == i1) | (row == i2), 1.0, 0.0)
    count = jnp.sum(chosen, axis=1, keepdims=True)
    padded = jnp.floor((count + (align - 1.0)) * (1.0 / align)) * align

    e_r = _iota((N_EXPERTS, N_EXPERTS), 0)
    e_c = _iota((N_EXPERTS, N_EXPERTS), 1)

    def expert_prefix(col, inclusive):
        tri = ((e_c <= e_r) if inclusive else (e_c < e_r)).astype(BF16)
        wide = jnp.broadcast_to(col, (N_EXPERTS, LANES)).astype(BF16)
        return jnp.dot(tri, wide, preferred_element_type=F32)[:, 0:1]

    @pl.when(sweep == 0)
    def _():
        loc_ref[...] = jnp.zeros(loc_ref.shape, loc_ref.dtype)
        w_ref[...] = jnp.zeros(w_ref.shape, w_ref.dtype)
        tab_ref[...] = jnp.zeros(tab_ref.shape, tab_ref.dtype)
        cnt_ref[...] = jnp.zeros(cnt_ref.shape, cnt_ref.dtype)

    @pl.when(sweep == 1)
    def _():
        total = total_s[...]
        blocks = jnp.floor((total + (block - 1.0)) * (1.0 / block))
        region = (expert_prefix(blocks, True) - blocks) * block + carry_s[...]
        run_start = expert_prefix(padded, False)
        earlier = (_iota((tm, tm), 0) < _iota((tm, tm), 1)).astype(BF16)
        slot = run_start + jnp.dot(chosen.astype(BF16), earlier, preferred_element_type=F32)
        l1 = jnp.sum(jnp.where(row == i1, slot, 0.0), axis=0, keepdims=True)
        l2 = jnp.sum(jnp.where(row == i2, slot, 0.0), axis=0, keepdims=True)
        out_row = _iota((8, tm), 0)
        loc_ref[...] = jnp.where(out_row == 0, l1, jnp.where(out_row == 1, l2, 0.0)).astype(jnp.int32)
        w_ref[...] = jnp.where(out_row == 0, w1, jnp.where(out_row == 1, w2, 0.0))
        lane = _iota((N_EXPERTS, LANES), 1)
        tab = jnp.where(lane == 0, padded, jnp.where(lane == 1, run_start, jnp.where(lane == 2, region, 0.0)))
        tab_ref[...] = tab.astype(jnp.int32)
        cnt_ref[...] = jnp.broadcast_to(total, cnt_ref.shape).astype(jnp.int32)

    carry_s[...] = carry_s[...] + padded


def _route_tokens(logits, rbias_col):
    t = logits.shape[0]
    tm = MOE_TILE
    n_tiles = t // tm
    tok = pl.BlockSpec((None, 8, tm), lambda s, i: (s, 0, i))
    loc, w, tab, totals = pl.pallas_call(
        _route_kernel,
        grid=(2, n_tiles),
        in_specs=[pl.BlockSpec((tm, LANES), lambda s, i: (i, 0)), pl.BlockSpec((LANES, 1), lambda s, i: (0, 0))],
        out_specs=[tok, tok, pl.BlockSpec((None, None, N_EXPERTS, LANES), lambda s, i: (s, i, 0, 0)),
                   pl.BlockSpec((N_EXPERTS, LANES), lambda s, i: (0, 0))],
        out_shape=[jax.ShapeDtypeStruct((2, 8, t), jnp.int32), jax.ShapeDtypeStruct((2, 8, t), F32),
                   jax.ShapeDtypeStruct((2, n_tiles, N_EXPERTS, LANES), jnp.int32),
                   jax.ShapeDtypeStruct((N_EXPERTS, LANES), jnp.int32)],
        scratch_shapes=[pltpu.VMEM((N_EXPERTS, 1), F32), pltpu.VMEM((N_EXPERTS, 1), F32)],
        compiler_params=_params("arbitrary", "arbitrary"),
        name="moe_route",
    )(logits, rbias_col)
    return loc[1], w[1], tab[1], totals[:, 0]


def _run_pieces(length, sizes):
    return [(size, length & ~(2 * size - 1), (length & size) != 0) for size in sizes]


def _tile_run_copies(cnt_ref, off_ref, region_ref, tile, compact, sorted_rows, sem, to_sorted):
    sizes = [MOE_TILE >> s for s in range((MOE_TILE // MOE_ALIGN).bit_length())]
    copies = []
    for e in range(N_EXPERTS):
        k = tile * N_EXPERTS + e
        for size, offset, used in _run_pieces(cnt_ref[k], sizes):
            small = compact.at[pl.ds(pl.multiple_of(off_ref[k] + offset, MOE_ALIGN), size)]
            big = sorted_rows.at[pl.ds(pl.multiple_of(region_ref[k] + offset, MOE_ALIGN), size)]
            copies.append((used, pltpu.make_async_copy(small, big, sem) if to_sorted
                           else pltpu.make_async_copy(big, small, sem)))
    return copies


def _dispatch_kernel(cnt_ref, off_ref, region_ref, pad_start_ref, pad_len_ref, n_valid_ref,
                     x_ref, loc_ref, w_ref, xs_ref, buf_s, zero_s, sem, zero_sem):
    i = pl.program_id(0)
    d = x_ref.shape[1]
    cap = buf_s.shape[0]
    tm = x_ref.shape[0]
    n_blocks = xs_ref.shape[0] // MOE_BLOCK_ROWS

    @pl.when(i == 0)
    def _():
        zero_s[...] = jnp.zeros(zero_s.shape, zero_s.dtype)
        sizes = [MOE_BLOCK_ROWS >> s for s in range((MOE_BLOCK_ROWS // MOE_ALIGN).bit_length())]
        copies = []
        for e in range(N_EXPERTS):
            for size, offset, used in _run_pieces(pad_len_ref[e], sizes):
                dst = xs_ref.at[pl.ds(pl.multiple_of(pad_start_ref[e] + offset, MOE_ALIGN), size)]
                copies.append((used, pltpu.make_async_copy(zero_s.at[pl.ds(0, size)], dst, zero_sem)))
        for j in range(n_blocks - N_EXPERTS, n_blocks):
            dst = xs_ref.at[pl.ds(j * MOE_BLOCK_ROWS, MOE_BLOCK_ROWS)]
            copies.append((j >= n_valid_ref[0], pltpu.make_async_copy(zero_s, dst, zero_sem)))
        for action in ("start", "wait"):
            for used, copy in copies:
                @pl.when(used)
                def _():
                    getattr(copy, action)()

    loc = loc_ref[...]
    w = w_ref[...]
    row = _iota((cap, tm), 0)
    hit1 = row == loc[0:1, :]
    hit2 = row == loc[1:2, :]
    perm = jnp.where(hit1, 1.0, jnp.where(hit2, 1.0, 0.0)).astype(BF16)
    gate = jnp.sum(jnp.where(hit1, w[0:1, :], jnp.where(hit2, w[1:2, :], 0.0)), axis=1, keepdims=True)
    g_hi = gate.astype(BF16)
    g_lo = (gate - g_hi.astype(F32)).astype(BF16)
    lane = _iota((cap, LANES), 1)
    buf_s[:, pl.ds(0, d)] = jnp.dot(perm, x_ref[...].astype(BF16), preferred_element_type=F32).astype(BF16)
    buf_s[:, pl.ds(d, LANES)] = jnp.where(lane == 0, g_hi.astype(F32),
                                          jnp.where(lane == 1, g_lo.astype(F32), 0.0)).astype(BF16)
    copies = _tile_run_copies(cnt_ref, off_ref, region_ref, i, buf_s, xs_ref, sem, True)
    for action in ("start", "wait"):
        for used, copy in copies:
            @pl.when(used)
            def _():
                getattr(copy, action)()


def _dispatch(tables, pad_start, pad_len, n_valid, x, loc, w, n_rows):
    t, d = x.shape
    tm = MOE_TILE
    cap = MOE_TOPK * tm + N_EXPERTS * MOE_ALIGN
    tok = pl.BlockSpec((8, tm), lambda i, *_: (0, i))
    grid_spec = pltpu.PrefetchScalarGridSpec(
        num_scalar_prefetch=6,
        grid=(t // tm,),
        in_specs=[pl.BlockSpec((tm, d), lambda i, *_: (i, 0)), tok, tok],
        out_specs=pl.BlockSpec(memory_space=pl.ANY),
        scratch_shapes=[pltpu.VMEM((cap, d + LANES), BF16), pltpu.VMEM((MOE_BLOCK_ROWS, d + LANES), BF16),
                        pltpu.SemaphoreType.DMA(()), pltpu.SemaphoreType.DMA(())],
    )
    return pl.pallas_call(
        _dispatch_kernel,
        grid_spec=grid_spec,
        out_shape=jax.ShapeDtypeStruct((n_rows, d + LANES), BF16),
        compiler_params=_params("arbitrary"),
        name="moe_dispatch",
    )(*tables, pad_start, pad_len, n_valid, x, loc, w)


def _expert_kernel(blk_expert_ref, next_expert_ref, n_valid_ref, x_ref, wg_ref, wu_ref, wd_ref, y_ref,
                   wg_s, wu_s, wd_s, wg_buf, wu_buf, wd_buf, slot_s, sem, *, layer):
    i = pl.program_id(0)

    def fetch(expert, slot):
        return [pltpu.make_async_copy(src.at[layer, expert], dst.at[slot], sem.at[slot])
                for src, dst in ((wg_ref, wg_buf), (wu_ref, wu_buf), (wd_ref, wd_buf))]

    @pl.when(i == 0)
    def _():
        slot_s[0] = 1
        for copy in fetch(blk_expert_ref[0], 0):
            copy.start()

    @pl.when((i == 0) | (blk_expert_ref[i] != blk_expert_ref[jnp.maximum(i - 1, 0)]))
    def _():
        slot = 1 - slot_s[0]
        slot_s[0] = slot
        for copy in fetch(blk_expert_ref[i], slot):
            copy.wait()
        wg_s[...] = wg_buf[slot].astype(BF16)
        wu_s[...] = wu_buf[slot].astype(BF16)
        wd_s[...] = wd_buf[slot].astype(BF16)

        @pl.when(next_expert_ref[i] >= 0)
        def _():
            for copy in fetch(next_expert_ref[i], 1 - slot):
                copy.start()

    @pl.when(i < n_valid_ref[0])
    def _():
        d = wg_s.shape[0]
        x = x_ref[:, pl.ds(0, d)]
        extra = x_ref[:, pl.ds(d, LANES)].astype(F32)
        gate = extra[:, 0:1] + extra[:, 1:2]
        hid = _silu(jnp.dot(x, wg_s[...], preferred_element_type=F32)) * jnp.dot(
            x, wu_s[...], preferred_element_type=F32)
        y_ref[...] = _dot(hid * gate, wd_s[...]).astype(y_ref.dtype)

    @pl.when(i >= n_valid_ref[0])
    def _():
        y_ref[...] = jnp.zeros(y_ref.shape, y_ref.dtype)


def _experts(blk_expert, next_expert, n_valid, xs, wg, wu, wd, layer, n_blocks):
    d = wg.shape[-2]
    f = wg.shape[-1]
    rows = MOE_BLOCK_ROWS
    n_rows = n_blocks * rows
    hbm = pl.BlockSpec(memory_space=pl.ANY)
    grid_spec = pltpu.PrefetchScalarGridSpec(
        num_scalar_prefetch=3,
        grid=(n_blocks,),
        in_specs=[pl.BlockSpec((rows, xs.shape[1]), lambda i, be, ne, nv: (i, 0)), hbm, hbm, hbm],
        out_specs=pl.BlockSpec((rows, d), lambda i, be, ne, nv: (i, 0)),
        scratch_shapes=[pltpu.VMEM((d, f), BF16), pltpu.VMEM((d, f), BF16), pltpu.VMEM((f, d), BF16),
                        pltpu.VMEM((2, d, f), F32), pltpu.VMEM((2, d, f), F32), pltpu.VMEM((2, f, d), F32),
                        pltpu.SMEM((1,), jnp.int32), pltpu.SemaphoreType.DMA((2,))],
    )
    return pl.pallas_call(
        functools.partial(_expert_kernel, layer=layer),
        grid_spec=grid_spec,
        out_shape=jax.ShapeDtypeStruct((n_rows, d), BF16),
        compiler_params=_params("arbitrary"),
        name="moe_experts",
    )(blk_expert, next_expert, n_valid, xs, wg, wu, wd)


def _combine_kernel(cnt_ref, off_ref, region_ref, ys_ref, loc_ref, h_ref, gain_ref, bias_ref, y_ref, yb_ref,
                    buf, sem):
    i = pl.program_id(0)
    n = pl.num_programs(0)
    tm = h_ref.shape[0]
    cap = buf.shape[1]

    def fetch(tile, slot, action):
        for used, copy in _tile_run_copies(cnt_ref, off_ref, region_ref, tile, buf.at[slot], ys_ref,
                                           sem.at[slot], False):
            @pl.when(used)
            def _():
                getattr(copy, action)()

    slot = i % 2

    @pl.when(i == 0)
    def _():
        buf[...] = jnp.zeros(buf.shape, buf.dtype)
        fetch(0, 0, "start")

    @pl.when(i + 1 < n)
    def _():
        fetch(i + 1, 1 - slot, "start")

    fetch(i, slot, "wait")
    loc = loc_ref[...].astype(F32)
    eye = _iota((tm, tm), 0) == _iota((tm, tm), 1)
    loc_col = [jnp.sum(jnp.where(eye, jnp.broadcast_to(loc[k:k + 1, :], (tm, tm)), 0.0), axis=1, keepdims=True)
               for k in range(MOE_TOPK)]
    lane = _iota((tm, cap), 1).astype(F32)
    pick = jnp.where(lane == loc_col[0], 1.0, jnp.where(lane == loc_col[1], 1.0, 0.0)).astype(BF16)
    ffn = jnp.dot(pick, buf[slot], preferred_element_type=F32)
    y = _layer_norm(DEEPNORM_ALPHA * h_ref[...] + ffn, gain_ref[...], bias_ref[...])
    y_ref[...] = y
    yb_ref[...] = y.astype(BF16)


def _combine_ln(tables, ys, loc, h, gain, bias):
    t, d = h.shape
    tm = MOE_TILE
    cap = MOE_TOPK * tm + N_EXPERTS * MOE_ALIGN
    grid_spec = pltpu.PrefetchScalarGridSpec(
        num_scalar_prefetch=3,
        grid=(t // tm,),
        in_specs=[pl.BlockSpec(memory_space=pl.ANY),
                  pl.BlockSpec((8, tm), lambda i, *_: (0, i)),
                  pl.BlockSpec((tm, d), lambda i, *_: (i, 0)),
                  pl.BlockSpec((1, d), lambda i, *_: (0, 0)),
                  pl.BlockSpec((1, d), lambda i, *_: (0, 0))],
        out_specs=[pl.BlockSpec((tm, d), lambda i, *_: (i, 0)), pl.BlockSpec((tm, d), lambda i, *_: (i, 0))],
        scratch_shapes=[pltpu.VMEM((2, cap, d), BF16), pltpu.SemaphoreType.DMA((2,))],
    )
    return pl.pallas_call(
        _combine_kernel,
        grid_spec=grid_spec,
        out_shape=[jax.ShapeDtypeStruct((t, d), F32), jax.ShapeDtypeStruct((t, d), BF16)],
        compiler_params=_params("arbitrary"),
        name="moe_combine_ln",
    )(*tables, ys, loc, h, gain, bias)


def _moe_ln(h, logits, rbias, wg, wu, wd, layer, gain, bias):
    t, d = h.shape
    rows = MOE_BLOCK_ROWS
    n_tiles = t // MOE_TILE
    n_blocks = -(-(MOE_TOPK * t + n_tiles * N_EXPERTS * (MOE_ALIGN - 1)) // rows) + N_EXPERTS
    loc, w_t, tab, counts = _route_tokens(logits, rbias)
    tables = tuple(tab[:, :, k].reshape(-1) for k in range(3))
    blocks_per_expert = (counts + rows - 1) // rows
    block_end = jnp.cumsum(blocks_per_expert)
    row_start = (block_end - blocks_per_expert) * rows
    n_valid = block_end[-1:].astype(jnp.int32)
    blk = jnp.arange(n_blocks, dtype=jnp.int32)
    blk_expert = jnp.sum(jnp.minimum(blk, n_valid - 1)[:, None] >= block_end[None, :], axis=1).astype(jnp.int32)
    pad_start = (row_start + counts).astype(jnp.int32)
    pad_len = (blocks_per_expert * rows - counts).astype(jnp.int32)
    xs = _dispatch(tables, pad_start, pad_len, n_valid, h, loc, w_t, n_blocks * rows)
    experts = jnp.arange(N_EXPERTS, dtype=jnp.int32)
    later_used = (experts[None, :] > experts[:, None]) & (blocks_per_expert[None, :] > 0)
    next_used = jnp.min(jnp.where(later_used, experts[None, :], N_EXPERTS), axis=1)
    next_expert = jnp.where(next_used < N_EXPERTS, next_used, -1).astype(jnp.int32)[blk_expert]
    ys = _experts(blk_expert, next_expert, n_valid, xs, wg, wu, wd, layer, n_blocks)
    return _combine_ln(tables, ys, loc, h, gain, bias)


def _rope_tables(seq):
    inv_freq = ROPE_THETA ** (-jnp.arange(0, HEAD_DIM, 2, dtype=F32) / HEAD_DIM)
    ang = jnp.arange(seq, dtype=F32)[:, None] * inv_freq[None, :]
    cos, sin = jnp.cos(ang), jnp.sin(ang)
    return jnp.concatenate([cos, cos], axis=-1), jnp.concatenate([-sin, sin], axis=-1)


def _even_mixer(hb, b, s, w_in, conv_w, a_log, dt_bias, gdn_norm, hgrn_norm, lower_bound):
    gw = GDN_WIDTH
    n_small = 2 * GDN_HEADS
    tail0 = 4 * gw
    w_t = w_in.T
    w_small = jnp.pad(w_t[tail0:tail0 + n_small], ((0, LANES - n_small), (0, 0))).astype(BF16)
    h_a = _matmul(hb, w_t[:tail0].astype(BF16), 1024, 1024, transposed=True).reshape(b, s, tail0)
    h_b = _matmul(hb, w_t[tail0 + n_small:].astype(BF16), 1024, 1024, transposed=True).reshape(b, s, 4 * HGRN_WIDTH)
    n_chunks = s // GDN_CHUNK
    small = _matmul(hb, w_small, 1024, LANES, transposed=True).reshape(b, s, LANES)
    to_rows = lambda a: a.transpose(0, 2, 1).reshape(b, GDN_HEADS, n_chunks, GDN_CHUNK)
    b_rows, a_rows = to_rows(small[..., :GDN_HEADS]), to_rows(small[..., GDN_HEADS:n_small])
    headvec = lambda v: jnp.broadcast_to(v.astype(F32)[:, None, None], (GDN_HEADS, 1, GDN_CHUNK))
    o_a = _gdn(h_a, conv_w.astype(F32), a_rows, b_rows, headvec(a_log), headvec(dt_bias),
               gdn_norm.astype(F32).reshape(1, HEAD_DIM))
    o_b = _hgrn(h_b, lower_bound.astype(F32).reshape(HGRN_HEADS, 1, HEAD_DIM),
                hgrn_norm.astype(F32).reshape(1, HEAD_DIM))
    return [o_a.reshape(b * s, GDN_WIDTH), o_b.reshape(b * s, HGRN_WIDTH)]


def _odd_mixer(hb, b, s, w_in, cos2, sin2):
    h = _matmul(hb, w_in.astype(BF16), 1024, 768)
    h3 = h.reshape(b, s, ODD_COLS)
    o_c = _dilated(h3, cos2, sin2)
    o_d = _moba(h3, cos2, sin2)
    return [o_c.reshape(b * s, -1), o_d.reshape(b * s, -1)]


def kernel(x, ev_w_in, ev_conv_w, ev_a_log, ev_dt_bias, ev_gdn_norm, ev_hgrn_norm, hgrn_lb_logits, ev_w_out,
           od_w_in, od_w_out, router_w, router_bias, moe_w_gate, moe_w_up, moe_w_down, ln_gain, ln_bias):
    b, s, d = x.shape
    t = b * s
    cos2, sin2 = _rope_tables(s)
    lower_bounds = jnp.cumsum(jax.nn.softmax(hgrn_lb_logits.astype(F32), axis=0), axis=0)
    rw = jnp.pad(router_w.astype(F32), ((0, 0), (0, LANES - N_EXPERTS)))
    rbias = jnp.pad(router_bias.astype(F32), (0, LANES - N_EXPERTS)).reshape(LANES, 1)
    vec = lambda v: v.astype(F32).reshape(1, d)

    h = x.reshape(t, d)
    hb = h
    for layer in range(DEPTH):
        if layer % 2 == 0:
            e = layer // 2
            parts = _even_mixer(hb, b, s, ev_w_in[e], ev_conv_w[e], ev_a_log[e], ev_dt_bias[e], ev_gdn_norm[e],
                                ev_hgrn_norm[e], lower_bounds[layer])
            w_out = ev_w_out[e].astype(BF16)
        else:
            o = layer // 2
            parts = _odd_mixer(hb, b, s, od_w_in[o], cos2, sin2)
            w_out = od_w_out[o].astype(BF16)
        splits = np.cumsum([p.shape[1] for p in parts])[:-1]
        weights = jnp.split(w_out, splits, axis=0)
        h, hb, logits = _out_ln(parts, weights, h, vec(ln_gain[layer, 0]), vec(ln_bias[layer, 0]), rw, 256)
        h, hb = _moe_ln(h, logits, rbias, moe_w_gate, moe_w_up, moe_w_down, layer,
                        vec(ln_gain[layer, 1]), vec(ln_bias[layer, 1]))
    return h.reshape(b, s, d)
```

```python
import functools
import math

import jax
import jax.numpy as jnp
import numpy as np
from jax import lax
from jax.experimental import pallas as pl
from jax.experimental.pallas import tpu as pltpu

F32 = jnp.float32
BF16 = jnp.bfloat16

D_MODEL = 2048
DEPTH = 2
HEAD_DIM = 128
GDN_HEADS = 8
GDN_CONV = 4
GDN_CHUNK = 64
GDN_WIDTH = GDN_HEADS * HEAD_DIM
HGRN_HEADS = 8
HGRN_CHUNK = 16
HGRN_WIDTH = HGRN_HEADS * HEAD_DIM
DIL_GROUPS = ((128, 1), (512, 4), (2048, 16))
DIL_HEADS_PER_GROUP = 4
DIL_HEADS = len(DIL_GROUPS) * DIL_HEADS_PER_GROUP
MOBA_HEADS = 4
MOBA_BLOCK = 256
MOBA_TOPK = 3
ROPE_THETA = 10000.0
N_EXPERTS = 16
N_EXPERT_GROUPS = 4
EXPERTS_PER_GROUP = N_EXPERTS // N_EXPERT_GROUPS
D_EXPERT = 512
MOE_TOPK = 2
MOE_BLOCK_ROWS = 512
MOE_TILE = 512
MOE_ALIGN = 16
DEEPNORM_ALPHA = (2.0 * DEPTH) ** 0.25
LN_EPS = 1e-5
RMS_EPS = 1e-6
NEG_INF = -1e30

LANES = 128
VMEM_LIMIT = 56 * 1024 * 1024
ATT_BLOCK = 256
MOBA_Q_GROUPS = ((7, 0, 6, 1), (5, 2, 4, 3))
DIL_BLOCK = 128
GDN_HEADS_PER_STEP = 2
GDN_GROUP = 16
HGRN_GROUP = 4
HGRN_ROWS = 256

ODD_COLS = 3 * DIL_HEADS * HEAD_DIM + 3 * MOBA_HEADS * HEAD_DIM


def _dot(a, b):
    return jnp.dot(a.astype(BF16), b.astype(BF16), preferred_element_type=F32)


def _dot_nt(a, b):
    return lax.dot_general(a.astype(BF16), b.astype(BF16), (((1,), (1,)), ((), ())),
                           preferred_element_type=F32)


def _dot_tn(a, b):
    return lax.dot_general(a.astype(BF16), b.astype(BF16), (((0,), (0,)), ((), ())),
                           preferred_element_type=F32)


def _dot_hi(a, b):
    return jnp.dot(a, b, preferred_element_type=F32, precision=lax.Precision.HIGHEST)


def _dot_nt_hi(a, b):
    return lax.dot_general(a, b, (((1,), (1,)), ((), ())), preferred_element_type=F32,
                           precision=lax.Precision.HIGHEST)


def _dot3(a, b):
    a_hi = a.astype(BF16)
    b_hi = b.astype(BF16)
    a_lo = (a - a_hi.astype(F32)).astype(BF16)
    b_lo = (b - b_hi.astype(F32)).astype(BF16)
    dot = functools.partial(jnp.dot, preferred_element_type=F32)
    return dot(a_hi, b_hi) + (dot(a_hi, b_lo) + dot(a_lo, b_hi))


def _dot_sel(sel, x):
    dot = functools.partial(jnp.dot, preferred_element_type=F32)
    x_hi = x.astype(BF16)
    r1 = x - x_hi.astype(F32)
    x_mid = r1.astype(BF16)
    x_lo = (r1 - x_mid.astype(F32)).astype(BF16)
    return dot(sel, x_hi) + (dot(sel, x_mid) + dot(sel, x_lo))


_dot_inv = _dot


def _silu(x):
    return x * jax.nn.sigmoid(x)


def _iota(shape, dim):
    return lax.broadcasted_iota(jnp.int32, shape, dim)


def _params(*sem):
    return pltpu.CompilerParams(dimension_semantics=sem, vmem_limit_bytes=VMEM_LIMIT)


def _mm_kernel(x_ref, w_ref, o_ref, *, transposed):
    x = x_ref[...].astype(BF16)
    prod = _dot_nt(x, w_ref[...]) if transposed else jnp.dot(x, w_ref[...], preferred_element_type=F32)
    o_ref[...] = prod.astype(o_ref.dtype)


def _matmul(x, w, tm, tn, transposed=False):
    m, k = x.shape
    n = w.shape[0] if transposed else w.shape[1]
    assert m % tm == 0 and n % tn == 0
    w_spec = pl.BlockSpec((tn, k), lambda i, j: (j, 0)) if transposed else pl.BlockSpec((k, tn), lambda i, j: (0, j))
    return pl.pallas_call(
        functools.partial(_mm_kernel, transposed=transposed),
        grid=(m // tm, n // tn),
        in_specs=[pl.BlockSpec((tm, k), lambda i, j: (i, 0)), w_spec],
        out_specs=pl.BlockSpec((tm, tn), lambda i, j: (i, j)),
        out_shape=jax.ShapeDtypeStruct((m, n), F32),
        compiler_params=_params("parallel", "parallel"),
        name="in_proj",
    )(x, w)


def _gdn_kernel(q_ref, k_ref, v_ref, z_ref, cwq_ref, cwk_ref, cwv_ref, a_ref, b_ref, alog_ref, dt_ref,
                gn_ref, o_ref, pad_s, q_s, k_s, v_s, gcum_s, beta_s, qe_s, ob_s, sm_s, sa_s):
    seq = q_ref.shape[0]
    c = GDN_CHUNK
    n_chunks = seq // c
    rows = 256
    heads = range(GDN_HEADS_PER_STEP)
    lanes = [slice(hh * HEAD_DIM, (hh + 1) * HEAD_DIM) for hh in heads]

    def conv_norm(hh):
        pad_s[pl.ds(0, 8), :] = jnp.zeros((8, HEAD_DIM), F32)
        for x_ref, cw_ref, dst, mode in ((q_ref, cwq_ref, q_s, "q"), (k_ref, cwk_ref, k_s, "k"),
                                         (v_ref, cwv_ref, v_s, "v")):
            pad_s[pl.ds(8, seq), :] = x_ref[:, lanes[hh]]
            cw = cw_ref[:, lanes[hh]]
            for r in range(seq // rows):
                acc = None
                for j in range(GDN_CONV):
                    tap = pad_s[pl.ds(8 + r * rows - (GDN_CONV - 1) + j, rows), :] * cw[j:j + 1, :]
                    acc = tap if acc is None else acc + tap
                y = _silu(acc)
                if mode != "v":
                    y = y * lax.rsqrt(jnp.sum(y * y, axis=-1, keepdims=True) + RMS_EPS)
                if mode == "q":
                    y = y * HEAD_DIM ** -0.5
                dst[pl.ds(r * rows, rows), :] = y

    upper = (_iota((c, c), 0) <= _iota((c, c), 1)).astype(F32)
    for hh in heads:
        g = -jnp.exp(alog_ref[hh]) * jax.nn.softplus(a_ref[hh] + dt_ref[hh])
        gcum_s[hh] = _dot_hi(g, upper)
        beta_s[hh] = jax.nn.sigmoid(b_ref[hh])

    ri = _iota((c, c), 0)
    ci = _iota((c, c), 1)
    eye = ri == ci
    strict = ri > ci
    incl = ri >= ci
    eye_f = eye.astype(F32)
    level1 = ri // 2 == ci // 2
    levels = []
    s = 2
    while s < c:
        levels.append((ri // (2 * s) == ci // (2 * s)) & ((ri // s) % 2 == 1) & ((ci // s) % 2 == 0))
        s *= 2

    dot = functools.partial(jnp.dot, preferred_element_type=F32)

    def to_col(row):
        return jnp.sum(jnp.where(eye, jnp.broadcast_to(row, (c, c)), 0.0), axis=1, keepdims=True)

    def prepare(i, _, hh):
        n0 = i * GDN_GROUP
        grp = range(GDN_GROUP)
        starts = [pl.multiple_of((n0 + j) * c, c) for j in grp]
        qc = [q_s[pl.ds(r0, c), :] for r0 in starts]
        kc = [k_s[pl.ds(r0, c), :] for r0 in starts]
        vc = [v_s[pl.ds(r0, c), :] for r0 in starts]
        g_row = [gcum_s[hh, pl.ds(n0 + j, 1), :] for j in grp]
        g_col = [to_col(g_row[j]) for j in grp]
        b_col = [to_col(beta_s[hh, pl.ds(n0 + j, 1), :]) for j in grp]
        decay = [jnp.exp(jnp.where(incl, g_col[j] - g_row[j], 0.0)) for j in grp]
        n_mat = [b_col[j] * jnp.where(strict, decay[j], 0.0) * _dot_nt(kc[j], kc[j]) for j in grp]
        inv = [eye_f - jnp.where(level1, n_mat[j], 0.0) for j in grp]
        for blk in levels:
            tmp = [_dot_inv(inv[j], jnp.where(blk, n_mat[j], 0.0)) for j in grp]
            inv = [inv[j] - _dot_inv(tmp[j], inv[j]) for j in grp]
        e_col = [jnp.exp(g_col[j]) for j in grp]
        sol = [_dot_inv(inv[j], jnp.concatenate([b_col[j] * vc[j], (b_col[j] * e_col[j]) * kc[j]], axis=1))
               for j in grp]
        qk = [(_dot_nt(qc[j], kc[j]) * jnp.where(incl, decay[j], 0.0)).astype(BF16) for j in grp]
        ub = [sol[j][:, :HEAD_DIM].astype(BF16) for j in grp]
        w = [sol[j][:, HEAD_DIM:].astype(BF16) for j in grp]
        kd = [(kc[j] * jnp.exp(g_row[j][:, c - 1:c] - g_col[j])).astype(BF16) for j in grp]
        q_eff = [(qc[j] * e_col[j] - dot(qk[j], w[j])).astype(BF16) for j in grp]
        o_base = [dot(qk[j], ub[j]) for j in grp]
        s_mat = [_dot_tn(kd[j], w[j]).astype(BF16) for j in grp]
        s_add = [_dot_tn(kd[j], ub[j]) for j in grp]
        for j, r0 in enumerate(starts):
            m0 = pl.multiple_of((n0 + j) * HEAD_DIM, HEAD_DIM)
            qe_s[hh, pl.ds(r0, c), :] = q_eff[j]
            ob_s[hh, pl.ds(r0, c), :] = o_base[j]
            sm_s[hh, pl.ds(m0, HEAD_DIM), :] = s_mat[j]
            sa_s[hh, pl.ds(m0, HEAD_DIM), :] = s_add[j]
        return 0

    for hh in heads:
        conv_norm(hh)
        lax.fori_loop(0, n_chunks // GDN_GROUP, functools.partial(prepare, hh=hh), 0)

    gn = gn_ref[...]

    def chunk(n, states):
        r0 = pl.multiple_of(n * c, c)
        m0 = pl.multiple_of(n * HEAD_DIM, HEAD_DIM)
        g_last = [gcum_s[hh, pl.ds(n, 1), :][:, c - 1:c] for hh in heads]
        lhs = [jnp.concatenate([qe_s[hh, pl.ds(r0, c), :], sm_s[hh, pl.ds(m0, HEAD_DIM), :]], axis=0)
               for hh in heads]
        prod = [dot(lhs[hh], states[hh].astype(BF16)) for hh in heads]
        for hh in heads:
            ob_s[hh, pl.ds(r0, c), :] = prod[hh][:c] + ob_s[hh, pl.ds(r0, c), :]
        return tuple(jnp.exp(g_last[hh]) * states[hh] - prod[hh][c:] + sa_s[hh, pl.ds(m0, HEAD_DIM), :]
                     for hh in heads)

    lax.fori_loop(0, n_chunks, chunk, tuple(jnp.zeros((HEAD_DIM, HEAD_DIM), F32) for _ in heads))

    for hh in heads:
        for r in range(seq // rows):
            sl = pl.ds(r * rows, rows)
            o = ob_s[hh, sl, :]
            o = o * lax.rsqrt(jnp.mean(o * o, axis=-1, keepdims=True) + RMS_EPS) * gn
            o_ref[sl, lanes[hh]] = (o * _silu(z_ref[sl, lanes[hh]])).astype(o_ref.dtype)


def _gdn(h3, conv_w, a_rows, b_rows, alog, dt, gn):
    b, s, _ = h3.shape
    hp = GDN_HEADS_PER_STEP
    nb = GDN_HEADS // hp
    wide = hp * HEAD_DIM
    n_chunks = s // GDN_CHUNK
    col = lambda off: pl.BlockSpec((None, s, wide), lambda bi, hi: (bi, 0, off + hi))
    cw = lambda off: pl.BlockSpec((GDN_CONV, wide), lambda bi, hi: (0, off + hi))
    rowspec = pl.BlockSpec((None, hp, n_chunks, GDN_CHUNK), lambda bi, hi: (bi, hi, 0, 0))
    headvec = pl.BlockSpec((hp, 1, GDN_CHUNK), lambda bi, hi: (hi, 0, 0))
    return pl.pallas_call(
        _gdn_kernel,
        grid=(b, nb),
        in_specs=[col(0), col(nb), col(2 * nb), col(3 * nb), cw(0), cw(nb), cw(2 * nb),
                  rowspec, rowspec, headvec, headvec,
                  pl.BlockSpec((1, HEAD_DIM), lambda bi, hi: (0, 0))],
        out_specs=pl.BlockSpec((None, s, wide), lambda bi, hi: (bi, 0, hi)),
        out_shape=jax.ShapeDtypeStruct((b, s, GDN_WIDTH), BF16),
        scratch_shapes=[pltpu.VMEM((s + 8, HEAD_DIM), F32), pltpu.VMEM((s, HEAD_DIM), F32),
                        pltpu.VMEM((s, HEAD_DIM), F32), pltpu.VMEM((s, HEAD_DIM), F32),
                        pltpu.VMEM((hp, n_chunks, GDN_CHUNK), F32), pltpu.VMEM((hp, n_chunks, GDN_CHUNK), F32),
                        pltpu.VMEM((hp, s, HEAD_DIM), BF16), pltpu.VMEM((hp, s, HEAD_DIM), F32),
                        pltpu.VMEM((hp, n_chunks * HEAD_DIM, HEAD_DIM), BF16),
                        pltpu.VMEM((hp, n_chunks * HEAD_DIM, HEAD_DIM), F32)],
        compiler_params=_params("parallel", "parallel"),
        name="gdn",
    )(h3, h3, h3, h3, conv_w, conv_w, conv_w, a_rows, b_rows, alog, dt, gn)


def _hgrn_kernel(q_ref, f_ref, i_ref, g_ref, lb_ref, hn_ref, o_ref):
    seq = q_ref.shape[0]
    c = HGRN_CHUNK
    rows = HGRN_ROWS
    ri = _iota((rows, rows), 0)
    ci = _iota((rows, rows), 1)
    causal = (ri // c == ci // c) & (ci <= ri)
    row_in_chunk = _iota((rows, HEAD_DIM), 0) % c
    lb = lb_ref[...]
    hn = hn_ref[...]

    chunks = [slice(j * c, (j + 1) * c) for j in range(rows // c)]
    grp = range(HGRN_GROUP)

    def chunk_scan(x, suffix):
        step = 1
        while step < c:
            if suffix:
                x = x + jnp.where(row_in_chunk < c - step, pltpu.roll(x, rows - step, axis=0), 0.0)
            else:
                x = x + jnp.where(row_in_chunk >= step, pltpu.roll(x, step, axis=0), 0.0)
            step *= 2
        return x

    def group(n, state_t):
        starts = [pl.multiple_of((n * HGRN_GROUP + j) * rows, rows) for j in grp]
        qc = [q_ref[pl.ds(r0, rows), :] for r0 in starts]
        ic = [i_ref[pl.ds(r0, rows), :].astype(BF16) for r0 in starts]
        f = [lb + (1.0 - lb) * jax.nn.sigmoid(f_ref[pl.ds(r0, rows), :]) for r0 in starts]
        log_f = [jnp.log(f[j]) for j in grp]
        bcum = [chunk_scan(log_f[j], False) for j in grp]
        to_end = [chunk_scan(log_f[j], True) - log_f[j] for j in grp]
        chunk_dec = [jnp.exp(bcum[j]) for j in grp]
        q_dec = [(qc[j] * chunk_dec[j]).astype(BF16) for j in grp]
        k_inv = [(1.0 - f[j]) * jnp.exp(-bcum[j]) for j in grp]
        k_dec = [((1.0 - f[j]) * jnp.exp(to_end[j])).astype(BF16) for j in grp]
        p = [jnp.where(causal, _dot_nt(q_dec[j], k_inv[j]), 0.0) for j in grp]
        o_intra = [_dot(p[j], ic[j]) for j in grp]
        updates = [[_dot_tn(ic[j][sl], k_dec[j][sl]) for sl in chunks] for j in grp]
        for j, r0 in enumerate(starts):
            outs = []
            for sl, upd in zip(chunks, updates[j]):
                outs.append(o_intra[j][sl] + _dot_nt(q_dec[j][sl], state_t))
                state_t = state_t * chunk_dec[j][sl.stop - 1:sl.stop] + upd
            o = jnp.concatenate(outs, axis=0)
            o = o * lax.rsqrt(jnp.mean(o * o, axis=-1, keepdims=True) + RMS_EPS) * hn
            o_ref[pl.ds(r0, rows), :] = (o * _silu(g_ref[pl.ds(r0, rows), :])).astype(o_ref.dtype)
        return state_t

    lax.fori_loop(0, seq // (rows * HGRN_GROUP), group, jnp.zeros((HEAD_DIM, HEAD_DIM), F32))


def _hgrn(h3, lb, hn):
    b, s, _ = h3.shape
    nh = HGRN_HEADS
    col = lambda off: pl.BlockSpec((None, s, HEAD_DIM), lambda bi, hi: (bi, 0, off + hi))
    return pl.pallas_call(
        _hgrn_kernel,
        grid=(b, nh),
        in_specs=[col(0), col(nh), col(2 * nh), col(3 * nh),
                  pl.BlockSpec((None, 1, HEAD_DIM), lambda bi, hi: (hi, 0, 0)),
                  pl.BlockSpec((1, HEAD_DIM), lambda bi, hi: (0, 0))],
        out_specs=pl.BlockSpec((None, s, HEAD_DIM), lambda bi, hi: (bi, 0, hi)),
        out_shape=jax.ShapeDtypeStruct((b, s, HGRN_WIDTH), BF16),
        compiler_params=_params("parallel", "parallel"),
        name="hgrn2",
    )(h3, h3, h3, h3, lb, hn)


def _rope(x, cos2, sin2):
    return x * cos2 + pltpu.roll(x, HEAD_DIM // 2, axis=1) * sin2


def _flash_step(q_blk, k_blk, v_blk, mask, carry):
    m, l, acc = carry
    s = jnp.where(mask, _dot_nt(q_blk, k_blk) * HEAD_DIM ** -0.5, NEG_INF)
    m_new = jnp.maximum(m, jnp.max(s, axis=-1, keepdims=True))
    alpha = jnp.exp(m - m_new)
    p = jnp.exp(s - m_new)
    l = alpha * l + jnp.sum(p, axis=-1, keepdims=True)
    acc = alpha * acc + _dot(p, v_blk)
    return m_new, l, acc


def _flash_init():
    blk = ATT_BLOCK
    return (jnp.full((blk, 1), NEG_INF, F32), jnp.zeros((blk, 1), F32), jnp.zeros((blk, HEAD_DIM), F32))


def _dilated_kernel(*refs):
    n_g = len(DIL_GROUPS)
    q_refs, k_refs, v_refs = refs[0:n_g], refs[n_g:2 * n_g], refs[2 * n_g:3 * n_g]
    cos_ref, sin_ref, o_ref = refs[3 * n_g:3 * n_g + 3]
    q_s, k_s, v_s, og_s, lse_s = refs[3 * n_g + 3:]
    seq = o_ref.shape[0]
    blk = DIL_BLOCK
    piece = 256
    grp = range(n_g)

    for gi, (window, d) in enumerate(DIL_GROUPS):
        assert window // d == blk
        seg = seq // d
        k_s[gi, pl.ds(0, blk), :] = jnp.zeros((blk, HEAD_DIM), BF16)
        v_s[gi, pl.ds(0, blk), :] = jnp.zeros((blk, HEAD_DIM), BF16)
        for r in range(d):
            for c0 in range(0, seg, piece):
                n = min(piece, seg)
                rows = pl.ds(r + c0 * d, n, stride=d) if d > 1 else pl.ds(c0, n)
                cos2 = cos_ref[rows, :]
                sin2 = sin_ref[rows, :]
                q_s[gi, pl.ds(r * seg + c0, n), :] = _rope(q_refs[gi][rows, :], cos2, sin2).astype(BF16)
                k_s[gi, pl.ds(blk + r * seg + c0, n), :] = _rope(k_refs[gi][rows, :], cos2, sin2).astype(BF16)
                v_s[gi, pl.ds(blk + r * seg + c0, n), :] = v_refs[gi][rows, :].astype(BF16)

    ri = _iota((blk, 2 * blk), 0)
    ci = _iota((blk, 2 * blk), 1)
    rel = ri + blk - ci
    in_window = (rel >= 0) & (rel <= blk)
    dot = functools.partial(jnp.dot, preferred_element_type=F32)

    def q_block(m, _):
        j0 = pl.multiple_of(m * blk, blk)
        segs = [seq // d for _, d in DIL_GROUPS]
        has_prev = [jnp.where(j0 % seg != 0, blk, 0) for seg in segs]
        mask = [in_window & (ci + has_prev[g] >= blk) for g in grp]
        q = [q_s[g, pl.ds(j0, blk), :] for g in grp]
        kw = [k_s[g, pl.ds(j0, 2 * blk), :] for g in grp]
        vw = [v_s[g, pl.ds(j0, 2 * blk), :] for g in grp]
        s = [jnp.where(mask[g], _dot_nt(q[g], kw[g]) * HEAD_DIM ** -0.5, NEG_INF) for g in grp]
        top = [jnp.max(s[g], axis=-1, keepdims=True) for g in grp]
        p = [jnp.exp(s[g] - top[g]) for g in grp]
        den = [jnp.sum(p[g], axis=-1, keepdims=True) for g in grp]
        o = [dot(p[g].astype(BF16), vw[g]) / den[g] for g in grp]
        lse = [top[g] + jnp.log(den[g]) for g in grp]
        for g, (_, d) in enumerate(DIL_GROUPS):
            seg = segs[g]
            dst = pl.ds((j0 % seg) * d + j0 // seg, blk, stride=d) if d > 1 else pl.ds(j0, blk)
            og_s[g, dst, :] = o[g]
            lse_s[g, dst, :] = jnp.broadcast_to(lse[g], (blk, HEAD_DIM))
        return 0

    lax.fori_loop(0, seq // blk, q_block, 0)

    for c0 in range(0, seq, piece):
        rows = pl.ds(c0, piece)
        lses = [lse_s[g, rows, :] for g in grp]
        top = functools.reduce(jnp.maximum, lses)
        wts = [jnp.exp(x - top) for x in lses]
        den = functools.reduce(lambda a, b: a + b, wts)
        o = functools.reduce(lambda a, b: a + b, [wts[g] * og_s[g, rows, :] for g in grp]) / den
        o_ref[rows, :] = o.astype(o_ref.dtype)


def _dilated(h3, cos2, sin2):
    b, s, _ = h3.shape
    hpg = DIL_HEADS_PER_GROUP
    n_g = len(DIL_GROUPS)
    col = lambda off: pl.BlockSpec((None, s, HEAD_DIM), lambda bi, hi: (bi, 0, off + hi))
    tab = pl.BlockSpec((s, HEAD_DIM), lambda bi, hi: (0, 0))
    specs = [col(part * DIL_HEADS + gi * hpg) for part in range(3) for gi in range(n_g)]
    return pl.pallas_call(
        _dilated_kernel,
        grid=(b, hpg),
        in_specs=specs + [tab, tab],
        out_specs=pl.BlockSpec((None, s, HEAD_DIM), lambda bi, hi: (bi, 0, hi)),
        out_shape=jax.ShapeDtypeStruct((b, s, hpg * HEAD_DIM), BF16),
        scratch_shapes=[pltpu.VMEM((n_g, s, HEAD_DIM), BF16), pltpu.VMEM((n_g, s + DIL_BLOCK, HEAD_DIM), BF16),
                        pltpu.VMEM((n_g, s + DIL_BLOCK, HEAD_DIM), BF16), pltpu.VMEM((n_g, s, HEAD_DIM), F32),
                        pltpu.VMEM((n_g, s, HEAD_DIM), F32)],
        compiler_params=_params("parallel", "parallel"),
        name="dilated_attention",
    )(*([h3] * (3 * n_g)), cos2, sin2)


def _moba_kernel(q_ref, k_ref, v_ref, cos_ref, sin_ref, o_ref, qf_s, q_s, k_s, v_s, km_s, sel_s):
    seq = o_ref.shape[0]
    blk = MOBA_BLOCK
    n_blk = seq // blk
    cos2 = cos_ref[...]
    sin2 = sin_ref[...]
    q = _rope(q_ref[...], cos2, sin2)
    qf_s[...] = q
    q_s[...] = q.astype(BF16)
    km_s[...] = jnp.zeros(km_s.shape, F32)
    for nb in range(n_blk):
        kb = _rope(k_ref[pl.ds(nb * blk, blk), :], cos2[nb * blk:(nb + 1) * blk], sin2[nb * blk:(nb + 1) * blk])
        k_s[pl.ds(nb * blk, blk), :] = kb.astype(BF16)
        km_s[pl.ds(nb, 1), :] = jnp.mean(kb, axis=0, keepdims=True)
    v_s[...] = v_ref[...].astype(BF16)

    lane = _iota((blk, LANES), 1).astype(F32)
    causal = _iota((blk, blk), 0) >= _iota((blk, blk), 1)
    all_true = _iota((blk, blk), 0) >= 0
    rows = lambda nb: pl.ds(nb * blk, blk)

    past = range(1, n_blk)
    km = km_s[...]
    gate = {qb: jnp.where(lane < qb, _dot_nt_hi(qf_s[rows(qb), :], km), -jnp.inf) for qb in past}
    sel = {qb: jnp.zeros((blk, LANES), F32) for qb in past}
    for _k in range(MOBA_TOPK):
        best = {qb: jnp.max(gate[qb], axis=-1, keepdims=True) for qb in past}
        first = {qb: jnp.min(jnp.where(gate[qb] == best[qb], lane, LANES), axis=-1, keepdims=True) for qb in past}
        pick = {qb: (lane == first[qb]) & (best[qb] > -jnp.inf) for qb in past}
        sel = {qb: jnp.where(pick[qb], 1.0, sel[qb]) for qb in past}
        gate = {qb: jnp.where(pick[qb], -jnp.inf, gate[qb]) for qb in past}
    for qb in past:
        sel_s[rows(qb), :] = sel[qb]

    assert sorted(qb for group in MOBA_Q_GROUPS for qb in group) == list(range(n_blk))
    for group in MOBA_Q_GROUPS:
        carry = {qb: _flash_init() for qb in group}
        for j in range(max(group) + 1):
            for qb in group:
                if j < qb:
                    mask = (sel_s[rows(qb), :][:, j:j + 1] > 0.0) & all_true
                elif j == qb:
                    mask = causal
                else:
                    continue
                carry[qb] = _flash_step(q_s[rows(qb), :], k_s[rows(j), :], v_s[rows(j), :], mask, carry[qb])
        for qb in group:
            m, l, acc = carry[qb]
            o_ref[rows(qb), :] = (acc / l).astype(o_ref.dtype)


def _moba(h3, cos2, sin2):
    b, s, _ = h3.shape
    base = 3 * DIL_HEADS
    col = lambda off: pl.BlockSpec((None, s, HEAD_DIM), lambda bi, hi: (bi, 0, base + off + hi))
    tab = pl.BlockSpec((s, HEAD_DIM), lambda bi, hi: (0, 0))
    return pl.pallas_call(
        _moba_kernel,
        grid=(b, MOBA_HEADS),
        in_specs=[col(0), col(MOBA_HEADS), col(2 * MOBA_HEADS), tab, tab],
        out_specs=pl.BlockSpec((None, s, HEAD_DIM), lambda bi, hi: (bi, 0, hi)),
        out_shape=jax.ShapeDtypeStruct((b, s, MOBA_HEADS * HEAD_DIM), BF16),
        scratch_shapes=[pltpu.VMEM((s, HEAD_DIM), F32), pltpu.VMEM((s, HEAD_DIM), BF16),
                        pltpu.VMEM((s, HEAD_DIM), BF16), pltpu.VMEM((s, HEAD_DIM), BF16),
                        pltpu.VMEM((LANES, HEAD_DIM), F32), pltpu.VMEM((s, LANES), F32)],
        compiler_params=_params("parallel", "parallel"),
        name="moba_attention",
    )(h3, h3, h3, cos2, sin2)


def _layer_norm(x, gain, bias):
    mu = jnp.mean(x, axis=-1, keepdims=True)
    xc = x - mu
    var = jnp.mean(xc * xc, axis=-1, keepdims=True)
    return xc * lax.rsqrt(var + LN_EPS) * gain + bias


def _out_ln_kernel(*refs, n_parts):
    o_refs = refs[0:n_parts]
    w_refs = refs[n_parts:2 * n_parts]
    h_ref, gain_ref, bias_ref, rw_ref, y_ref, yb_ref, logit_ref = refs[2 * n_parts:]
    mix = None
    for o_r, w_r in zip(o_refs, w_refs):
        part = jnp.dot(o_r[...], w_r[...], preferred_element_type=F32)
        mix = part if mix is None else mix + part
    y = _layer_norm(DEEPNORM_ALPHA * h_ref[...] + mix, gain_ref[...], bias_ref[...])
    y_ref[...] = y
    yb_ref[...] = y.astype(BF16)
    logit_ref[...] = _dot3(y, rw_ref[...])


def _out_ln(parts, weights, h, gain, bias, router_w, tm):
    t, d = h.shape
    n_parts = len(parts)
    row = lambda width: pl.BlockSpec((tm, width), lambda i: (i, 0))
    full = lambda a: pl.BlockSpec(a.shape, lambda i: (0, 0))
    return pl.pallas_call(
        functools.partial(_out_ln_kernel, n_parts=n_parts),
        grid=(t // tm,),
        in_specs=[row(p.shape[1]) for p in parts] + [full(w) for w in weights]
                 + [row(d), full(gain), full(bias), full(router_w)],
        out_specs=[row(d), row(d), row(LANES)],
        out_shape=[jax.ShapeDtypeStruct((t, d), F32), jax.ShapeDtypeStruct((t, d), BF16),
                   jax.ShapeDtypeStruct((t, LANES), F32)],
        compiler_params=_params("parallel"),
        name="out_proj_ln",
    )(*parts, *weights, h, gain, bias, router_w)


def _route(logits_t, rbias_col):
    row = _iota(logits_t.shape, 0)
    scores = jax.nn.sigmoid(logits_t)
    biased = scores + rbias_col

    def first_argmax(vals):
        best = jnp.max(vals, axis=0, keepdims=True)
        return best, jnp.min(jnp.where(vals == best, row, N_EXPERTS), axis=0, keepdims=True)

    best_score = None
    best_group = None
    for g in range(N_EXPERT_GROUPS):
        vals = jnp.where(row // EXPERTS_PER_GROUP == g, biased, -jnp.inf)
        top1, idx1 = first_argmax(vals)
        top2, _ = first_argmax(jnp.where(row == idx1, -jnp.inf, vals))
        score = top1 + top2
        if g == 0:
            best_score, best_group = score, jnp.zeros_like(idx1)
        else:
            better = score > best_score
            best_group = jnp.where(better, g, best_group)
            best_score = jnp.where(better, score, best_score)
    masked = jnp.where(row // EXPERTS_PER_GROUP == best_group, biased, NEG_INF)
    _, i1 = first_argmax(masked)
    _, i2 = first_argmax(jnp.where(row == i1, -jnp.inf, masked))
    s1 = jnp.sum(jnp.where(row == i1, scores, 0.0), axis=0, keepdims=True)
    s2 = jnp.sum(jnp.where(row == i2, scores, 0.0), axis=0, keepdims=True)
    tot = s1 + s2
    return i1, i2, s1 / tot, s2 / tot


def _route_kernel(logit_ref, rbias_ref, loc_ref, w_ref, tab_ref, cnt_ref, carry_s, total_s):
    sweep = pl.program_id(0)
    i = pl.program_id(1)
    tm = logit_ref.shape[0]
    block = float(MOE_BLOCK_ROWS)
    align = float(MOE_ALIGN)

    @pl.when((i == 0) & (sweep == 1))
    def _():
        total_s[...] = carry_s[...]

    @pl.when(i == 0)
    def _():
        carry_s[...] = jnp.zeros(carry_s.shape, F32)

    logits_t = logit_ref[...].T[:N_EXPERTS]
    i1, i2, w1, w2 = _route(logits_t, rbias_ref[...][:N_EXPERTS])
    row = _iota((N_EXPERTS, tm), 0)
    chosen = jnp.where((row == i1) | (row == i2), 1.0, 0.0)
    count = jnp.sum(chosen, axis=1, keepdims=True)
    padded = jnp.floor((count + (align - 1.0)) * (1.0 / align)) * align

    e_r = _iota((N_EXPERTS, N_EXPERTS), 0)
    e_c = _iota((N_EXPERTS, N_EXPERTS), 1)

    def expert_prefix(col, inclusive):
        tri = ((e_c <= e_r) if inclusive else (e_c < e_r)).astype(BF16)
        wide = jnp.broadcast_to(col, (N_EXPERTS, LANES)).astype(BF16)
        return jnp.dot(tri, wide, preferred_element_type=F32)[:, 0:1]

    @pl.when(sweep == 0)
    def _():
        loc_ref[...] = jnp.zeros(loc_ref.shape, loc_ref.dtype)
        w_ref[...] = jnp.zeros(w_ref.shape, w_ref.dtype)
        tab_ref[...] = jnp.zeros(tab_ref.shape, tab_ref.dtype)
        cnt_ref[...] = jnp.zeros(cnt_ref.shape, cnt_ref.dtype)

    @pl.when(sweep == 1)
    def _():
        total = total_s[...]
        blocks = jnp.floor((total + (block - 1.0)) * (1.0 / block))
        region = (expert_prefix(blocks, True) - blocks) * block + carry_s[...]
        run_start = expert_prefix(padded, False)
        earlier = (_iota((tm, tm), 0) < _iota((tm, tm), 1)).astype(BF16)
        slot = run_start + jnp.dot(chosen.astype(BF16), earlier, preferred_element_type=F32)
        l1 = jnp.sum(jnp.where(row == i1, slot, 0.0), axis=0, keepdims=True)
        l2 = jnp.sum(jnp.where(row == i2, slot, 0.0), axis=0, keepdims=True)
        out_row = _iota((8, tm), 0)
        loc_ref[...] = jnp.where(out_row == 0, l1, jnp.where(out_row == 1, l2, 0.0)).astype(jnp.int32)
        w_ref[...] = jnp.where(out_row == 0, w1, jnp.where(out_row == 1, w2, 0.0))
        lane = _iota((N_EXPERTS, LANES), 1)
        tab = jnp.where(lane == 0, padded, jnp.where(lane == 1, run_start, jnp.where(lane == 2, region, 0.0)))
        tab_ref[...] = tab.astype(jnp.int32)
        cnt_ref[...] = jnp.broadcast_to(total, cnt_ref.shape).astype(jnp.int32)

    carry_s[...] = carry_s[...] + padded


def _route_tokens(logits, rbias_col):
    t = logits.shape[0]
    tm = MOE_TILE
    n_tiles = t // tm
    tok = pl.BlockSpec((None, 8, tm), lambda s, i: (s, 0, i))
    loc, w, tab, totals = pl.pallas_call(
        _route_kernel,
        grid=(2, n_tiles),
        in_specs=[pl.BlockSpec((tm, LANES), lambda s, i: (i, 0)), pl.BlockSpec((LANES, 1), lambda s, i: (0, 0))],
        out_specs=[tok, tok, pl.BlockSpec((None, None, N_EXPERTS, LANES), lambda s, i: (s, i, 0, 0)),
                   pl.BlockSpec((N_EXPERTS, LANES), lambda s, i: (0, 0))],
        out_shape=[jax.ShapeDtypeStruct((2, 8, t), jnp.int32), jax.ShapeDtypeStruct((2, 8, t), F32),
                   jax.ShapeDtypeStruct((2, n_tiles, N_EXPERTS, LANES), jnp.int32),
                   jax.ShapeDtypeStruct((N_EXPERTS, LANES), jnp.int32)],
        scratch_shapes=[pltpu.VMEM((N_EXPERTS, 1), F32), pltpu.VMEM((N_EXPERTS, 1), F32)],
        compiler_params=_params("arbitrary", "arbitrary"),
        name="moe_route",
    )(logits, rbias_col)
    return loc[1], w[1], tab[1], totals[:, 0]


def _run_pieces(length, sizes):
    return [(size, length & ~(2 * size - 1), (length & size) != 0) for size in sizes]


def _tile_run_copies(cnt_ref, off_ref, region_ref, tile, compact, sorted_rows, sem, to_sorted):
    copies = []
    for e in range(N_EXPERTS):
        k = tile * N_EXPERTS + e
        length = pl.multiple_of(cnt_ref[k], MOE_ALIGN)
        small = compact.at[pl.ds(pl.multiple_of(off_ref[k], MOE_ALIGN), length)]
        big = sorted_rows.at[pl.ds(pl.multiple_of(region_ref[k], MOE_ALIGN), length)]
        copies.append((length > 0, pltpu.make_async_copy(small, big, sem) if to_sorted
                       else pltpu.make_async_copy(big, small, sem)))
    return copies


def _dispatch_kernel(cnt_ref, off_ref, region_ref, pad_start_ref, pad_len_ref, n_valid_ref,
                     x_ref, loc_ref, w_ref, xs_ref, buf_s, zero_s, sem, zero_sem):
    i = pl.program_id(0)
    d = x_ref.shape[1]
    cap = buf_s.shape[1]
    tm = x_ref.shape[0]
    n_blocks = xs_ref.shape[0] // MOE_BLOCK_ROWS

    @pl.when(i == 0)
    def _():
        zero_s[...] = jnp.zeros(zero_s.shape, zero_s.dtype)
        sizes = [MOE_BLOCK_ROWS >> s for s in range((MOE_BLOCK_ROWS // MOE_ALIGN).bit_length())]
        copies = []
        for e in range(N_EXPERTS):
            for size, offset, used in _run_pieces(pad_len_ref[e], sizes):
                dst = xs_ref.at[pl.ds(pl.multiple_of(pad_start_ref[e] + offset, MOE_ALIGN), size)]
                copies.append((used, pltpu.make_async_copy(zero_s.at[pl.ds(0, size)], dst, zero_sem)))
        for j in range(n_blocks - N_EXPERTS, n_blocks):
            dst = xs_ref.at[pl.ds(j * MOE_BLOCK_ROWS, MOE_BLOCK_ROWS)]
            copies.append((j >= n_valid_ref[0], pltpu.make_async_copy(zero_s, dst, zero_sem)))
        for action in ("start", "wait"):
            for used, copy in copies:
                @pl.when(used)
                def _():
                    getattr(copy, action)()

    loc = loc_ref[...]
    w = w_ref[...]
    row = _iota((cap, tm), 0)
    hit1 = row == loc[0:1, :]
    hit2 = row == loc[1:2, :]
    perm = jnp.where(hit1, 1.0, jnp.where(hit2, 1.0, 0.0)).astype(BF16)
    gate = jnp.sum(jnp.where(hit1, w[0:1, :], jnp.where(hit2, w[1:2, :], 0.0)), axis=1, keepdims=True)
    g_hi = gate.astype(BF16)
    g_lo = (gate - g_hi.astype(F32)).astype(BF16)
    lane = _iota((cap, LANES), 1)
    slot = i % 2
    buf_s[slot, :, pl.ds(0, d)] = jnp.dot(perm, x_ref[...].astype(BF16), preferred_element_type=F32).astype(BF16)
    buf_s[slot, :, pl.ds(d, LANES)] = jnp.where(lane == 0, g_hi.astype(F32),
                                                jnp.where(lane == 1, g_lo.astype(F32), 0.0)).astype(BF16)

    def run_copies(tile, which, action):
        for used, copy in _tile_run_copies(cnt_ref, off_ref, region_ref, tile, buf_s.at[which], xs_ref,
                                           sem.at[which], True):
            @pl.when(used)
            def _():
                getattr(copy, action)()

    run_copies(i, slot, "start")

    @pl.when(i > 0)
    def _():
        run_copies(i - 1, 1 - slot, "wait")

    @pl.when(i == pl.num_programs(0) - 1)
    def _():
        run_copies(i, slot, "wait")


def _dispatch(tables, pad_start, pad_len, n_valid, x, loc, w, n_rows):
    t, d = x.shape
    tm = MOE_TILE
    cap = MOE_TOPK * tm + N_EXPERTS * MOE_ALIGN
    tok = pl.BlockSpec((8, tm), lambda i, *_: (0, i))
    grid_spec = pltpu.PrefetchScalarGridSpec(
        num_scalar_prefetch=6,
        grid=(t // tm,),
        in_specs=[pl.BlockSpec((tm, d), lambda i, *_: (i, 0)), tok, tok],
        out_specs=pl.BlockSpec(memory_space=pl.ANY),
        scratch_shapes=[pltpu.VMEM((2, cap, d + LANES), BF16), pltpu.VMEM((MOE_BLOCK_ROWS, d + LANES), BF16),
                        pltpu.SemaphoreType.DMA((2,)), pltpu.SemaphoreType.DMA(())],
    )
    return pl.pallas_call(
        _dispatch_kernel,
        grid_spec=grid_spec,
        out_shape=jax.ShapeDtypeStruct((n_rows, d + LANES), BF16),
        compiler_params=_params("arbitrary"),
        name="moe_dispatch",
    )(*tables, pad_start, pad_len, n_valid, x, loc, w)


def _expert_kernel(blk_expert_ref, next_expert_ref, n_valid_ref, x_ref, wg_ref, wu_ref, wd_ref, y_ref,
                   wg_s, wu_s, wd_s, wg_buf, wu_buf, wd_buf, slot_s, sem, *, layer):
    i = pl.program_id(0)

    def fetch(expert, slot):
        return [pltpu.make_async_copy(src.at[layer, expert], dst.at[slot], sem.at[slot])
                for src, dst in ((wg_ref, wg_buf), (wu_ref, wu_buf), (wd_ref, wd_buf))]

    @pl.when(i == 0)
    def _():
        slot_s[0] = 1
        for copy in fetch(blk_expert_ref[0], 0):
            copy.start()

    @pl.when((i == 0) | (blk_expert_ref[i] != blk_expert_ref[jnp.maximum(i - 1, 0)]))
    def _():
        slot = 1 - slot_s[0]
        slot_s[0] = slot
        for copy in fetch(blk_expert_ref[i], slot):
            copy.wait()
        wg_s[...] = wg_buf[slot].astype(BF16)
        wu_s[...] = wu_buf[slot].astype(BF16)
        wd_s[...] = wd_buf[slot].astype(BF16)

        @pl.when(next_expert_ref[i] >= 0)
        def _():
            for copy in fetch(next_expert_ref[i], 1 - slot):
                copy.start()

    @pl.when(i < n_valid_ref[0])
    def _():
        d = wg_s.shape[0]
        x = x_ref[:, pl.ds(0, d)]
        extra = x_ref[:, pl.ds(d, LANES)].astype(F32)
        gate = extra[:, 0:1] + extra[:, 1:2]
        hid = _silu(jnp.dot(x, wg_s[...], preferred_element_type=F32)) * jnp.dot(
            x, wu_s[...], preferred_element_type=F32)
        y_ref[...] = _dot(hid * gate, wd_s[...]).astype(y_ref.dtype)

    @pl.when(i >= n_valid_ref[0])
    def _():
        y_ref[...] = jnp.zeros(y_ref.shape, y_ref.dtype)


def _experts(blk_expert, next_expert, n_valid, xs, wg, wu, wd, layer, n_blocks):
    d = wg.shape[-2]
    f = wg.shape[-1]
    rows = MOE_BLOCK_ROWS
    n_rows = n_blocks * rows
    hbm = pl.BlockSpec(memory_space=pl.ANY)
    grid_spec = pltpu.PrefetchScalarGridSpec(
        num_scalar_prefetch=3,
        grid=(n_blocks,),
        in_specs=[pl.BlockSpec((rows, xs.shape[1]), lambda i, be, ne, nv: (i, 0)), hbm, hbm, hbm],
        out_specs=pl.BlockSpec((rows, d), lambda i, be, ne, nv: (i, 0)),
        scratch_shapes=[pltpu.VMEM((d, f), BF16), pltpu.VMEM((d, f), BF16), pltpu.VMEM((f, d), BF16),
                        pltpu.VMEM((2, d, f), F32), pltpu.VMEM((2, d, f), F32), pltpu.VMEM((2, f, d), F32),
                        pltpu.SMEM((1,), jnp.int32), pltpu.SemaphoreType.DMA((2,))],
    )
    return pl.pallas_call(
        functools.partial(_expert_kernel, layer=layer),
        grid_spec=grid_spec,
        out_shape=jax.ShapeDtypeStruct((n_rows, d), BF16),
        compiler_params=_params("arbitrary"),
        name="moe_experts",
    )(blk_expert, next_expert, n_valid, xs, wg, wu, wd)


def _combine_kernel(cnt_ref, off_ref, region_ref, ys_ref, loc_ref, h_ref, gain_ref, bias_ref, y_ref, yb_ref,
                    buf, sem):
    i = pl.program_id(0)
    n = pl.num_programs(0)
    tm = h_ref.shape[0]
    cap = buf.shape[1]

    def fetch(tile, slot, action):
        for used, copy in _tile_run_copies(cnt_ref, off_ref, region_ref, tile, buf.at[slot], ys_ref,
                                           sem.at[slot], False):
            @pl.when(used)
            def _():
                getattr(copy, action)()

    slot = i % 2

    @pl.when(i == 0)
    def _():
        buf[...] = jnp.zeros(buf.shape, buf.dtype)
        fetch(0, 0, "start")

    @pl.when(i + 1 < n)
    def _():
        fetch(i + 1, 1 - slot, "start")

    fetch(i, slot, "wait")
    loc = loc_ref[...].astype(F32)
    eye = _iota((tm, tm), 0) == _iota((tm, tm), 1)
    loc_col = [jnp.sum(jnp.where(eye, jnp.broadcast_to(loc[k:k + 1, :], (tm, tm)), 0.0), axis=1, keepdims=True)
               for k in range(MOE_TOPK)]
    lane = _iota((tm, cap), 1).astype(F32)
    pick = jnp.where(lane == loc_col[0], 1.0, jnp.where(lane == loc_col[1], 1.0, 0.0)).astype(BF16)
    ffn = jnp.dot(pick, buf[slot], preferred_element_type=F32)
    y = _layer_norm(DEEPNORM_ALPHA * h_ref[...] + ffn, gain_ref[...], bias_ref[...])
    y_ref[...] = y
    yb_ref[...] = y.astype(BF16)


def _combine_ln(tables, ys, loc, h, gain, bias):
    t, d = h.shape
    tm = MOE_TILE
    cap = MOE_TOPK * tm + N_EXPERTS * MOE_ALIGN
    grid_spec = pltpu.PrefetchScalarGridSpec(
        num_scalar_prefetch=3,
        grid=(t // tm,),
        in_specs=[pl.BlockSpec(memory_space=pl.ANY),
                  pl.BlockSpec((8, tm), lambda i, *_: (0, i)),
                  pl.BlockSpec((tm, d), lambda i, *_: (i, 0)),
                  pl.BlockSpec((1, d), lambda i, *_: (0, 0)),
                  pl.BlockSpec((1, d), lambda i, *_: (0, 0))],
        out_specs=[pl.BlockSpec((tm, d), lambda i, *_: (i, 0)), pl.BlockSpec((tm, d), lambda i, *_: (i, 0))],
        scratch_shapes=[pltpu.VMEM((2, cap, d), BF16), pltpu.SemaphoreType.DMA((2,))],
    )
    return pl.pallas_call(
        _combine_kernel,
        grid_spec=grid_spec,
        out_shape=[jax.ShapeDtypeStruct((t, d), F32), jax.ShapeDtypeStruct((t, d), BF16)],
        compiler_params=_params("arbitrary"),
        name="moe_combine_ln",
    )(*tables, ys, loc, h, gain, bias)


def _moe_ln(h, logits, rbias, wg, wu, wd, layer, gain, bias):
    t, d = h.shape
    rows = MOE_BLOCK_ROWS
    n_tiles = t // MOE_TILE
    n_blocks = -(-(MOE_TOPK * t + n_tiles * N_EXPERTS * (MOE_ALIGN - 1)) // rows) + N_EXPERTS
    loc, w_t, tab, counts = _route_tokens(logits, rbias)
    tables = tuple(tab[:, :, k].reshape(-1) for k in range(3))
    blocks_per_expert = (counts + rows - 1) // rows
    block_end = jnp.cumsum(blocks_per_expert)
    row_start = (block_end - blocks_per_expert) * rows
    n_valid = block_end[-1:].astype(jnp.int32)
    blk = jnp.arange(n_blocks, dtype=jnp.int32)
    blk_expert = jnp.sum(jnp.minimum(blk, n_valid - 1)[:, None] >= block_end[None, :], axis=1).astype(jnp.int32)
    pad_start = (row_start + counts).astype(jnp.int32)
    pad_len = (blocks_per_expert * rows - counts).astype(jnp.int32)
    xs = _dispatch(tables, pad_start, pad_len, n_valid, h, loc, w_t, n_blocks * rows)
    experts = jnp.arange(N_EXPERTS, dtype=jnp.int32)
    later_used = (experts[None, :] > experts[:, None]) & (blocks_per_expert[None, :] > 0)
    next_used = jnp.min(jnp.where(later_used, experts[None, :], N_EXPERTS), axis=1)
    next_expert = jnp.where(next_used < N_EXPERTS, next_used, -1).astype(jnp.int32)[blk_expert]
    ys = _experts(blk_expert, next_expert, n_valid, xs, wg, wu, wd, layer, n_blocks)
    return _combine_ln(tables, ys, loc, h, gain, bias)


def _rope_tables(seq):
    inv_freq = ROPE_THETA ** (-jnp.arange(0, HEAD_DIM, 2, dtype=F32) / HEAD_DIM)
    ang = jnp.arange(seq, dtype=F32)[:, None] * inv_freq[None, :]
    cos, sin = jnp.cos(ang), jnp.sin(ang)
    return jnp.concatenate([cos, cos], axis=-1), jnp.concatenate([-sin, sin], axis=-1)


def _even_mixer(hb, b, s, w_in, conv_w, a_log, dt_bias, gdn_norm, hgrn_norm, lower_bound):
    gw = GDN_WIDTH
    n_small = 2 * GDN_HEADS
    tail0 = 4 * gw
    w_t = w_in.T
    w_small = jnp.pad(w_t[tail0:tail0 + n_small], ((0, LANES - n_small), (0, 0))).astype(BF16)
    h_a = _matmul(hb, w_t[:tail0].astype(BF16), 1024, 1024, transposed=True).reshape(b, s, tail0)
    h_b = _matmul(hb, w_t[tail0 + n_small:].astype(BF16), 1024, 1024, transposed=True).reshape(b, s, 4 * HGRN_WIDTH)
    n_chunks = s // GDN_CHUNK
    small = _matmul(hb, w_small, 1024, LANES, transposed=True).reshape(b, s, LANES)
    to_rows = lambda a: a.transpose(0, 2, 1).reshape(b, GDN_HEADS, n_chunks, GDN_CHUNK)
    b_rows, a_rows = to_rows(small[..., :GDN_HEADS]), to_rows(small[..., GDN_HEADS:n_small])
    headvec = lambda v: jnp.broadcast_to(v.astype(F32)[:, None, None], (GDN_HEADS, 1, GDN_CHUNK))
    o_a = _gdn(h_a, conv_w.astype(F32), a_rows, b_rows, headvec(a_log), headvec(dt_bias),
               gdn_norm.astype(F32).reshape(1, HEAD_DIM))
    o_b = _hgrn(h_b, lower_bound.astype(F32).reshape(HGRN_HEADS, 1, HEAD_DIM),
                hgrn_norm.astype(F32).reshape(1, HEAD_DIM))
    return [o_a.reshape(b * s, GDN_WIDTH), o_b.reshape(b * s, HGRN_WIDTH)]


def _odd_mixer(hb, b, s, w_in, cos2, sin2):
    h = _matmul(hb, w_in.astype(BF16), 1024, 768)
    h3 = h.reshape(b, s, ODD_COLS)
    o_c = _dilated(h3, cos2, sin2)
    o_d = _moba(h3, cos2, sin2)
    return [o_c.reshape(b * s, -1), o_d.reshape(b * s, -1)]


def kernel(x, ev_w_in, ev_conv_w, ev_a_log, ev_dt_bias, ev_gdn_norm, ev_hgrn_norm, hgrn_lb_logits, ev_w_out,
           od_w_in, od_w_out, router_w, router_bias, moe_w_gate, moe_w_up, moe_w_down, ln_gain, ln_bias):
    b, s, d = x.shape
    t = b * s
    cos2, sin2 = _rope_tables(s)
    lower_bounds = jnp.cumsum(jax.nn.softmax(hgrn_lb_logits.astype(F32), axis=0), axis=0)
    rw = jnp.pad(router_w.astype(F32), ((0, 0), (0, LANES - N_EXPERTS)))
    rbias = jnp.pad(router_bias.astype(F32), (0, LANES - N_EXPERTS)).reshape(LANES, 1)
    vec = lambda v: v.astype(F32).reshape(1, d)

    h = x.reshape(t, d)
    hb = h
    for layer in range(DEPTH):
        if layer % 2 == 0:
            e = layer // 2
            parts = _even_mixer(hb, b, s, ev_w_in[e], ev_conv_w[e], ev_a_log[e], ev_dt_bias[e], ev_gdn_norm[e],
                                ev_hgrn_norm[e], lower_bounds[layer])
            w_out = ev_w_out[e].astype(BF16)
        else:
            o = layer // 2
            parts = _odd_mixer(hb, b, s, od_w_in[o], cos2, sin2)
            w_out = od_w_out[o].astype(BF16)
        splits = np.cumsum([p.shape[1] for p in parts])[:-1]
        weights = jnp.split(w_out, splits, axis=0)
        h, hb, logits = _out_ln(parts, weights, h, vec(ln_gain[layer, 0]), vec(ln_bias[layer, 0]), rw, 256)
        h, hb = _moe_ln(h, logits, rbias, moe_w_gate, moe_w_up, moe_w_down, layer,
                        vec(ln_gain[layer, 1]), vec(ln_bias[layer, 1]))
    return h.reshape(b, s, d)
```

```python
import functools
import math

import jax
import jax.numpy as jnp
import numpy as np
from jax import lax
from jax.experimental import pallas as pl
from jax.experimental.pallas import tpu as pltpu

F32 = jnp.float32
BF16 = jnp.bfloat16

D_MODEL = 2048
DEPTH = 2
HEAD_DIM = 128
GDN_HEADS = 8
GDN_CONV = 4
GDN_CHUNK = 64
GDN_WIDTH = GDN_HEADS * HEAD_DIM
HGRN_HEADS = 8
HGRN_CHUNK = 16
HGRN_WIDTH = HGRN_HEADS * HEAD_DIM
DIL_GROUPS = ((128, 1), (512, 4), (2048, 16))
DIL_HEADS_PER_GROUP = 4
DIL_HEADS = len(DIL_GROUPS) * DIL_HEADS_PER_GROUP
MOBA_HEADS = 4
MOBA_BLOCK = 256
MOBA_TOPK = 3
ROPE_THETA = 10000.0
N_EXPERTS = 16
N_EXPERT_GROUPS = 4
EXPERTS_PER_GROUP = N_EXPERTS // N_EXPERT_GROUPS
D_EXPERT = 512
MOE_TOPK = 2
MOE_BLOCK_ROWS = 512
MOE_TILE = 512
MOE_ALIGN = 16
DEEPNORM_ALPHA = (2.0 * DEPTH) ** 0.25
LN_EPS = 1e-5
RMS_EPS = 1e-6
NEG_INF = -1e30

LANES = 128
VMEM_LIMIT = 56 * 1024 * 1024
ATT_BLOCK = 256
MOBA_Q_GROUPS = ((7, 0, 6, 1), (5, 2, 4, 3))
DIL_BLOCK = 128
GDN_HEADS_PER_STEP = 2
GDN_GROUP = 16
HGRN_GROUP = 4
HGRN_ROWS = 256

ODD_COLS = 3 * DIL_HEADS * HEAD_DIM + 3 * MOBA_HEADS * HEAD_DIM


def _dot(a, b):
    return jnp.dot(a.astype(BF16), b.astype(BF16), preferred_element_type=F32)


def _dot_nt(a, b):
    return lax.dot_general(a.astype(BF16), b.astype(BF16), (((1,), (1,)), ((), ())),
                           preferred_element_type=F32)


def _dot_tn(a, b):
    return lax.dot_general(a.astype(BF16), b.astype(BF16), (((0,), (0,)), ((), ())),
                           preferred_element_type=F32)


def _dot_hi(a, b):
    return jnp.dot(a, b, preferred_element_type=F32, precision=lax.Precision.HIGHEST)


def _dot_nt_hi(a, b):
    return lax.dot_general(a, b, (((1,), (1,)), ((), ())), preferred_element_type=F32,
                           precision=lax.Precision.HIGHEST)


def _dot3(a, b):
    a_hi = a.astype(BF16)
    b_hi = b.astype(BF16)
    a_lo = (a - a_hi.astype(F32)).astype(BF16)
    b_lo = (b - b_hi.astype(F32)).astype(BF16)
    dot = functools.partial(jnp.dot, preferred_element_type=F32)
    return dot(a_hi, b_hi) + (dot(a_hi, b_lo) + dot(a_lo, b_hi))


def _dot_sel(sel, x):
    dot = functools.partial(jnp.dot, preferred_element_type=F32)
    x_hi = x.astype(BF16)
    r1 = x - x_hi.astype(F32)
    x_mid = r1.astype(BF16)
    x_lo = (r1 - x_mid.astype(F32)).astype(BF16)
    return dot(sel, x_hi) + (dot(sel, x_mid) + dot(sel, x_lo))


_dot_inv = _dot


def _silu(x):
    return x * jax.nn.sigmoid(x)


def _iota(shape, dim):
    return lax.broadcasted_iota(jnp.int32, shape, dim)


def _params(*sem):
    return pltpu.CompilerParams(dimension_semantics=sem, vmem_limit_bytes=VMEM_LIMIT)


def _mm_kernel(x_ref, w_ref, o_ref, *, transposed):
    x = x_ref[...].astype(BF16)
    prod = _dot_nt(x, w_ref[...]) if transposed else jnp.dot(x, w_ref[...], preferred_element_type=F32)
    o_ref[...] = prod.astype(o_ref.dtype)


def _matmul(x, w, tm, tn, transposed=False):
    m, k = x.shape
    n = w.shape[0] if transposed else w.shape[1]
    assert m % tm == 0 and n % tn == 0
    w_spec = pl.BlockSpec((tn, k), lambda i, j: (j, 0)) if transposed else pl.BlockSpec((k, tn), lambda i, j: (0, j))
    return pl.pallas_call(
        functools.partial(_mm_kernel, transposed=transposed),
        grid=(m // tm, n // tn),
        in_specs=[pl.BlockSpec((tm, k), lambda i, j: (i, 0)), w_spec],
        out_specs=pl.BlockSpec((tm, tn), lambda i, j: (i, j)),
        out_shape=jax.ShapeDtypeStruct((m, n), F32),
        compiler_params=_params("parallel", "parallel"),
        name="in_proj",
    )(x, w)


def _gdn_kernel(q_ref, k_ref, v_ref, z_ref, cwq_ref, cwk_ref, cwv_ref, a_ref, b_ref, alog_ref, dt_ref,
                gn_ref, o_ref, pad_s, q_s, k_s, v_s, gcum_s, beta_s, qe_s, ob_s, sm_s, sa_s):
    seq = q_ref.shape[0]
    c = GDN_CHUNK
    n_chunks = seq // c
    rows = 256
    heads = range(GDN_HEADS_PER_STEP)
    lanes = [slice(hh * HEAD_DIM, (hh + 1) * HEAD_DIM) for hh in heads]

    def conv_norm(hh):
        pad_s[pl.ds(0, 8), :] = jnp.zeros((8, HEAD_DIM), F32)
        for x_ref, cw_ref, dst, mode in ((q_ref, cwq_ref, q_s, "q"), (k_ref, cwk_ref, k_s, "k"),
                                         (v_ref, cwv_ref, v_s, "v")):
            pad_s[pl.ds(8, seq), :] = x_ref[:, lanes[hh]]
            cw = cw_ref[:, lanes[hh]]
            for r in range(seq // rows):
                acc = None
                for j in range(GDN_CONV):
                    tap = pad_s[pl.ds(8 + r * rows - (GDN_CONV - 1) + j, rows), :] * cw[j:j + 1, :]
                    acc = tap if acc is None else acc + tap
                y = _silu(acc)
                if mode != "v":
                    y = y * lax.rsqrt(jnp.sum(y * y, axis=-1, keepdims=True) + RMS_EPS)
                if mode == "q":
                    y = y * HEAD_DIM ** -0.5
                dst[pl.ds(r * rows, rows), :] = y

    upper = (_iota((c, c), 0) <= _iota((c, c), 1)).astype(F32)
    for hh in heads:
        g = -jnp.exp(alog_ref[hh]) * jax.nn.softplus(a_ref[hh] + dt_ref[hh])
        gcum_s[hh] = _dot_hi(g, upper)
        beta_s[hh] = jax.nn.sigmoid(b_ref[hh])

    ri = _iota((c, c), 0)
    ci = _iota((c, c), 1)
    eye = ri == ci
    strict = ri > ci
    incl = ri >= ci
    eye_f = eye.astype(F32)
    level1 = ri // 2 == ci // 2
    levels = []
    s = 2
    while s < c:
        levels.append((ri // (2 * s) == ci // (2 * s)) & ((ri // s) % 2 == 1) & ((ci // s) % 2 == 0))
        s *= 2

    dot = functools.partial(jnp.dot, preferred_element_type=F32)

    def to_col(row):
        return jnp.sum(jnp.where(eye, jnp.broadcast_to(row, (c, c)), 0.0), axis=1, keepdims=True)

    def prepare(i, _, hh):
        n0 = i * GDN_GROUP
        grp = range(GDN_GROUP)
        starts = [pl.multiple_of((n0 + j) * c, c) for j in grp]
        qc = [q_s[pl.ds(r0, c), :] for r0 in starts]
        kc = [k_s[pl.ds(r0, c), :] for r0 in starts]
        vc = [v_s[pl.ds(r0, c), :] for r0 in starts]
        g_row = [gcum_s[hh, pl.ds(n0 + j, 1), :] for j in grp]
        g_col = [to_col(g_row[j]) for j in grp]
        b_col = [to_col(beta_s[hh, pl.ds(n0 + j, 1), :]) for j in grp]
        decay = [jnp.exp(jnp.where(incl, g_col[j] - g_row[j], 0.0)) for j in grp]
        n_mat = [b_col[j] * jnp.where(strict, decay[j], 0.0) * _dot_nt(kc[j], kc[j]) for j in grp]
        inv = [eye_f - jnp.where(level1, n_mat[j], 0.0) for j in grp]
        for blk in levels:
            tmp = [_dot_inv(inv[j], jnp.where(blk, n_mat[j], 0.0)) for j in grp]
            inv = [inv[j] - _dot_inv(tmp[j], inv[j]) for j in grp]
        e_col = [jnp.exp(g_col[j]) for j in grp]
        sol = [_dot_inv(inv[j], jnp.concatenate([b_col[j] * vc[j], (b_col[j] * e_col[j]) * kc[j]], axis=1))
               for j in grp]
        qk = [(_dot_nt(qc[j], kc[j]) * jnp.where(incl, decay[j], 0.0)).astype(BF16) for j in grp]
        ub = [sol[j][:, :HEAD_DIM].astype(BF16) for j in grp]
        w = [sol[j][:, HEAD_DIM:].astype(BF16) for j in grp]
        kd = [(kc[j] * jnp.exp(g_row[j][:, c - 1:c] - g_col[j])).astype(BF16) for j in grp]
        q_eff = [(qc[j] * e_col[j] - dot(qk[j], w[j])).astype(BF16) for j in grp]
        o_base = [dot(qk[j], ub[j]) for j in grp]
        s_mat = [_dot_tn(kd[j], w[j]).astype(BF16) for j in grp]
        s_add = [_dot_tn(kd[j], ub[j]) for j in grp]
        for j, r0 in enumerate(starts):
            m0 = pl.multiple_of((n0 + j) * HEAD_DIM, HEAD_DIM)
            qe_s[hh, pl.ds(r0, c), :] = q_eff[j]
            ob_s[hh, pl.ds(r0, c), :] = o_base[j]
            sm_s[hh, pl.ds(m0, HEAD_DIM), :] = s_mat[j]
            sa_s[hh, pl.ds(m0, HEAD_DIM), :] = s_add[j]
        return 0

    for hh in heads:
        conv_norm(hh)
        lax.fori_loop(0, n_chunks // GDN_GROUP, functools.partial(prepare, hh=hh), 0)

    gn = gn_ref[...]

    def chunk(n, states):
        r0 = pl.multiple_of(n * c, c)
        m0 = pl.multiple_of(n * HEAD_DIM, HEAD_DIM)
        g_last = [gcum_s[hh, pl.ds(n, 1), :][:, c - 1:c] for hh in heads]
        lhs = [jnp.concatenate([qe_s[hh, pl.ds(r0, c), :], sm_s[hh, pl.ds(m0, HEAD_DIM), :]], axis=0)
               for hh in heads]
        prod = [dot(lhs[hh], states[hh].astype(BF16)) for hh in heads]
        for hh in heads:
            ob_s[hh, pl.ds(r0, c), :] = prod[hh][:c] + ob_s[hh, pl.ds(r0, c), :]
        return tuple(jnp.exp(g_last[hh]) * states[hh] - prod[hh][c:] + sa_s[hh, pl.ds(m0, HEAD_DIM), :]
                     for hh in heads)

    lax.fori_loop(0, n_chunks, chunk, tuple(jnp.zeros((HEAD_DIM, HEAD_DIM), F32) for _ in heads))

    for hh in heads:
        for r in range(seq // rows):
            sl = pl.ds(r * rows, rows)
            o = ob_s[hh, sl, :]
            o = o * lax.rsqrt(jnp.mean(o * o, axis=-1, keepdims=True) + RMS_EPS) * gn
            o_ref[sl, lanes[hh]] = (o * _silu(z_ref[sl, lanes[hh]])).astype(o_ref.dtype)


def _gdn(h3, conv_w, a_rows, b_rows, alog, dt, gn):
    b, s, _ = h3.shape
    hp = GDN_HEADS_PER_STEP
    nb = GDN_HEADS // hp
    wide = hp * HEAD_DIM
    n_chunks = s // GDN_CHUNK
    col = lambda off: pl.BlockSpec((None, s, wide), lambda bi, hi: (bi, 0, off + hi))
    cw = lambda off: pl.BlockSpec((GDN_CONV, wide), lambda bi, hi: (0, off + hi))
    rowspec = pl.BlockSpec((None, hp, n_chunks, GDN_CHUNK), lambda bi, hi: (bi, hi, 0, 0))
    headvec = pl.BlockSpec((hp, 1, GDN_CHUNK), lambda bi, hi: (hi, 0, 0))
    return pl.pallas_call(
        _gdn_kernel,
        grid=(b, nb),
        in_specs=[col(0), col(nb), col(2 * nb), col(3 * nb), cw(0), cw(nb), cw(2 * nb),
                  rowspec, rowspec, headvec, headvec,
                  pl.BlockSpec((1, HEAD_DIM), lambda bi, hi: (0, 0))],
        out_specs=pl.BlockSpec((None, s, wide), lambda bi, hi: (bi, 0, hi)),
        out_shape=jax.ShapeDtypeStruct((b, s, GDN_WIDTH), BF16),
        scratch_shapes=[pltpu.VMEM((s + 8, HEAD_DIM), F32), pltpu.VMEM((s, HEAD_DIM), F32),
                        pltpu.VMEM((s, HEAD_DIM), F32), pltpu.VMEM((s, HEAD_DIM), F32),
                        pltpu.VMEM((hp, n_chunks, GDN_CHUNK), F32), pltpu.VMEM((hp, n_chunks, GDN_CHUNK), F32),
                        pltpu.VMEM((hp, s, HEAD_DIM), BF16), pltpu.VMEM((hp, s, HEAD_DIM), F32),
                        pltpu.VMEM((hp, n_chunks * HEAD_DIM, HEAD_DIM), BF16),
                        pltpu.VMEM((hp, n_chunks * HEAD_DIM, HEAD_DIM), F32)],
        compiler_params=_params("parallel", "parallel"),
        name="gdn",
    )(h3, h3, h3, h3, conv_w, conv_w, conv_w, a_rows, b_rows, alog, dt, gn)


def _hgrn_kernel(q_ref, f_ref, i_ref, g_ref, lb_ref, hn_ref, o_ref):
    seq = q_ref.shape[0]
    c = HGRN_CHUNK
    rows = HGRN_ROWS
    ri = _iota((rows, rows), 0)
    ci = _iota((rows, rows), 1)
    causal = (ri // c == ci // c) & (ci <= ri)
    row_in_chunk = _iota((rows, HEAD_DIM), 0) % c
    lb = lb_ref[...]
    hn = hn_ref[...]

    chunks = [slice(j * c, (j + 1) * c) for j in range(rows // c)]
    grp = range(HGRN_GROUP)

    def chunk_scan(x, suffix):
        step = 1
        while step < c:
            if suffix:
                x = x + jnp.where(row_in_chunk < c - step, pltpu.roll(x, rows - step, axis=0), 0.0)
            else:
                x = x + jnp.where(row_in_chunk >= step, pltpu.roll(x, step, axis=0), 0.0)
            step *= 2
        return x

    def group(n, state_t):
        starts = [pl.multiple_of((n * HGRN_GROUP + j) * rows, rows) for j in grp]
        qc = [q_ref[pl.ds(r0, rows), :] for r0 in starts]
        ic = [i_ref[pl.ds(r0, rows), :].astype(BF16) for r0 in starts]
        f = [lb + (1.0 - lb) * jax.nn.sigmoid(f_ref[pl.ds(r0, rows), :]) for r0 in starts]
        log_f = [jnp.log(f[j]) for j in grp]
        bcum = [chunk_scan(log_f[j], False) for j in grp]
        to_end = [chunk_scan(log_f[j], True) - log_f[j] for j in grp]
        chunk_dec = [jnp.exp(bcum[j]) for j in grp]
        q_dec = [(qc[j] * chunk_dec[j]).astype(BF16) for j in grp]
        k_inv = [(1.0 - f[j]) * jnp.exp(-bcum[j]) for j in grp]
        k_dec = [((1.0 - f[j]) * jnp.exp(to_end[j])).astype(BF16) for j in grp]
        p = [jnp.where(causal, _dot_nt(q_dec[j], k_inv[j]), 0.0) for j in grp]
        o_intra = [_dot(p[j], ic[j]) for j in grp]
        updates = [[_dot_tn(ic[j][sl], k_dec[j][sl]) for sl in chunks] for j in grp]
        for j, r0 in enumerate(starts):
            outs = []
            for sl, upd in zip(chunks, updates[j]):
                outs.append(o_intra[j][sl] + _dot_nt(q_dec[j][sl], state_t))
                state_t = state_t * chunk_dec[j][sl.stop - 1:sl.stop] + upd
            o = jnp.concatenate(outs, axis=0)
            o = o * lax.rsqrt(jnp.mean(o * o, axis=-1, keepdims=True) + RMS_EPS) * hn
            o_ref[pl.ds(r0, rows), :] = (o * _silu(g_ref[pl.ds(r0, rows), :])).astype(o_ref.dtype)
        return state_t

    lax.fori_loop(0, seq // (rows * HGRN_GROUP), group, jnp.zeros((HEAD_DIM, HEAD_DIM), F32))


def _hgrn(h3, lb, hn):
    b, s, _ = h3.shape
    nh = HGRN_HEADS
    col = lambda off: pl.BlockSpec((None, s, HEAD_DIM), lambda bi, hi: (bi, 0, off + hi))
    return pl.pallas_call(
        _hgrn_kernel,
        grid=(b, nh),
        in_specs=[col(0), col(nh), col(2 * nh), col(3 * nh),
                  pl.BlockSpec((None, 1, HEAD_DIM), lambda bi, hi: (hi, 0, 0)),
                  pl.BlockSpec((1, HEAD_DIM), lambda bi, hi: (0, 0))],
        out_specs=pl.BlockSpec((None, s, HEAD_DIM), lambda bi, hi: (bi, 0, hi)),
        out_shape=jax.ShapeDtypeStruct((b, s, HGRN_WIDTH), BF16),
        compiler_params=_params("parallel", "parallel"),
        name="hgrn2",
    )(h3, h3, h3, h3, lb, hn)


def _rope(x, cos2, sin2):
    return x * cos2 + pltpu.roll(x, HEAD_DIM // 2, axis=1) * sin2


def _flash_step(q_blk, k_blk, v_blk, mask, carry):
    m, l, acc = carry
    s = jnp.where(mask, _dot_nt(q_blk, k_blk) * HEAD_DIM ** -0.5, NEG_INF)
    m_new = jnp.maximum(m, jnp.max(s, axis=-1, keepdims=True))
    alpha = jnp.exp(m - m_new)
    p = jnp.exp(s - m_new)
    l = alpha * l + jnp.sum(p, axis=-1, keepdims=True)
    acc = alpha * acc + _dot(p, v_blk)
    return m_new, l, acc


def _flash_init():
    blk = ATT_BLOCK
    return (jnp.full((blk, 1), NEG_INF, F32), jnp.zeros((blk, 1), F32), jnp.zeros((blk, HEAD_DIM), F32))


def _dilated_kernel(*refs):
    n_g = len(DIL_GROUPS)
    q_refs, k_refs, v_refs = refs[0:n_g], refs[n_g:2 * n_g], refs[2 * n_g:3 * n_g]
    cos_ref, sin_ref, o_ref = refs[3 * n_g:3 * n_g + 3]
    q_s, k_s, v_s, og_s, lse_s = refs[3 * n_g + 3:]
    seq = o_ref.shape[0]
    blk = DIL_BLOCK
    piece = 256
    grp = range(n_g)

    for gi, (window, d) in enumerate(DIL_GROUPS):
        assert window // d == blk
        seg = seq // d
        k_s[gi, pl.ds(0, blk), :] = jnp.zeros((blk, HEAD_DIM), BF16)
        v_s[gi, pl.ds(0, blk), :] = jnp.zeros((blk, HEAD_DIM), BF16)
        for r in range(d):
            for c0 in range(0, seg, piece):
                n = min(piece, seg)
                rows = pl.ds(r + c0 * d, n, stride=d) if d > 1 else pl.ds(c0, n)
                cos2 = cos_ref[gi, pl.ds(r * seg + c0, n), :]
                sin2 = sin_ref[gi, pl.ds(r * seg + c0, n), :]
                q_s[gi, pl.ds(r * seg + c0, n), :] = _rope(q_refs[gi][rows, :], cos2, sin2).astype(BF16)
                k_s[gi, pl.ds(blk + r * seg + c0, n), :] = _rope(k_refs[gi][rows, :], cos2, sin2).astype(BF16)
                v_s[gi, pl.ds(blk + r * seg + c0, n), :] = v_refs[gi][rows, :].astype(BF16)

    ri = _iota((blk, 2 * blk), 0)
    ci = _iota((blk, 2 * blk), 1)
    rel = ri + blk - ci
    in_window = (rel >= 0) & (rel <= blk)
    dot = functools.partial(jnp.dot, preferred_element_type=F32)

    def q_block(m, _):
        j0 = pl.multiple_of(m * blk, blk)
        segs = [seq // d for _, d in DIL_GROUPS]
        has_prev = [jnp.where(j0 % seg != 0, blk, 0) for seg in segs]
        mask = [ci + has_prev[g] >= blk for g in grp]
        q = [q_s[g, pl.ds(j0, blk), :] for g in grp]
        kw = [k_s[g, pl.ds(j0, 2 * blk), :] for g in grp]
        vw = [v_s[g, pl.ds(j0, 2 * blk), :] for g in grp]
        s = [jnp.where(in_window, jnp.where(mask[g], _dot_nt(q[g], kw[g]) * HEAD_DIM ** -0.5, NEG_INF), NEG_INF)
             for g in grp]
        top = [jnp.max(s[g], axis=-1, keepdims=True) for g in grp]
        p = [jnp.exp(s[g] - top[g]) for g in grp]
        den = [jnp.sum(p[g], axis=-1, keepdims=True) for g in grp]
        o = [dot(p[g].astype(BF16), vw[g]) / den[g] for g in grp]
        lse = [top[g] + jnp.log(den[g]) for g in grp]
        for g, (_, d) in enumerate(DIL_GROUPS):
            seg = segs[g]
            dst = pl.ds((j0 % seg) * d + j0 // seg, blk, stride=d) if d > 1 else pl.ds(j0, blk)
            og_s[g, dst, :] = o[g]
            lse_s[g, dst, :] = jnp.broadcast_to(lse[g], (blk, HEAD_DIM))
        return 0

    lax.fori_loop(0, seq // blk, q_block, 0)

    for c0 in range(0, seq, piece):
        rows = pl.ds(c0, piece)
        lses = [lse_s[g, rows, :] for g in grp]
        top = functools.reduce(jnp.maximum, lses)
        wts = [jnp.exp(x - top) for x in lses]
        den = functools.reduce(lambda a, b: a + b, wts)
        o = functools.reduce(lambda a, b: a + b, [wts[g] * og_s[g, rows, :] for g in grp]) / den
        o_ref[rows, :] = o.astype(o_ref.dtype)


def _dilated(h3, cos2, sin2):
    b, s, _ = h3.shape
    hpg = DIL_HEADS_PER_GROUP
    n_g = len(DIL_GROUPS)
    col = lambda off: pl.BlockSpec((None, s, HEAD_DIM), lambda bi, hi: (bi, 0, off + hi))
    tab = pl.BlockSpec((n_g, s, HEAD_DIM), lambda bi, hi: (0, 0, 0))
    specs = [col(part * DIL_HEADS + gi * hpg) for part in range(3) for gi in range(n_g)]
    residue_major = lambda t: jnp.stack([t.reshape(s // d, d, HEAD_DIM).transpose(1, 0, 2).reshape(s, HEAD_DIM)
                                         for _, d in DIL_GROUPS])
    cos2, sin2 = residue_major(cos2), residue_major(sin2)
    return pl.pallas_call(
        _dilated_kernel,
        grid=(b, hpg),
        in_specs=specs + [tab, tab],
        out_specs=pl.BlockSpec((None, s, HEAD_DIM), lambda bi, hi: (bi, 0, hi)),
        out_shape=jax.ShapeDtypeStruct((b, s, hpg * HEAD_DIM), BF16),
        scratch_shapes=[pltpu.VMEM((n_g, s, HEAD_DIM), BF16), pltpu.VMEM((n_g, s + DIL_BLOCK, HEAD_DIM), BF16),
                        pltpu.VMEM((n_g, s + DIL_BLOCK, HEAD_DIM), BF16), pltpu.VMEM((n_g, s, HEAD_DIM), F32),
                        pltpu.VMEM((n_g, s, HEAD_DIM), F32)],
        compiler_params=_params("parallel", "parallel"),
        name="dilated_attention",
    )(*([h3] * (3 * n_g)), cos2, sin2)


def _moba_kernel(q_ref, k_ref, v_ref, cos_ref, sin_ref, o_ref, qf_s, q_s, k_s, v_s, km_s, sel_s):
    seq = o_ref.shape[0]
    blk = MOBA_BLOCK
    n_blk = seq // blk
    cos2 = cos_ref[...]
    sin2 = sin_ref[...]
    q = _rope(q_ref[...], cos2, sin2)
    qf_s[...] = q
    q_s[...] = q.astype(BF16)
    km_s[...] = jnp.zeros(km_s.shape, F32)
    for nb in range(n_blk):
        kb = _rope(k_ref[pl.ds(nb * blk, blk), :], cos2[nb * blk:(nb + 1) * blk], sin2[nb * blk:(nb + 1) * blk])
        k_s[pl.ds(nb * blk, blk), :] = kb.astype(BF16)
        km_s[pl.ds(nb, 1), :] = jnp.mean(kb, axis=0, keepdims=True)
    v_s[...] = v_ref[...].astype(BF16)

    lane = _iota((blk, LANES), 1).astype(F32)
    causal = _iota((blk, blk), 0) >= _iota((blk, blk), 1)
    rows = lambda nb: pl.ds(nb * blk, blk)

    past = range(1, n_blk)
    km = km_s[...]
    gate = {qb: jnp.where(lane < qb, _dot_nt_hi(qf_s[rows(qb), :], km), -jnp.inf) for qb in past}
    sel = {qb: jnp.zeros((blk, LANES), F32) for qb in past}
    for _k in range(MOBA_TOPK):
        best = {qb: jnp.max(gate[qb], axis=-1, keepdims=True) for qb in past}
        first = {qb: jnp.min(jnp.where(gate[qb] == best[qb], lane, LANES), axis=-1, keepdims=True) for qb in past}
        pick = {qb: (lane == first[qb]) & (best[qb] > -jnp.inf) for qb in past}
        sel = {qb: jnp.where(pick[qb], 1.0, sel[qb]) for qb in past}
        gate = {qb: jnp.where(pick[qb], -jnp.inf, gate[qb]) for qb in past}
    for qb in past:
        sel_s[rows(qb), :] = sel[qb]

    assert sorted(qb for group in MOBA_Q_GROUPS for qb in group) == list(range(n_blk))
    for group in MOBA_Q_GROUPS:
        carry = {qb: _flash_init() for qb in group}
        for j in range(max(group) + 1):
            for qb in group:
                if j < qb:
                    mask = sel_s[rows(qb), :][:, j:j + 1] > 0.0
                elif j == qb:
                    mask = causal
                else:
                    continue
                carry[qb] = _flash_step(q_s[rows(qb), :], k_s[rows(j), :], v_s[rows(j), :], mask, carry[qb])
        for qb in group:
            m, l, acc = carry[qb]
            o_ref[rows(qb), :] = (acc / l).astype(o_ref.dtype)


def _moba(h3, cos2, sin2):
    b, s, _ = h3.shape
    base = 3 * DIL_HEADS
    col = lambda off: pl.BlockSpec((None, s, HEAD_DIM), lambda bi, hi: (bi, 0, base + off + hi))
    tab = pl.BlockSpec((s, HEAD_DIM), lambda bi, hi: (0, 0))
    return pl.pallas_call(
        _moba_kernel,
        grid=(b, MOBA_HEADS),
        in_specs=[col(0), col(MOBA_HEADS), col(2 * MOBA_HEADS), tab, tab],
        out_specs=pl.BlockSpec((None, s, HEAD_DIM), lambda bi, hi: (bi, 0, hi)),
        out_shape=jax.ShapeDtypeStruct((b, s, MOBA_HEADS * HEAD_DIM), BF16),
        scratch_shapes=[pltpu.VMEM((s, HEAD_DIM), F32), pltpu.VMEM((s, HEAD_DIM), BF16),
                        pltpu.VMEM((s, HEAD_DIM), BF16), pltpu.VMEM((s, HEAD_DIM), BF16),
                        pltpu.VMEM((LANES, HEAD_DIM), F32), pltpu.VMEM((s, LANES), F32)],
        compiler_params=_params("parallel", "parallel"),
        name="moba_attention",
    )(h3, h3, h3, cos2, sin2)


def _layer_norm(x, gain, bias):
    mu = jnp.mean(x, axis=-1, keepdims=True)
    xc = x - mu
    var = jnp.mean(xc * xc, axis=-1, keepdims=True)
    return xc * lax.rsqrt(var + LN_EPS) * gain + bias


def _out_ln_kernel(*refs, n_parts):
    o_refs = refs[0:n_parts]
    w_refs = refs[n_parts:2 * n_parts]
    h_ref, gain_ref, bias_ref, rw_ref, y_ref, logit_ref = refs[2 * n_parts:]
    half = h_ref.shape[0] // 2
    halves = [pl.ds(k * half, half) for k in range(2)]
    mix = []
    for rows in halves:
        parts = [jnp.dot(o_r[rows, :], w_r[...], preferred_element_type=F32) for o_r, w_r in zip(o_refs, w_refs)]
        mix.append(functools.reduce(lambda a, b: a + b, parts))
    y = [_layer_norm(DEEPNORM_ALPHA * h_ref[rows, :] + mix[k], gain_ref[...], bias_ref[...])
         for k, rows in enumerate(halves)]
    logits = [_dot3(y[k], rw_ref[...]) for k in range(2)]
    for k, rows in enumerate(halves):
        y_ref[rows, :] = y[k]
        logit_ref[rows, :] = logits[k]


def _out_ln(parts, weights, h, gain, bias, router_w, tm):
    t, d = h.shape
    n_parts = len(parts)
    row = lambda width: pl.BlockSpec((tm, width), lambda i: (i, 0))
    full = lambda a: pl.BlockSpec(a.shape, lambda i: (0, 0))
    return pl.pallas_call(
        functools.partial(_out_ln_kernel, n_parts=n_parts),
        grid=(t // tm,),
        in_specs=[row(p.shape[1]) for p in parts] + [full(w) for w in weights]
                 + [row(d), full(gain), full(bias), full(router_w)],
        out_specs=[row(d), row(LANES)],
        out_shape=[jax.ShapeDtypeStruct((t, d), F32), jax.ShapeDtypeStruct((t, LANES), F32)],
        compiler_params=_params("parallel"),
        name="out_proj_ln",
    )(*parts, *weights, h, gain, bias, router_w)


def _route(logits_t, rbias_col):
    row = _iota(logits_t.shape, 0)
    scores = jax.nn.sigmoid(logits_t)
    biased = scores + rbias_col

    def first_argmax(vals):
        best = jnp.max(vals, axis=0, keepdims=True)
        return best, jnp.min(jnp.where(vals == best, row, N_EXPERTS), axis=0, keepdims=True)

    best_score = None
    best_group = None
    for g in range(N_EXPERT_GROUPS):
        vals = jnp.where(row // EXPERTS_PER_GROUP == g, biased, -jnp.inf)
        top1, idx1 = first_argmax(vals)
        top2, _ = first_argmax(jnp.where(row == idx1, -jnp.inf, vals))
        score = top1 + top2
        if g == 0:
            best_score, best_group = score, jnp.zeros_like(idx1)
        else:
            better = score > best_score
            best_group = jnp.where(better, g, best_group)
            best_score = jnp.where(better, score, best_score)
    masked = jnp.where(row // EXPERTS_PER_GROUP == best_group, biased, NEG_INF)
    _, i1 = first_argmax(masked)
    _, i2 = first_argmax(jnp.where(row == i1, -jnp.inf, masked))
    s1 = jnp.sum(jnp.where(row == i1, scores, 0.0), axis=0, keepdims=True)
    s2 = jnp.sum(jnp.where(row == i2, scores, 0.0), axis=0, keepdims=True)
    tot = s1 + s2
    return i1, i2, s1 / tot, s2 / tot


def _route_kernel(logit_ref, rbias_ref, loc_ref, w_ref, tab_ref, cnt_ref, carry_s, total_s):
    sweep = pl.program_id(0)
    i = pl.program_id(1)
    tm = logit_ref.shape[0]
    block = float(MOE_BLOCK_ROWS)
    align = float(MOE_ALIGN)

    @pl.when((i == 0) & (sweep == 1))
    def _():
        total_s[...] = carry_s[...]

    @pl.when(i == 0)
    def _():
        carry_s[...] = jnp.zeros(carry_s.shape, F32)

    logits_t = logit_ref[...].T[:N_EXPERTS]
    i1, i2, w1, w2 = _route(logits_t, rbias_ref[...][:N_EXPERTS])
    row = _iota((N_EXPERTS, tm), 0)
    chosen = jnp.where((row == i1) | (row == i2), 1.0, 0.0)
    count = jnp.sum(chosen, axis=1, keepdims=True)
    padded = jnp.floor((count + (align - 1.0)) * (1.0 / align)) * align

    e_r = _iota((N_EXPERTS, N_EXPERTS), 0)
    e_c = _iota((N_EXPERTS, N_EXPERTS), 1)

    def expert_prefix(col, inclusive):
        tri = ((e_c <= e_r) if inclusive else (e_c < e_r)).astype(BF16)
        wide = jnp.broadcast_to(col, (N_EXPERTS, LANES)).astype(BF16)
        return jnp.dot(tri, wide, preferred_element_type=F32)[:, 0:1]

    @pl.when(sweep == 0)
    def _():
        loc_ref[...] = jnp.zeros(loc_ref.shape, loc_ref.dtype)
        w_ref[...] = jnp.zeros(w_ref.shape, w_ref.dtype)
        tab_ref[...] = jnp.zeros(tab_ref.shape, tab_ref.dtype)
        cnt_ref[...] = jnp.zeros(cnt_ref.shape, cnt_ref.dtype)

    @pl.when(sweep == 1)
    def _():
        total = total_s[...]
        blocks = jnp.floor((total + (block - 1.0)) * (1.0 / block))
        region = (expert_prefix(blocks, True) - blocks) * block + carry_s[...]
        run_start = expert_prefix(padded, False)
        earlier = (_iota((tm, tm), 0) < _iota((tm, tm), 1)).astype(BF16)
        slot = run_start + jnp.dot(chosen.astype(BF16), earlier, preferred_element_type=F32)
        l1 = jnp.sum(jnp.where(row == i1, slot, 0.0), axis=0, keepdims=True)
        l2 = jnp.sum(jnp.where(row == i2, slot, 0.0), axis=0, keepdims=True)
        out_row = _iota((8, tm), 0)
        loc_ref[...] = jnp.where(out_row == 0, l1, jnp.where(out_row == 1, l2, 0.0)).astype(jnp.int32)
        w_ref[...] = jnp.where(out_row == 0, w1, jnp.where(out_row == 1, w2, 0.0))
        lane = _iota((N_EXPERTS, LANES), 1)
        tab = jnp.where(lane == 0, padded, jnp.where(lane == 1, run_start, jnp.where(lane == 2, region, 0.0)))
        tab_ref[...] = tab.astype(jnp.int32)
        cnt_ref[...] = jnp.broadcast_to(total, cnt_ref.shape).astype(jnp.int32)

    carry_s[...] = carry_s[...] + padded


def _route_tokens(logits, rbias_col):
    t = logits.shape[0]
    tm = MOE_TILE
    n_tiles = t // tm
    tok = pl.BlockSpec((None, 8, tm), lambda s, i: (s, 0, i))
    loc, w, tab, totals = pl.pallas_call(
        _route_kernel,
        grid=(2, n_tiles),
        in_specs=[pl.BlockSpec((tm, LANES), lambda s, i: (i, 0)), pl.BlockSpec((LANES, 1), lambda s, i: (0, 0))],
        out_specs=[tok, tok, pl.BlockSpec((None, None, N_EXPERTS, LANES), lambda s, i: (s, i, 0, 0)),
                   pl.BlockSpec((N_EXPERTS, LANES), lambda s, i: (0, 0))],
        out_shape=[jax.ShapeDtypeStruct((2, 8, t), jnp.int32), jax.ShapeDtypeStruct((2, 8, t), F32),
                   jax.ShapeDtypeStruct((2, n_tiles, N_EXPERTS, LANES), jnp.int32),
                   jax.ShapeDtypeStruct((N_EXPERTS, LANES), jnp.int32)],
        scratch_shapes=[pltpu.VMEM((N_EXPERTS, 1), F32), pltpu.VMEM((N_EXPERTS, 1), F32)],
        compiler_params=_params("arbitrary", "arbitrary"),
        name="moe_route",
    )(logits, rbias_col)
    return loc[1], w[1], tab[1], totals[:, 0]


def _run_pieces(length, sizes):
    return [(size, length & ~(2 * size - 1), (length & size) != 0) for size in sizes]


def _tile_run_copies(cnt_ref, off_ref, region_ref, tile, compact, sorted_rows, sem, to_sorted):
    copies = []
    for e in range(N_EXPERTS):
        k = tile * N_EXPERTS + e
        length = pl.multiple_of(cnt_ref[k], MOE_ALIGN)
        small = compact.at[pl.ds(pl.multiple_of(off_ref[k], MOE_ALIGN), length)]
        big = sorted_rows.at[pl.ds(pl.multiple_of(region_ref[k], MOE_ALIGN), length)]
        copies.append((length > 0, pltpu.make_async_copy(small, big, sem) if to_sorted
                       else pltpu.make_async_copy(big, small, sem)))
    return copies


def _dispatch_kernel(cnt_ref, off_ref, region_ref, pad_start_ref, pad_len_ref, n_valid_ref,
                     x_ref, loc_ref, w_ref, xs_ref, buf_s, zero_s, sem, zero_sem):
    i = pl.program_id(0)
    d = x_ref.shape[1]
    cap = buf_s.shape[1]
    tm = x_ref.shape[0]
    n_blocks = xs_ref.shape[0] // MOE_BLOCK_ROWS

    @pl.when(i == 0)
    def _():
        zero_s[...] = jnp.zeros(zero_s.shape, zero_s.dtype)
        sizes = [MOE_BLOCK_ROWS >> s for s in range((MOE_BLOCK_ROWS // MOE_ALIGN).bit_length())]
        copies = []
        for e in range(N_EXPERTS):
            for size, offset, used in _run_pieces(pad_len_ref[e], sizes):
                dst = xs_ref.at[pl.ds(pl.multiple_of(pad_start_ref[e] + offset, MOE_ALIGN), size)]
                copies.append((used, pltpu.make_async_copy(zero_s.at[pl.ds(0, size)], dst, zero_sem)))
        for j in range(n_blocks - N_EXPERTS, n_blocks):
            dst = xs_ref.at[pl.ds(j * MOE_BLOCK_ROWS, MOE_BLOCK_ROWS)]
            copies.append((j >= n_valid_ref[0], pltpu.make_async_copy(zero_s, dst, zero_sem)))
        for action in ("start", "wait"):
            for used, copy in copies:
                @pl.when(used)
                def _():
                    getattr(copy, action)()

    loc = loc_ref[...]
    w = w_ref[...]
    row = _iota((cap, tm), 0)
    hit1 = row == loc[0:1, :]
    hit2 = row == loc[1:2, :]
    perm = jnp.where(hit1, 1.0, jnp.where(hit2, 1.0, 0.0)).astype(BF16)
    gate = jnp.sum(jnp.where(hit1, w[0:1, :], jnp.where(hit2, w[1:2, :], 0.0)), axis=1, keepdims=True)
    g_hi = gate.astype(BF16)
    g_lo = (gate - g_hi.astype(F32)).astype(BF16)
    lane = _iota((cap, LANES), 1)
    slot = i % 2
    buf_s[slot, :, pl.ds(0, d)] = jnp.dot(perm, x_ref[...].astype(BF16), preferred_element_type=F32).astype(BF16)
    buf_s[slot, :, pl.ds(d, LANES)] = jnp.where(lane == 0, g_hi.astype(F32),
                                                jnp.where(lane == 1, g_lo.astype(F32), 0.0)).astype(BF16)

    def run_copies(tile, which, action):
        for used, copy in _tile_run_copies(cnt_ref, off_ref, region_ref, tile, buf_s.at[which], xs_ref,
                                           sem.at[which], True):
            @pl.when(used)
            def _():
                getattr(copy, action)()

    run_copies(i, slot, "start")

    @pl.when(i > 0)
    def _():
        run_copies(i - 1, 1 - slot, "wait")

    @pl.when(i == pl.num_programs(0) - 1)
    def _():
        run_copies(i, slot, "wait")


def _dispatch(tables, pad_start, pad_len, n_valid, x, loc, w, n_rows):
    t, d = x.shape
    tm = MOE_TILE
    cap = MOE_TOPK * tm + N_EXPERTS * MOE_ALIGN
    tok = pl.BlockSpec((8, tm), lambda i, *_: (0, i))
    grid_spec = pltpu.PrefetchScalarGridSpec(
        num_scalar_prefetch=6,
        grid=(t // tm,),
        in_specs=[pl.BlockSpec((tm, d), lambda i, *_: (i, 0)), tok, tok],
        out_specs=pl.BlockSpec(memory_space=pl.ANY),
        scratch_shapes=[pltpu.VMEM((2, cap, d + LANES), BF16), pltpu.VMEM((MOE_BLOCK_ROWS, d + LANES), BF16),
                        pltpu.SemaphoreType.DMA((2,)), pltpu.SemaphoreType.DMA(())],
    )
    return pl.pallas_call(
        _dispatch_kernel,
        grid_spec=grid_spec,
        out_shape=jax.ShapeDtypeStruct((n_rows, d + LANES), BF16),
        compiler_params=_params("arbitrary"),
        name="moe_dispatch",
    )(*tables, pad_start, pad_len, n_valid, x, loc, w)


def _expert_kernel(blk_expert_ref, next_expert_ref, n_valid_ref, x_ref, wg_ref, wu_ref, wd_ref, y_ref,
                   wg_s, wu_s, wd_s, wg_buf, wu_buf, wd_buf, slot_s, sem, *, layer):
    i = pl.program_id(0)

    def fetch(expert, slot):
        return [pltpu.make_async_copy(src.at[layer, expert], dst.at[slot], sem.at[slot])
                for src, dst in ((wg_ref, wg_buf), (wu_ref, wu_buf), (wd_ref, wd_buf))]

    @pl.when(i == 0)
    def _():
        slot_s[0] = 1
        for copy in fetch(blk_expert_ref[0], 0):
            copy.start()

    @pl.when((i == 0) | (blk_expert_ref[i] != blk_expert_ref[jnp.maximum(i - 1, 0)]))
    def _():
        slot = 1 - slot_s[0]
        slot_s[0] = slot
        for copy in fetch(blk_expert_ref[i], slot):
            copy.wait()
        wg_s[...] = wg_buf[slot].astype(BF16)
        wu_s[...] = wu_buf[slot].astype(BF16)
        wd_s[...] = wd_buf[slot].astype(BF16)

        @pl.when(next_expert_ref[i] >= 0)
        def _():
            for copy in fetch(next_expert_ref[i], 1 - slot):
                copy.start()

    @pl.when(i < n_valid_ref[0])
    def _():
        d = wg_s.shape[0]
        x = x_ref[:, pl.ds(0, d)]
        extra = x_ref[:, pl.ds(d, LANES)].astype(F32)
        gate = extra[:, 0:1] + extra[:, 1:2]
        hid = _silu(jnp.dot(x, wg_s[...], preferred_element_type=F32)) * jnp.dot(
            x, wu_s[...], preferred_element_type=F32)
        y_ref[...] = _dot(hid * gate, wd_s[...]).astype(y_ref.dtype)

    @pl.when(i >= n_valid_ref[0])
    def _():
        y_ref[...] = jnp.zeros(y_ref.shape, y_ref.dtype)


def _experts(blk_expert, next_expert, n_valid, xs, wg, wu, wd, layer, n_blocks):
    d = wg.shape[-2]
    f = wg.shape[-1]
    rows = MOE_BLOCK_ROWS
    n_rows = n_blocks * rows
    hbm = pl.BlockSpec(memory_space=pl.ANY)
    grid_spec = pltpu.PrefetchScalarGridSpec(
        num_scalar_prefetch=3,
        grid=(n_blocks,),
        in_specs=[pl.BlockSpec((rows, xs.shape[1]), lambda i, be, ne, nv: (i, 0)), hbm, hbm, hbm],
        out_specs=pl.BlockSpec((rows, d), lambda i, be, ne, nv: (i, 0)),
        scratch_shapes=[pltpu.VMEM((d, f), BF16), pltpu.VMEM((d, f), BF16), pltpu.VMEM((f, d), BF16),
                        pltpu.VMEM((2, d, f), F32), pltpu.VMEM((2, d, f), F32), pltpu.VMEM((2, f, d), F32),
                        pltpu.SMEM((1,), jnp.int32), pltpu.SemaphoreType.DMA((2,))],
    )
    return pl.pallas_call(
        functools.partial(_expert_kernel, layer=layer),
        grid_spec=grid_spec,
        out_shape=jax.ShapeDtypeStruct((n_rows, d), BF16),
        compiler_params=_params("arbitrary"),
        name="moe_experts",
    )(blk_expert, next_expert, n_valid, xs, wg, wu, wd)


def _combine_kernel(cnt_ref, off_ref, region_ref, ys_ref, loc_ref, h_ref, gain_ref, bias_ref, y_ref, yb_ref,
                    buf, sem):
    i = pl.program_id(0)
    n = pl.num_programs(0)
    tm = h_ref.shape[0]
    cap = buf.shape[1]

    def fetch(tile, slot, action):
        for used, copy in _tile_run_copies(cnt_ref, off_ref, region_ref, tile, buf.at[slot], ys_ref,
                                           sem.at[slot], False):
            @pl.when(used)
            def _():
                getattr(copy, action)()

    slot = i % 2

    @pl.when(i == 0)
    def _():
        buf[...] = jnp.zeros(buf.shape, buf.dtype)
        fetch(0, 0, "start")

    @pl.when(i + 1 < n)
    def _():
        fetch(i + 1, 1 - slot, "start")

    fetch(i, slot, "wait")
    loc = loc_ref[...].astype(F32)
    eye = _iota((tm, tm), 0) == _iota((tm, tm), 1)
    loc_col = [jnp.sum(jnp.where(eye, jnp.broadcast_to(loc[k:k + 1, :], (tm, tm)), 0.0), axis=1, keepdims=True)
               for k in range(MOE_TOPK)]
    lane = _iota((tm, cap), 1).astype(F32)
    pick = jnp.where(lane == loc_col[0], 1.0, jnp.where(lane == loc_col[1], 1.0, 0.0)).astype(BF16)
    ffn = jnp.dot(pick, buf[slot], preferred_element_type=F32)
    y = _layer_norm(DEEPNORM_ALPHA * h_ref[...] + ffn, gain_ref[...], bias_ref[...])
    y_ref[...] = y
    yb_ref[...] = y.astype(BF16)


def _combine_ln(tables, ys, loc, h, gain, bias):
    t, d = h.shape
    tm = MOE_TILE
    cap = MOE_TOPK * tm + N_EXPERTS * MOE_ALIGN
    grid_spec = pltpu.PrefetchScalarGridSpec(
        num_scalar_prefetch=3,
        grid=(t // tm,),
        in_specs=[pl.BlockSpec(memory_space=pl.ANY),
                  pl.BlockSpec((8, tm), lambda i, *_: (0, i)),
                  pl.BlockSpec((tm, d), lambda i, *_: (i, 0)),
                  pl.BlockSpec((1, d), lambda i, *_: (0, 0)),
                  pl.BlockSpec((1, d), lambda i, *_: (0, 0))],
        out_specs=[pl.BlockSpec((tm, d), lambda i, *_: (i, 0)), pl.BlockSpec((tm, d), lambda i, *_: (i, 0))],
        scratch_shapes=[pltpu.VMEM((2, cap, d), BF16), pltpu.SemaphoreType.DMA((2,))],
    )
    return pl.pallas_call(
        _combine_kernel,
        grid_spec=grid_spec,
        out_shape=[jax.ShapeDtypeStruct((t, d), F32), jax.ShapeDtypeStruct((t, d), BF16)],
        compiler_params=_params("arbitrary"),
        name="moe_combine_ln",
    )(*tables, ys, loc, h, gain, bias)


def _moe_ln(h, logits, rbias, wg, wu, wd, layer, gain, bias):
    t, d = h.shape
    rows = MOE_BLOCK_ROWS
    n_tiles = t // MOE_TILE
    n_blocks = -(-(MOE_TOPK * t + n_tiles * N_EXPERTS * (MOE_ALIGN - 1)) // rows) + N_EXPERTS
    loc, w_t, tab, counts = _route_tokens(logits, rbias)
    tables = tuple(tab[:, :, k].reshape(-1) for k in range(3))
    blocks_per_expert = (counts + rows - 1) // rows
    block_end = jnp.cumsum(blocks_per_expert)
    row_start = (block_end - blocks_per_expert) * rows
    n_valid = block_end[-1:].astype(jnp.int32)
    blk = jnp.arange(n_blocks, dtype=jnp.int32)
    blk_expert = jnp.sum(jnp.minimum(blk, n_valid - 1)[:, None] >= block_end[None, :], axis=1).astype(jnp.int32)
    pad_start = (row_start + counts).astype(jnp.int32)
    pad_len = (blocks_per_expert * rows - counts).astype(jnp.int32)
    xs = _dispatch(tables, pad_start, pad_len, n_valid, h, loc, w_t, n_blocks * rows)
    experts = jnp.arange(N_EXPERTS, dtype=jnp.int32)
    later_used = (experts[None, :] > experts[:, None]) & (blocks_per_expert[None, :] > 0)
    next_used = jnp.min(jnp.where(later_used, experts[None, :], N_EXPERTS), axis=1)
    next_expert = jnp.where(next_used < N_EXPERTS, next_used, -1).astype(jnp.int32)[blk_expert]
    ys = _experts(blk_expert, next_expert, n_valid, xs, wg, wu, wd, layer, n_blocks)
    return _combine_ln(tables, ys, loc, h, gain, bias)


def _rope_tables(seq):
    inv_freq = ROPE_THETA ** (-jnp.arange(0, HEAD_DIM, 2, dtype=F32) / HEAD_DIM)
    ang = jnp.arange(seq, dtype=F32)[:, None] * inv_freq[None, :]
    cos, sin = jnp.cos(ang), jnp.sin(ang)
    return jnp.concatenate([cos, cos], axis=-1), jnp.concatenate([-sin, sin], axis=-1)


def _even_mixer(hb, b, s, w_in, conv_w, a_log, dt_bias, gdn_norm, hgrn_norm, lower_bound):
    gw = GDN_WIDTH
    n_small = 2 * GDN_HEADS
    tail0 = 4 * gw
    w_t = w_in.T
    w_small = jnp.pad(w_t[tail0:tail0 + n_small], ((0, LANES - n_small), (0, 0))).astype(BF16)
    h_a = _matmul(hb, w_t[:tail0].astype(BF16), 1024, 1024, transposed=True).reshape(b, s, tail0)
    h_b = _matmul(hb, w_t[tail0 + n_small:].astype(BF16), 1024, 1024, transposed=True).reshape(b, s, 4 * HGRN_WIDTH)
    n_chunks = s // GDN_CHUNK
    small = _matmul(hb, w_small, 1024, LANES, transposed=True).reshape(b, s, LANES)
    to_rows = lambda a: a.transpose(0, 2, 1).reshape(b, GDN_HEADS, n_chunks, GDN_CHUNK)
    b_rows, a_rows = to_rows(small[..., :GDN_HEADS]), to_rows(small[..., GDN_HEADS:n_small])
    headvec = lambda v: jnp.broadcast_to(v.astype(F32)[:, None, None], (GDN_HEADS, 1, GDN_CHUNK))
    o_a = _gdn(h_a, conv_w.astype(F32), a_rows, b_rows, headvec(a_log), headvec(dt_bias),
               gdn_norm.astype(F32).reshape(1, HEAD_DIM))
    o_b = _hgrn(h_b, lower_bound.astype(F32).reshape(HGRN_HEADS, 1, HEAD_DIM),
                hgrn_norm.astype(F32).reshape(1, HEAD_DIM))
    return [o_a.reshape(b * s, GDN_WIDTH), o_b.reshape(b * s, HGRN_WIDTH)]


def _odd_mixer(hb, b, s, w_in, cos2, sin2):
    h = _matmul(hb, w_in.astype(BF16), 1024, 768)
    h3 = h.reshape(b, s, ODD_COLS)
    o_c = _dilated(h3, cos2, sin2)
    o_d = _moba(h3, cos2, sin2)
    return [o_c.reshape(b * s, -1), o_d.reshape(b * s, -1)]


def kernel(x, ev_w_in, ev_conv_w, ev_a_log, ev_dt_bias, ev_gdn_norm, ev_hgrn_norm, hgrn_lb_logits, ev_w_out,
           od_w_in, od_w_out, router_w, router_bias, moe_w_gate, moe_w_up, moe_w_down, ln_gain, ln_bias):
    b, s, d = x.shape
    t = b * s
    cos2, sin2 = _rope_tables(s)
    lower_bounds = jnp.cumsum(jax.nn.softmax(hgrn_lb_logits.astype(F32), axis=0), axis=0)
    rw = jnp.pad(router_w.astype(F32), ((0, 0), (0, LANES - N_EXPERTS)))
    rbias = jnp.pad(router_bias.astype(F32), (0, LANES - N_EXPERTS)).reshape(LANES, 1)
    vec = lambda v: v.astype(F32).reshape(1, d)

    h = x.reshape(t, d)
    hb = h
    for layer in range(DEPTH):
        if layer % 2 == 0:
            e = layer // 2
            parts = _even_mixer(hb, b, s, ev_w_in[e], ev_conv_w[e], ev_a_log[e], ev_dt_bias[e], ev_gdn_norm[e],
                                ev_hgrn_norm[e], lower_bounds[layer])
            w_out = ev_w_out[e].astype(BF16)
        else:
            o = layer // 2
            parts = _odd_mixer(hb, b, s, od_w_in[o], cos2, sin2)
            w_out = od_w_out[o].astype(BF16)
        splits = np.cumsum([p.shape[1] for p in parts])[:-1]
        weights = jnp.split(w_out, splits, axis=0)
        h, logits = _out_ln(parts, weights, h, vec(ln_gain[layer, 0]), vec(ln_bias[layer, 0]), rw, 256)
        h, hb = _moe_ln(h, logits, rbias, moe_w_gate, moe_w_up, moe_w_down, layer,
                        vec(ln_gain[layer, 1]), vec(ln_bias[layer, 1]))
    return h.reshape(b, s, d)
```

```python
import functools

import jax
import jax.numpy as jnp
import numpy as np
from jax import lax
from jax.experimental import pallas as pl
from jax.experimental.pallas import tpu as pltpu

F32 = jnp.float32
BF16 = jnp.bfloat16

DEPTH = 2
HEAD_DIM = 128
GDN_HEADS = 8
GDN_CONV = 4
GDN_CHUNK = 64
GDN_WIDTH = GDN_HEADS * HEAD_DIM
HGRN_HEADS = 8
HGRN_CHUNK = 16
HGRN_WIDTH = HGRN_HEADS * HEAD_DIM
DIL_GROUPS = ((128, 1), (512, 4), (2048, 16))
DIL_HEADS_PER_GROUP = 4
DIL_HEADS = len(DIL_GROUPS) * DIL_HEADS_PER_GROUP
MOBA_HEADS = 4
MOBA_BLOCK = 256
MOBA_TOPK = 3
ROPE_THETA = 10000.0
N_EXPERTS = 16
N_EXPERT_GROUPS = 4
EXPERTS_PER_GROUP = N_EXPERTS // N_EXPERT_GROUPS
MOE_TOPK = 2
MOE_BLOCK_ROWS = 512
MOE_TILE = 512
MOE_ALIGN = 16
DEEPNORM_ALPHA = (2.0 * DEPTH) ** 0.25
LN_EPS = 1e-5
RMS_EPS = 1e-6
NEG_INF = -1e30

LANES = 128
VMEM_LIMIT = 56 * 1024 * 1024
EVEN_PROJ_TILE = (1024, 1024)
ODD_PROJ_TILE = (2048, 768)
OUT_PROJ_ROWS = 256
ATT_BLOCK = 256
MOBA_Q_GROUPS = ((7, 0, 6, 1), (5, 2, 4, 3))
DIL_Q_PER_STEP = 4
DIL_BLOCK = 128
GDN_HEADS_PER_STEP = 2
GDN_GROUP = 16
HGRN_GROUP = 4
HGRN_ROWS = 256

ODD_COLS = 3 * DIL_HEADS * HEAD_DIM + 3 * MOBA_HEADS * HEAD_DIM


def _dot(a, b):
    return jnp.dot(a.astype(BF16), b.astype(BF16), preferred_element_type=F32)


def _dot_nt(a, b):
    return lax.dot_general(a.astype(BF16), b.astype(BF16), (((1,), (1,)), ((), ())),
                           preferred_element_type=F32)


def _dot_tn(a, b):
    return lax.dot_general(a.astype(BF16), b.astype(BF16), (((0,), (0,)), ((), ())),
                           preferred_element_type=F32)


def _dot_hi(a, b):
    return jnp.dot(a, b, preferred_element_type=F32, precision=lax.Precision.HIGHEST)


def _dot_nt_hi(a, b):
    return lax.dot_general(a, b, (((1,), (1,)), ((), ())), preferred_element_type=F32,
                           precision=lax.Precision.HIGHEST)


def _dot3(a, b):
    a_hi = a.astype(BF16)
    b_hi = b.astype(BF16)
    a_lo = (a - a_hi.astype(F32)).astype(BF16)
    b_lo = (b - b_hi.astype(F32)).astype(BF16)
    dot = functools.partial(jnp.dot, preferred_element_type=F32)
    return dot(a_hi, b_hi) + (dot(a_hi, b_lo) + dot(a_lo, b_hi))


_dot_inv = _dot


def _silu(x):
    return x * jax.nn.sigmoid(x)


def _iota(shape, dim):
    return lax.broadcasted_iota(jnp.int32, shape, dim)


def _params(*sem):
    return pltpu.CompilerParams(dimension_semantics=sem, vmem_limit_bytes=VMEM_LIMIT)


def _mm_kernel(x_ref, w_ref, o_ref, *, transposed):
    x = x_ref[...].astype(BF16)
    prod = _dot_nt(x, w_ref[...]) if transposed else jnp.dot(x, w_ref[...], preferred_element_type=F32)
    o_ref[...] = prod.astype(o_ref.dtype)


def _matmul(x, w, tm, tn, transposed=False):
    m, k = x.shape
    n = w.shape[0] if transposed else w.shape[1]
    assert m % tm == 0 and n % tn == 0
    w_spec = pl.BlockSpec((tn, k), lambda i, j: (j, 0)) if transposed else pl.BlockSpec((k, tn), lambda i, j: (0, j))
    return pl.pallas_call(
        functools.partial(_mm_kernel, transposed=transposed),
        grid=(m // tm, n // tn),
        in_specs=[pl.BlockSpec((tm, k), lambda i, j: (i, 0)), w_spec],
        out_specs=pl.BlockSpec((tm, tn), lambda i, j: (i, j)),
        out_shape=jax.ShapeDtypeStruct((m, n), F32),
        compiler_params=_params("parallel", "parallel"),
        name="in_proj",
    )(x, w)


def _gdn_kernel(q_ref, k_ref, v_ref, z_ref, cwq_ref, cwk_ref, cwv_ref, a_ref, b_ref, alog_ref, dt_ref,
                gn_ref, o_ref, pad_s, q_s, k_s, v_s, gcum_s, beta_s, qe_s, ob_s, sm_s, sa_s):
    seq = q_ref.shape[0]
    c = GDN_CHUNK
    n_chunks = seq // c
    rows = 256
    heads = range(GDN_HEADS_PER_STEP)
    lanes = [slice(hh * HEAD_DIM, (hh + 1) * HEAD_DIM) for hh in heads]

    def conv_norm(hh):
        pad_s[pl.ds(0, 8), :] = jnp.zeros((8, HEAD_DIM), F32)
        for x_ref, cw_ref, dst, mode in ((q_ref, cwq_ref, q_s, "q"), (k_ref, cwk_ref, k_s, "k"),
                                         (v_ref, cwv_ref, v_s, "v")):
            pad_s[pl.ds(8, seq), :] = x_ref[:, lanes[hh]]
            cw = cw_ref[:, lanes[hh]]
            for r in range(seq // rows):
                acc = None
                for j in range(GDN_CONV):
                    tap = pad_s[pl.ds(8 + r * rows - (GDN_CONV - 1) + j, rows), :] * cw[j:j + 1, :]
                    acc = tap if acc is None else acc + tap
                y = _silu(acc)
                if mode != "v":
                    y = y * lax.rsqrt(jnp.sum(y * y, axis=-1, keepdims=True) + RMS_EPS)
                if mode == "q":
                    y = y * HEAD_DIM ** -0.5
                dst[pl.ds(r * rows, rows), :] = y

    upper = (_iota((c, c), 0) <= _iota((c, c), 1)).astype(F32)
    for hh in heads:
        g = -jnp.exp(alog_ref[hh]) * jax.nn.softplus(a_ref[hh] + dt_ref[hh])
        gcum_s[hh] = _dot_hi(g, upper)
        beta_s[hh] = jax.nn.sigmoid(b_ref[hh])

    ri = _iota((c, c), 0)
    ci = _iota((c, c), 1)
    eye = ri == ci
    strict = ri > ci
    incl = ri >= ci
    eye_f = eye.astype(F32)
    level1 = ri // 2 == ci // 2
    levels = []
    s = 2
    while s < c:
        levels.append((ri // (2 * s) == ci // (2 * s)) & ((ri // s) % 2 == 1) & ((ci // s) % 2 == 0))
        s *= 2

    dot = functools.partial(jnp.dot, preferred_element_type=F32)

    def to_col(row):
        return jnp.sum(jnp.where(eye, jnp.broadcast_to(row, (c, c)), 0.0), axis=1, keepdims=True)

    def prepare(i, _, hh):
        n0 = i * GDN_GROUP
        grp = range(GDN_GROUP)
        starts = [pl.multiple_of((n0 + j) * c, c) for j in grp]
        qc = [q_s[pl.ds(r0, c), :] for r0 in starts]
        kc = [k_s[pl.ds(r0, c), :] for r0 in starts]
        vc = [v_s[pl.ds(r0, c), :] for r0 in starts]
        g_row = [gcum_s[hh, pl.ds(n0 + j, 1), :] for j in grp]
        g_col = [to_col(g_row[j]) for j in grp]
        b_col = [to_col(beta_s[hh, pl.ds(n0 + j, 1), :]) for j in grp]
        decay = [jnp.exp(jnp.where(incl, g_col[j] - g_row[j], 0.0)) for j in grp]
        n_mat = [b_col[j] * jnp.where(strict, decay[j], 0.0) * _dot_nt(kc[j], kc[j]) for j in grp]
        inv = [eye_f - jnp.where(level1, n_mat[j], 0.0) for j in grp]
        for blk in levels:
            tmp = [_dot_inv(inv[j], jnp.where(blk, n_mat[j], 0.0)) for j in grp]
            inv = [inv[j] - _dot_inv(tmp[j], inv[j]) for j in grp]
        e_col = [jnp.exp(g_col[j]) for j in grp]
        sol = [_dot_inv(inv[j], jnp.concatenate([b_col[j] * vc[j], (b_col[j] * e_col[j]) * kc[j]], axis=1))
               for j in grp]
        qk = [(_dot_nt(qc[j], kc[j]) * jnp.where(incl, decay[j], 0.0)).astype(BF16) for j in grp]
        ub = [sol[j][:, :HEAD_DIM].astype(BF16) for j in grp]
        w = [sol[j][:, HEAD_DIM:].astype(BF16) for j in grp]
        kd = [(kc[j] * jnp.exp(g_row[j][:, c - 1:c] - g_col[j])).astype(BF16) for j in grp]
        q_eff = [(qc[j] * e_col[j] - dot(qk[j], w[j])).astype(BF16) for j in grp]
        o_base = [dot(qk[j], ub[j]) for j in grp]
        s_mat = [_dot_tn(kd[j], w[j]).astype(BF16) for j in grp]
        s_add = [_dot_tn(kd[j], ub[j]) for j in grp]
        for j, r0 in enumerate(starts):
            m0 = pl.multiple_of((n0 + j) * HEAD_DIM, HEAD_DIM)
            qe_s[hh, pl.ds(r0, c), :] = q_eff[j]
            ob_s[hh, pl.ds(r0, c), :] = o_base[j]
            sm_s[hh, pl.ds(m0, HEAD_DIM), :] = s_mat[j]
            sa_s[hh, pl.ds(m0, HEAD_DIM), :] = s_add[j]
        return 0

    for hh in heads:
        conv_norm(hh)
        lax.fori_loop(0, n_chunks // GDN_GROUP, functools.partial(prepare, hh=hh), 0)

    gn = gn_ref[...]

    def chunk(n, states):
        r0 = pl.multiple_of(n * c, c)
        m0 = pl.multiple_of(n * HEAD_DIM, HEAD_DIM)
        g_last = [gcum_s[hh, pl.ds(n, 1), :][:, c - 1:c] for hh in heads]
        lhs = [jnp.concatenate([qe_s[hh, pl.ds(r0, c), :], sm_s[hh, pl.ds(m0, HEAD_DIM), :]], axis=0)
               for hh in heads]
        prod = [dot(lhs[hh], states[hh].astype(BF16)) for hh in heads]
        for hh in heads:
            ob_s[hh, pl.ds(r0, c), :] = prod[hh][:c] + ob_s[hh, pl.ds(r0, c), :]
        return tuple(jnp.exp(g_last[hh]) * states[hh] - prod[hh][c:] + sa_s[hh, pl.ds(m0, HEAD_DIM), :]
                     for hh in heads)

    lax.fori_loop(0, n_chunks, chunk, tuple(jnp.zeros((HEAD_DIM, HEAD_DIM), F32) for _ in heads))

    for hh in heads:
        for r in range(seq // rows):
            sl = pl.ds(r * rows, rows)
            o = ob_s[hh, sl, :]
            o = o * lax.rsqrt(jnp.mean(o * o, axis=-1, keepdims=True) + RMS_EPS) * gn
            o_ref[sl, lanes[hh]] = (o * _silu(z_ref[sl, lanes[hh]])).astype(o_ref.dtype)


def _gdn(h3, conv_w, a_rows, b_rows, alog, dt, gn):
    b, s, _ = h3.shape
    hp = GDN_HEADS_PER_STEP
    nb = GDN_HEADS // hp
    wide = hp * HEAD_DIM
    n_chunks = s // GDN_CHUNK
    col = lambda off: pl.BlockSpec((None, s, wide), lambda bi, hi: (bi, 0, off + hi))
    cw = lambda off: pl.BlockSpec((GDN_CONV, wide), lambda bi, hi: (0, off + hi))
    rowspec = pl.BlockSpec((None, hp, n_chunks, GDN_CHUNK), lambda bi, hi: (bi, hi, 0, 0))
    headvec = pl.BlockSpec((hp, 1, GDN_CHUNK), lambda bi, hi: (hi, 0, 0))
    return pl.pallas_call(
        _gdn_kernel,
        grid=(b, nb),
        in_specs=[col(0), col(nb), col(2 * nb), col(3 * nb), cw(0), cw(nb), cw(2 * nb),
                  rowspec, rowspec, headvec, headvec,
                  pl.BlockSpec((1, HEAD_DIM), lambda bi, hi: (0, 0))],
        out_specs=pl.BlockSpec((None, s, wide), lambda bi, hi: (bi, 0, hi)),
        out_shape=jax.ShapeDtypeStruct((b, s, GDN_WIDTH), BF16),
        scratch_shapes=[pltpu.VMEM((s + 8, HEAD_DIM), F32), pltpu.VMEM((s, HEAD_DIM), F32),
                        pltpu.VMEM((s, HEAD_DIM), F32), pltpu.VMEM((s, HEAD_DIM), F32),
                        pltpu.VMEM((hp, n_chunks, GDN_CHUNK), F32), pltpu.VMEM((hp, n_chunks, GDN_CHUNK), F32),
                        pltpu.VMEM((hp, s, HEAD_DIM), BF16), pltpu.VMEM((hp, s, HEAD_DIM), F32),
                        pltpu.VMEM((hp, n_chunks * HEAD_DIM, HEAD_DIM), BF16),
                        pltpu.VMEM((hp, n_chunks * HEAD_DIM, HEAD_DIM), F32)],
        compiler_params=_params("parallel", "parallel"),
        name="gdn",
    )(h3, h3, h3, h3, conv_w, conv_w, conv_w, a_rows, b_rows, alog, dt, gn)


def _hgrn_kernel(q_ref, f_ref, i_ref, g_ref, lb_ref, hn_ref, o_ref):
    seq = q_ref.shape[0]
    c = HGRN_CHUNK
    rows = HGRN_ROWS
    ri = _iota((rows, rows), 0)
    ci = _iota((rows, rows), 1)
    causal = (ri // c == ci // c) & (ci <= ri)
    row_in_chunk = _iota((rows, HEAD_DIM), 0) % c
    lb = lb_ref[...]
    hn = hn_ref[...]

    chunks = [slice(j * c, (j + 1) * c) for j in range(rows // c)]
    grp = range(HGRN_GROUP)

    def chunk_scan(x, suffix):
        step = 1
        while step < c:
            if suffix:
                x = x + jnp.where(row_in_chunk < c - step, pltpu.roll(x, rows - step, axis=0), 0.0)
            else:
                x = x + jnp.where(row_in_chunk >= step, pltpu.roll(x, step, axis=0), 0.0)
            step *= 2
        return x

    def group(n, state_t):
        starts = [pl.multiple_of((n * HGRN_GROUP + j) * rows, rows) for j in grp]
        qc = [q_ref[pl.ds(r0, rows), :] for r0 in starts]
        ic = [i_ref[pl.ds(r0, rows), :].astype(BF16) for r0 in starts]
        f = [lb + (1.0 - lb) * jax.nn.sigmoid(f_ref[pl.ds(r0, rows), :]) for r0 in starts]
        log_f = [jnp.log(f[j]) for j in grp]
        bcum = [chunk_scan(log_f[j], False) for j in grp]
        to_end = [chunk_scan(log_f[j], True) - log_f[j] for j in grp]
        chunk_dec = [jnp.exp(bcum[j]) for j in grp]
        q_dec = [(qc[j] * chunk_dec[j]).astype(BF16) for j in grp]
        k_inv = [(1.0 - f[j]) * jnp.exp(-bcum[j]) for j in grp]
        k_dec = [((1.0 - f[j]) * jnp.exp(to_end[j])).astype(BF16) for j in grp]
        p = [jnp.where(causal, _dot_nt(q_dec[j], k_inv[j]), 0.0) for j in grp]
        o_intra = [_dot(p[j], ic[j]) for j in grp]
        updates = [[_dot_tn(ic[j][sl], k_dec[j][sl]) for sl in chunks] for j in grp]
        for j, r0 in enumerate(starts):
            outs = []
            for sl, upd in zip(chunks, updates[j]):
                outs.append(o_intra[j][sl] + _dot_nt(q_dec[j][sl], state_t))
                state_t = state_t * chunk_dec[j][sl.stop - 1:sl.stop] + upd
            o = jnp.concatenate(outs, axis=0)
            o = o * lax.rsqrt(jnp.mean(o * o, axis=-1, keepdims=True) + RMS_EPS) * hn
            o_ref[pl.ds(r0, rows), :] = (o * _silu(g_ref[pl.ds(r0, rows), :])).astype(o_ref.dtype)
        return state_t

    lax.fori_loop(0, seq // (rows * HGRN_GROUP), group, jnp.zeros((HEAD_DIM, HEAD_DIM), F32))


def _hgrn(h3, lb, hn):
    b, s, _ = h3.shape
    nh = HGRN_HEADS
    col = lambda off: pl.BlockSpec((None, s, HEAD_DIM), lambda bi, hi: (bi, 0, off + hi))
    return pl.pallas_call(
        _hgrn_kernel,
        grid=(b, nh),
        in_specs=[col(0), col(nh), col(2 * nh), col(3 * nh),
                  pl.BlockSpec((None, 1, HEAD_DIM), lambda bi, hi: (hi, 0, 0)),
                  pl.BlockSpec((1, HEAD_DIM), lambda bi, hi: (0, 0))],
        out_specs=pl.BlockSpec((None, s, HEAD_DIM), lambda bi, hi: (bi, 0, hi)),
        out_shape=jax.ShapeDtypeStruct((b, s, HGRN_WIDTH), BF16),
        compiler_params=_params("parallel", "parallel"),
        name="hgrn2",
    )(h3, h3, h3, h3, lb, hn)


def _rope(x, cos2, sin2):
    return x * cos2 + pltpu.roll(x, HEAD_DIM // 2, axis=1) * sin2


def _flash_step(q_blk, k_blk, v_blk, mask, carry):
    m, l, acc = carry
    s = jnp.where(mask, _dot_nt(q_blk, k_blk) * HEAD_DIM ** -0.5, NEG_INF)
    m_new = jnp.maximum(m, jnp.max(s, axis=-1, keepdims=True))
    alpha = jnp.exp(m - m_new)
    p = jnp.exp(s - m_new)
    l = alpha * l + jnp.sum(p, axis=-1, keepdims=True)
    acc = alpha * acc + _dot(p, v_blk)
    return m_new, l, acc


def _flash_init():
    blk = ATT_BLOCK
    return (jnp.full((blk, 1), NEG_INF, F32), jnp.zeros((blk, 1), F32), jnp.zeros((blk, HEAD_DIM), F32))


def _dilated_kernel(*refs):
    n_g = len(DIL_GROUPS)
    q_refs, k_refs, v_refs = refs[0:n_g], refs[n_g:2 * n_g], refs[2 * n_g:3 * n_g]
    cos_ref, sin_ref, o_ref = refs[3 * n_g:3 * n_g + 3]
    q_s, k_s, v_s, og_s, lse_s = refs[3 * n_g + 3:]
    seq = o_ref.shape[0]
    blk = DIL_BLOCK
    piece = 256
    grp = range(n_g)

    for gi, (window, d) in enumerate(DIL_GROUPS):
        assert window // d == blk
        seg = seq // d
        k_s[gi, pl.ds(0, blk), :] = jnp.zeros((blk, HEAD_DIM), BF16)
        v_s[gi, pl.ds(0, blk), :] = jnp.zeros((blk, HEAD_DIM), BF16)
        for r in range(d):
            for c0 in range(0, seg, piece):
                n = min(piece, seg)
                rows = pl.ds(r + c0 * d, n, stride=d) if d > 1 else pl.ds(c0, n)
                cos2 = cos_ref[gi, pl.ds(r * seg + c0, n), :]
                sin2 = sin_ref[gi, pl.ds(r * seg + c0, n), :]
                q_s[gi, pl.ds(r * seg + c0, n), :] = _rope(q_refs[gi][rows, :], cos2, sin2).astype(BF16)
                k_s[gi, pl.ds(blk + r * seg + c0, n), :] = _rope(k_refs[gi][rows, :], cos2, sin2).astype(BF16)
                v_s[gi, pl.ds(blk + r * seg + c0, n), :] = v_refs[gi][rows, :].astype(BF16)

    ri = _iota((blk, 2 * blk), 0)
    ci = _iota((blk, 2 * blk), 1)
    rel = ri + blk - ci
    in_window = (rel >= 0) & (rel <= blk)
    dot = functools.partial(jnp.dot, preferred_element_type=F32)

    def q_block(m, _):
        segs = [seq // d for _, d in DIL_GROUPS]
        work = [(g, pl.multiple_of((m * DIL_Q_PER_STEP + u) * blk, blk)) for u in range(DIL_Q_PER_STEP) for g in grp]
        ids = range(len(work))
        has_prev = [jnp.where(j0 % segs[g] != 0, blk, 0) for g, j0 in work]
        mask = [ci + has_prev[i] >= blk for i in ids]
        q = [q_s[g, pl.ds(j0, blk), :] for g, j0 in work]
        kw = [k_s[g, pl.ds(j0, 2 * blk), :] for g, j0 in work]
        vw = [v_s[g, pl.ds(j0, 2 * blk), :] for g, j0 in work]
        s = [jnp.where(in_window, jnp.where(mask[i], _dot_nt(q[i], kw[i]) * HEAD_DIM ** -0.5, NEG_INF), NEG_INF)
             for i in ids]
        top = [jnp.max(s[i], axis=-1, keepdims=True) for i in ids]
        p = [jnp.exp(s[i] - top[i]) for i in ids]
        den = [jnp.sum(p[i], axis=-1, keepdims=True) for i in ids]
        o = [dot(p[i].astype(BF16), vw[i]) / den[i] for i in ids]
        lse = [top[i] + jnp.log(den[i]) for i in ids]
        for i, (g, j0) in enumerate(work):
            seg, d = segs[g], DIL_GROUPS[g][1]
            dst = pl.ds((j0 % seg) * d + j0 // seg, blk, stride=d) if d > 1 else pl.ds(j0, blk)
            og_s[g, dst, :] = o[i]
            lse_s[g, dst, :] = jnp.broadcast_to(lse[i], (blk, HEAD_DIM))
        return 0

    lax.fori_loop(0, seq // (blk * DIL_Q_PER_STEP), q_block, 0)

    for c0 in range(0, seq, piece):
        rows = pl.ds(c0, piece)
        lses = [lse_s[g, rows, :] for g in grp]
        top = functools.reduce(jnp.maximum, lses)
        wts = [jnp.exp(x - top) for x in lses]
        den = functools.reduce(lambda a, b: a + b, wts)
        o = functools.reduce(lambda a, b: a + b, [wts[g] * og_s[g, rows, :] for g in grp]) / den
        o_ref[rows, :] = o.astype(o_ref.dtype)


def _dilated(h3, cos2, sin2):
    b, s, _ = h3.shape
    hpg = DIL_HEADS_PER_GROUP
    n_g = len(DIL_GROUPS)
    col = lambda off: pl.BlockSpec((None, s, HEAD_DIM), lambda bi, hi: (bi, 0, off + hi))
    tab = pl.BlockSpec((n_g, s, HEAD_DIM), lambda bi, hi: (0, 0, 0))
    specs = [col(part * DIL_HEADS + gi * hpg) for part in range(3) for gi in range(n_g)]
    residue_major = lambda t: jnp.stack([t.reshape(s // d, d, HEAD_DIM).transpose(1, 0, 2).reshape(s, HEAD_DIM)
                                         for _, d in DIL_GROUPS])
    cos2, sin2 = residue_major(cos2), residue_major(sin2)
    return pl.pallas_call(
        _dilated_kernel,
        grid=(b, hpg),
        in_specs=specs + [tab, tab],
        out_specs=pl.BlockSpec((None, s, HEAD_DIM), lambda bi, hi: (bi, 0, hi)),
        out_shape=jax.ShapeDtypeStruct((b, s, hpg * HEAD_DIM), BF16),
        scratch_shapes=[pltpu.VMEM((n_g, s, HEAD_DIM), BF16), pltpu.VMEM((n_g, s + DIL_BLOCK, HEAD_DIM), BF16),
                        pltpu.VMEM((n_g, s + DIL_BLOCK, HEAD_DIM), BF16), pltpu.VMEM((n_g, s, HEAD_DIM), F32),
                        pltpu.VMEM((n_g, s, HEAD_DIM), F32)],
        compiler_params=_params("parallel", "parallel"),
        name="dilated_attention",
    )(*([h3] * (3 * n_g)), cos2, sin2)


def _moba_kernel(q_ref, k_ref, v_ref, cos_ref, sin_ref, o_ref, qf_s, q_s, k_s, v_s, km_s, sel_s):
    seq = o_ref.shape[0]
    blk = MOBA_BLOCK
    n_blk = seq // blk
    cos2 = cos_ref[...]
    sin2 = sin_ref[...]
    q = _rope(q_ref[...], cos2, sin2)
    qf_s[...] = q
    q_s[...] = q.astype(BF16)
    km_s[...] = jnp.zeros(km_s.shape, F32)
    for nb in range(n_blk):
        kb = _rope(k_ref[pl.ds(nb * blk, blk), :], cos2[nb * blk:(nb + 1) * blk], sin2[nb * blk:(nb + 1) * blk])
        k_s[pl.ds(nb * blk, blk), :] = kb.astype(BF16)
        km_s[pl.ds(nb, 1), :] = jnp.mean(kb, axis=0, keepdims=True)
    v_s[...] = v_ref[...].astype(BF16)

    lane = _iota((blk, LANES), 1).astype(F32)
    causal = _iota((blk, blk), 0) >= _iota((blk, blk), 1)
    rows = lambda nb: pl.ds(nb * blk, blk)

    past = range(1, n_blk)
    km = km_s[...]
    gate = {qb: jnp.where(lane < qb, _dot_nt_hi(qf_s[rows(qb), :], km), -jnp.inf) for qb in past}
    sel = {qb: jnp.zeros((blk, LANES), F32) for qb in past}
    for _k in range(MOBA_TOPK):
        best = {qb: jnp.max(gate[qb], axis=-1, keepdims=True) for qb in past}
        first = {qb: jnp.min(jnp.where(gate[qb] == best[qb], lane, LANES), axis=-1, keepdims=True) for qb in past}
        pick = {qb: (lane == first[qb]) & (best[qb] > -jnp.inf) for qb in past}
        sel = {qb: jnp.where(pick[qb], 1.0, sel[qb]) for qb in past}
        gate = {qb: jnp.where(pick[qb], -jnp.inf, gate[qb]) for qb in past}
    for qb in past:
        sel_s[rows(qb), :] = sel[qb]

    assert sorted(qb for group in MOBA_Q_GROUPS for qb in group) == list(range(n_blk))
    for group in MOBA_Q_GROUPS:
        carry = {qb: _flash_init() for qb in group}
        for j in range(max(group) + 1):
            for qb in group:
                if j < qb:
                    mask = sel_s[rows(qb), :][:, j:j + 1] > 0.0
                elif j == qb:
                    mask = causal
                else:
                    continue
                carry[qb] = _flash_step(q_s[rows(qb), :], k_s[rows(j), :], v_s[rows(j), :], mask, carry[qb])
        for qb in group:
            m, l, acc = carry[qb]
            o_ref[rows(qb), :] = (acc / l).astype(o_ref.dtype)


def _moba(h3, cos2, sin2):
    b, s, _ = h3.shape
    base = 3 * DIL_HEADS
    col = lambda off: pl.BlockSpec((None, s, HEAD_DIM), lambda bi, hi: (bi, 0, base + off + hi))
    tab = pl.BlockSpec((s, HEAD_DIM), lambda bi, hi: (0, 0))
    return pl.pallas_call(
        _moba_kernel,
        grid=(b, MOBA_HEADS),
        in_specs=[col(0), col(MOBA_HEADS), col(2 * MOBA_HEADS), tab, tab],
        out_specs=pl.BlockSpec((None, s, HEAD_DIM), lambda bi, hi: (bi, 0, hi)),
        out_shape=jax.ShapeDtypeStruct((b, s, MOBA_HEADS * HEAD_DIM), BF16),
        scratch_shapes=[pltpu.VMEM((s, HEAD_DIM), F32), pltpu.VMEM((s, HEAD_DIM), BF16),
                        pltpu.VMEM((s, HEAD_DIM), BF16), pltpu.VMEM((s, HEAD_DIM), BF16),
                        pltpu.VMEM((LANES, HEAD_DIM), F32), pltpu.VMEM((s, LANES), F32)],
        compiler_params=_params("parallel", "parallel"),
        name="moba_attention",
    )(h3, h3, h3, cos2, sin2)


def _layer_norm(x, gain, bias):
    mu = jnp.mean(x, axis=-1, keepdims=True)
    xc = x - mu
    var = jnp.mean(xc * xc, axis=-1, keepdims=True)
    return xc * lax.rsqrt(var + LN_EPS) * gain + bias


def _out_ln_kernel(*refs, n_parts):
    o_refs = refs[0:n_parts]
    w_refs = refs[n_parts:2 * n_parts]
    h_ref, gain_ref, bias_ref, rw_ref, y_ref, logit_ref = refs[2 * n_parts:]
    half = h_ref.shape[0] // 2
    halves = [pl.ds(k * half, half) for k in range(2)]
    mix = []
    for rows in halves:
        parts = [jnp.dot(o_r[rows, :], w_r[...], preferred_element_type=F32) for o_r, w_r in zip(o_refs, w_refs)]
        mix.append(functools.reduce(lambda a, b: a + b, parts))
    y = [_layer_norm(DEEPNORM_ALPHA * h_ref[rows, :] + mix[k], gain_ref[...], bias_ref[...])
         for k, rows in enumerate(halves)]
    logits = [_dot3(y[k], rw_ref[...]) for k in range(2)]
    for k, rows in enumerate(halves):
        y_ref[rows, :] = y[k]
        logit_ref[rows, :] = logits[k]


def _out_ln(parts, weights, h, gain, bias, router_w, tm):
    t, d = h.shape
    n_parts = len(parts)
    row = lambda width: pl.BlockSpec((tm, width), lambda i: (i, 0))
    full = lambda a: pl.BlockSpec(a.shape, lambda i: (0, 0))
    return pl.pallas_call(
        functools.partial(_out_ln_kernel, n_parts=n_parts),
        grid=(t // tm,),
        in_specs=[row(p.shape[1]) for p in parts] + [full(w) for w in weights]
                 + [row(d), full(gain), full(bias), full(router_w)],
        out_specs=[row(d), row(LANES)],
        out_shape=[jax.ShapeDtypeStruct((t, d), F32), jax.ShapeDtypeStruct((t, LANES), F32)],
        compiler_params=_params("parallel"),
        name="out_proj_ln",
    )(*parts, *weights, h, gain, bias, router_w)


def _route(logits_t, rbias_col):
    row = _iota(logits_t.shape, 0)
    scores = jax.nn.sigmoid(logits_t)
    biased = scores + rbias_col

    def first_argmax(vals):
        best = jnp.max(vals, axis=0, keepdims=True)
        return best, jnp.min(jnp.where(vals == best, row, N_EXPERTS), axis=0, keepdims=True)

    best_score = None
    best_group = None
    for g in range(N_EXPERT_GROUPS):
        vals = jnp.where(row // EXPERTS_PER_GROUP == g, biased, -jnp.inf)
        top1, idx1 = first_argmax(vals)
        top2, _ = first_argmax(jnp.where(row == idx1, -jnp.inf, vals))
        score = top1 + top2
        if g == 0:
            best_score, best_group = score, jnp.zeros_like(idx1)
        else:
            better = score > best_score
            best_group = jnp.where(better, g, best_group)
            best_score = jnp.where(better, score, best_score)
    masked = jnp.where(row // EXPERTS_PER_GROUP == best_group, biased, NEG_INF)
    _, i1 = first_argmax(masked)
    _, i2 = first_argmax(jnp.where(row == i1, -jnp.inf, masked))
    s1 = jnp.sum(jnp.where(row == i1, scores, 0.0), axis=0, keepdims=True)
    s2 = jnp.sum(jnp.where(row == i2, scores, 0.0), axis=0, keepdims=True)
    tot = s1 + s2
    return i1, i2, s1 / tot, s2 / tot


def _route_kernel(logit_ref, rbias_ref, loc_ref, w_ref, tab_ref, cnt_ref, carry_s, total_s):
    sweep = pl.program_id(0)
    i = pl.program_id(1)
    tm = logit_ref.shape[0]
    block = float(MOE_BLOCK_ROWS)
    align = float(MOE_ALIGN)

    @pl.when((i == 0) & (sweep == 1))
    def _():
        total_s[...] = carry_s[...]

    @pl.when(i == 0)
    def _():
        carry_s[...] = jnp.zeros(carry_s.shape, F32)

    logits_t = logit_ref[...].T[:N_EXPERTS]
    i1, i2, w1, w2 = _route(logits_t, rbias_ref[...][:N_EXPERTS])
    row = _iota((N_EXPERTS, tm), 0)
    chosen = jnp.where((row == i1) | (row == i2), 1.0, 0.0)
    count = jnp.sum(chosen, axis=1, keepdims=True)
    padded = jnp.floor((count + (align - 1.0)) * (1.0 / align)) * align

    e_r = _iota((N_EXPERTS, N_EXPERTS), 0)
    e_c = _iota((N_EXPERTS, N_EXPERTS), 1)

    def expert_prefix(col, inclusive):
        tri = ((e_c <= e_r) if inclusive else (e_c < e_r)).astype(BF16)
        wide = jnp.broadcast_to(col, (N_EXPERTS, LANES)).astype(BF16)
        return jnp.dot(tri, wide, preferred_element_type=F32)[:, 0:1]

    @pl.when(sweep == 0)
    def _():
        loc_ref[...] = jnp.zeros(loc_ref.shape, loc_ref.dtype)
        w_ref[...] = jnp.zeros(w_ref.shape, w_ref.dtype)
        tab_ref[...] = jnp.zeros(tab_ref.shape, tab_ref.dtype)
        cnt_ref[...] = jnp.zeros(cnt_ref.shape, cnt_ref.dtype)

    @pl.when(sweep == 1)
    def _():
        total = total_s[...]
        blocks = jnp.floor((total + (block - 1.0)) * (1.0 / block))
        region = (expert_prefix(blocks, True) - blocks) * block + carry_s[...]
        run_start = expert_prefix(padded, False)
        earlier = (_iota((tm, tm), 0) < _iota((tm, tm), 1)).astype(BF16)
        slot = run_start + jnp.dot(chosen.astype(BF16), earlier, preferred_element_type=F32)
        l1 = jnp.sum(jnp.where(row == i1, slot, 0.0), axis=0, keepdims=True)
        l2 = jnp.sum(jnp.where(row == i2, slot, 0.0), axis=0, keepdims=True)
        out_row = _iota((8, tm), 0)
        loc_ref[...] = jnp.where(out_row == 0, l1, jnp.where(out_row == 1, l2, 0.0)).astype(jnp.int32)
        w_ref[...] = jnp.where(out_row == 0, w1, jnp.where(out_row == 1, w2, 0.0))
        lane = _iota((N_EXPERTS, LANES), 1)
        tab = jnp.where(lane == 0, padded, jnp.where(lane == 1, run_start, jnp.where(lane == 2, region, 0.0)))
        tab_ref[...] = tab.astype(jnp.int32)
        cnt_ref[...] = jnp.broadcast_to(total, cnt_ref.shape).astype(jnp.int32)

    carry_s[...] = carry_s[...] + padded


def _route_tokens(logits, rbias_col):
    t = logits.shape[0]
    tm = MOE_TILE
    n_tiles = t // tm
    tok = pl.BlockSpec((None, 8, tm), lambda s, i: (s, 0, i))
    loc, w, tab, totals = pl.pallas_call(
        _route_kernel,
        grid=(2, n_tiles),
        in_specs=[pl.BlockSpec((tm, LANES), lambda s, i: (i, 0)), pl.BlockSpec((LANES, 1), lambda s, i: (0, 0))],
        out_specs=[tok, tok, pl.BlockSpec((None, None, N_EXPERTS, LANES), lambda s, i: (s, i, 0, 0)),
                   pl.BlockSpec((N_EXPERTS, LANES), lambda s, i: (0, 0))],
        out_shape=[jax.ShapeDtypeStruct((2, 8, t), jnp.int32), jax.ShapeDtypeStruct((2, 8, t), F32),
                   jax.ShapeDtypeStruct((2, n_tiles, N_EXPERTS, LANES), jnp.int32),
                   jax.ShapeDtypeStruct((N_EXPERTS, LANES), jnp.int32)],
        scratch_shapes=[pltpu.VMEM((N_EXPERTS, 1), F32), pltpu.VMEM((N_EXPERTS, 1), F32)],
        compiler_params=_params("arbitrary", "arbitrary"),
        name="moe_route",
    )(logits, rbias_col)
    return loc[1], w[1], tab[1], totals[:, 0]


def _run_pieces(length, sizes):
    return [(size, length & ~(2 * size - 1), (length & size) != 0) for size in sizes]


def _tile_run_copies(cnt_ref, off_ref, region_ref, tile, compact, sorted_rows, sem, to_sorted):
    copies = []
    for e in range(N_EXPERTS):
        k = tile * N_EXPERTS + e
        length = pl.multiple_of(cnt_ref[k], MOE_ALIGN)
        small = compact.at[pl.ds(pl.multiple_of(off_ref[k], MOE_ALIGN), length)]
        big = sorted_rows.at[pl.ds(pl.multiple_of(region_ref[k], MOE_ALIGN), length)]
        copies.append((length > 0, pltpu.make_async_copy(small, big, sem) if to_sorted
                       else pltpu.make_async_copy(big, small, sem)))
    return copies


def _dispatch_kernel(cnt_ref, off_ref, region_ref, pad_start_ref, pad_len_ref, n_valid_ref,
                     x_ref, loc_ref, w_ref, xs_ref, buf_s, zero_s, sem, zero_sem):
    i = pl.program_id(0)
    d = x_ref.shape[1]
    cap = buf_s.shape[1]
    tm = x_ref.shape[0]
    n_blocks = xs_ref.shape[0] // MOE_BLOCK_ROWS

    @pl.when(i == 0)
    def _():
        zero_s[...] = jnp.zeros(zero_s.shape, zero_s.dtype)
        sizes = [MOE_BLOCK_ROWS >> s for s in range((MOE_BLOCK_ROWS // MOE_ALIGN).bit_length())]
        copies = []
        for e in range(N_EXPERTS):
            for size, offset, used in _run_pieces(pad_len_ref[e], sizes):
                dst = xs_ref.at[pl.ds(pl.multiple_of(pad_start_ref[e] + offset, MOE_ALIGN), size)]
                copies.append((used, pltpu.make_async_copy(zero_s.at[pl.ds(0, size)], dst, zero_sem)))
        for j in range(n_blocks - N_EXPERTS, n_blocks):
            dst = xs_ref.at[pl.ds(j * MOE_BLOCK_ROWS, MOE_BLOCK_ROWS)]
            copies.append((j >= n_valid_ref[0], pltpu.make_async_copy(zero_s, dst, zero_sem)))
        for action in ("start", "wait"):
            for used, copy in copies:
                @pl.when(used)
                def _():
                    getattr(copy, action)()

    loc = loc_ref[...]
    w = w_ref[...]
    row = _iota((cap, tm), 0)
    hit1 = row == loc[0:1, :]
    hit2 = row == loc[1:2, :]
    perm = jnp.where(hit1, 1.0, jnp.where(hit2, 1.0, 0.0)).astype(BF16)
    gate = jnp.sum(jnp.where(hit1, w[0:1, :], jnp.where(hit2, w[1:2, :], 0.0)), axis=1, keepdims=True)
    g_hi = gate.astype(BF16)
    g_lo = (gate - g_hi.astype(F32)).astype(BF16)
    lane = _iota((cap, LANES), 1)
    slot = i % 2
    buf_s[slot, :, pl.ds(0, d)] = jnp.dot(perm, x_ref[...].astype(BF16), preferred_element_type=F32).astype(BF16)
    buf_s[slot, :, pl.ds(d, LANES)] = jnp.where(lane == 0, g_hi.astype(F32),
                                                jnp.where(lane == 1, g_lo.astype(F32), 0.0)).astype(BF16)

    def run_copies(tile, which, action):
        for used, copy in _tile_run_copies(cnt_ref, off_ref, region_ref, tile, buf_s.at[which], xs_ref,
                                           sem.at[which], True):
            @pl.when(used)
            def _():
                getattr(copy, action)()

    run_copies(i, slot, "start")

    @pl.when(i > 0)
    def _():
        run_copies(i - 1, 1 - slot, "wait")

    @pl.when(i == pl.num_programs(0) - 1)
    def _():
        run_copies(i, slot, "wait")


def _dispatch(tables, pad_start, pad_len, n_valid, x, loc, w, n_rows):
    t, d = x.shape
    tm = MOE_TILE
    cap = MOE_TOPK * tm + N_EXPERTS * MOE_ALIGN
    tok = pl.BlockSpec((8, tm), lambda i, *_: (0, i))
    grid_spec = pltpu.PrefetchScalarGridSpec(
        num_scalar_prefetch=6,
        grid=(t // tm,),
        in_specs=[pl.BlockSpec((tm, d), lambda i, *_: (i, 0)), tok, tok],
        out_specs=pl.BlockSpec(memory_space=pl.ANY),
        scratch_shapes=[pltpu.VMEM((2, cap, d + LANES), BF16), pltpu.VMEM((MOE_BLOCK_ROWS, d + LANES), BF16),
                        pltpu.SemaphoreType.DMA((2,)), pltpu.SemaphoreType.DMA(())],
    )
    return pl.pallas_call(
        _dispatch_kernel,
        grid_spec=grid_spec,
        out_shape=jax.ShapeDtypeStruct((n_rows, d + LANES), BF16),
        compiler_params=_params("arbitrary"),
        name="moe_dispatch",
    )(*tables, pad_start, pad_len, n_valid, x, loc, w)


def _expert_kernel(blk_expert_ref, next_expert_ref, n_valid_ref, x_ref, wg_ref, wu_ref, wd_ref, y_ref,
                   wg_s, wu_s, wd_s, wg_buf, wu_buf, wd_buf, slot_s, sem, *, layer):
    i = pl.program_id(0)

    def fetch(expert, slot):
        return [pltpu.make_async_copy(src.at[layer, expert], dst.at[slot], sem.at[slot])
                for src, dst in ((wg_ref, wg_buf), (wu_ref, wu_buf), (wd_ref, wd_buf))]

    @pl.when(i == 0)
    def _():
        slot_s[0] = 1
        for copy in fetch(blk_expert_ref[0], 0):
            copy.start()

    @pl.when((i == 0) | (blk_expert_ref[i] != blk_expert_ref[jnp.maximum(i - 1, 0)]))
    def _():
        slot = 1 - slot_s[0]
        slot_s[0] = slot
        for copy in fetch(blk_expert_ref[i], slot):
            copy.wait()
        wg_s[...] = wg_buf[slot].astype(BF16)
        wu_s[...] = wu_buf[slot].astype(BF16)
        wd_s[...] = wd_buf[slot].astype(BF16)

        @pl.when(next_expert_ref[i] >= 0)
        def _():
            for copy in fetch(next_expert_ref[i], 1 - slot):
                copy.start()

    @pl.when(i < n_valid_ref[0])
    def _():
        d = wg_s.shape[0]
        x = x_ref[:, pl.ds(0, d)]
        extra = x_ref[:, pl.ds(d, LANES)].astype(F32)
        gate = extra[:, 0:1] + extra[:, 1:2]
        hid = _silu(jnp.dot(x, wg_s[...], preferred_element_type=F32)) * jnp.dot(
            x, wu_s[...], preferred_element_type=F32)
        y_ref[...] = _dot(hid * gate, wd_s[...]).astype(y_ref.dtype)

    @pl.when(i >= n_valid_ref[0])
    def _():
        y_ref[...] = jnp.zeros(y_ref.shape, y_ref.dtype)


def _experts(blk_expert, next_expert, n_valid, xs, wg, wu, wd, layer, n_blocks):
    d = wg.shape[-2]
    f = wg.shape[-1]
    rows = MOE_BLOCK_ROWS
    n_rows = n_blocks * rows
    hbm = pl.BlockSpec(memory_space=pl.ANY)
    grid_spec = pltpu.PrefetchScalarGridSpec(
        num_scalar_prefetch=3,
        grid=(n_blocks,),
        in_specs=[pl.BlockSpec((rows, xs.shape[1]), lambda i, be, ne, nv: (i, 0)), hbm, hbm, hbm],
        out_specs=pl.BlockSpec((rows, d), lambda i, be, ne, nv: (i, 0)),
        scratch_shapes=[pltpu.VMEM((d, f), BF16), pltpu.VMEM((d, f), BF16), pltpu.VMEM((f, d), BF16),
                        pltpu.VMEM((2, d, f), F32), pltpu.VMEM((2, d, f), F32), pltpu.VMEM((2, f, d), F32),
                        pltpu.SMEM((1,), jnp.int32), pltpu.SemaphoreType.DMA((2,))],
    )
    return pl.pallas_call(
        functools.partial(_expert_kernel, layer=layer),
        grid_spec=grid_spec,
        out_shape=jax.ShapeDtypeStruct((n_rows, d), BF16),
        compiler_params=_params("arbitrary"),
        name="moe_experts",
    )(blk_expert, next_expert, n_valid, xs, wg, wu, wd)


def _combine_kernel(cnt_ref, off_ref, region_ref, ys_ref, loc_ref, h_ref, gain_ref, bias_ref, y_ref, yb_ref,
                    buf, sem):
    i = pl.program_id(0)
    n = pl.num_programs(0)
    tm = h_ref.shape[0]
    cap = buf.shape[1]

    def fetch(tile, slot, action):
        for used, copy in _tile_run_copies(cnt_ref, off_ref, region_ref, tile, buf.at[slot], ys_ref,
                                           sem.at[slot], False):
            @pl.when(used)
            def _():
                getattr(copy, action)()

    slot = i % 2

    @pl.when(i == 0)
    def _():
        buf[...] = jnp.zeros(buf.shape, buf.dtype)
        fetch(0, 0, "start")

    @pl.when(i + 1 < n)
    def _():
        fetch(i + 1, 1 - slot, "start")

    fetch(i, slot, "wait")
    loc = loc_ref[...].astype(F32)
    eye = _iota((tm, tm), 0) == _iota((tm, tm), 1)
    loc_col = [jnp.sum(jnp.where(eye, jnp.broadcast_to(loc[k:k + 1, :], (tm, tm)), 0.0), axis=1, keepdims=True)
               for k in range(MOE_TOPK)]
    lane = _iota((tm, cap), 1).astype(F32)
    pick = jnp.where(lane == loc_col[0], 1.0, jnp.where(lane == loc_col[1], 1.0, 0.0)).astype(BF16)
    ffn = jnp.dot(pick, buf[slot], preferred_element_type=F32)
    y = _layer_norm(DEEPNORM_ALPHA * h_ref[...] + ffn, gain_ref[...], bias_ref[...])
    y_ref[...] = y
    yb_ref[...] = y.astype(BF16)


def _combine_ln(tables, ys, loc, h, gain, bias):
    t, d = h.shape
    tm = MOE_TILE
    cap = MOE_TOPK * tm + N_EXPERTS * MOE_ALIGN
    grid_spec = pltpu.PrefetchScalarGridSpec(
        num_scalar_prefetch=3,
        grid=(t // tm,),
        in_specs=[pl.BlockSpec(memory_space=pl.ANY),
                  pl.BlockSpec((8, tm), lambda i, *_: (0, i)),
                  pl.BlockSpec((tm, d), lambda i, *_: (i, 0)),
                  pl.BlockSpec((1, d), lambda i, *_: (0, 0)),
                  pl.BlockSpec((1, d), lambda i, *_: (0, 0))],
        out_specs=[pl.BlockSpec((tm, d), lambda i, *_: (i, 0)), pl.BlockSpec((tm, d), lambda i, *_: (i, 0))],
        scratch_shapes=[pltpu.VMEM((2, cap, d), BF16), pltpu.SemaphoreType.DMA((2,))],
    )
    return pl.pallas_call(
        _combine_kernel,
        grid_spec=grid_spec,
        out_shape=[jax.ShapeDtypeStruct((t, d), F32), jax.ShapeDtypeStruct((t, d), BF16)],
        compiler_params=_params("arbitrary"),
        name="moe_combine_ln",
    )(*tables, ys, loc, h, gain, bias)


def _moe_ln(h, logits, rbias, wg, wu, wd, layer, gain, bias):
    t, d = h.shape
    rows = MOE_BLOCK_ROWS
    n_tiles = t // MOE_TILE
    n_blocks = -(-(MOE_TOPK * t + n_tiles * N_EXPERTS * (MOE_ALIGN - 1)) // rows) + N_EXPERTS
    loc, w_t, tab, counts = _route_tokens(logits, rbias)
    tables = tuple(tab[:, :, k].reshape(-1) for k in range(3))
    blocks_per_expert = (counts + rows - 1) // rows
    block_end = jnp.cumsum(blocks_per_expert)
    row_start = (block_end - blocks_per_expert) * rows
    n_valid = block_end[-1:].astype(jnp.int32)
    blk = jnp.arange(n_blocks, dtype=jnp.int32)
    blk_expert = jnp.sum(jnp.minimum(blk, n_valid - 1)[:, None] >= block_end[None, :], axis=1).astype(jnp.int32)
    pad_start = (row_start + counts).astype(jnp.int32)
    pad_len = (blocks_per_expert * rows - counts).astype(jnp.int32)
    xs = _dispatch(tables, pad_start, pad_len, n_valid, h, loc, w_t, n_blocks * rows)
    experts = jnp.arange(N_EXPERTS, dtype=jnp.int32)
    later_used = (experts[None, :] > experts[:, None]) & (blocks_per_expert[None, :] > 0)
    next_used = jnp.min(jnp.where(later_used, experts[None, :], N_EXPERTS), axis=1)
    next_expert = jnp.where(next_used < N_EXPERTS, next_used, -1).astype(jnp.int32)[blk_expert]
    ys = _experts(blk_expert, next_expert, n_valid, xs, wg, wu, wd, layer, n_blocks)
    return _combine_ln(tables, ys, loc, h, gain, bias)


def _rope_tables(seq):
    inv_freq = ROPE_THETA ** (-jnp.arange(0, HEAD_DIM, 2, dtype=F32) / HEAD_DIM)
    ang = jnp.arange(seq, dtype=F32)[:, None] * inv_freq[None, :]
    cos, sin = jnp.cos(ang), jnp.sin(ang)
    return jnp.concatenate([cos, cos], axis=-1), jnp.concatenate([-sin, sin], axis=-1)


def _even_mixer(hb, b, s, w_in, conv_w, a_log, dt_bias, gdn_norm, hgrn_norm, lower_bound):
    gw = GDN_WIDTH
    n_small = 2 * GDN_HEADS
    tail0 = 4 * gw
    w_t = w_in.T
    w_small = jnp.pad(w_t[tail0:tail0 + n_small], ((0, LANES - n_small), (0, 0))).astype(BF16)
    h_a = _matmul(hb, w_t[:tail0].astype(BF16), *EVEN_PROJ_TILE, transposed=True).reshape(b, s, tail0)
    h_b = _matmul(hb, w_t[tail0 + n_small:].astype(BF16), *EVEN_PROJ_TILE, transposed=True)
    h_b = h_b.reshape(b, s, 4 * HGRN_WIDTH)
    n_chunks = s // GDN_CHUNK
    small = _matmul(hb, w_small, EVEN_PROJ_TILE[0], LANES, transposed=True).reshape(b, s, LANES)
    to_rows = lambda a: a.transpose(0, 2, 1).reshape(b, GDN_HEADS, n_chunks, GDN_CHUNK)
    b_rows, a_rows = to_rows(small[..., :GDN_HEADS]), to_rows(small[..., GDN_HEADS:n_small])
    headvec = lambda v: jnp.broadcast_to(v.astype(F32)[:, None, None], (GDN_HEADS, 1, GDN_CHUNK))
    o_a = _gdn(h_a, conv_w.astype(F32), a_rows, b_rows, headvec(a_log), headvec(dt_bias),
               gdn_norm.astype(F32).reshape(1, HEAD_DIM))
    o_b = _hgrn(h_b, lower_bound.astype(F32).reshape(HGRN_HEADS, 1, HEAD_DIM),
                hgrn_norm.astype(F32).reshape(1, HEAD_DIM))
    return [o_a.reshape(b * s, GDN_WIDTH), o_b.reshape(b * s, HGRN_WIDTH)]


def _odd_mixer(hb, b, s, w_in, cos2, sin2):
    h = _matmul(hb, w_in.astype(BF16), *ODD_PROJ_TILE)
    h3 = h.reshape(b, s, ODD_COLS)
    o_c = _dilated(h3, cos2, sin2)
    o_d = _moba(h3, cos2, sin2)
    return [o_c.reshape(b * s, -1), o_d.reshape(b * s, -1)]


def kernel(x, ev_w_in, ev_conv_w, ev_a_log, ev_dt_bias, ev_gdn_norm, ev_hgrn_norm, hgrn_lb_logits, ev_w_out,
           od_w_in, od_w_out, router_w, router_bias, moe_w_gate, moe_w_up, moe_w_down, ln_gain, ln_bias):
    b, s, d = x.shape
    t = b * s
    cos2, sin2 = _rope_tables(s)
    lower_bounds = jnp.cumsum(jax.nn.softmax(hgrn_lb_logits.astype(F32), axis=0), axis=0)
    rw = jnp.pad(router_w.astype(F32), ((0, 0), (0, LANES - N_EXPERTS)))
    rbias = jnp.pad(router_bias.astype(F32), (0, LANES - N_EXPERTS)).reshape(LANES, 1)
    vec = lambda v: v.astype(F32).reshape(1, d)

    h = x.reshape(t, d)
    hb = h
    for layer in range(DEPTH):
        if layer % 2 == 0:
            e = layer // 2
            parts = _even_mixer(hb, b, s, ev_w_in[e], ev_conv_w[e], ev_a_log[e], ev_dt_bias[e], ev_gdn_norm[e],
                                ev_hgrn_norm[e], lower_bounds[layer])
            w_out = ev_w_out[e].astype(BF16)
        else:
            o = layer // 2
            parts = _odd_mixer(hb, b, s, od_w_in[o], cos2, sin2)
            w_out = od_w_out[o].astype(BF16)
        splits = np.cumsum([p.shape[1] for p in parts])[:-1]
        weights = jnp.split(w_out, splits, axis=0)
        h, logits = _out_ln(parts, weights, h, vec(ln_gain[layer, 0]), vec(ln_bias[layer, 0]), rw, OUT_PROJ_ROWS)
        h, hb = _moe_ln(h, logits, rbias, moe_w_gate, moe_w_up, moe_w_down, layer,
                        vec(ln_gain[layer, 1]), vec(ln_bias[layer, 1]))
    return h.reshape(b, s, d)
```

```python
import functools

import jax
import jax.numpy as jnp
import numpy as np
from jax import lax
from jax.experimental import pallas as pl
from jax.experimental.pallas import tpu as pltpu

F32 = jnp.float32
BF16 = jnp.bfloat16

DEPTH = 2
HEAD_DIM = 128
GDN_HEADS = 8
GDN_CONV = 4
GDN_CHUNK = 64
GDN_WIDTH = GDN_HEADS * HEAD_DIM
HGRN_HEADS = 8
HGRN_CHUNK = 16
HGRN_WIDTH = HGRN_HEADS * HEAD_DIM
DIL_GROUPS = ((128, 1), (512, 4), (2048, 16))
DIL_HEADS_PER_GROUP = 4
DIL_HEADS = len(DIL_GROUPS) * DIL_HEADS_PER_GROUP
MOBA_HEADS = 4
MOBA_BLOCK = 256
MOBA_TOPK = 3
ROPE_THETA = 10000.0
N_EXPERTS = 16
N_EXPERT_GROUPS = 4
EXPERTS_PER_GROUP = N_EXPERTS // N_EXPERT_GROUPS
MOE_TOPK = 2
MOE_BLOCK_ROWS = 512
MOE_TILE = 512
MOE_ALIGN = 16
DEEPNORM_ALPHA = (2.0 * DEPTH) ** 0.25
LN_EPS = 1e-5
RMS_EPS = 1e-6
NEG_INF = -1e30

LANES = 128
VMEM_LIMIT = 56 * 1024 * 1024
EVEN_PROJ_TILE = (1024, 2048)
ODD_PROJ_TILE = (2048, 768)
OUT_PROJ_ROWS = 256
ATT_BLOCK = 256
MOBA_Q_GROUPS = ((7, 0, 6, 1), (5, 2, 4, 3))
DIL_Q_PER_STEP = 4
DIL_BLOCK = 128
GDN_HEADS_PER_STEP = 2
GDN_GROUP = 16
HGRN_GROUP = 4
HGRN_ROWS = 256

ODD_COLS = 3 * DIL_HEADS * HEAD_DIM + 3 * MOBA_HEADS * HEAD_DIM


def _dot(a, b):
    return jnp.dot(a.astype(BF16), b.astype(BF16), preferred_element_type=F32)


def _dot_nt(a, b):
    return lax.dot_general(a.astype(BF16), b.astype(BF16), (((1,), (1,)), ((), ())),
                           preferred_element_type=F32)


def _dot_tn(a, b):
    return lax.dot_general(a.astype(BF16), b.astype(BF16), (((0,), (0,)), ((), ())),
                           preferred_element_type=F32)


def _dot_hi(a, b):
    return jnp.dot(a, b, preferred_element_type=F32, precision=lax.Precision.HIGHEST)


def _dot_nt_hi(a, b):
    return lax.dot_general(a, b, (((1,), (1,)), ((), ())), preferred_element_type=F32,
                           precision=lax.Precision.HIGHEST)


def _dot3(a, b):
    a_hi = a.astype(BF16)
    b_hi = b.astype(BF16)
    a_lo = (a - a_hi.astype(F32)).astype(BF16)
    b_lo = (b - b_hi.astype(F32)).astype(BF16)
    dot = functools.partial(jnp.dot, preferred_element_type=F32)
    return dot(a_hi, b_hi) + (dot(a_hi, b_lo) + dot(a_lo, b_hi))


_dot_inv = _dot


def _silu(x):
    return x * jax.nn.sigmoid(x)


def _iota(shape, dim):
    return lax.broadcasted_iota(jnp.int32, shape, dim)


def _params(*sem):
    return pltpu.CompilerParams(dimension_semantics=sem, vmem_limit_bytes=VMEM_LIMIT)


def _mm_kernel(x_ref, w_ref, o_ref, *, transposed):
    x = x_ref[...].astype(BF16)
    prod = _dot_nt(x, w_ref[...]) if transposed else jnp.dot(x, w_ref[...], preferred_element_type=F32)
    o_ref[...] = prod.astype(o_ref.dtype)


def _matmul(x, w, tm, tn, transposed=False):
    m, k = x.shape
    n = w.shape[0] if transposed else w.shape[1]
    assert m % tm == 0 and n % tn == 0
    w_spec = pl.BlockSpec((tn, k), lambda i, j: (j, 0)) if transposed else pl.BlockSpec((k, tn), lambda i, j: (0, j))
    return pl.pallas_call(
        functools.partial(_mm_kernel, transposed=transposed),
        grid=(m // tm, n // tn),
        in_specs=[pl.BlockSpec((tm, k), lambda i, j: (i, 0)), w_spec],
        out_specs=pl.BlockSpec((tm, tn), lambda i, j: (i, j)),
        out_shape=jax.ShapeDtypeStruct((m, n), F32),
        compiler_params=_params("parallel", "parallel"),
        name="in_proj",
    )(x, w)


def _gdn_kernel(q_ref, k_ref, v_ref, z_ref, cwq_ref, cwk_ref, cwv_ref, a_ref, b_ref, alog_ref, dt_ref,
                gn_ref, o_ref, pad_s, q_s, k_s, v_s, gcum_s, beta_s, qe_s, ob_s, sm_s, sa_s):
    seq = q_ref.shape[0]
    c = GDN_CHUNK
    n_chunks = seq // c
    rows = 256
    heads = range(GDN_HEADS_PER_STEP)
    lanes = [slice(hh * HEAD_DIM, (hh + 1) * HEAD_DIM) for hh in heads]

    def conv_norm(hh):
        pad_s[pl.ds(0, 8), :] = jnp.zeros((8, HEAD_DIM), F32)
        for x_ref, cw_ref, dst, mode in ((q_ref, cwq_ref, q_s, "q"), (k_ref, cwk_ref, k_s, "k"),
                                         (v_ref, cwv_ref, v_s, "v")):
            pad_s[pl.ds(8, seq), :] = x_ref[:, lanes[hh]]
            cw = cw_ref[:, lanes[hh]]
            for r in range(seq // rows):
                acc = None
                for j in range(GDN_CONV):
                    tap = pad_s[pl.ds(8 + r * rows - (GDN_CONV - 1) + j, rows), :] * cw[j:j + 1, :]
                    acc = tap if acc is None else acc + tap
                y = _silu(acc)
                if mode != "v":
                    y = y * lax.rsqrt(jnp.sum(y * y, axis=-1, keepdims=True) + RMS_EPS)
                if mode == "q":
                    y = y * HEAD_DIM ** -0.5
                dst[pl.ds(r * rows, rows), :] = y

    upper = (_iota((c, c), 0) <= _iota((c, c), 1)).astype(F32)
    for hh in heads:
        g = -jnp.exp(alog_ref[hh]) * jax.nn.softplus(a_ref[hh] + dt_ref[hh])
        gcum_s[hh] = _dot_hi(g, upper)
        beta_s[hh] = jax.nn.sigmoid(b_ref[hh])

    ri = _iota((c, c), 0)
    ci = _iota((c, c), 1)
    eye = ri == ci
    strict = ri > ci
    incl = ri >= ci
    eye_f = eye.astype(F32)
    level1 = ri // 2 == ci // 2
    levels = []
    s = 2
    while s < c:
        levels.append((ri // (2 * s) == ci // (2 * s)) & ((ri // s) % 2 == 1) & ((ci // s) % 2 == 0))
        s *= 2

    dot = functools.partial(jnp.dot, preferred_element_type=F32)

    def to_col(row):
        return jnp.sum(jnp.where(eye, jnp.broadcast_to(row, (c, c)), 0.0), axis=1, keepdims=True)

    def prepare(i, _, hh):
        n0 = i * GDN_GROUP
        grp = range(GDN_GROUP)
        starts = [pl.multiple_of((n0 + j) * c, c) for j in grp]
        qc = [q_s[pl.ds(r0, c), :] for r0 in starts]
        kc = [k_s[pl.ds(r0, c), :] for r0 in starts]
        vc = [v_s[pl.ds(r0, c), :] for r0 in starts]
        g_row = [gcum_s[hh, pl.ds(n0 + j, 1), :] for j in grp]
        g_col = [to_col(g_row[j]) for j in grp]
        b_col = [to_col(beta_s[hh, pl.ds(n0 + j, 1), :]) for j in grp]
        decay = [jnp.exp(jnp.where(incl, g_col[j] - g_row[j], 0.0)) for j in grp]
        n_mat = [b_col[j] * jnp.where(strict, decay[j], 0.0) * _dot_nt(kc[j], kc[j]) for j in grp]
        inv = [eye_f - jnp.where(level1, n_mat[j], 0.0) for j in grp]
        for blk in levels:
            tmp = [_dot_inv(inv[j], jnp.where(blk, n_mat[j], 0.0)) for j in grp]
            inv = [inv[j] - _dot_inv(tmp[j], inv[j]) for j in grp]
        e_col = [jnp.exp(g_col[j]) for j in grp]
        sol = [_dot_inv(inv[j], jnp.concatenate([b_col[j] * vc[j], (b_col[j] * e_col[j]) * kc[j]], axis=1))
               for j in grp]
        qk = [(_dot_nt(qc[j], kc[j]) * jnp.where(incl, decay[j], 0.0)).astype(BF16) for j in grp]
        ub = [sol[j][:, :HEAD_DIM].astype(BF16) for j in grp]
        w = [sol[j][:, HEAD_DIM:].astype(BF16) for j in grp]
        kd = [(kc[j] * jnp.exp(g_row[j][:, c - 1:c] - g_col[j])).astype(BF16) for j in grp]
        q_eff = [(qc[j] * e_col[j] - dot(qk[j], w[j])).astype(BF16) for j in grp]
        o_base = [dot(qk[j], ub[j]) for j in grp]
        s_mat = [_dot_tn(kd[j], w[j]).astype(BF16) for j in grp]
        s_add = [_dot_tn(kd[j], ub[j]) for j in grp]
        for j, r0 in enumerate(starts):
            m0 = pl.multiple_of((n0 + j) * HEAD_DIM, HEAD_DIM)
            qe_s[hh, pl.ds(r0, c), :] = q_eff[j]
            ob_s[hh, pl.ds(r0, c), :] = o_base[j]
            sm_s[hh, pl.ds(m0, HEAD_DIM), :] = s_mat[j]
            sa_s[hh, pl.ds(m0, HEAD_DIM), :] = s_add[j]
        return 0

    for hh in heads:
        conv_norm(hh)
        lax.fori_loop(0, n_chunks // GDN_GROUP, functools.partial(prepare, hh=hh), 0)

    gn = gn_ref[...]

    def chunk(n, states):
        r0 = pl.multiple_of(n * c, c)
        m0 = pl.multiple_of(n * HEAD_DIM, HEAD_DIM)
        g_last = [gcum_s[hh, pl.ds(n, 1), :][:, c - 1:c] for hh in heads]
        lhs = [jnp.concatenate([qe_s[hh, pl.ds(r0, c), :], sm_s[hh, pl.ds(m0, HEAD_DIM), :]], axis=0)
               for hh in heads]
        prod = [dot(lhs[hh], states[hh].astype(BF16)) for hh in heads]
        for hh in heads:
            ob_s[hh, pl.ds(r0, c), :] = prod[hh][:c] + ob_s[hh, pl.ds(r0, c), :]
        return tuple(jnp.exp(g_last[hh]) * states[hh] - prod[hh][c:] + sa_s[hh, pl.ds(m0, HEAD_DIM), :]
                     for hh in heads)

    lax.fori_loop(0, n_chunks, chunk, tuple(jnp.zeros((HEAD_DIM, HEAD_DIM), F32) for _ in heads))

    for hh in heads:
        for r in range(seq // rows):
            sl = pl.ds(r * rows, rows)
            o = ob_s[hh, sl, :]
            o = o * lax.rsqrt(jnp.mean(o * o, axis=-1, keepdims=True) + RMS_EPS) * gn
            o_ref[sl, lanes[hh]] = (o * _silu(z_ref[sl, lanes[hh]])).astype(o_ref.dtype)


def _gdn(h3, conv_w, a_rows, b_rows, alog, dt, gn):
    b, s, _ = h3.shape
    hp = GDN_HEADS_PER_STEP
    nb = GDN_HEADS // hp
    wide = hp * HEAD_DIM
    n_chunks = s // GDN_CHUNK
    col = lambda off: pl.BlockSpec((None, s, wide), lambda bi, hi: (bi, 0, off + hi))
    cw = lambda off: pl.BlockSpec((GDN_CONV, wide), lambda bi, hi: (0, off + hi))
    rowspec = pl.BlockSpec((None, hp, n_chunks, GDN_CHUNK), lambda bi, hi: (bi, hi, 0, 0))
    headvec = pl.BlockSpec((hp, 1, GDN_CHUNK), lambda bi, hi: (hi, 0, 0))
    return pl.pallas_call(
        _gdn_kernel,
        grid=(b, nb),
        in_specs=[col(0), col(nb), col(2 * nb), col(3 * nb), cw(0), cw(nb), cw(2 * nb),
                  rowspec, rowspec, headvec, headvec,
                  pl.BlockSpec((1, HEAD_DIM), lambda bi, hi: (0, 0))],
        out_specs=pl.BlockSpec((None, s, wide), lambda bi, hi: (bi, 0, hi)),
        out_shape=jax.ShapeDtypeStruct((b, s, GDN_WIDTH), BF16),
        scratch_shapes=[pltpu.VMEM((s + 8, HEAD_DIM), F32), pltpu.VMEM((s, HEAD_DIM), F32),
                        pltpu.VMEM((s, HEAD_DIM), F32), pltpu.VMEM((s, HEAD_DIM), F32),
                        pltpu.VMEM((hp, n_chunks, GDN_CHUNK), F32), pltpu.VMEM((hp, n_chunks, GDN_CHUNK), F32),
                        pltpu.VMEM((hp, s, HEAD_DIM), BF16), pltpu.VMEM((hp, s, HEAD_DIM), F32),
                        pltpu.VMEM((hp, n_chunks * HEAD_DIM, HEAD_DIM), BF16),
                        pltpu.VMEM((hp, n_chunks * HEAD_DIM, HEAD_DIM), F32)],
        compiler_params=_params("parallel", "parallel"),
        name="gdn",
    )(h3, h3, h3, h3, conv_w, conv_w, conv_w, a_rows, b_rows, alog, dt, gn)


def _hgrn_kernel(q_ref, f_ref, i_ref, g_ref, lb_ref, hn_ref, o_ref):
    seq = q_ref.shape[0]
    c = HGRN_CHUNK
    rows = HGRN_ROWS
    ri = _iota((rows, rows), 0)
    ci = _iota((rows, rows), 1)
    causal = (ri // c == ci // c) & (ci <= ri)
    row_in_chunk = _iota((rows, HEAD_DIM), 0) % c
    lb = lb_ref[...]
    hn = hn_ref[...]

    chunks = [slice(j * c, (j + 1) * c) for j in range(rows // c)]
    grp = range(HGRN_GROUP)

    def chunk_scan(x):
        step = 1
        while step < c:
            x = x + jnp.where(row_in_chunk >= step, pltpu.roll(x, step, axis=0), 0.0)
            step *= 2
        return x

    def group(n, state_t):
        starts = [pl.multiple_of((n * HGRN_GROUP + j) * rows, rows) for j in grp]
        qc = [q_ref[pl.ds(r0, rows), :] for r0 in starts]
        ic = [i_ref[pl.ds(r0, rows), :].astype(BF16) for r0 in starts]
        f = [lb + (1.0 - lb) * jax.nn.sigmoid(f_ref[pl.ds(r0, rows), :]) for r0 in starts]
        log_f = [jnp.log(f[j]) for j in grp]
        bcum = [chunk_scan(log_f[j]) for j in grp]
        chunk_dec = [jnp.exp(bcum[j]) for j in grp]
        q_dec = [(qc[j] * chunk_dec[j]).astype(BF16) for j in grp]
        k_inv = [((1.0 - f[j]) * jnp.exp(-bcum[j])).astype(BF16) for j in grp]
        p = [jnp.where(causal, _dot_nt(q_dec[j], k_inv[j]), 0.0) for j in grp]
        o_intra = [_dot(p[j], ic[j]) for j in grp]
        updates = [[_dot_tn(ic[j][sl], k_inv[j][sl]) for sl in chunks] for j in grp]
        for j, r0 in enumerate(starts):
            outs = []
            for sl, upd in zip(chunks, updates[j]):
                outs.append(o_intra[j][sl] + _dot_nt(q_dec[j][sl], state_t))
                state_t = (state_t + upd) * chunk_dec[j][sl.stop - 1:sl.stop]
            o = jnp.concatenate(outs, axis=0)
            o = o * lax.rsqrt(jnp.mean(o * o, axis=-1, keepdims=True) + RMS_EPS) * hn
            o_ref[pl.ds(r0, rows), :] = (o * _silu(g_ref[pl.ds(r0, rows), :])).astype(o_ref.dtype)
        return state_t

    lax.fori_loop(0, seq // (rows * HGRN_GROUP), group, jnp.zeros((HEAD_DIM, HEAD_DIM), F32))


def _hgrn(h3, lb, hn):
    b, s, _ = h3.shape
    nh = HGRN_HEADS
    col = lambda off: pl.BlockSpec((None, s, HEAD_DIM), lambda bi, hi: (bi, 0, off + hi))
    return pl.pallas_call(
        _hgrn_kernel,
        grid=(b, nh),
        in_specs=[col(0), col(nh), col(2 * nh), col(3 * nh),
                  pl.BlockSpec((None, 1, HEAD_DIM), lambda bi, hi: (hi, 0, 0)),
                  pl.BlockSpec((1, HEAD_DIM), lambda bi, hi: (0, 0))],
        out_specs=pl.BlockSpec((None, s, HEAD_DIM), lambda bi, hi: (bi, 0, hi)),
        out_shape=jax.ShapeDtypeStruct((b, s, HGRN_WIDTH), BF16),
        compiler_params=_params("parallel", "parallel"),
        name="hgrn2",
    )(h3, h3, h3, h3, lb, hn)


def _rope(x, cos2, sin2):
    return x * cos2 + pltpu.roll(x, HEAD_DIM // 2, axis=1) * sin2


def _flash_step(q_blk, k_blk, v_blk, mask, carry):
    m, l, acc = carry
    s = jnp.where(mask, _dot_nt(q_blk, k_blk) * HEAD_DIM ** -0.5, NEG_INF)
    m_new = jnp.maximum(m, jnp.max(s, axis=-1, keepdims=True))
    alpha = jnp.exp(m - m_new)
    p = jnp.exp(s - m_new)
    l = alpha * l + jnp.sum(p, axis=-1, keepdims=True)
    acc = alpha * acc + _dot(p, v_blk)
    return m_new, l, acc


def _flash_init():
    blk = ATT_BLOCK
    return (jnp.full((blk, 1), NEG_INF, F32), jnp.zeros((blk, 1), F32), jnp.zeros((blk, HEAD_DIM), F32))


def _dilated_kernel(*refs):
    n_g = len(DIL_GROUPS)
    q_refs, k_refs, v_refs = refs[0:n_g], refs[n_g:2 * n_g], refs[2 * n_g:3 * n_g]
    cos_ref, sin_ref, o_ref = refs[3 * n_g:3 * n_g + 3]
    q_s, k_s, v_s, og_s, lse_s = refs[3 * n_g + 3:]
    seq = o_ref.shape[0]
    blk = DIL_BLOCK
    piece = 256
    grp = range(n_g)

    for gi, (window, d) in enumerate(DIL_GROUPS):
        assert window // d == blk
        seg = seq // d
        k_s[gi, pl.ds(0, blk), :] = jnp.zeros((blk, HEAD_DIM), BF16)
        v_s[gi, pl.ds(0, blk), :] = jnp.zeros((blk, HEAD_DIM), BF16)
        for r in range(d):
            for c0 in range(0, seg, piece):
                n = min(piece, seg)
                rows = pl.ds(r + c0 * d, n, stride=d) if d > 1 else pl.ds(c0, n)
                cos2 = cos_ref[gi, pl.ds(r * seg + c0, n), :]
                sin2 = sin_ref[gi, pl.ds(r * seg + c0, n), :]
                q_s[gi, pl.ds(r * seg + c0, n), :] = _rope(q_refs[gi][rows, :], cos2, sin2).astype(BF16)
                k_s[gi, pl.ds(blk + r * seg + c0, n), :] = _rope(k_refs[gi][rows, :], cos2, sin2).astype(BF16)
                v_s[gi, pl.ds(blk + r * seg + c0, n), :] = v_refs[gi][rows, :].astype(BF16)

    ri = _iota((blk, 2 * blk), 0)
    ci = _iota((blk, 2 * blk), 1)
    rel = ri + blk - ci
    in_window = (rel >= 0) & (rel <= blk)
    dot = functools.partial(jnp.dot, preferred_element_type=F32)

    def q_block(m, _):
        segs = [seq // d for _, d in DIL_GROUPS]
        work = [(g, pl.multiple_of((m * DIL_Q_PER_STEP + u) * blk, blk)) for u in range(DIL_Q_PER_STEP) for g in grp]
        ids = range(len(work))
        has_prev = [jnp.where(j0 % segs[g] != 0, blk, 0) for g, j0 in work]
        mask = [ci + has_prev[i] >= blk for i in ids]
        q = [q_s[g, pl.ds(j0, blk), :] for g, j0 in work]
        kw = [k_s[g, pl.ds(j0, 2 * blk), :] for g, j0 in work]
        vw = [v_s[g, pl.ds(j0, 2 * blk), :] for g, j0 in work]
        s = [jnp.where(in_window, jnp.where(mask[i], _dot_nt(q[i], kw[i]) * HEAD_DIM ** -0.5, NEG_INF), NEG_INF)
             for i in ids]
        top = [jnp.max(s[i], axis=-1, keepdims=True) for i in ids]
        p = [jnp.exp(s[i] - top[i]) for i in ids]
        den = [jnp.sum(p[i], axis=-1, keepdims=True) for i in ids]
        o = [dot(p[i].astype(BF16), vw[i]) / den[i] for i in ids]
        lse = [top[i] + jnp.log(den[i]) for i in ids]
        for i, (g, j0) in enumerate(work):
            seg, d = segs[g], DIL_GROUPS[g][1]
            dst = pl.ds((j0 % seg) * d + j0 // seg, blk, stride=d) if d > 1 else pl.ds(j0, blk)
            og_s[g, dst, :] = o[i]
            lse_s[g, dst, :] = jnp.broadcast_to(lse[i], (blk, HEAD_DIM))
        return 0

    lax.fori_loop(0, seq // (blk * DIL_Q_PER_STEP), q_block, 0)

    for c0 in range(0, seq, piece):
        rows = pl.ds(c0, piece)
        lses = [lse_s[g, rows, :] for g in grp]
        top = functools.reduce(jnp.maximum, lses)
        wts = [jnp.exp(x - top) for x in lses]
        den = functools.reduce(lambda a, b: a + b, wts)
        o = functools.reduce(lambda a, b: a + b, [wts[g] * og_s[g, rows, :] for g in grp]) / den
        o_ref[rows, :] = o.astype(o_ref.dtype)


def _dilated(h3, cos2, sin2):
    b, s, _ = h3.shape
    hpg = DIL_HEADS_PER_GROUP
    n_g = len(DIL_GROUPS)
    col = lambda off: pl.BlockSpec((None, s, HEAD_DIM), lambda bi, hi: (bi, 0, off + hi))
    tab = pl.BlockSpec((n_g, s, HEAD_DIM), lambda bi, hi: (0, 0, 0))
    specs = [col(part * DIL_HEADS + gi * hpg) for part in range(3) for gi in range(n_g)]
    residue_major = lambda t: jnp.stack([t.reshape(s // d, d, HEAD_DIM).transpose(1, 0, 2).reshape(s, HEAD_DIM)
                                         for _, d in DIL_GROUPS])
    cos2, sin2 = residue_major(cos2), residue_major(sin2)
    return pl.pallas_call(
        _dilated_kernel,
        grid=(b, hpg),
        in_specs=specs + [tab, tab],
        out_specs=pl.BlockSpec((None, s, HEAD_DIM), lambda bi, hi: (bi, 0, hi)),
        out_shape=jax.ShapeDtypeStruct((b, s, hpg * HEAD_DIM), BF16),
        scratch_shapes=[pltpu.VMEM((n_g, s, HEAD_DIM), BF16), pltpu.VMEM((n_g, s + DIL_BLOCK, HEAD_DIM), BF16),
                        pltpu.VMEM((n_g, s + DIL_BLOCK, HEAD_DIM), BF16), pltpu.VMEM((n_g, s, HEAD_DIM), F32),
                        pltpu.VMEM((n_g, s, HEAD_DIM), F32)],
        compiler_params=_params("parallel", "parallel"),
        name="dilated_attention",
    )(*([h3] * (3 * n_g)), cos2, sin2)


def _moba_kernel(q_ref, k_ref, v_ref, cos_ref, sin_ref, o_ref, qf_s, q_s, k_s, v_s, km_s, sel_s):
    seq = o_ref.shape[0]
    blk = MOBA_BLOCK
    n_blk = seq // blk
    cos2 = cos_ref[...]
    sin2 = sin_ref[...]
    q = _rope(q_ref[...], cos2, sin2)
    qf_s[...] = q
    q_s[...] = q.astype(BF16)
    km_s[...] = jnp.zeros(km_s.shape, F32)
    for nb in range(n_blk):
        kb = _rope(k_ref[pl.ds(nb * blk, blk), :], cos2[nb * blk:(nb + 1) * blk], sin2[nb * blk:(nb + 1) * blk])
        k_s[pl.ds(nb * blk, blk), :] = kb.astype(BF16)
        km_s[pl.ds(nb, 1), :] = jnp.mean(kb, axis=0, keepdims=True)
    v_s[...] = v_ref[...].astype(BF16)

    lane = _iota((blk, LANES), 1).astype(F32)
    causal = _iota((blk, blk), 0) >= _iota((blk, blk), 1)
    rows = lambda nb: pl.ds(nb * blk, blk)

    past = range(1, n_blk)
    km = km_s[...]
    gate = {qb: jnp.where(lane < qb, _dot_nt_hi(qf_s[rows(qb), :], km), -jnp.inf) for qb in past}
    sel = {qb: jnp.zeros((blk, LANES), F32) for qb in past}
    for _k in range(MOBA_TOPK):
        best = {qb: jnp.max(gate[qb], axis=-1, keepdims=True) for qb in past}
        first = {qb: jnp.min(jnp.where(gate[qb] == best[qb], lane, LANES), axis=-1, keepdims=True) for qb in past}
        pick = {qb: (lane == first[qb]) & (best[qb] > -jnp.inf) for qb in past}
        sel = {qb: jnp.where(pick[qb], 1.0, sel[qb]) for qb in past}
        gate = {qb: jnp.where(pick[qb], -jnp.inf, gate[qb]) for qb in past}
    for qb in past:
        sel_s[rows(qb), :] = sel[qb]

    assert sorted(qb for group in MOBA_Q_GROUPS for qb in group) == list(range(n_blk))
    for group in MOBA_Q_GROUPS:
        carry = {qb: _flash_init() for qb in group}
        for j in range(max(group) + 1):
            for qb in group:
                if j < qb:
                    mask = sel_s[rows(qb), :][:, j:j + 1] > 0.0
                elif j == qb:
                    mask = causal
                else:
                    continue
                carry[qb] = _flash_step(q_s[rows(qb), :], k_s[rows(j), :], v_s[rows(j), :], mask, carry[qb])
        for qb in group:
            m, l, acc = carry[qb]
            o_ref[rows(qb), :] = (acc / l).astype(o_ref.dtype)


def _moba(h3, cos2, sin2):
    b, s, _ = h3.shape
    base = 3 * DIL_HEADS
    col = lambda off: pl.BlockSpec((None, s, HEAD_DIM), lambda bi, hi: (bi, 0, base + off + hi))
    tab = pl.BlockSpec((s, HEAD_DIM), lambda bi, hi: (0, 0))
    return pl.pallas_call(
        _moba_kernel,
        grid=(b, MOBA_HEADS),
        in_specs=[col(0), col(MOBA_HEADS), col(2 * MOBA_HEADS), tab, tab],
        out_specs=pl.BlockSpec((None, s, HEAD_DIM), lambda bi, hi: (bi, 0, hi)),
        out_shape=jax.ShapeDtypeStruct((b, s, MOBA_HEADS * HEAD_DIM), BF16),
        scratch_shapes=[pltpu.VMEM((s, HEAD_DIM), F32), pltpu.VMEM((s, HEAD_DIM), BF16),
                        pltpu.VMEM((s, HEAD_DIM), BF16), pltpu.VMEM((s, HEAD_DIM), BF16),
                        pltpu.VMEM((LANES, HEAD_DIM), F32), pltpu.VMEM((s, LANES), F32)],
        compiler_params=_params("parallel", "parallel"),
        name="moba_attention",
    )(h3, h3, h3, cos2, sin2)


def _layer_norm(x, gain, bias):
    mu = jnp.mean(x, axis=-1, keepdims=True)
    xc = x - mu
    var = jnp.mean(xc * xc, axis=-1, keepdims=True)
    return xc * lax.rsqrt(var + LN_EPS) * gain + bias


def _out_ln_kernel(*refs, n_parts):
    o_refs = refs[0:n_parts]
    w_refs = refs[n_parts:2 * n_parts]
    h_ref, gain_ref, bias_ref, rw_ref, y_ref, logit_ref = refs[2 * n_parts:]
    half = h_ref.shape[0] // 2
    halves = [pl.ds(k * half, half) for k in range(2)]
    mix = []
    for rows in halves:
        parts = [jnp.dot(o_r[rows, :], w_r[...], preferred_element_type=F32) for o_r, w_r in zip(o_refs, w_refs)]
        mix.append(functools.reduce(lambda a, b: a + b, parts))
    y = [_layer_norm(DEEPNORM_ALPHA * h_ref[rows, :] + mix[k], gain_ref[...], bias_ref[...])
         for k, rows in enumerate(halves)]
    logits = [_dot3(y[k], rw_ref[...]) for k in range(2)]
    for k, rows in enumerate(halves):
        y_ref[rows, :] = y[k]
        logit_ref[rows, :] = logits[k]


def _out_ln(parts, weights, h, gain, bias, router_w, tm):
    t, d = h.shape
    n_parts = len(parts)
    row = lambda width: pl.BlockSpec((tm, width), lambda i: (i, 0))
    full = lambda a: pl.BlockSpec(a.shape, lambda i: (0, 0))
    return pl.pallas_call(
        functools.partial(_out_ln_kernel, n_parts=n_parts),
        grid=(t // tm,),
        in_specs=[row(p.shape[1]) for p in parts] + [full(w) for w in weights]
                 + [row(d), full(gain), full(bias), full(router_w)],
        out_specs=[row(d), row(LANES)],
        out_shape=[jax.ShapeDtypeStruct((t, d), F32), jax.ShapeDtypeStruct((t, LANES), F32)],
        compiler_params=_params("parallel"),
        name="out_proj_ln",
    )(*parts, *weights, h, gain, bias, router_w)


def _route(logits_t, rbias_col):
    row = _iota(logits_t.shape, 0)
    scores = jax.nn.sigmoid(logits_t)
    biased = scores + rbias_col

    def first_argmax(vals):
        best = jnp.max(vals, axis=0, keepdims=True)
        return best, jnp.min(jnp.where(vals == best, row, N_EXPERTS), axis=0, keepdims=True)

    best_score = None
    best_group = None
    for g in range(N_EXPERT_GROUPS):
        vals = jnp.where(row // EXPERTS_PER_GROUP == g, biased, -jnp.inf)
        top1, idx1 = first_argmax(vals)
        top2, _ = first_argmax(jnp.where(row == idx1, -jnp.inf, vals))
        score = top1 + top2
        if g == 0:
            best_score, best_group = score, jnp.zeros_like(idx1)
        else:
            better = score > best_score
            best_group = jnp.where(better, g, best_group)
            best_score = jnp.where(better, score, best_score)
    masked = jnp.where(row // EXPERTS_PER_GROUP == best_group, biased, NEG_INF)
    _, i1 = first_argmax(masked)
    _, i2 = first_argmax(jnp.where(row == i1, -jnp.inf, masked))
    s1 = jnp.sum(jnp.where(row == i1, scores, 0.0), axis=0, keepdims=True)
    s2 = jnp.sum(jnp.where(row == i2, scores, 0.0), axis=0, keepdims=True)
    tot = s1 + s2
    return i1, i2, s1 / tot, s2 / tot


def _route_kernel(logit_ref, rbias_ref, loc_ref, w_ref, tab_ref, cnt_ref, carry_s, total_s):
    sweep = pl.program_id(0)
    i = pl.program_id(1)
    tm = logit_ref.shape[0]
    block = float(MOE_BLOCK_ROWS)
    align = float(MOE_ALIGN)

    @pl.when((i == 0) & (sweep == 1))
    def _():
        total_s[...] = carry_s[...]

    @pl.when(i == 0)
    def _():
        carry_s[...] = jnp.zeros(carry_s.shape, F32)

    logits_t = logit_ref[...].T[:N_EXPERTS]
    i1, i2, w1, w2 = _route(logits_t, rbias_ref[...][:N_EXPERTS])
    row = _iota((N_EXPERTS, tm), 0)
    chosen = jnp.where((row == i1) | (row == i2), 1.0, 0.0)
    count = jnp.sum(chosen, axis=1, keepdims=True)
    padded = jnp.floor((count + (align - 1.0)) * (1.0 / align)) * align

    e_r = _iota((N_EXPERTS, N_EXPERTS), 0)
    e_c = _iota((N_EXPERTS, N_EXPERTS), 1)

    def expert_prefix(col, inclusive):
        tri = ((e_c <= e_r) if inclusive else (e_c < e_r)).astype(BF16)
        wide = jnp.broadcast_to(col, (N_EXPERTS, LANES)).astype(BF16)
        return jnp.dot(tri, wide, preferred_element_type=F32)[:, 0:1]

    @pl.when(sweep == 0)
    def _():
        loc_ref[...] = jnp.zeros(loc_ref.shape, loc_ref.dtype)
        w_ref[...] = jnp.zeros(w_ref.shape, w_ref.dtype)
        tab_ref[...] = jnp.zeros(tab_ref.shape, tab_ref.dtype)
        cnt_ref[...] = jnp.zeros(cnt_ref.shape, cnt_ref.dtype)

    @pl.when(sweep == 1)
    def _():
        total = total_s[...]
        blocks = jnp.floor((total + (block - 1.0)) * (1.0 / block))
        region = (expert_prefix(blocks, True) - blocks) * block + carry_s[...]
        run_start = expert_prefix(padded, False)
        earlier = (_iota((tm, tm), 0) < _iota((tm, tm), 1)).astype(BF16)
        slot = run_start + jnp.dot(chosen.astype(BF16), earlier, preferred_element_type=F32)
        l1 = jnp.sum(jnp.where(row == i1, slot, 0.0), axis=0, keepdims=True)
        l2 = jnp.sum(jnp.where(row == i2, slot, 0.0), axis=0, keepdims=True)
        out_row = _iota((8, tm), 0)
        loc_ref[...] = jnp.where(out_row == 0, l1, jnp.where(out_row == 1, l2, 0.0)).astype(jnp.int32)
        w_ref[...] = jnp.where(out_row == 0, w1, jnp.where(out_row == 1, w2, 0.0))
        lane = _iota((N_EXPERTS, LANES), 1)
        tab = jnp.where(lane == 0, padded, jnp.where(lane == 1, run_start, jnp.where(lane == 2, region, 0.0)))
        tab_ref[...] = tab.astype(jnp.int32)
        cnt_ref[...] = jnp.broadcast_to(total, cnt_ref.shape).astype(jnp.int32)

    carry_s[...] = carry_s[...] + padded


def _route_tokens(logits, rbias_col):
    t = logits.shape[0]
    tm = MOE_TILE
    n_tiles = t // tm
    tok = pl.BlockSpec((None, 8, tm), lambda s, i: (s, 0, i))
    loc, w, tab, totals = pl.pallas_call(
        _route_kernel,
        grid=(2, n_tiles),
        in_specs=[pl.BlockSpec((tm, LANES), lambda s, i: (i, 0)), pl.BlockSpec((LANES, 1), lambda s, i: (0, 0))],
        out_specs=[tok, tok, pl.BlockSpec((None, None, N_EXPERTS, LANES), lambda s, i: (s, i, 0, 0)),
                   pl.BlockSpec((N_EXPERTS, LANES), lambda s, i: (0, 0))],
        out_shape=[jax.ShapeDtypeStruct((2, 8, t), jnp.int32), jax.ShapeDtypeStruct((2, 8, t), F32),
                   jax.ShapeDtypeStruct((2, n_tiles, N_EXPERTS, LANES), jnp.int32),
                   jax.ShapeDtypeStruct((N_EXPERTS, LANES), jnp.int32)],
        scratch_shapes=[pltpu.VMEM((N_EXPERTS, 1), F32), pltpu.VMEM((N_EXPERTS, 1), F32)],
        compiler_params=_params("arbitrary", "arbitrary"),
        name="moe_route",
    )(logits, rbias_col)
    return loc[1], w[1], tab[1], totals[:, 0]


def _run_pieces(length, sizes):
    return [(size, length & ~(2 * size - 1), (length & size) != 0) for size in sizes]


def _tile_run_copies(cnt_ref, off_ref, region_ref, tile, compact, sorted_rows, sem, to_sorted):
    copies = []
    for e in range(N_EXPERTS):
        k = tile * N_EXPERTS + e
        length = pl.multiple_of(cnt_ref[k], MOE_ALIGN)
        small = compact.at[pl.ds(pl.multiple_of(off_ref[k], MOE_ALIGN), length)]
        big = sorted_rows.at[pl.ds(pl.multiple_of(region_ref[k], MOE_ALIGN), length)]
        copies.append((length > 0, pltpu.make_async_copy(small, big, sem) if to_sorted
                       else pltpu.make_async_copy(big, small, sem)))
    return copies


def _dispatch_kernel(cnt_ref, off_ref, region_ref, pad_start_ref, pad_len_ref, n_valid_ref,
                     x_ref, loc_ref, w_ref, xs_ref, buf_s, zero_s, sem, zero_sem):
    i = pl.program_id(0)
    d = x_ref.shape[1]
    cap = buf_s.shape[1]
    tm = x_ref.shape[0]
    n_blocks = xs_ref.shape[0] // MOE_BLOCK_ROWS

    @pl.when(i == 0)
    def _():
        zero_s[...] = jnp.zeros(zero_s.shape, zero_s.dtype)
        sizes = [MOE_BLOCK_ROWS >> s for s in range((MOE_BLOCK_ROWS // MOE_ALIGN).bit_length())]
        copies = []
        for e in range(N_EXPERTS):
            for size, offset, used in _run_pieces(pad_len_ref[e], sizes):
                dst = xs_ref.at[pl.ds(pl.multiple_of(pad_start_ref[e] + offset, MOE_ALIGN), size)]
                copies.append((used, pltpu.make_async_copy(zero_s.at[pl.ds(0, size)], dst, zero_sem)))
        for j in range(n_blocks - N_EXPERTS, n_blocks):
            dst = xs_ref.at[pl.ds(j * MOE_BLOCK_ROWS, MOE_BLOCK_ROWS)]
            copies.append((j >= n_valid_ref[0], pltpu.make_async_copy(zero_s, dst, zero_sem)))
        for action in ("start", "wait"):
            for used, copy in copies:
                @pl.when(used)
                def _():
                    getattr(copy, action)()

    loc = loc_ref[...]
    w = w_ref[...]
    row = _iota((cap, tm), 0)
    hit1 = row == loc[0:1, :]
    hit2 = row == loc[1:2, :]
    perm = jnp.where(hit1, 1.0, jnp.where(hit2, 1.0, 0.0)).astype(BF16)
    gate = jnp.sum(jnp.where(hit1, w[0:1, :], jnp.where(hit2, w[1:2, :], 0.0)), axis=1, keepdims=True)
    g_hi = gate.astype(BF16)
    g_lo = (gate - g_hi.astype(F32)).astype(BF16)
    lane = _iota((cap, LANES), 1)
    slot = i % 2
    buf_s[slot, :, pl.ds(0, d)] = jnp.dot(perm, x_ref[...].astype(BF16), preferred_element_type=F32).astype(BF16)
    buf_s[slot, :, pl.ds(d, LANES)] = jnp.where(lane == 0, g_hi.astype(F32),
                                                jnp.where(lane == 1, g_lo.astype(F32), 0.0)).astype(BF16)

    def run_copies(tile, which, action):
        for used, copy in _tile_run_copies(cnt_ref, off_ref, region_ref, tile, buf_s.at[which], xs_ref,
                                           sem.at[which], True):
            @pl.when(used)
            def _():
                getattr(copy, action)()

    run_copies(i, slot, "start")

    @pl.when(i > 0)
    def _():
        run_copies(i - 1, 1 - slot, "wait")

    @pl.when(i == pl.num_programs(0) - 1)
    def _():
        run_copies(i, slot, "wait")


def _dispatch(tables, pad_start, pad_len, n_valid, x, loc, w, n_rows):
    t, d = x.shape
    tm = MOE_TILE
    cap = MOE_TOPK * tm + N_EXPERTS * MOE_ALIGN
    tok = pl.BlockSpec((8, tm), lambda i, *_: (0, i))
    grid_spec = pltpu.PrefetchScalarGridSpec(
        num_scalar_prefetch=6,
        grid=(t // tm,),
        in_specs=[pl.BlockSpec((tm, d), lambda i, *_: (i, 0)), tok, tok],
        out_specs=pl.BlockSpec(memory_space=pl.ANY),
        scratch_shapes=[pltpu.VMEM((2, cap, d + LANES), BF16), pltpu.VMEM((MOE_BLOCK_ROWS, d + LANES), BF16),
                        pltpu.SemaphoreType.DMA((2,)), pltpu.SemaphoreType.DMA(())],
    )
    return pl.pallas_call(
        _dispatch_kernel,
        grid_spec=grid_spec,
        out_shape=jax.ShapeDtypeStruct((n_rows, d + LANES), BF16),
        compiler_params=_params("arbitrary"),
        name="moe_dispatch",
    )(*tables, pad_start, pad_len, n_valid, x, loc, w)


def _expert_kernel(blk_expert_ref, next_expert_ref, n_valid_ref, x_ref, wg_ref, wu_ref, wd_ref, y_ref,
                   wg_s, wu_s, wd_s, wg_buf, wu_buf, wd_buf, slot_s, sem, *, layer):
    i = pl.program_id(0)

    def fetch(expert, slot):
        return [pltpu.make_async_copy(src.at[layer, expert], dst.at[slot], sem.at[slot])
                for src, dst in ((wg_ref, wg_buf), (wu_ref, wu_buf), (wd_ref, wd_buf))]

    @pl.when(i == 0)
    def _():
        slot_s[0] = 1
        for copy in fetch(blk_expert_ref[0], 0):
            copy.start()

    @pl.when((i == 0) | (blk_expert_ref[i] != blk_expert_ref[jnp.maximum(i - 1, 0)]))
    def _():
        slot = 1 - slot_s[0]
        slot_s[0] = slot
        for copy in fetch(blk_expert_ref[i], slot):
            copy.wait()
        wg_s[...] = wg_buf[slot].astype(BF16)
        wu_s[...] = wu_buf[slot].astype(BF16)
        wd_s[...] = wd_buf[slot].astype(BF16)

        @pl.when(next_expert_ref[i] >= 0)
        def _():
            for copy in fetch(next_expert_ref[i], 1 - slot):
                copy.start()

    @pl.when(i < n_valid_ref[0])
    def _():
        d = wg_s.shape[0]
        x = x_ref[:, pl.ds(0, d)]
        extra = x_ref[:, pl.ds(d, LANES)].astype(F32)
        gate = extra[:, 0:1] + extra[:, 1:2]
        hid = _silu(jnp.dot(x, wg_s[...], preferred_element_type=F32)) * jnp.dot(
            x, wu_s[...], preferred_element_type=F32)
        y_ref[...] = _dot(hid * gate, wd_s[...]).astype(y_ref.dtype)

    @pl.when(i >= n_valid_ref[0])
    def _():
        y_ref[...] = jnp.zeros(y_ref.shape, y_ref.dtype)


def _experts(blk_expert, next_expert, n_valid, xs, wg, wu, wd, layer, n_blocks):
    d = wg.shape[-2]
    f = wg.shape[-1]
    rows = MOE_BLOCK_ROWS
    n_rows = n_blocks * rows
    hbm = pl.BlockSpec(memory_space=pl.ANY)
    grid_spec = pltpu.PrefetchScalarGridSpec(
        num_scalar_prefetch=3,
        grid=(n_blocks,),
        in_specs=[pl.BlockSpec((rows, xs.shape[1]), lambda i, be, ne, nv: (i, 0)), hbm, hbm, hbm],
        out_specs=pl.BlockSpec((rows, d), lambda i, be, ne, nv: (i, 0)),
        scratch_shapes=[pltpu.VMEM((d, f), BF16), pltpu.VMEM((d, f), BF16), pltpu.VMEM((f, d), BF16),
                        pltpu.VMEM((2, d, f), F32), pltpu.VMEM((2, d, f), F32), pltpu.VMEM((2, f, d), F32),
                        pltpu.SMEM((1,), jnp.int32), pltpu.SemaphoreType.DMA((2,))],
    )
    return pl.pallas_call(
        functools.partial(_expert_kernel, layer=layer),
        grid_spec=grid_spec,
        out_shape=jax.ShapeDtypeStruct((n_rows, d), BF16),
        compiler_params=_params("arbitrary"),
        name="moe_experts",
    )(blk_expert, next_expert, n_valid, xs, wg, wu, wd)


def _combine_kernel(cnt_ref, off_ref, region_ref, ys_ref, loc_ref, h_ref, gain_ref, bias_ref, y_ref, yb_ref,
                    buf, sem):
    i = pl.program_id(0)
    n = pl.num_programs(0)
    tm = h_ref.shape[0]
    cap = buf.shape[1]

    def fetch(tile, slot, action):
        for used, copy in _tile_run_copies(cnt_ref, off_ref, region_ref, tile, buf.at[slot], ys_ref,
                                           sem.at[slot], False):
            @pl.when(used)
            def _():
                getattr(copy, action)()

    slot = i % 2

    @pl.when(i == 0)
    def _():
        buf[...] = jnp.zeros(buf.shape, buf.dtype)
        fetch(0, 0, "start")

    @pl.when(i + 1 < n)
    def _():
        fetch(i + 1, 1 - slot, "start")

    fetch(i, slot, "wait")
    loc = loc_ref[...].astype(F32)
    eye = _iota((tm, tm), 0) == _iota((tm, tm), 1)
    loc_col = [jnp.sum(jnp.where(eye, jnp.broadcast_to(loc[k:k + 1, :], (tm, tm)), 0.0), axis=1, keepdims=True)
               for k in range(MOE_TOPK)]
    lane = _iota((tm, cap), 1).astype(F32)
    pick = jnp.where(lane == loc_col[0], 1.0, jnp.where(lane == loc_col[1], 1.0, 0.0)).astype(BF16)
    ffn = jnp.dot(pick, buf[slot], preferred_element_type=F32)
    y = _layer_norm(DEEPNORM_ALPHA * h_ref[...] + ffn, gain_ref[...], bias_ref[...])
    y_ref[...] = y
    yb_ref[...] = y.astype(BF16)


def _combine_ln(tables, ys, loc, h, gain, bias):
    t, d = h.shape
    tm = MOE_TILE
    cap = MOE_TOPK * tm + N_EXPERTS * MOE_ALIGN
    grid_spec = pltpu.PrefetchScalarGridSpec(
        num_scalar_prefetch=3,
        grid=(t // tm,),
        in_specs=[pl.BlockSpec(memory_space=pl.ANY),
                  pl.BlockSpec((8, tm), lambda i, *_: (0, i)),
                  pl.BlockSpec((tm, d), lambda i, *_: (i, 0)),
                  pl.BlockSpec((1, d), lambda i, *_: (0, 0)),
                  pl.BlockSpec((1, d), lambda i, *_: (0, 0))],
        out_specs=[pl.BlockSpec((tm, d), lambda i, *_: (i, 0)), pl.BlockSpec((tm, d), lambda i, *_: (i, 0))],
        scratch_shapes=[pltpu.VMEM((2, cap, d), BF16), pltpu.SemaphoreType.DMA((2,))],
    )
    return pl.pallas_call(
        _combine_kernel,
        grid_spec=grid_spec,
        out_shape=[jax.ShapeDtypeStruct((t, d), F32), jax.ShapeDtypeStruct((t, d), BF16)],
        compiler_params=_params("arbitrary"),
        name="moe_combine_ln",
    )(*tables, ys, loc, h, gain, bias)


def _moe_ln(h, logits, rbias, wg, wu, wd, layer, gain, bias):
    t, d = h.shape
    rows = MOE_BLOCK_ROWS
    n_tiles = t // MOE_TILE
    n_blocks = -(-(MOE_TOPK * t + n_tiles * N_EXPERTS * (MOE_ALIGN - 1)) // rows) + N_EXPERTS
    loc, w_t, tab, counts = _route_tokens(logits, rbias)
    tables = tuple(tab[:, :, k].reshape(-1) for k in range(3))
    blocks_per_expert = (counts + rows - 1) // rows
    block_end = jnp.cumsum(blocks_per_expert)
    row_start = (block_end - blocks_per_expert) * rows
    n_valid = block_end[-1:].astype(jnp.int32)
    blk = jnp.arange(n_blocks, dtype=jnp.int32)
    blk_expert = jnp.sum(jnp.minimum(blk, n_valid - 1)[:, None] >= block_end[None, :], axis=1).astype(jnp.int32)
    pad_start = (row_start + counts).astype(jnp.int32)
    pad_len = (blocks_per_expert * rows - counts).astype(jnp.int32)
    xs = _dispatch(tables, pad_start, pad_len, n_valid, h, loc, w_t, n_blocks * rows)
    experts = jnp.arange(N_EXPERTS, dtype=jnp.int32)
    later_used = (experts[None, :] > experts[:, None]) & (blocks_per_expert[None, :] > 0)
    next_used = jnp.min(jnp.where(later_used, experts[None, :], N_EXPERTS), axis=1)
    next_expert = jnp.where(next_used < N_EXPERTS, next_used, -1).astype(jnp.int32)[blk_expert]
    ys = _experts(blk_expert, next_expert, n_valid, xs, wg, wu, wd, layer, n_blocks)
    return _combine_ln(tables, ys, loc, h, gain, bias)


def _rope_tables(seq):
    inv_freq = ROPE_THETA ** (-jnp.arange(0, HEAD_DIM, 2, dtype=F32) / HEAD_DIM)
    ang = jnp.arange(seq, dtype=F32)[:, None] * inv_freq[None, :]
    cos, sin = jnp.cos(ang), jnp.sin(ang)
    return jnp.concatenate([cos, cos], axis=-1), jnp.concatenate([-sin, sin], axis=-1)


def _even_mixer(hb, b, s, w_in, conv_w, a_log, dt_bias, gdn_norm, hgrn_norm, lower_bound):
    gw = GDN_WIDTH
    n_small = 2 * GDN_HEADS
    tail0 = 4 * gw
    w_t = w_in.T
    w_small = jnp.pad(w_t[tail0:tail0 + n_small], ((0, LANES - n_small), (0, 0))).astype(BF16)
    h_a = _matmul(hb, w_t[:tail0].astype(BF16), *EVEN_PROJ_TILE, transposed=True).reshape(b, s, tail0)
    h_b = _matmul(hb, w_t[tail0 + n_small:].astype(BF16), *EVEN_PROJ_TILE, transposed=True)
    h_b = h_b.reshape(b, s, 4 * HGRN_WIDTH)
    n_chunks = s // GDN_CHUNK
    small = _matmul(hb, w_small, EVEN_PROJ_TILE[0], LANES, transposed=True).reshape(b, s, LANES)
    to_rows = lambda a: a.transpose(0, 2, 1).reshape(b, GDN_HEADS, n_chunks, GDN_CHUNK)
    b_rows, a_rows = to_rows(small[..., :GDN_HEADS]), to_rows(small[..., GDN_HEADS:n_small])
    headvec = lambda v: jnp.broadcast_to(v.astype(F32)[:, None, None], (GDN_HEADS, 1, GDN_CHUNK))
    o_a = _gdn(h_a, conv_w.astype(F32), a_rows, b_rows, headvec(a_log), headvec(dt_bias),
               gdn_norm.astype(F32).reshape(1, HEAD_DIM))
    o_b = _hgrn(h_b, lower_bound.astype(F32).reshape(HGRN_HEADS, 1, HEAD_DIM),
                hgrn_norm.astype(F32).reshape(1, HEAD_DIM))
    return [o_a.reshape(b * s, GDN_WIDTH), o_b.reshape(b * s, HGRN_WIDTH)]


def _odd_mixer(hb, b, s, w_in, cos2, sin2):
    h = _matmul(hb, w_in.astype(BF16), *ODD_PROJ_TILE)
    h3 = h.reshape(b, s, ODD_COLS)
    o_c = _dilated(h3, cos2, sin2)
    o_d = _moba(h3, cos2, sin2)
    return [o_c.reshape(b * s, -1), o_d.reshape(b * s, -1)]


def kernel(x, ev_w_in, ev_conv_w, ev_a_log, ev_dt_bias, ev_gdn_norm, ev_hgrn_norm, hgrn_lb_logits, ev_w_out,
           od_w_in, od_w_out, router_w, router_bias, moe_w_gate, moe_w_up, moe_w_down, ln_gain, ln_bias):
    b, s, d = x.shape
    t = b * s
    cos2, sin2 = _rope_tables(s)
    lower_bounds = jnp.cumsum(jax.nn.softmax(hgrn_lb_logits.astype(F32), axis=0), axis=0)
    rw = jnp.pad(router_w.astype(F32), ((0, 0), (0, LANES - N_EXPERTS)))
    rbias = jnp.pad(router_bias.astype(F32), (0, LANES - N_EXPERTS)).reshape(LANES, 1)
    vec = lambda v: v.astype(F32).reshape(1, d)

    h = x.reshape(t, d)
    hb = h
    for layer in range(DEPTH):
        if layer % 2 == 0:
            e = layer // 2
            parts = _even_mixer(hb, b, s, ev_w_in[e], ev_conv_w[e], ev_a_log[e], ev_dt_bias[e], ev_gdn_norm[e],
                                ev_hgrn_norm[e], lower_bounds[layer])
            w_out = ev_w_out[e].astype(BF16)
        else:
            o = layer // 2
            parts = _odd_mixer(hb, b, s, od_w_in[o], cos2, sin2)
            w_out = od_w_out[o].astype(BF16)
        splits = np.cumsum([p.shape[1] for p in parts])[:-1]
        weights = jnp.split(w_out, splits, axis=0)
        h, logits = _out_ln(parts, weights, h, vec(ln_gain[layer, 0]), vec(ln_bias[layer, 0]), rw, OUT_PROJ_ROWS)
        h, hb = _moe_ln(h, logits, rbias, moe_w_gate, moe_w_up, moe_w_down, layer,
                        vec(ln_gain[layer, 1]), vec(ln_bias[layer, 1]))
    return h.reshape(b, s, d)
```

```python
import functools

import jax
import jax.numpy as jnp
import numpy as np
from jax import lax
from jax.experimental import pallas as pl
from jax.experimental.pallas import tpu as pltpu

F32 = jnp.float32
BF16 = jnp.bfloat16

DEPTH = 2
HEAD_DIM = 128
GDN_HEADS = 8
GDN_CONV = 4
GDN_CHUNK = 64
GDN_WIDTH = GDN_HEADS * HEAD_DIM
HGRN_HEADS = 8
HGRN_CHUNK = 16
HGRN_WIDTH = HGRN_HEADS * HEAD_DIM
DIL_GROUPS = ((128, 1), (512, 4), (2048, 16))
DIL_HEADS_PER_GROUP = 4
DIL_HEADS = len(DIL_GROUPS) * DIL_HEADS_PER_GROUP
MOBA_HEADS = 4
MOBA_BLOCK = 256
MOBA_TOPK = 3
ROPE_THETA = 10000.0
N_EXPERTS = 16
N_EXPERT_GROUPS = 4
EXPERTS_PER_GROUP = N_EXPERTS // N_EXPERT_GROUPS
MOE_TOPK = 2
MOE_BLOCK_ROWS = 512
MOE_TILE = 512
MOE_ALIGN = 16
DEEPNORM_ALPHA = (2.0 * DEPTH) ** 0.25
LN_EPS = 1e-5
RMS_EPS = 1e-6
NEG_INF = -1e30

LANES = 128
VMEM_LIMIT = 56 * 1024 * 1024
EVEN_PROJ_TILE = (1024, 2048)
ODD_PROJ_TILE = (2048, 1024)
OUT_PROJ_ROWS = 256
ATT_BLOCK = 256
MOBA_Q_GROUPS = ((7, 0, 6, 1), (5, 2, 4, 3))
DIL_Q_PER_STEP = 4
DIL_BLOCK = 128
GDN_HEADS_PER_STEP = 2
GDN_GROUP = 32
HGRN_GROUP = 4
HGRN_ROWS = 256

ODD_COLS = 3 * DIL_HEADS * HEAD_DIM + 3 * MOBA_HEADS * HEAD_DIM


def _dot(a, b):
    return jnp.dot(a.astype(BF16), b.astype(BF16), preferred_element_type=F32)


def _dot_nt(a, b):
    return lax.dot_general(a.astype(BF16), b.astype(BF16), (((1,), (1,)), ((), ())),
                           preferred_element_type=F32)


def _dot_tn(a, b):
    return lax.dot_general(a.astype(BF16), b.astype(BF16), (((0,), (0,)), ((), ())),
                           preferred_element_type=F32)


def _dot_hi(a, b):
    return jnp.dot(a, b, preferred_element_type=F32, precision=lax.Precision.HIGHEST)


def _dot_nt_hi(a, b):
    return lax.dot_general(a, b, (((1,), (1,)), ((), ())), preferred_element_type=F32,
                           precision=lax.Precision.HIGHEST)


def _dot3(a, b):
    a_hi = a.astype(BF16)
    b_hi = b.astype(BF16)
    a_lo = (a - a_hi.astype(F32)).astype(BF16)
    b_lo = (b - b_hi.astype(F32)).astype(BF16)
    dot = functools.partial(jnp.dot, preferred_element_type=F32)
    return dot(a_hi, b_hi) + (dot(a_hi, b_lo) + dot(a_lo, b_hi))


_dot_inv = _dot


def _silu(x):
    return x * jax.nn.sigmoid(x)


def _iota(shape, dim):
    return lax.broadcasted_iota(jnp.int32, shape, dim)


def _params(*sem):
    return pltpu.CompilerParams(dimension_semantics=sem, vmem_limit_bytes=VMEM_LIMIT)


def _mm_kernel(x_ref, w_ref, o_ref, *, transposed):
    x = x_ref[...].astype(BF16)
    prod = _dot_nt(x, w_ref[...]) if transposed else jnp.dot(x, w_ref[...], preferred_element_type=F32)
    o_ref[...] = prod.astype(o_ref.dtype)


def _matmul(x, w, tm, tn, transposed=False):
    m, k = x.shape
    n = w.shape[0] if transposed else w.shape[1]
    assert m % tm == 0 and n % tn == 0
    w_spec = pl.BlockSpec((tn, k), lambda i, j: (j, 0)) if transposed else pl.BlockSpec((k, tn), lambda i, j: (0, j))
    return pl.pallas_call(
        functools.partial(_mm_kernel, transposed=transposed),
        grid=(m // tm, n // tn),
        in_specs=[pl.BlockSpec((tm, k), lambda i, j: (i, 0)), w_spec],
        out_specs=pl.BlockSpec((tm, tn), lambda i, j: (i, j)),
        out_shape=jax.ShapeDtypeStruct((m, n), F32),
        compiler_params=_params("parallel", "parallel"),
        name="in_proj",
    )(x, w)


def _gdn_kernel(q_ref, k_ref, v_ref, z_ref, cwq_ref, cwk_ref, cwv_ref, a_ref, b_ref, alog_ref, dt_ref,
                gn_ref, o_ref, pad_s, q_s, k_s, v_s, gcum_s, beta_s, qe_s, ob_s, sm_s, sa_s):
    seq = q_ref.shape[0]
    c = GDN_CHUNK
    n_chunks = seq // c
    rows = 256
    heads = range(GDN_HEADS_PER_STEP)
    lanes = [slice(hh * HEAD_DIM, (hh + 1) * HEAD_DIM) for hh in heads]

    def conv_norm(hh):
        pad_s[pl.ds(0, 8), :] = jnp.zeros((8, HEAD_DIM), F32)
        for x_ref, cw_ref, dst, mode in ((q_ref, cwq_ref, q_s, "q"), (k_ref, cwk_ref, k_s, "k"),
                                         (v_ref, cwv_ref, v_s, "v")):
            pad_s[pl.ds(8, seq), :] = x_ref[:, lanes[hh]]
            cw = cw_ref[:, lanes[hh]]
            for r in range(seq // rows):
                acc = None
                for j in range(GDN_CONV):
                    tap = pad_s[pl.ds(8 + r * rows - (GDN_CONV - 1) + j, rows), :] * cw[j:j + 1, :]
                    acc = tap if acc is None else acc + tap
                y = _silu(acc)
                if mode != "v":
                    y = y * lax.rsqrt(jnp.sum(y * y, axis=-1, keepdims=True) + RMS_EPS)
                if mode == "q":
                    y = y * HEAD_DIM ** -0.5
                dst[pl.ds(r * rows, rows), :] = y

    upper = (_iota((c, c), 0) <= _iota((c, c), 1)).astype(F32)
    for hh in heads:
        g = -jnp.exp(alog_ref[hh]) * jax.nn.softplus(a_ref[hh] + dt_ref[hh])
        gcum_s[hh] = _dot_hi(g, upper)
        beta_s[hh] = jax.nn.sigmoid(b_ref[hh])

    ri = _iota((c, c), 0)
    ci = _iota((c, c), 1)
    eye = ri == ci
    strict = ri > ci
    incl = ri >= ci
    eye_f = eye.astype(F32)
    level1 = ri // 2 == ci // 2
    levels = []
    s = 2
    while s < c:
        levels.append((ri // (2 * s) == ci // (2 * s)) & ((ri // s) % 2 == 1) & ((ci // s) % 2 == 0))
        s *= 2

    dot = functools.partial(jnp.dot, preferred_element_type=F32)

    def to_col(row):
        return jnp.sum(jnp.where(eye, jnp.broadcast_to(row, (c, c)), 0.0), axis=1, keepdims=True)

    def prepare(i, _, hh):
        n0 = i * GDN_GROUP
        grp = range(GDN_GROUP)
        starts = [pl.multiple_of((n0 + j) * c, c) for j in grp]
        qc = [q_s[pl.ds(r0, c), :] for r0 in starts]
        kc = [k_s[pl.ds(r0, c), :] for r0 in starts]
        vc = [v_s[pl.ds(r0, c), :] for r0 in starts]
        g_row = [gcum_s[hh, pl.ds(n0 + j, 1), :] for j in grp]
        g_col = [to_col(g_row[j]) for j in grp]
        b_col = [to_col(beta_s[hh, pl.ds(n0 + j, 1), :]) for j in grp]
        decay = [jnp.exp(jnp.where(incl, g_col[j] - g_row[j], 0.0)) for j in grp]
        n_mat = [b_col[j] * jnp.where(strict, decay[j], 0.0) * _dot_nt(kc[j], kc[j]) for j in grp]
        inv = [eye_f - jnp.where(level1, n_mat[j], 0.0) for j in grp]
        for blk in levels:
            tmp = [_dot_inv(inv[j], jnp.where(blk, n_mat[j], 0.0)) for j in grp]
            inv = [inv[j] - _dot_inv(tmp[j], inv[j]) for j in grp]
        e_col = [jnp.exp(g_col[j]) for j in grp]
        sol = [_dot_inv(inv[j], jnp.concatenate([b_col[j] * vc[j], (b_col[j] * e_col[j]) * kc[j]], axis=1))
               for j in grp]
        qk = [(_dot_nt(qc[j], kc[j]) * jnp.where(incl, decay[j], 0.0)).astype(BF16) for j in grp]
        ub = [sol[j][:, :HEAD_DIM].astype(BF16) for j in grp]
        w = [sol[j][:, HEAD_DIM:].astype(BF16) for j in grp]
        kd = [(kc[j] * jnp.exp(g_row[j][:, c - 1:c] - g_col[j])).astype(BF16) for j in grp]
        q_eff = [(qc[j] * e_col[j] - dot(qk[j], w[j])).astype(BF16) for j in grp]
        o_base = [dot(qk[j], ub[j]) for j in grp]
        s_mat = [_dot_tn(kd[j], w[j]).astype(BF16) for j in grp]
        s_add = [_dot_tn(kd[j], ub[j]) for j in grp]
        for j, r0 in enumerate(starts):
            m0 = pl.multiple_of((n0 + j) * HEAD_DIM, HEAD_DIM)
            qe_s[hh, pl.ds(r0, c), :] = q_eff[j]
            ob_s[hh, pl.ds(r0, c), :] = o_base[j]
            sm_s[hh, pl.ds(m0, HEAD_DIM), :] = s_mat[j]
            sa_s[hh, pl.ds(m0, HEAD_DIM), :] = s_add[j]
        return 0

    for hh in heads:
        conv_norm(hh)
        lax.fori_loop(0, n_chunks // GDN_GROUP, functools.partial(prepare, hh=hh), 0)

    gn = gn_ref[...]

    def chunk(n, states):
        r0 = pl.multiple_of(n * c, c)
        m0 = pl.multiple_of(n * HEAD_DIM, HEAD_DIM)
        g_last = [gcum_s[hh, pl.ds(n, 1), :][:, c - 1:c] for hh in heads]
        lhs = [jnp.concatenate([qe_s[hh, pl.ds(r0, c), :], sm_s[hh, pl.ds(m0, HEAD_DIM), :]], axis=0)
               for hh in heads]
        prod = [dot(lhs[hh], states[hh].astype(BF16)) for hh in heads]
        for hh in heads:
            ob_s[hh, pl.ds(r0, c), :] = prod[hh][:c] + ob_s[hh, pl.ds(r0, c), :]
        return tuple(jnp.exp(g_last[hh]) * states[hh] - prod[hh][c:] + sa_s[hh, pl.ds(m0, HEAD_DIM), :]
                     for hh in heads)

    lax.fori_loop(0, n_chunks, chunk, tuple(jnp.zeros((HEAD_DIM, HEAD_DIM), F32) for _ in heads))

    for hh in heads:
        for r in range(seq // rows):
            sl = pl.ds(r * rows, rows)
            o = ob_s[hh, sl, :]
            o = o * lax.rsqrt(jnp.mean(o * o, axis=-1, keepdims=True) + RMS_EPS) * gn
            o_ref[sl, lanes[hh]] = (o * _silu(z_ref[sl, lanes[hh]])).astype(o_ref.dtype)


def _gdn(h3, conv_w, a_rows, b_rows, alog, dt, gn):
    b, s, _ = h3.shape
    hp = GDN_HEADS_PER_STEP
    nb = GDN_HEADS // hp
    wide = hp * HEAD_DIM
    n_chunks = s // GDN_CHUNK
    col = lambda off: pl.BlockSpec((None, s, wide), lambda bi, hi: (bi, 0, off + hi))
    cw = lambda off: pl.BlockSpec((GDN_CONV, wide), lambda bi, hi: (0, off + hi))
    rowspec = pl.BlockSpec((None, hp, n_chunks, GDN_CHUNK), lambda bi, hi: (bi, hi, 0, 0))
    headvec = pl.BlockSpec((hp, 1, GDN_CHUNK), lambda bi, hi: (hi, 0, 0))
    return pl.pallas_call(
        _gdn_kernel,
        grid=(b, nb),
        in_specs=[col(0), col(nb), col(2 * nb), col(3 * nb), cw(0), cw(nb), cw(2 * nb),
                  rowspec, rowspec, headvec, headvec,
                  pl.BlockSpec((1, HEAD_DIM), lambda bi, hi: (0, 0))],
        out_specs=pl.BlockSpec((None, s, wide), lambda bi, hi: (bi, 0, hi)),
        out_shape=jax.ShapeDtypeStruct((b, s, GDN_WIDTH), BF16),
        scratch_shapes=[pltpu.VMEM((s + 8, HEAD_DIM), F32), pltpu.VMEM((s, HEAD_DIM), F32),
                        pltpu.VMEM((s, HEAD_DIM), F32), pltpu.VMEM((s, HEAD_DIM), F32),
                        pltpu.VMEM((hp, n_chunks, GDN_CHUNK), F32), pltpu.VMEM((hp, n_chunks, GDN_CHUNK), F32),
                        pltpu.VMEM((hp, s, HEAD_DIM), BF16), pltpu.VMEM((hp, s, HEAD_DIM), F32),
                        pltpu.VMEM((hp, n_chunks * HEAD_DIM, HEAD_DIM), BF16),
                        pltpu.VMEM((hp, n_chunks * HEAD_DIM, HEAD_DIM), F32)],
        compiler_params=_params("parallel", "parallel"),
        name="gdn",
    )(h3, h3, h3, h3, conv_w, conv_w, conv_w, a_rows, b_rows, alog, dt, gn)


def _hgrn_kernel(q_ref, f_ref, i_ref, g_ref, lb_ref, hn_ref, o_ref):
    seq = q_ref.shape[0]
    c = HGRN_CHUNK
    rows = HGRN_ROWS
    ri = _iota((rows, rows), 0)
    ci = _iota((rows, rows), 1)
    causal = (ri // c == ci // c) & (ci <= ri)
    row_in_chunk = _iota((rows, HEAD_DIM), 0) % c
    lb = lb_ref[...]
    hn = hn_ref[...]

    chunks = [slice(j * c, (j + 1) * c) for j in range(rows // c)]
    grp = range(HGRN_GROUP)

    def chunk_scan(x):
        step = 1
        while step < c:
            x = x + jnp.where(row_in_chunk >= step, pltpu.roll(x, step, axis=0), 0.0)
            step *= 2
        return x

    def group(n, state_t):
        starts = [pl.multiple_of((n * HGRN_GROUP + j) * rows, rows) for j in grp]
        qc = [q_ref[pl.ds(r0, rows), :] for r0 in starts]
        ic = [i_ref[pl.ds(r0, rows), :].astype(BF16) for r0 in starts]
        f = [lb + (1.0 - lb) * jax.nn.sigmoid(f_ref[pl.ds(r0, rows), :]) for r0 in starts]
        log_f = [jnp.log(f[j]) for j in grp]
        bcum = [chunk_scan(log_f[j]) for j in grp]
        chunk_dec = [jnp.exp(bcum[j]) for j in grp]
        q_dec = [(qc[j] * chunk_dec[j]).astype(BF16) for j in grp]
        k_inv = [((1.0 - f[j]) * jnp.exp(-bcum[j])).astype(BF16) for j in grp]
        p = [jnp.where(causal, _dot_nt(q_dec[j], k_inv[j]), 0.0) for j in grp]
        o_intra = [_dot(p[j], ic[j]) for j in grp]
        updates = [[_dot_tn(ic[j][sl], k_inv[j][sl]) for sl in chunks] for j in grp]
        for j, r0 in enumerate(starts):
            outs = []
            for sl, upd in zip(chunks, updates[j]):
                outs.append(o_intra[j][sl] + _dot_nt(q_dec[j][sl], state_t))
                state_t = (state_t + upd) * chunk_dec[j][sl.stop - 1:sl.stop]
            o = jnp.concatenate(outs, axis=0)
            o = o * lax.rsqrt(jnp.mean(o * o, axis=-1, keepdims=True) + RMS_EPS) * hn
            o_ref[pl.ds(r0, rows), :] = (o * _silu(g_ref[pl.ds(r0, rows), :])).astype(o_ref.dtype)
        return state_t

    lax.fori_loop(0, seq // (rows * HGRN_GROUP), group, jnp.zeros((HEAD_DIM, HEAD_DIM), F32))


def _hgrn(h3, lb, hn):
    b, s, _ = h3.shape
    nh = HGRN_HEADS
    col = lambda off: pl.BlockSpec((None, s, HEAD_DIM), lambda bi, hi: (bi, 0, off + hi))
    return pl.pallas_call(
        _hgrn_kernel,
        grid=(b, nh),
        in_specs=[col(0), col(nh), col(2 * nh), col(3 * nh),
                  pl.BlockSpec((None, 1, HEAD_DIM), lambda bi, hi: (hi, 0, 0)),
                  pl.BlockSpec((1, HEAD_DIM), lambda bi, hi: (0, 0))],
        out_specs=pl.BlockSpec((None, s, HEAD_DIM), lambda bi, hi: (bi, 0, hi)),
        out_shape=jax.ShapeDtypeStruct((b, s, HGRN_WIDTH), BF16),
        compiler_params=_params("parallel", "parallel"),
        name="hgrn2",
    )(h3, h3, h3, h3, lb, hn)


def _rope(x, cos2, sin2):
    return x * cos2 + pltpu.roll(x, HEAD_DIM // 2, axis=1) * sin2


def _flash_step(q_blk, k_blk, v_blk, mask, carry):
    m, l, acc = carry
    s = jnp.where(mask, _dot_nt(q_blk, k_blk), NEG_INF)
    m_new = jnp.maximum(m, jnp.max(s, axis=-1, keepdims=True))
    alpha = jnp.exp(m - m_new)
    p = jnp.exp(s - m_new)
    l = alpha * l + jnp.sum(p, axis=-1, keepdims=True)
    acc = alpha * acc + _dot(p, v_blk)
    return m_new, l, acc


def _flash_init():
    blk = ATT_BLOCK
    return (jnp.full((blk, 1), NEG_INF, F32), jnp.zeros((blk, 1), F32), jnp.zeros((blk, HEAD_DIM), F32))


def _dilated_kernel(*refs):
    n_g = len(DIL_GROUPS)
    q_refs, k_refs, v_refs = refs[0:n_g], refs[n_g:2 * n_g], refs[2 * n_g:3 * n_g]
    cos_ref, sin_ref, o_ref = refs[3 * n_g:3 * n_g + 3]
    q_s, k_s, v_s, og_s, lse_s = refs[3 * n_g + 3:]
    seq = o_ref.shape[0]
    blk = DIL_BLOCK
    piece = 256
    grp = range(n_g)

    for gi, (window, d) in enumerate(DIL_GROUPS):
        assert window // d == blk
        seg = seq // d
        k_s[gi, pl.ds(0, blk), :] = jnp.zeros((blk, HEAD_DIM), BF16)
        v_s[gi, pl.ds(0, blk), :] = jnp.zeros((blk, HEAD_DIM), BF16)
        for r in range(d):
            for c0 in range(0, seg, piece):
                n = min(piece, seg)
                rows = pl.ds(r + c0 * d, n, stride=d) if d > 1 else pl.ds(c0, n)
                cos2 = cos_ref[gi, pl.ds(r * seg + c0, n), :]
                sin2 = sin_ref[gi, pl.ds(r * seg + c0, n), :]
                q_s[gi, pl.ds(r * seg + c0, n), :] = (_rope(q_refs[gi][rows, :], cos2, sin2)
                                                      * HEAD_DIM ** -0.5).astype(BF16)
                k_s[gi, pl.ds(blk + r * seg + c0, n), :] = _rope(k_refs[gi][rows, :], cos2, sin2).astype(BF16)
                v_s[gi, pl.ds(blk + r * seg + c0, n), :] = v_refs[gi][rows, :].astype(BF16)

    ri = _iota((blk, 2 * blk), 0)
    ci = _iota((blk, 2 * blk), 1)
    rel = ri + blk - ci
    in_window = (rel >= 0) & (rel <= blk)
    dot = functools.partial(jnp.dot, preferred_element_type=F32)

    def q_block(m, _):
        segs = [seq // d for _, d in DIL_GROUPS]
        work = [(g, pl.multiple_of((m * DIL_Q_PER_STEP + u) * blk, blk)) for u in range(DIL_Q_PER_STEP) for g in grp]
        ids = range(len(work))
        has_prev = [jnp.where(j0 % segs[g] != 0, blk, 0) for g, j0 in work]
        mask = [ci + has_prev[i] >= blk for i in ids]
        q = [q_s[g, pl.ds(j0, blk), :] for g, j0 in work]
        kw = [k_s[g, pl.ds(j0, 2 * blk), :] for g, j0 in work]
        vw = [v_s[g, pl.ds(j0, 2 * blk), :] for g, j0 in work]
        s = [jnp.where(in_window, jnp.where(mask[i], _dot_nt(q[i], kw[i]), NEG_INF), NEG_INF) for i in ids]
        top = [jnp.max(s[i], axis=-1, keepdims=True) for i in ids]
        p = [jnp.exp(s[i] - top[i]) for i in ids]
        den = [jnp.sum(p[i], axis=-1, keepdims=True) for i in ids]
        o = [dot(p[i].astype(BF16), vw[i]) / den[i] for i in ids]
        lse = [top[i] + jnp.log(den[i]) for i in ids]
        for i, (g, j0) in enumerate(work):
            seg, d = segs[g], DIL_GROUPS[g][1]
            dst = pl.ds((j0 % seg) * d + j0 // seg, blk, stride=d) if d > 1 else pl.ds(j0, blk)
            og_s[g, dst, :] = o[i]
            lse_s[g, dst, :] = jnp.broadcast_to(lse[i], (blk, HEAD_DIM))
        return 0

    lax.fori_loop(0, seq // (blk * DIL_Q_PER_STEP), q_block, 0)

    for c0 in range(0, seq, piece):
        rows = pl.ds(c0, piece)
        lses = [lse_s[g, rows, :] for g in grp]
        top = functools.reduce(jnp.maximum, lses)
        wts = [jnp.exp(x - top) for x in lses]
        den = functools.reduce(lambda a, b: a + b, wts)
        o = functools.reduce(lambda a, b: a + b, [wts[g] * og_s[g, rows, :] for g in grp]) / den
        o_ref[rows, :] = o.astype(o_ref.dtype)


def _dilated(h3, cos2, sin2):
    b, s, _ = h3.shape
    hpg = DIL_HEADS_PER_GROUP
    n_g = len(DIL_GROUPS)
    col = lambda off: pl.BlockSpec((None, s, HEAD_DIM), lambda bi, hi: (bi, 0, off + hi))
    tab = pl.BlockSpec((n_g, s, HEAD_DIM), lambda bi, hi: (0, 0, 0))
    specs = [col(part * DIL_HEADS + gi * hpg) for part in range(3) for gi in range(n_g)]
    residue_major = lambda t: jnp.stack([t.reshape(s // d, d, HEAD_DIM).transpose(1, 0, 2).reshape(s, HEAD_DIM)
                                         for _, d in DIL_GROUPS])
    cos2, sin2 = residue_major(cos2), residue_major(sin2)
    return pl.pallas_call(
        _dilated_kernel,
        grid=(b, hpg),
        in_specs=specs + [tab, tab],
        out_specs=pl.BlockSpec((None, s, HEAD_DIM), lambda bi, hi: (bi, 0, hi)),
        out_shape=jax.ShapeDtypeStruct((b, s, hpg * HEAD_DIM), BF16),
        scratch_shapes=[pltpu.VMEM((n_g, s, HEAD_DIM), BF16), pltpu.VMEM((n_g, s + DIL_BLOCK, HEAD_DIM), BF16),
                        pltpu.VMEM((n_g, s + DIL_BLOCK, HEAD_DIM), BF16), pltpu.VMEM((n_g, s, HEAD_DIM), F32),
                        pltpu.VMEM((n_g, s, HEAD_DIM), F32)],
        compiler_params=_params("parallel", "parallel"),
        name="dilated_attention",
    )(*([h3] * (3 * n_g)), cos2, sin2)


def _moba_kernel(q_ref, k_ref, v_ref, cos_ref, sin_ref, o_ref, qf_s, q_s, k_s, v_s, km_s, sel_s):
    seq = o_ref.shape[0]
    blk = MOBA_BLOCK
    n_blk = seq // blk
    cos2 = cos_ref[...]
    sin2 = sin_ref[...]
    q = _rope(q_ref[...], cos2, sin2)
    qf_s[...] = q
    q_s[...] = (q * HEAD_DIM ** -0.5).astype(BF16)
    km_s[...] = jnp.zeros(km_s.shape, F32)
    for nb in range(n_blk):
        kb = _rope(k_ref[pl.ds(nb * blk, blk), :], cos2[nb * blk:(nb + 1) * blk], sin2[nb * blk:(nb + 1) * blk])
        k_s[pl.ds(nb * blk, blk), :] = kb.astype(BF16)
        km_s[pl.ds(nb, 1), :] = jnp.mean(kb, axis=0, keepdims=True)
    v_s[...] = v_ref[...].astype(BF16)

    lane = _iota((blk, LANES), 1).astype(F32)
    causal = _iota((blk, blk), 0) >= _iota((blk, blk), 1)
    rows = lambda nb: pl.ds(nb * blk, blk)

    past = range(1, n_blk)
    km = km_s[...]
    gate = {qb: jnp.where(lane < qb, _dot_nt_hi(qf_s[rows(qb), :], km), -jnp.inf) for qb in past}
    sel = {qb: jnp.zeros((blk, LANES), F32) for qb in past}
    for _k in range(MOBA_TOPK):
        best = {qb: jnp.max(gate[qb], axis=-1, keepdims=True) for qb in past}
        first = {qb: jnp.min(jnp.where(gate[qb] == best[qb], lane, LANES), axis=-1, keepdims=True) for qb in past}
        pick = {qb: (lane == first[qb]) & (best[qb] > -jnp.inf) for qb in past}
        sel = {qb: jnp.where(pick[qb], 1.0, sel[qb]) for qb in past}
        gate = {qb: jnp.where(pick[qb], -jnp.inf, gate[qb]) for qb in past}
    for qb in past:
        sel_s[rows(qb), :] = sel[qb]

    assert sorted(qb for group in MOBA_Q_GROUPS for qb in group) == list(range(n_blk))
    for group in MOBA_Q_GROUPS:
        carry = {qb: _flash_init() for qb in group}
        for j in range(max(group) + 1):
            for qb in group:
                if j < qb:
                    mask = sel_s[rows(qb), :][:, j:j + 1] > 0.0
                elif j == qb:
                    mask = causal
                else:
                    continue
                carry[qb] = _flash_step(q_s[rows(qb), :], k_s[rows(j), :], v_s[rows(j), :], mask, carry[qb])
        for qb in group:
            m, l, acc = carry[qb]
            o_ref[rows(qb), :] = (acc / l).astype(o_ref.dtype)


def _moba(h3, cos2, sin2):
    b, s, _ = h3.shape
    base = 3 * DIL_HEADS
    col = lambda off: pl.BlockSpec((None, s, HEAD_DIM), lambda bi, hi: (bi, 0, base + off + hi))
    tab = pl.BlockSpec((s, HEAD_DIM), lambda bi, hi: (0, 0))
    return pl.pallas_call(
        _moba_kernel,
        grid=(b, MOBA_HEADS),
        in_specs=[col(0), col(MOBA_HEADS), col(2 * MOBA_HEADS), tab, tab],
        out_specs=pl.BlockSpec((None, s, HEAD_DIM), lambda bi, hi: (bi, 0, hi)),
        out_shape=jax.ShapeDtypeStruct((b, s, MOBA_HEADS * HEAD_DIM), BF16),
        scratch_shapes=[pltpu.VMEM((s, HEAD_DIM), F32), pltpu.VMEM((s, HEAD_DIM), BF16),
                        pltpu.VMEM((s, HEAD_DIM), BF16), pltpu.VMEM((s, HEAD_DIM), BF16),
                        pltpu.VMEM((LANES, HEAD_DIM), F32), pltpu.VMEM((s, LANES), F32)],
        compiler_params=_params("parallel", "parallel"),
        name="moba_attention",
    )(h3, h3, h3, cos2, sin2)


def _layer_norm(x, gain, bias):
    mu = jnp.mean(x, axis=-1, keepdims=True)
    xc = x - mu
    var = jnp.mean(xc * xc, axis=-1, keepdims=True)
    return xc * lax.rsqrt(var + LN_EPS) * gain + bias


def _out_ln_kernel(*refs, n_parts):
    o_refs = refs[0:n_parts]
    w_refs = refs[n_parts:2 * n_parts]
    h_ref, gain_ref, bias_ref, rw_ref, y_ref, logit_ref = refs[2 * n_parts:]
    half = h_ref.shape[0] // 2
    halves = [pl.ds(k * half, half) for k in range(2)]
    mix = []
    for rows in halves:
        parts = [jnp.dot(o_r[rows, :], w_r[...], preferred_element_type=F32) for o_r, w_r in zip(o_refs, w_refs)]
        mix.append(functools.reduce(lambda a, b: a + b, parts))
    y = [_layer_norm(DEEPNORM_ALPHA * h_ref[rows, :] + mix[k], gain_ref[...], bias_ref[...])
         for k, rows in enumerate(halves)]
    logits = [_dot3(y[k], rw_ref[...]) for k in range(2)]
    for k, rows in enumerate(halves):
        y_ref[rows, :] = y[k]
        logit_ref[rows, :] = logits[k]


def _out_ln(parts, weights, h, gain, bias, router_w, tm):
    t, d = h.shape
    n_parts = len(parts)
    row = lambda width: pl.BlockSpec((tm, width), lambda i: (i, 0))
    full = lambda a: pl.BlockSpec(a.shape, lambda i: (0, 0))
    return pl.pallas_call(
        functools.partial(_out_ln_kernel, n_parts=n_parts),
        grid=(t // tm,),
        in_specs=[row(p.shape[1]) for p in parts] + [full(w) for w in weights]
                 + [row(d), full(gain), full(bias), full(router_w)],
        out_specs=[row(d), row(LANES)],
        out_shape=[jax.ShapeDtypeStruct((t, d), F32), jax.ShapeDtypeStruct((t, LANES), F32)],
        compiler_params=_params("parallel"),
        name="out_proj_ln",
    )(*parts, *weights, h, gain, bias, router_w)


def _route(logits_t, rbias_col):
    row = _iota(logits_t.shape, 0)
    scores = jax.nn.sigmoid(logits_t)
    biased = scores + rbias_col

    def first_argmax(vals):
        best = jnp.max(vals, axis=0, keepdims=True)
        return best, jnp.min(jnp.where(vals == best, row, N_EXPERTS), axis=0, keepdims=True)

    best_score = None
    best_group = None
    for g in range(N_EXPERT_GROUPS):
        vals = jnp.where(row // EXPERTS_PER_GROUP == g, biased, -jnp.inf)
        top1, idx1 = first_argmax(vals)
        top2, _ = first_argmax(jnp.where(row == idx1, -jnp.inf, vals))
        score = top1 + top2
        if g == 0:
            best_score, best_group = score, jnp.zeros_like(idx1)
        else:
            better = score > best_score
            best_group = jnp.where(better, g, best_group)
            best_score = jnp.where(better, score, best_score)
    masked = jnp.where(row // EXPERTS_PER_GROUP == best_group, biased, NEG_INF)
    _, i1 = first_argmax(masked)
    _, i2 = first_argmax(jnp.where(row == i1, -jnp.inf, masked))
    s1 = jnp.sum(jnp.where(row == i1, scores, 0.0), axis=0, keepdims=True)
    s2 = jnp.sum(jnp.where(row == i2, scores, 0.0), axis=0, keepdims=True)
    tot = s1 + s2
    return i1, i2, s1 / tot, s2 / tot


def _route_kernel(logit_ref, rbias_ref, loc_ref, w_ref, tab_ref, cnt_ref, carry_s, total_s):
    sweep = pl.program_id(0)
    i = pl.program_id(1)
    tm = logit_ref.shape[0]
    block = float(MOE_BLOCK_ROWS)
    align = float(MOE_ALIGN)

    @pl.when((i == 0) & (sweep == 1))
    def _():
        total_s[...] = carry_s[...]

    @pl.when(i == 0)
    def _():
        carry_s[...] = jnp.zeros(carry_s.shape, F32)

    logits_t = logit_ref[...].T[:N_EXPERTS]
    i1, i2, w1, w2 = _route(logits_t, rbias_ref[...][:N_EXPERTS])
    row = _iota((N_EXPERTS, tm), 0)
    chosen = jnp.where((row == i1) | (row == i2), 1.0, 0.0)
    count = jnp.sum(chosen, axis=1, keepdims=True)
    padded = jnp.floor((count + (align - 1.0)) * (1.0 / align)) * align

    e_r = _iota((N_EXPERTS, N_EXPERTS), 0)
    e_c = _iota((N_EXPERTS, N_EXPERTS), 1)

    def expert_prefix(col, inclusive):
        tri = ((e_c <= e_r) if inclusive else (e_c < e_r)).astype(BF16)
        wide = jnp.broadcast_to(col, (N_EXPERTS, LANES)).astype(BF16)
        return jnp.dot(tri, wide, preferred_element_type=F32)[:, 0:1]

    @pl.when(sweep == 0)
    def _():
        loc_ref[...] = jnp.zeros(loc_ref.shape, loc_ref.dtype)
        w_ref[...] = jnp.zeros(w_ref.shape, w_ref.dtype)
        tab_ref[...] = jnp.zeros(tab_ref.shape, tab_ref.dtype)
        cnt_ref[...] = jnp.zeros(cnt_ref.shape, cnt_ref.dtype)

    @pl.when(sweep == 1)
    def _():
        total = total_s[...]
        blocks = jnp.floor((total + (block - 1.0)) * (1.0 / block))
        region = (expert_prefix(blocks, True) - blocks) * block + carry_s[...]
        run_start = expert_prefix(padded, False)
        earlier = (_iota((tm, tm), 0) < _iota((tm, tm), 1)).astype(BF16)
        slot = run_start + jnp.dot(chosen.astype(BF16), earlier, preferred_element_type=F32)
        l1 = jnp.sum(jnp.where(row == i1, slot, 0.0), axis=0, keepdims=True)
        l2 = jnp.sum(jnp.where(row == i2, slot, 0.0), axis=0, keepdims=True)
        out_row = _iota((8, tm), 0)
        loc_ref[...] = jnp.where(out_row == 0, l1, jnp.where(out_row == 1, l2, 0.0)).astype(jnp.int32)
        w_ref[...] = jnp.where(out_row == 0, w1, jnp.where(out_row == 1, w2, 0.0))
        lane = _iota((N_EXPERTS, LANES), 1)
        tab = jnp.where(lane == 0, padded, jnp.where(lane == 1, run_start, jnp.where(lane == 2, region, 0.0)))
        tab_ref[...] = tab.astype(jnp.int32)
        cnt_ref[...] = jnp.broadcast_to(total, cnt_ref.shape).astype(jnp.int32)

    carry_s[...] = carry_s[...] + padded


def _route_tokens(logits, rbias_col):
    t = logits.shape[0]
    tm = MOE_TILE
    n_tiles = t // tm
    tok = pl.BlockSpec((None, 8, tm), lambda s, i: (s, 0, i))
    loc, w, tab, totals = pl.pallas_call(
        _route_kernel,
        grid=(2, n_tiles),
        in_specs=[pl.BlockSpec((tm, LANES), lambda s, i: (i, 0)), pl.BlockSpec((LANES, 1), lambda s, i: (0, 0))],
        out_specs=[tok, tok, pl.BlockSpec((None, None, N_EXPERTS, LANES), lambda s, i: (s, i, 0, 0)),
                   pl.BlockSpec((N_EXPERTS, LANES), lambda s, i: (0, 0))],
        out_shape=[jax.ShapeDtypeStruct((2, 8, t), jnp.int32), jax.ShapeDtypeStruct((2, 8, t), F32),
                   jax.ShapeDtypeStruct((2, n_tiles, N_EXPERTS, LANES), jnp.int32),
                   jax.ShapeDtypeStruct((N_EXPERTS, LANES), jnp.int32)],
        scratch_shapes=[pltpu.VMEM((N_EXPERTS, 1), F32), pltpu.VMEM((N_EXPERTS, 1), F32)],
        compiler_params=_params("arbitrary", "arbitrary"),
        name="moe_route",
    )(logits, rbias_col)
    return loc[1], w[1], tab[1], totals[:, 0]


def _run_pieces(length, sizes):
    return [(size, length & ~(2 * size - 1), (length & size) != 0) for size in sizes]


def _tile_run_copies(cnt_ref, off_ref, region_ref, tile, compact, sorted_rows, sem, to_sorted):
    copies = []
    for e in range(N_EXPERTS):
        k = tile * N_EXPERTS + e
        length = pl.multiple_of(cnt_ref[k], MOE_ALIGN)
        small = compact.at[pl.ds(pl.multiple_of(off_ref[k], MOE_ALIGN), length)]
        big = sorted_rows.at[pl.ds(pl.multiple_of(region_ref[k], MOE_ALIGN), length)]
        copies.append((length > 0, pltpu.make_async_copy(small, big, sem) if to_sorted
                       else pltpu.make_async_copy(big, small, sem)))
    return copies


def _dispatch_kernel(cnt_ref, off_ref, region_ref, pad_start_ref, pad_len_ref, n_valid_ref,
                     x_ref, loc_ref, w_ref, xs_ref, buf_s, zero_s, sem, zero_sem):
    i = pl.program_id(0)
    d = x_ref.shape[1]
    cap = buf_s.shape[1]
    tm = x_ref.shape[0]
    n_blocks = xs_ref.shape[0] // MOE_BLOCK_ROWS

    @pl.when(i == 0)
    def _():
        zero_s[...] = jnp.zeros(zero_s.shape, zero_s.dtype)
        sizes = [MOE_BLOCK_ROWS >> s for s in range((MOE_BLOCK_ROWS // MOE_ALIGN).bit_length())]
        copies = []
        for e in range(N_EXPERTS):
            for size, offset, used in _run_pieces(pad_len_ref[e], sizes):
                dst = xs_ref.at[pl.ds(pl.multiple_of(pad_start_ref[e] + offset, MOE_ALIGN), size)]
                copies.append((used, pltpu.make_async_copy(zero_s.at[pl.ds(0, size)], dst, zero_sem)))
        for j in range(n_blocks - N_EXPERTS, n_blocks):
            dst = xs_ref.at[pl.ds(j * MOE_BLOCK_ROWS, MOE_BLOCK_ROWS)]
            copies.append((j >= n_valid_ref[0], pltpu.make_async_copy(zero_s, dst, zero_sem)))
        for action in ("start", "wait"):
            for used, copy in copies:
                @pl.when(used)
                def _():
                    getattr(copy, action)()

    loc = loc_ref[...]
    w = w_ref[...]
    row = _iota((cap, tm), 0)
    hit1 = row == loc[0:1, :]
    hit2 = row == loc[1:2, :]
    perm = jnp.where(hit1, 1.0, jnp.where(hit2, 1.0, 0.0)).astype(BF16)
    gate = jnp.sum(jnp.where(hit1, w[0:1, :], jnp.where(hit2, w[1:2, :], 0.0)), axis=1, keepdims=True)
    g_hi = gate.astype(BF16)
    g_lo = (gate - g_hi.astype(F32)).astype(BF16)
    lane = _iota((cap, LANES), 1)
    slot = i % 2
    buf_s[slot, :, pl.ds(0, d)] = jnp.dot(perm, x_ref[...].astype(BF16), preferred_element_type=F32).astype(BF16)
    buf_s[slot, :, pl.ds(d, LANES)] = jnp.where(lane == 0, g_hi.astype(F32),
                                                jnp.where(lane == 1, g_lo.astype(F32), 0.0)).astype(BF16)

    def run_copies(tile, which, action):
        for used, copy in _tile_run_copies(cnt_ref, off_ref, region_ref, tile, buf_s.at[which], xs_ref,
                                           sem.at[which], True):
            @pl.when(used)
            def _():
                getattr(copy, action)()

    run_copies(i, slot, "start")

    @pl.when(i > 0)
    def _():
        run_copies(i - 1, 1 - slot, "wait")

    @pl.when(i == pl.num_programs(0) - 1)
    def _():
        run_copies(i, slot, "wait")


def _dispatch(tables, pad_start, pad_len, n_valid, x, loc, w, n_rows):
    t, d = x.shape
    tm = MOE_TILE
    cap = MOE_TOPK * tm + N_EXPERTS * MOE_ALIGN
    tok = pl.BlockSpec((8, tm), lambda i, *_: (0, i))
    grid_spec = pltpu.PrefetchScalarGridSpec(
        num_scalar_prefetch=6,
        grid=(t // tm,),
        in_specs=[pl.BlockSpec((tm, d), lambda i, *_: (i, 0)), tok, tok],
        out_specs=pl.BlockSpec(memory_space=pl.ANY),
        scratch_shapes=[pltpu.VMEM((2, cap, d + LANES), BF16), pltpu.VMEM((MOE_BLOCK_ROWS, d + LANES), BF16),
                        pltpu.SemaphoreType.DMA((2,)), pltpu.SemaphoreType.DMA(())],
    )
    return pl.pallas_call(
        _dispatch_kernel,
        grid_spec=grid_spec,
        out_shape=jax.ShapeDtypeStruct((n_rows, d + LANES), BF16),
        compiler_params=_params("arbitrary"),
        name="moe_dispatch",
    )(*tables, pad_start, pad_len, n_valid, x, loc, w)


def _expert_kernel(blk_expert_ref, next_expert_ref, n_valid_ref, x_ref, wg_ref, wu_ref, wd_ref, y_ref,
                   wg_s, wu_s, wd_s, wg_buf, wu_buf, wd_buf, slot_s, sem, *, layer):
    i = pl.program_id(0)

    def fetch(expert, slot):
        return [pltpu.make_async_copy(src.at[layer, expert], dst.at[slot], sem.at[slot])
                for src, dst in ((wg_ref, wg_buf), (wu_ref, wu_buf), (wd_ref, wd_buf))]

    @pl.when(i == 0)
    def _():
        slot_s[0] = 1
        for copy in fetch(blk_expert_ref[0], 0):
            copy.start()

    @pl.when((i == 0) | (blk_expert_ref[i] != blk_expert_ref[jnp.maximum(i - 1, 0)]))
    def _():
        slot = 1 - slot_s[0]
        slot_s[0] = slot
        for copy in fetch(blk_expert_ref[i], slot):
            copy.wait()
        wg_s[...] = wg_buf[slot].astype(BF16)
        wu_s[...] = wu_buf[slot].astype(BF16)
        wd_s[...] = wd_buf[slot].astype(BF16)

        @pl.when(next_expert_ref[i] >= 0)
        def _():
            for copy in fetch(next_expert_ref[i], 1 - slot):
                copy.start()

    @pl.when(i < n_valid_ref[0])
    def _():
        d = wg_s.shape[0]
        x = x_ref[:, pl.ds(0, d)]
        extra = x_ref[:, pl.ds(d, LANES)].astype(F32)
        gate = extra[:, 0:1] + extra[:, 1:2]
        hid = _silu(jnp.dot(x, wg_s[...], preferred_element_type=F32)) * jnp.dot(
            x, wu_s[...], preferred_element_type=F32)
        y_ref[...] = _dot(hid * gate, wd_s[...]).astype(y_ref.dtype)

    @pl.when(i >= n_valid_ref[0])
    def _():
        y_ref[...] = jnp.zeros(y_ref.shape, y_ref.dtype)


def _experts(blk_expert, next_expert, n_valid, xs, wg, wu, wd, layer, n_blocks):
    d = wg.shape[-2]
    f = wg.shape[-1]
    rows = MOE_BLOCK_ROWS
    n_rows = n_blocks * rows
    hbm = pl.BlockSpec(memory_space=pl.ANY)
    grid_spec = pltpu.PrefetchScalarGridSpec(
        num_scalar_prefetch=3,
        grid=(n_blocks,),
        in_specs=[pl.BlockSpec((rows, xs.shape[1]), lambda i, be, ne, nv: (i, 0)), hbm, hbm, hbm],
        out_specs=pl.BlockSpec((rows, d), lambda i, be, ne, nv: (i, 0)),
        scratch_shapes=[pltpu.VMEM((d, f), BF16), pltpu.VMEM((d, f), BF16), pltpu.VMEM((f, d), BF16),
                        pltpu.VMEM((2, d, f), F32), pltpu.VMEM((2, d, f), F32), pltpu.VMEM((2, f, d), F32),
                        pltpu.SMEM((1,), jnp.int32), pltpu.SemaphoreType.DMA((2,))],
    )
    return pl.pallas_call(
        functools.partial(_expert_kernel, layer=layer),
        grid_spec=grid_spec,
        out_shape=jax.ShapeDtypeStruct((n_rows, d), BF16),
        compiler_params=_params("arbitrary"),
        name="moe_experts",
    )(blk_expert, next_expert, n_valid, xs, wg, wu, wd)


def _combine_kernel(cnt_ref, off_ref, region_ref, ys_ref, loc_ref, h_ref, gain_ref, bias_ref, y_ref, yb_ref,
                    buf, sem):
    i = pl.program_id(0)
    n = pl.num_programs(0)
    tm = h_ref.shape[0]
    cap = buf.shape[1]

    def fetch(tile, slot, action):
        for used, copy in _tile_run_copies(cnt_ref, off_ref, region_ref, tile, buf.at[slot], ys_ref,
                                           sem.at[slot], False):
            @pl.when(used)
            def _():
                getattr(copy, action)()

    slot = i % 2

    @pl.when(i == 0)
    def _():
        buf[...] = jnp.zeros(buf.shape, buf.dtype)
        fetch(0, 0, "start")

    @pl.when(i + 1 < n)
    def _():
        fetch(i + 1, 1 - slot, "start")

    fetch(i, slot, "wait")
    loc = loc_ref[...].astype(F32)
    eye = _iota((tm, tm), 0) == _iota((tm, tm), 1)
    loc_col = [jnp.sum(jnp.where(eye, jnp.broadcast_to(loc[k:k + 1, :], (tm, tm)), 0.0), axis=1, keepdims=True)
               for k in range(MOE_TOPK)]
    lane = _iota((tm, cap), 1).astype(F32)
    pick = jnp.where(lane == loc_col[0], 1.0, jnp.where(lane == loc_col[1], 1.0, 0.0)).astype(BF16)
    ffn = jnp.dot(pick, buf[slot], preferred_element_type=F32)
    y = _layer_norm(DEEPNORM_ALPHA * h_ref[...] + ffn, gain_ref[...], bias_ref[...])
    y_ref[...] = y
    yb_ref[...] = y.astype(BF16)


def _combine_ln(tables, ys, loc, h, gain, bias):
    t, d = h.shape
    tm = MOE_TILE
    cap = MOE_TOPK * tm + N_EXPERTS * MOE_ALIGN
    grid_spec = pltpu.PrefetchScalarGridSpec(
        num_scalar_prefetch=3,
        grid=(t // tm,),
        in_specs=[pl.BlockSpec(memory_space=pl.ANY),
                  pl.BlockSpec((8, tm), lambda i, *_: (0, i)),
                  pl.BlockSpec((tm, d), lambda i, *_: (i, 0)),
                  pl.BlockSpec((1, d), lambda i, *_: (0, 0)),
                  pl.BlockSpec((1, d), lambda i, *_: (0, 0))],
        out_specs=[pl.BlockSpec((tm, d), lambda i, *_: (i, 0)), pl.BlockSpec((tm, d), lambda i, *_: (i, 0))],
        scratch_shapes=[pltpu.VMEM((2, cap, d), BF16), pltpu.SemaphoreType.DMA((2,))],
    )
    return pl.pallas_call(
        _combine_kernel,
        grid_spec=grid_spec,
        out_shape=[jax.ShapeDtypeStruct((t, d), F32), jax.ShapeDtypeStruct((t, d), BF16)],
        compiler_params=_params("arbitrary"),
        name="moe_combine_ln",
    )(*tables, ys, loc, h, gain, bias)


def _moe_ln(h, logits, rbias, wg, wu, wd, layer, gain, bias):
    t, d = h.shape
    rows = MOE_BLOCK_ROWS
    n_tiles = t // MOE_TILE
    n_blocks = -(-(MOE_TOPK * t + n_tiles * N_EXPERTS * (MOE_ALIGN - 1)) // rows) + N_EXPERTS
    loc, w_t, tab, counts = _route_tokens(logits, rbias)
    tables = tuple(tab[:, :, k].reshape(-1) for k in range(3))
    blocks_per_expert = (counts + rows - 1) // rows
    block_end = jnp.cumsum(blocks_per_expert)
    row_start = (block_end - blocks_per_expert) * rows
    n_valid = block_end[-1:].astype(jnp.int32)
    blk = jnp.arange(n_blocks, dtype=jnp.int32)
    blk_expert = jnp.sum(jnp.minimum(blk, n_valid - 1)[:, None] >= block_end[None, :], axis=1).astype(jnp.int32)
    pad_start = (row_start + counts).astype(jnp.int32)
    pad_len = (blocks_per_expert * rows - counts).astype(jnp.int32)
    xs = _dispatch(tables, pad_start, pad_len, n_valid, h, loc, w_t, n_blocks * rows)
    experts = jnp.arange(N_EXPERTS, dtype=jnp.int32)
    later_used = (experts[None, :] > experts[:, None]) & (blocks_per_expert[None, :] > 0)
    next_used = jnp.min(jnp.where(later_used, experts[None, :], N_EXPERTS), axis=1)
    next_expert = jnp.where(next_used < N_EXPERTS, next_used, -1).astype(jnp.int32)[blk_expert]
    ys = _experts(blk_expert, next_expert, n_valid, xs, wg, wu, wd, layer, n_blocks)
    return _combine_ln(tables, ys, loc, h, gain, bias)


def _rope_tables(seq):
    inv_freq = ROPE_THETA ** (-jnp.arange(0, HEAD_DIM, 2, dtype=F32) / HEAD_DIM)
    ang = jnp.arange(seq, dtype=F32)[:, None] * inv_freq[None, :]
    cos, sin = jnp.cos(ang), jnp.sin(ang)
    return jnp.concatenate([cos, cos], axis=-1), jnp.concatenate([-sin, sin], axis=-1)


def _even_mixer(hb, b, s, w_in, conv_w, a_log, dt_bias, gdn_norm, hgrn_norm, lower_bound):
    gw = GDN_WIDTH
    n_small = 2 * GDN_HEADS
    tail0 = 4 * gw
    w_t = w_in.T
    w_small = jnp.pad(w_t[tail0:tail0 + n_small], ((0, LANES - n_small), (0, 0))).astype(BF16)
    h_a = _matmul(hb, w_t[:tail0].astype(BF16), *EVEN_PROJ_TILE, transposed=True).reshape(b, s, tail0)
    h_b = _matmul(hb, w_t[tail0 + n_small:].astype(BF16), *EVEN_PROJ_TILE, transposed=True)
    h_b = h_b.reshape(b, s, 4 * HGRN_WIDTH)
    n_chunks = s // GDN_CHUNK
    small = _matmul(hb, w_small, EVEN_PROJ_TILE[0], LANES, transposed=True).reshape(b, s, LANES)
    to_rows = lambda a: a.transpose(0, 2, 1).reshape(b, GDN_HEADS, n_chunks, GDN_CHUNK)
    b_rows, a_rows = to_rows(small[..., :GDN_HEADS]), to_rows(small[..., GDN_HEADS:n_small])
    headvec = lambda v: jnp.broadcast_to(v.astype(F32)[:, None, None], (GDN_HEADS, 1, GDN_CHUNK))
    o_a = _gdn(h_a, conv_w.astype(F32), a_rows, b_rows, headvec(a_log), headvec(dt_bias),
               gdn_norm.astype(F32).reshape(1, HEAD_DIM))
    o_b = _hgrn(h_b, lower_bound.astype(F32).reshape(HGRN_HEADS, 1, HEAD_DIM),
                hgrn_norm.astype(F32).reshape(1, HEAD_DIM))
    return [o_a.reshape(b * s, GDN_WIDTH), o_b.reshape(b * s, HGRN_WIDTH)]


def _odd_mixer(hb, b, s, w_in, cos2, sin2):
    h = _matmul(hb, w_in.astype(BF16), *ODD_PROJ_TILE)
    h3 = h.reshape(b, s, ODD_COLS)
    o_c = _dilated(h3, cos2, sin2)
    o_d = _moba(h3, cos2, sin2)
    return [o_c.reshape(b * s, -1), o_d.reshape(b * s, -1)]


def kernel(x, ev_w_in, ev_conv_w, ev_a_log, ev_dt_bias, ev_gdn_norm, ev_hgrn_norm, hgrn_lb_logits, ev_w_out,
           od_w_in, od_w_out, router_w, router_bias, moe_w_gate, moe_w_up, moe_w_down, ln_gain, ln_bias):
    b, s, d = x.shape
    t = b * s
    cos2, sin2 = _rope_tables(s)
    lower_bounds = jnp.cumsum(jax.nn.softmax(hgrn_lb_logits.astype(F32), axis=0), axis=0)
    rw = jnp.pad(router_w.astype(F32), ((0, 0), (0, LANES - N_EXPERTS)))
    rbias = jnp.pad(router_bias.astype(F32), (0, LANES - N_EXPERTS)).reshape(LANES, 1)
    vec = lambda v: v.astype(F32).reshape(1, d)

    h = x.reshape(t, d)
    hb = h
    for layer in range(DEPTH):
        if layer % 2 == 0:
            e = layer // 2
            parts = _even_mixer(hb, b, s, ev_w_in[e], ev_conv_w[e], ev_a_log[e], ev_dt_bias[e], ev_gdn_norm[e],
                                ev_hgrn_norm[e], lower_bounds[layer])
            w_out = ev_w_out[e].astype(BF16)
        else:
            o = layer // 2
            parts = _odd_mixer(hb, b, s, od_w_in[o], cos2, sin2)
            w_out = od_w_out[o].astype(BF16)
        splits = np.cumsum([p.shape[1] for p in parts])[:-1]
        weights = jnp.split(w_out, splits, axis=0)
        h, logits = _out_ln(parts, weights, h, vec(ln_gain[layer, 0]), vec(ln_bias[layer, 0]), rw, OUT_PROJ_ROWS)
        h, hb = _moe_ln(h, logits, rbias, moe_w_gate, moe_w_up, moe_w_down, layer,
                        vec(ln_gain[layer, 1]), vec(ln_bias[layer, 1]))
    return h.reshape(b, s, d)
```

```python
import functools

import jax
import jax.numpy as jnp
import numpy as np
from jax import lax
from jax.experimental import pallas as pl
from jax.experimental.pallas import tpu as pltpu

F32 = jnp.float32
BF16 = jnp.bfloat16

DEPTH = 2
HEAD_DIM = 128
GDN_HEADS = 8
GDN_CONV = 4
GDN_CHUNK = 64
GDN_WIDTH = GDN_HEADS * HEAD_DIM
HGRN_HEADS = 8
HGRN_CHUNK = 16
HGRN_WIDTH = HGRN_HEADS * HEAD_DIM
DIL_GROUPS = ((128, 1), (512, 4), (2048, 16))
DIL_HEADS_PER_GROUP = 4
DIL_HEADS = len(DIL_GROUPS) * DIL_HEADS_PER_GROUP
MOBA_HEADS = 4
MOBA_BLOCK = 256
MOBA_TOPK = 3
ROPE_THETA = 10000.0
N_EXPERTS = 16
N_EXPERT_GROUPS = 4
EXPERTS_PER_GROUP = N_EXPERTS // N_EXPERT_GROUPS
MOE_TOPK = 2
MOE_BLOCK_ROWS = 512
MOE_TILE = 512
MOE_ALIGN = 16
DEEPNORM_ALPHA = (2.0 * DEPTH) ** 0.25
LN_EPS = 1e-5
RMS_EPS = 1e-6
NEG_INF = -1e30

LANES = 128
VMEM_LIMIT = 56 * 1024 * 1024
EVEN_PROJ_TILE = (1024, 2048)
ODD_PROJ_TILE = (2048, 1024)
OUT_PROJ_ROWS = 256
ATT_BLOCK = 256
MOBA_Q_GROUPS = ((7, 0, 6, 1), (5, 2, 4, 3))
DIL_Q_PER_STEP = 4
DIL_BLOCK = 128
GDN_HEADS_PER_STEP = 2
GDN_GROUP = 32
HGRN_GROUP = 4
HGRN_ROWS = 256

ODD_COLS = 3 * DIL_HEADS * HEAD_DIM + 3 * MOBA_HEADS * HEAD_DIM


def _dot(a, b):
    return jnp.dot(a.astype(BF16), b.astype(BF16), preferred_element_type=F32)


def _dot_nt(a, b):
    return lax.dot_general(a.astype(BF16), b.astype(BF16), (((1,), (1,)), ((), ())),
                           preferred_element_type=F32)


def _dot_tn(a, b):
    return lax.dot_general(a.astype(BF16), b.astype(BF16), (((0,), (0,)), ((), ())),
                           preferred_element_type=F32)


def _dot_hi(a, b):
    return jnp.dot(a, b, preferred_element_type=F32, precision=lax.Precision.HIGHEST)


def _dot_nt_hi(a, b):
    return lax.dot_general(a, b, (((1,), (1,)), ((), ())), preferred_element_type=F32,
                           precision=lax.Precision.HIGHEST)


def _dot3(a, b):
    a_hi = a.astype(BF16)
    b_hi = b.astype(BF16)
    a_lo = (a - a_hi.astype(F32)).astype(BF16)
    b_lo = (b - b_hi.astype(F32)).astype(BF16)
    dot = functools.partial(jnp.dot, preferred_element_type=F32)
    return dot(a_hi, b_hi) + (dot(a_hi, b_lo) + dot(a_lo, b_hi))


_dot_inv = _dot


def _silu(x):
    return x * jax.nn.sigmoid(x)


def _iota(shape, dim):
    return lax.broadcasted_iota(jnp.int32, shape, dim)


def _params(*sem):
    return pltpu.CompilerParams(dimension_semantics=sem, vmem_limit_bytes=VMEM_LIMIT)


def _mm_kernel(x_ref, w_ref, *refs, transposed):
    x = x_ref[...].astype(BF16)
    prod = _dot_nt(x, w_ref[...]) if transposed else jnp.dot(x, w_ref[...], preferred_element_type=F32)
    refs[-1 if len(refs) == 1 else -2][...] = prod.astype(F32)
    if len(refs) == 3:
        side_w_ref, _, side_ref = refs

        @pl.when(pl.program_id(1) == 0)
        def _():
            side_ref[...] = _dot_nt(x, side_w_ref[...])


def _matmul(x, w, tm, tn, transposed=False, side_w=None):
    m, k = x.shape
    n = w.shape[0] if transposed else w.shape[1]
    assert m % tm == 0 and n % tn == 0
    w_spec = pl.BlockSpec((tn, k), lambda i, j: (j, 0)) if transposed else pl.BlockSpec((k, tn), lambda i, j: (0, j))
    in_specs = [pl.BlockSpec((tm, k), lambda i, j: (i, 0)), w_spec]
    out_specs = [pl.BlockSpec((tm, tn), lambda i, j: (i, j))]
    out_shape = [jax.ShapeDtypeStruct((m, n), F32)]
    operands = [x, w]
    if side_w is not None:
        in_specs.append(pl.BlockSpec((LANES, k), lambda i, j: (0, 0)))
        out_specs.append(pl.BlockSpec((tm, LANES), lambda i, j: (i, 0)))
        out_shape.append(jax.ShapeDtypeStruct((m, LANES), F32))
        operands.append(side_w)
    outs = pl.pallas_call(
        functools.partial(_mm_kernel, transposed=transposed),
        grid=(m // tm, n // tn),
        in_specs=in_specs,
        out_specs=out_specs,
        out_shape=out_shape,
        compiler_params=_params("parallel", "arbitrary" if side_w is not None else "parallel"),
        name="in_proj",
    )(*operands)
    return outs if side_w is not None else outs[0]


def _gdn_kernel(q_ref, k_ref, v_ref, z_ref, cwq_ref, cwk_ref, cwv_ref, a_ref, b_ref, alog_ref, dt_ref,
                gn_ref, o_ref, pad_s, q_s, k_s, v_s, gcum_s, beta_s, qe_s, ob_s, sm_s, sa_s):
    seq = q_ref.shape[0]
    c = GDN_CHUNK
    n_chunks = seq // c
    rows = 256
    heads = range(GDN_HEADS_PER_STEP)
    lanes = [slice(hh * HEAD_DIM, (hh + 1) * HEAD_DIM) for hh in heads]

    def conv_norm(hh):
        pad_s[pl.ds(0, 8), :] = jnp.zeros((8, HEAD_DIM), F32)
        for x_ref, cw_ref, dst, mode in ((q_ref, cwq_ref, q_s, "q"), (k_ref, cwk_ref, k_s, "k"),
                                         (v_ref, cwv_ref, v_s, "v")):
            pad_s[pl.ds(8, seq), :] = x_ref[:, lanes[hh]]
            cw = cw_ref[:, lanes[hh]]
            for r in range(seq // rows):
                acc = None
                for j in range(GDN_CONV):
                    tap = pad_s[pl.ds(8 + r * rows - (GDN_CONV - 1) + j, rows), :] * cw[j:j + 1, :]
                    acc = tap if acc is None else acc + tap
                y = _silu(acc)
                if mode != "v":
                    y = y * lax.rsqrt(jnp.sum(y * y, axis=-1, keepdims=True) + RMS_EPS)
                if mode == "q":
                    y = y * HEAD_DIM ** -0.5
                dst[pl.ds(r * rows, rows), :] = y

    upper = (_iota((c, c), 0) <= _iota((c, c), 1)).astype(F32)
    for hh in heads:
        g = -jnp.exp(alog_ref[hh]) * jax.nn.softplus(a_ref[hh] + dt_ref[hh])
        gcum_s[hh] = _dot_hi(g, upper)
        beta_s[hh] = jax.nn.sigmoid(b_ref[hh])

    ri = _iota((c, c), 0)
    ci = _iota((c, c), 1)
    eye = ri == ci
    strict = ri > ci
    incl = ri >= ci
    eye_f = eye.astype(F32)
    level1 = ri // 2 == ci // 2
    levels = []
    s = 2
    while s < c:
        levels.append((ri // (2 * s) == ci // (2 * s)) & ((ri // s) % 2 == 1) & ((ci // s) % 2 == 0))
        s *= 2

    dot = functools.partial(jnp.dot, preferred_element_type=F32)

    def to_col(row):
        return jnp.sum(jnp.where(eye, jnp.broadcast_to(row, (c, c)), 0.0), axis=1, keepdims=True)

    def prepare(i, _, hh):
        n0 = i * GDN_GROUP
        grp = range(GDN_GROUP)
        starts = [pl.multiple_of((n0 + j) * c, c) for j in grp]
        qc = [q_s[pl.ds(r0, c), :] for r0 in starts]
        kc = [k_s[pl.ds(r0, c), :] for r0 in starts]
        vc = [v_s[pl.ds(r0, c), :] for r0 in starts]
        g_row = [gcum_s[hh, pl.ds(n0 + j, 1), :] for j in grp]
        g_col = [to_col(g_row[j]) for j in grp]
        b_col = [to_col(beta_s[hh, pl.ds(n0 + j, 1), :]) for j in grp]
        decay = [jnp.exp(jnp.where(incl, g_col[j] - g_row[j], 0.0)) for j in grp]
        n_mat = [b_col[j] * jnp.where(strict, decay[j], 0.0) * _dot_nt(kc[j], kc[j]) for j in grp]
        inv = [eye_f - jnp.where(level1, n_mat[j], 0.0) for j in grp]
        for blk in levels:
            tmp = [_dot_inv(inv[j], jnp.where(blk, n_mat[j], 0.0)) for j in grp]
            inv = [inv[j] - _dot_inv(tmp[j], inv[j]) for j in grp]
        e_col = [jnp.exp(g_col[j]) for j in grp]
        sol = [_dot_inv(inv[j], jnp.concatenate([b_col[j] * vc[j], (b_col[j] * e_col[j]) * kc[j]], axis=1))
               for j in grp]
        qk = [(_dot_nt(qc[j], kc[j]) * jnp.where(incl, decay[j], 0.0)).astype(BF16) for j in grp]
        ub = [sol[j][:, :HEAD_DIM].astype(BF16) for j in grp]
        w = [sol[j][:, HEAD_DIM:].astype(BF16) for j in grp]
        kd = [(kc[j] * jnp.exp(g_row[j][:, c - 1:c] - g_col[j])).astype(BF16) for j in grp]
        q_eff = [(qc[j] * e_col[j] - dot(qk[j], w[j])).astype(BF16) for j in grp]
        o_base = [dot(qk[j], ub[j]) for j in grp]
        s_mat = [_dot_tn(kd[j], w[j]).astype(BF16) for j in grp]
        s_add = [_dot_tn(kd[j], ub[j]) for j in grp]
        for j, r0 in enumerate(starts):
            m0 = pl.multiple_of((n0 + j) * HEAD_DIM, HEAD_DIM)
            qe_s[hh, pl.ds(r0, c), :] = q_eff[j]
            ob_s[hh, pl.ds(r0, c), :] = o_base[j]
            sm_s[hh, pl.ds(m0, HEAD_DIM), :] = s_mat[j]
            sa_s[hh, pl.ds(m0, HEAD_DIM), :] = s_add[j]
        return 0

    for hh in heads:
        conv_norm(hh)
        lax.fori_loop(0, n_chunks // GDN_GROUP, functools.partial(prepare, hh=hh), 0)

    gn = gn_ref[...]

    def chunk(n, states):
        r0 = pl.multiple_of(n * c, c)
        m0 = pl.multiple_of(n * HEAD_DIM, HEAD_DIM)
        g_last = [gcum_s[hh, pl.ds(n, 1), :][:, c - 1:c] for hh in heads]
        lhs = [jnp.concatenate([qe_s[hh, pl.ds(r0, c), :], sm_s[hh, pl.ds(m0, HEAD_DIM), :]], axis=0)
               for hh in heads]
        prod = [dot(lhs[hh], states[hh].astype(BF16)) for hh in heads]
        for hh in heads:
            ob_s[hh, pl.ds(r0, c), :] = prod[hh][:c] + ob_s[hh, pl.ds(r0, c), :]
        return tuple(jnp.exp(g_last[hh]) * states[hh] - prod[hh][c:] + sa_s[hh, pl.ds(m0, HEAD_DIM), :]
                     for hh in heads)

    lax.fori_loop(0, n_chunks, chunk, tuple(jnp.zeros((HEAD_DIM, HEAD_DIM), F32) for _ in heads))

    for hh in heads:
        for r in range(seq // rows):
            sl = pl.ds(r * rows, rows)
            o = ob_s[hh, sl, :]
            o = o * lax.rsqrt(jnp.mean(o * o, axis=-1, keepdims=True) + RMS_EPS) * gn
            o_ref[sl, lanes[hh]] = (o * _silu(z_ref[sl, lanes[hh]])).astype(o_ref.dtype)


def _gdn(h3, conv_w, a_rows, b_rows, alog, dt, gn):
    b, s, _ = h3.shape
    hp = GDN_HEADS_PER_STEP
    nb = GDN_HEADS // hp
    wide = hp * HEAD_DIM
    n_chunks = s // GDN_CHUNK
    col = lambda off: pl.BlockSpec((None, s, wide), lambda bi, hi: (bi, 0, off + hi))
    cw = lambda off: pl.BlockSpec((GDN_CONV, wide), lambda bi, hi: (0, off + hi))
    rowspec = pl.BlockSpec((None, hp, n_chunks, GDN_CHUNK), lambda bi, hi: (bi, hi, 0, 0))
    headvec = pl.BlockSpec((hp, 1, GDN_CHUNK), lambda bi, hi: (hi, 0, 0))
    return pl.pallas_call(
        _gdn_kernel,
        grid=(b, nb),
        in_specs=[col(0), col(nb), col(2 * nb), col(3 * nb), cw(0), cw(nb), cw(2 * nb),
                  rowspec, rowspec, headvec, headvec,
                  pl.BlockSpec((1, HEAD_DIM), lambda bi, hi: (0, 0))],
        out_specs=pl.BlockSpec((None, s, wide), lambda bi, hi: (bi, 0, hi)),
        out_shape=jax.ShapeDtypeStruct((b, s, GDN_WIDTH), BF16),
        scratch_shapes=[pltpu.VMEM((s + 8, HEAD_DIM), F32), pltpu.VMEM((s, HEAD_DIM), F32),
                        pltpu.VMEM((s, HEAD_DIM), F32), pltpu.VMEM((s, HEAD_DIM), F32),
                        pltpu.VMEM((hp, n_chunks, GDN_CHUNK), F32), pltpu.VMEM((hp, n_chunks, GDN_CHUNK), F32),
                        pltpu.VMEM((hp, s, HEAD_DIM), BF16), pltpu.VMEM((hp, s, HEAD_DIM), F32),
                        pltpu.VMEM((hp, n_chunks * HEAD_DIM, HEAD_DIM), BF16),
                        pltpu.VMEM((hp, n_chunks * HEAD_DIM, HEAD_DIM), F32)],
        compiler_params=_params("parallel", "parallel"),
        name="gdn",
    )(h3, h3, h3, h3, conv_w, conv_w, conv_w, a_rows, b_rows, alog, dt, gn)


def _hgrn_kernel(q_ref, f_ref, i_ref, g_ref, lb_ref, hn_ref, o_ref):
    seq = q_ref.shape[0]
    c = HGRN_CHUNK
    rows = HGRN_ROWS
    ri = _iota((rows, rows), 0)
    ci = _iota((rows, rows), 1)
    causal = (ri // c == ci // c) & (ci <= ri)
    row_in_chunk = _iota((rows, HEAD_DIM), 0) % c
    lb = lb_ref[...]
    hn = hn_ref[...]

    chunks = [slice(j * c, (j + 1) * c) for j in range(rows // c)]
    grp = range(HGRN_GROUP)

    def chunk_scan(x):
        step = 1
        while step < c:
            x = x + jnp.where(row_in_chunk >= step, pltpu.roll(x, step, axis=0), 0.0)
            step *= 2
        return x

    def group(n, state_t):
        starts = [pl.multiple_of((n * HGRN_GROUP + j) * rows, rows) for j in grp]
        qc = [q_ref[pl.ds(r0, rows), :] for r0 in starts]
        ic = [i_ref[pl.ds(r0, rows), :].astype(BF16) for r0 in starts]
        f = [lb + (1.0 - lb) * jax.nn.sigmoid(f_ref[pl.ds(r0, rows), :]) for r0 in starts]
        log_f = [jnp.log(f[j]) for j in grp]
        bcum = [chunk_scan(log_f[j]) for j in grp]
        chunk_dec = [jnp.exp(bcum[j]) for j in grp]
        q_dec = [(qc[j] * chunk_dec[j]).astype(BF16) for j in grp]
        k_inv = [((1.0 - f[j]) * jnp.exp(-bcum[j])).astype(BF16) for j in grp]
        p = [jnp.where(causal, _dot_nt(q_dec[j], k_inv[j]), 0.0) for j in grp]
        o_intra = [_dot(p[j], ic[j]) for j in grp]
        updates = [[_dot_tn(ic[j][sl], k_inv[j][sl]) for sl in chunks] for j in grp]
        for j, r0 in enumerate(starts):
            outs = []
            for sl, upd in zip(chunks, updates[j]):
                outs.append(o_intra[j][sl] + _dot_nt(q_dec[j][sl], state_t))
                state_t = (state_t + upd) * chunk_dec[j][sl.stop - 1:sl.stop]
            o = jnp.concatenate(outs, axis=0)
            o = o * lax.rsqrt(jnp.mean(o * o, axis=-1, keepdims=True) + RMS_EPS) * hn
            o_ref[pl.ds(r0, rows), :] = (o * _silu(g_ref[pl.ds(r0, rows), :])).astype(o_ref.dtype)
        return state_t

    lax.fori_loop(0, seq // (rows * HGRN_GROUP), group, jnp.zeros((HEAD_DIM, HEAD_DIM), F32))


def _hgrn(h3, lb, hn):
    b, s, _ = h3.shape
    nh = HGRN_HEADS
    col = lambda off: pl.BlockSpec((None, s, HEAD_DIM), lambda bi, hi: (bi, 0, off + hi))
    return pl.pallas_call(
        _hgrn_kernel,
        grid=(b, nh),
        in_specs=[col(0), col(nh), col(2 * nh), col(3 * nh),
                  pl.BlockSpec((None, 1, HEAD_DIM), lambda bi, hi: (hi, 0, 0)),
                  pl.BlockSpec((1, HEAD_DIM), lambda bi, hi: (0, 0))],
        out_specs=pl.BlockSpec((None, s, HEAD_DIM), lambda bi, hi: (bi, 0, hi)),
        out_shape=jax.ShapeDtypeStruct((b, s, HGRN_WIDTH), BF16),
        compiler_params=_params("parallel", "parallel"),
        name="hgrn2",
    )(h3, h3, h3, h3, lb, hn)


def _rope(x, cos2, sin2):
    return x * cos2 + pltpu.roll(x, HEAD_DIM // 2, axis=1) * sin2


def _flash_step(q_blk, k_blk, v_blk, mask, carry):
    m, l, acc = carry
    s = jnp.where(mask, _dot_nt(q_blk, k_blk), NEG_INF)
    m_new = jnp.maximum(m, jnp.max(s, axis=-1, keepdims=True))
    alpha = jnp.exp(m - m_new)
    p = jnp.exp(s - m_new)
    l = alpha * l + jnp.sum(p, axis=-1, keepdims=True)
    acc = alpha * acc + _dot(p, v_blk)
    return m_new, l, acc


def _flash_init():
    blk = ATT_BLOCK
    return (jnp.full((blk, 1), NEG_INF, F32), jnp.zeros((blk, 1), F32), jnp.zeros((blk, HEAD_DIM), F32))


def _dilated_kernel(*refs):
    n_g = len(DIL_GROUPS)
    q_refs, k_refs, v_refs = refs[0:n_g], refs[n_g:2 * n_g], refs[2 * n_g:3 * n_g]
    cos_ref, sin_ref, o_ref = refs[3 * n_g:3 * n_g + 3]
    q_s, k_s, v_s, og_s, lse_s = refs[3 * n_g + 3:]
    seq = o_ref.shape[0]
    blk = DIL_BLOCK
    piece = 256
    grp = range(n_g)

    for gi, (window, d) in enumerate(DIL_GROUPS):
        assert window // d == blk
        seg = seq // d
        k_s[gi, pl.ds(0, blk), :] = jnp.zeros((blk, HEAD_DIM), BF16)
        v_s[gi, pl.ds(0, blk), :] = jnp.zeros((blk, HEAD_DIM), BF16)
        for r in range(d):
            for c0 in range(0, seg, piece):
                n = min(piece, seg)
                rows = pl.ds(r + c0 * d, n, stride=d) if d > 1 else pl.ds(c0, n)
                cos2 = cos_ref[gi, pl.ds(r * seg + c0, n), :]
                sin2 = sin_ref[gi, pl.ds(r * seg + c0, n), :]
                q_s[gi, pl.ds(r * seg + c0, n), :] = (_rope(q_refs[gi][rows, :], cos2, sin2)
                                                      * HEAD_DIM ** -0.5).astype(BF16)
                k_s[gi, pl.ds(blk + r * seg + c0, n), :] = _rope(k_refs[gi][rows, :], cos2, sin2).astype(BF16)
                v_s[gi, pl.ds(blk + r * seg + c0, n), :] = v_refs[gi][rows, :].astype(BF16)

    ri = _iota((blk, 2 * blk), 0)
    ci = _iota((blk, 2 * blk), 1)
    rel = ri + blk - ci
    in_window = (rel >= 0) & (rel <= blk)
    dot = functools.partial(jnp.dot, preferred_element_type=F32)

    def q_block(m, _):
        segs = [seq // d for _, d in DIL_GROUPS]
        work = [(g, pl.multiple_of((m * DIL_Q_PER_STEP + u) * blk, blk)) for u in range(DIL_Q_PER_STEP) for g in grp]
        ids = range(len(work))
        has_prev = [jnp.where(j0 % segs[g] != 0, blk, 0) for g, j0 in work]
        mask = [ci + has_prev[i] >= blk for i in ids]
        q = [q_s[g, pl.ds(j0, blk), :] for g, j0 in work]
        kw = [k_s[g, pl.ds(j0, 2 * blk), :] for g, j0 in work]
        vw = [v_s[g, pl.ds(j0, 2 * blk), :] for g, j0 in work]
        s = [jnp.where(in_window, jnp.where(mask[i], _dot_nt(q[i], kw[i]), NEG_INF), NEG_INF) for i in ids]
        top = [jnp.max(s[i], axis=-1, keepdims=True) for i in ids]
        p = [jnp.exp(s[i] - top[i]) for i in ids]
        den = [jnp.sum(p[i], axis=-1, keepdims=True) for i in ids]
        o = [dot(p[i].astype(BF16), vw[i]) / den[i] for i in ids]
        lse = [top[i] + jnp.log(den[i]) for i in ids]
        for i, (g, j0) in enumerate(work):
            seg, d = segs[g], DIL_GROUPS[g][1]
            dst = pl.ds((j0 % seg) * d + j0 // seg, blk, stride=d) if d > 1 else pl.ds(j0, blk)
            og_s[g, dst, :] = o[i]
            lse_s[g, dst, :] = jnp.broadcast_to(lse[i], (blk, HEAD_DIM))
        return 0

    lax.fori_loop(0, seq // (blk * DIL_Q_PER_STEP), q_block, 0)

    for c0 in range(0, seq, piece):
        rows = pl.ds(c0, piece)
        lses = [lse_s[g, rows, :] for g in grp]
        top = functools.reduce(jnp.maximum, lses)
        wts = [jnp.exp(x - top) for x in lses]
        den = functools.reduce(lambda a, b: a + b, wts)
        o = functools.reduce(lambda a, b: a + b, [wts[g] * og_s[g, rows, :] for g in grp]) / den
        o_ref[rows, :] = o.astype(o_ref.dtype)


def _dilated(h3, cos2, sin2):
    b, s, _ = h3.shape
    hpg = DIL_HEADS_PER_GROUP
    n_g = len(DIL_GROUPS)
    col = lambda off: pl.BlockSpec((None, s, HEAD_DIM), lambda bi, hi: (bi, 0, off + hi))
    tab = pl.BlockSpec((n_g, s, HEAD_DIM), lambda bi, hi: (0, 0, 0))
    specs = [col(part * DIL_HEADS + gi * hpg) for part in range(3) for gi in range(n_g)]
    residue_major = lambda t: jnp.stack([t.reshape(s // d, d, HEAD_DIM).transpose(1, 0, 2).reshape(s, HEAD_DIM)
                                         for _, d in DIL_GROUPS])
    cos2, sin2 = residue_major(cos2), residue_major(sin2)
    return pl.pallas_call(
        _dilated_kernel,
        grid=(b, hpg),
        in_specs=specs + [tab, tab],
        out_specs=pl.BlockSpec((None, s, HEAD_DIM), lambda bi, hi: (bi, 0, hi)),
        out_shape=jax.ShapeDtypeStruct((b, s, hpg * HEAD_DIM), BF16),
        scratch_shapes=[pltpu.VMEM((n_g, s, HEAD_DIM), BF16), pltpu.VMEM((n_g, s + DIL_BLOCK, HEAD_DIM), BF16),
                        pltpu.VMEM((n_g, s + DIL_BLOCK, HEAD_DIM), BF16), pltpu.VMEM((n_g, s, HEAD_DIM), F32),
                        pltpu.VMEM((n_g, s, HEAD_DIM), F32)],
        compiler_params=_params("parallel", "parallel"),
        name="dilated_attention",
    )(*([h3] * (3 * n_g)), cos2, sin2)


def _moba_kernel(q_ref, k_ref, v_ref, cos_ref, sin_ref, o_ref, qf_s, q_s, k_s, v_s, km_s, sel_s):
    seq = o_ref.shape[0]
    blk = MOBA_BLOCK
    n_blk = seq // blk
    cos2 = cos_ref[...]
    sin2 = sin_ref[...]
    q = _rope(q_ref[...], cos2, sin2)
    qf_s[...] = q
    q_s[...] = (q * HEAD_DIM ** -0.5).astype(BF16)
    km_s[...] = jnp.zeros(km_s.shape, F32)
    for nb in range(n_blk):
        kb = _rope(k_ref[pl.ds(nb * blk, blk), :], cos2[nb * blk:(nb + 1) * blk], sin2[nb * blk:(nb + 1) * blk])
        k_s[pl.ds(nb * blk, blk), :] = kb.astype(BF16)
        km_s[pl.ds(nb, 1), :] = jnp.mean(kb, axis=0, keepdims=True)
    v_s[...] = v_ref[...].astype(BF16)

    lane = _iota((blk, LANES), 1).astype(F32)
    causal = _iota((blk, blk), 0) >= _iota((blk, blk), 1)
    rows = lambda nb: pl.ds(nb * blk, blk)

    past = range(1, n_blk)
    km = km_s[...]
    gate = {qb: jnp.where(lane < qb, _dot_nt_hi(qf_s[rows(qb), :], km), -jnp.inf) for qb in past}
    sel = {qb: jnp.zeros((blk, LANES), F32) for qb in past}
    for _k in range(MOBA_TOPK):
        best = {qb: jnp.max(gate[qb], axis=-1, keepdims=True) for qb in past}
        first = {qb: jnp.min(jnp.where(gate[qb] == best[qb], lane, LANES), axis=-1, keepdims=True) for qb in past}
        pick = {qb: (lane == first[qb]) & (best[qb] > -jnp.inf) for qb in past}
        sel = {qb: jnp.where(pick[qb], 1.0, sel[qb]) for qb in past}
        gate = {qb: jnp.where(pick[qb], -jnp.inf, gate[qb]) for qb in past}
    for qb in past:
        sel_s[rows(qb), :] = sel[qb]

    assert sorted(qb for group in MOBA_Q_GROUPS for qb in group) == list(range(n_blk))
    for group in MOBA_Q_GROUPS:
        carry = {qb: _flash_init() for qb in group}
        for j in range(max(group) + 1):
            for qb in group:
                if j < qb:
                    mask = sel_s[rows(qb), :][:, j:j + 1] > 0.0
                elif j == qb:
                    mask = causal
                else:
                    continue
                carry[qb] = _flash_step(q_s[rows(qb), :], k_s[rows(j), :], v_s[rows(j), :], mask, carry[qb])
        for qb in group:
            m, l, acc = carry[qb]
            o_ref[rows(qb), :] = (acc / l).astype(o_ref.dtype)


def _moba(h3, cos2, sin2):
    b, s, _ = h3.shape
    base = 3 * DIL_HEADS
    col = lambda off: pl.BlockSpec((None, s, HEAD_DIM), lambda bi, hi: (bi, 0, base + off + hi))
    tab = pl.BlockSpec((s, HEAD_DIM), lambda bi, hi: (0, 0))
    return pl.pallas_call(
        _moba_kernel,
        grid=(b, MOBA_HEADS),
        in_specs=[col(0), col(MOBA_HEADS), col(2 * MOBA_HEADS), tab, tab],
        out_specs=pl.BlockSpec((None, s, HEAD_DIM), lambda bi, hi: (bi, 0, hi)),
        out_shape=jax.ShapeDtypeStruct((b, s, MOBA_HEADS * HEAD_DIM), BF16),
        scratch_shapes=[pltpu.VMEM((s, HEAD_DIM), F32), pltpu.VMEM((s, HEAD_DIM), BF16),
                        pltpu.VMEM((s, HEAD_DIM), BF16), pltpu.VMEM((s, HEAD_DIM), BF16),
                        pltpu.VMEM((LANES, HEAD_DIM), F32), pltpu.VMEM((s, LANES), F32)],
        compiler_params=_params("parallel", "parallel"),
        name="moba_attention",
    )(h3, h3, h3, cos2, sin2)


def _layer_norm(x, gain, bias):
    mu = jnp.mean(x, axis=-1, keepdims=True)
    xc = x - mu
    var = jnp.mean(xc * xc, axis=-1, keepdims=True)
    return xc * lax.rsqrt(var + LN_EPS) * gain + bias


def _out_ln_kernel(*refs, n_parts):
    o_refs = refs[0:n_parts]
    w_refs = refs[n_parts:2 * n_parts]
    h_ref, gain_ref, bias_ref, rw_ref, y_ref, logit_ref = refs[2 * n_parts:]
    half = h_ref.shape[0] // 2
    halves = [pl.ds(k * half, half) for k in range(2)]
    mix = []
    for rows in halves:
        parts = [jnp.dot(o_r[rows, :], w_r[...], preferred_element_type=F32) for o_r, w_r in zip(o_refs, w_refs)]
        mix.append(functools.reduce(lambda a, b: a + b, parts))
    y = [_layer_norm(DEEPNORM_ALPHA * h_ref[rows, :] + mix[k], gain_ref[...], bias_ref[...])
         for k, rows in enumerate(halves)]
    logits = [_dot3(y[k], rw_ref[...]) for k in range(2)]
    for k, rows in enumerate(halves):
        y_ref[rows, :] = y[k]
        logit_ref[rows, :] = logits[k]


def _out_ln(parts, weights, h, gain, bias, router_w, tm):
    t, d = h.shape
    n_parts = len(parts)
    row = lambda width: pl.BlockSpec((tm, width), lambda i: (i, 0))
    full = lambda a: pl.BlockSpec(a.shape, lambda i: (0, 0))
    return pl.pallas_call(
        functools.partial(_out_ln_kernel, n_parts=n_parts),
        grid=(t // tm,),
        in_specs=[row(p.shape[1]) for p in parts] + [full(w) for w in weights]
                 + [row(d), full(gain), full(bias), full(router_w)],
        out_specs=[row(d), row(LANES)],
        out_shape=[jax.ShapeDtypeStruct((t, d), F32), jax.ShapeDtypeStruct((t, LANES), F32)],
        compiler_params=_params("parallel"),
        name="out_proj_ln",
    )(*parts, *weights, h, gain, bias, router_w)


def _route(logits_t, rbias_col):
    row = _iota(logits_t.shape, 0)
    scores = jax.nn.sigmoid(logits_t)
    biased = scores + rbias_col

    def first_argmax(vals):
        best = jnp.max(vals, axis=0, keepdims=True)
        return best, jnp.min(jnp.where(vals == best, row, N_EXPERTS), axis=0, keepdims=True)

    best_score = None
    best_group = None
    for g in range(N_EXPERT_GROUPS):
        vals = jnp.where(row // EXPERTS_PER_GROUP == g, biased, -jnp.inf)
        top1, idx1 = first_argmax(vals)
        top2, _ = first_argmax(jnp.where(row == idx1, -jnp.inf, vals))
        score = top1 + top2
        if g == 0:
            best_score, best_group = score, jnp.zeros_like(idx1)
        else:
            better = score > best_score
            best_group = jnp.where(better, g, best_group)
            best_score = jnp.where(better, score, best_score)
    masked = jnp.where(row // EXPERTS_PER_GROUP == best_group, biased, NEG_INF)
    _, i1 = first_argmax(masked)
    _, i2 = first_argmax(jnp.where(row == i1, -jnp.inf, masked))
    s1 = jnp.sum(jnp.where(row == i1, scores, 0.0), axis=0, keepdims=True)
    s2 = jnp.sum(jnp.where(row == i2, scores, 0.0), axis=0, keepdims=True)
    tot = s1 + s2
    return i1, i2, s1 / tot, s2 / tot


def _route_kernel(logit_ref, rbias_ref, loc_ref, w_ref, tab_ref, cnt_ref, carry_s, total_s):
    sweep = pl.program_id(0)
    i = pl.program_id(1)
    tm = logit_ref.shape[0]
    block = float(MOE_BLOCK_ROWS)
    align = float(MOE_ALIGN)

    @pl.when((i == 0) & (sweep == 1))
    def _():
        total_s[...] = carry_s[...]

    @pl.when(i == 0)
    def _():
        carry_s[...] = jnp.zeros(carry_s.shape, F32)

    logits_t = logit_ref[...].T[:N_EXPERTS]
    i1, i2, w1, w2 = _route(logits_t, rbias_ref[...][:N_EXPERTS])
    row = _iota((N_EXPERTS, tm), 0)
    chosen = jnp.where((row == i1) | (row == i2), 1.0, 0.0)
    count = jnp.sum(chosen, axis=1, keepdims=True)
    padded = jnp.floor((count + (align - 1.0)) * (1.0 / align)) * align

    e_r = _iota((N_EXPERTS, N_EXPERTS), 0)
    e_c = _iota((N_EXPERTS, N_EXPERTS), 1)

    def expert_prefix(col, inclusive):
        tri = ((e_c <= e_r) if inclusive else (e_c < e_r)).astype(BF16)
        wide = jnp.broadcast_to(col, (N_EXPERTS, LANES)).astype(BF16)
        return jnp.dot(tri, wide, preferred_element_type=F32)[:, 0:1]

    @pl.when(sweep == 0)
    def _():
        loc_ref[...] = jnp.zeros(loc_ref.shape, loc_ref.dtype)
        w_ref[...] = jnp.zeros(w_ref.shape, w_ref.dtype)
        tab_ref[...] = jnp.zeros(tab_ref.shape, tab_ref.dtype)
        cnt_ref[...] = jnp.zeros(cnt_ref.shape, cnt_ref.dtype)

    @pl.when(sweep == 1)
    def _():
        total = total_s[...]
        blocks = jnp.floor((total + (block - 1.0)) * (1.0 / block))
        region = (expert_prefix(blocks, True) - blocks) * block + carry_s[...]
        run_start = expert_prefix(padded, False)
        earlier = (_iota((tm, tm), 0) < _iota((tm, tm), 1)).astype(BF16)
        slot = run_start + jnp.dot(chosen.astype(BF16), earlier, preferred_element_type=F32)
        l1 = jnp.sum(jnp.where(row == i1, slot, 0.0), axis=0, keepdims=True)
        l2 = jnp.sum(jnp.where(row == i2, slot, 0.0), axis=0, keepdims=True)
        out_row = _iota((8, tm), 0)
        loc_ref[...] = jnp.where(out_row == 0, l1, jnp.where(out_row == 1, l2, 0.0)).astype(jnp.int32)
        w_ref[...] = jnp.where(out_row == 0, w1, jnp.where(out_row == 1, w2, 0.0))
        lane = _iota((N_EXPERTS, LANES), 1)
        tab = jnp.where(lane == 0, padded, jnp.where(lane == 1, run_start, jnp.where(lane == 2, region, 0.0)))
        tab_ref[...] = tab.astype(jnp.int32)
        cnt_ref[...] = jnp.broadcast_to(total, cnt_ref.shape).astype(jnp.int32)

    carry_s[...] = carry_s[...] + padded


def _route_tokens(logits, rbias_col):
    t = logits.shape[0]
    tm = MOE_TILE
    n_tiles = t // tm
    tok = pl.BlockSpec((None, 8, tm), lambda s, i: (s, 0, i))
    loc, w, tab, totals = pl.pallas_call(
        _route_kernel,
        grid=(2, n_tiles),
        in_specs=[pl.BlockSpec((tm, LANES), lambda s, i: (i, 0)), pl.BlockSpec((LANES, 1), lambda s, i: (0, 0))],
        out_specs=[tok, tok, pl.BlockSpec((None, None, N_EXPERTS, LANES), lambda s, i: (s, i, 0, 0)),
                   pl.BlockSpec((N_EXPERTS, LANES), lambda s, i: (0, 0))],
        out_shape=[jax.ShapeDtypeStruct((2, 8, t), jnp.int32), jax.ShapeDtypeStruct((2, 8, t), F32),
                   jax.ShapeDtypeStruct((2, n_tiles, N_EXPERTS, LANES), jnp.int32),
                   jax.ShapeDtypeStruct((N_EXPERTS, LANES), jnp.int32)],
        scratch_shapes=[pltpu.VMEM((N_EXPERTS, 1), F32), pltpu.VMEM((N_EXPERTS, 1), F32)],
        compiler_params=_params("arbitrary", "arbitrary"),
        name="moe_route",
    )(logits, rbias_col)
    return loc[1], w[1], tab[1], totals[:, 0]


def _run_pieces(length, sizes):
    return [(size, length & ~(2 * size - 1), (length & size) != 0) for size in sizes]


def _tile_run_copies(cnt_ref, off_ref, region_ref, tile, compact, sorted_rows, sem, to_sorted):
    copies = []
    for e in range(N_EXPERTS):
        k = tile * N_EXPERTS + e
        length = pl.multiple_of(cnt_ref[k], MOE_ALIGN)
        small = compact.at[pl.ds(pl.multiple_of(off_ref[k], MOE_ALIGN), length)]
        big = sorted_rows.at[pl.ds(pl.multiple_of(region_ref[k], MOE_ALIGN), length)]
        copies.append((length > 0, pltpu.make_async_copy(small, big, sem) if to_sorted
                       else pltpu.make_async_copy(big, small, sem)))
    return copies


def _dispatch_kernel(cnt_ref, off_ref, region_ref, pad_start_ref, pad_len_ref, n_valid_ref,
                     x_ref, loc_ref, w_ref, xs_ref, buf_s, zero_s, sem, zero_sem):
    i = pl.program_id(0)
    d = x_ref.shape[1]
    cap = buf_s.shape[1]
    tm = x_ref.shape[0]
    n_blocks = xs_ref.shape[0] // MOE_BLOCK_ROWS

    @pl.when(i == 0)
    def _():
        zero_s[...] = jnp.zeros(zero_s.shape, zero_s.dtype)
        sizes = [MOE_BLOCK_ROWS >> s for s in range((MOE_BLOCK_ROWS // MOE_ALIGN).bit_length())]
        copies = []
        for e in range(N_EXPERTS):
            for size, offset, used in _run_pieces(pad_len_ref[e], sizes):
                dst = xs_ref.at[pl.ds(pl.multiple_of(pad_start_ref[e] + offset, MOE_ALIGN), size)]
                copies.append((used, pltpu.make_async_copy(zero_s.at[pl.ds(0, size)], dst, zero_sem)))
        for j in range(n_blocks - N_EXPERTS, n_blocks):
            dst = xs_ref.at[pl.ds(j * MOE_BLOCK_ROWS, MOE_BLOCK_ROWS)]
            copies.append((j >= n_valid_ref[0], pltpu.make_async_copy(zero_s, dst, zero_sem)))
        for action in ("start", "wait"):
            for used, copy in copies:
                @pl.when(used)
                def _():
                    getattr(copy, action)()

    loc = loc_ref[...]
    w = w_ref[...]
    slot = i % 2
    x = x_ref[...].astype(BF16)
    half = cap // 2
    parts = [pl.ds(k * half, half) for k in range(2)]
    row = [_iota((half, tm), 0) + k * half for k in range(2)]
    hit1 = [row[k] == loc[0:1, :] for k in range(2)]
    hit2 = [row[k] == loc[1:2, :] for k in range(2)]
    perm = [jnp.where(hit1[k], 1.0, jnp.where(hit2[k], 1.0, 0.0)).astype(BF16) for k in range(2)]
    sorted_x = [jnp.dot(perm[k], x, preferred_element_type=F32).astype(BF16) for k in range(2)]
    gate = [jnp.sum(jnp.where(hit1[k], w[0:1, :], jnp.where(hit2[k], w[1:2, :], 0.0)), axis=1, keepdims=True)
            for k in range(2)]
    lane = _iota((half, LANES), 1)
    for k in range(2):
        g_hi = gate[k].astype(BF16)
        g_lo = (gate[k] - g_hi.astype(F32)).astype(BF16)
        buf_s[slot, parts[k], pl.ds(0, d)] = sorted_x[k]
        buf_s[slot, parts[k], pl.ds(d, LANES)] = jnp.where(lane == 0, g_hi.astype(F32),
                                                          jnp.where(lane == 1, g_lo.astype(F32), 0.0)).astype(BF16)

    def run_copies(tile, which, action):
        for used, copy in _tile_run_copies(cnt_ref, off_ref, region_ref, tile, buf_s.at[which], xs_ref,
                                           sem.at[which], True):
            @pl.when(used)
            def _():
                getattr(copy, action)()

    run_copies(i, slot, "start")

    @pl.when(i > 0)
    def _():
        run_copies(i - 1, 1 - slot, "wait")

    @pl.when(i == pl.num_programs(0) - 1)
    def _():
        run_copies(i, slot, "wait")


def _dispatch(tables, pad_start, pad_len, n_valid, x, loc, w, n_rows):
    t, d = x.shape
    tm = MOE_TILE
    cap = MOE_TOPK * tm + N_EXPERTS * MOE_ALIGN
    tok = pl.BlockSpec((8, tm), lambda i, *_: (0, i))
    grid_spec = pltpu.PrefetchScalarGridSpec(
        num_scalar_prefetch=6,
        grid=(t // tm,),
        in_specs=[pl.BlockSpec((tm, d), lambda i, *_: (i, 0)), tok, tok],
        out_specs=pl.BlockSpec(memory_space=pl.ANY),
        scratch_shapes=[pltpu.VMEM((2, cap, d + LANES), BF16), pltpu.VMEM((MOE_BLOCK_ROWS, d + LANES), BF16),
                        pltpu.SemaphoreType.DMA((2,)), pltpu.SemaphoreType.DMA(())],
    )
    return pl.pallas_call(
        _dispatch_kernel,
        grid_spec=grid_spec,
        out_shape=jax.ShapeDtypeStruct((n_rows, d + LANES), BF16),
        compiler_params=_params("arbitrary"),
        name="moe_dispatch",
    )(*tables, pad_start, pad_len, n_valid, x, loc, w)


def _expert_kernel(blk_expert_ref, next_expert_ref, n_valid_ref, x_ref, wg_ref, wu_ref, wd_ref, y_ref,
                   wg_s, wu_s, wd_s, wg_buf, wu_buf, wd_buf, slot_s, sem, *, layer):
    i = pl.program_id(0)

    def fetch(expert, slot):
        return [pltpu.make_async_copy(src.at[layer, expert], dst.at[slot], sem.at[slot])
                for src, dst in ((wg_ref, wg_buf), (wu_ref, wu_buf), (wd_ref, wd_buf))]

    @pl.when(i == 0)
    def _():
        slot_s[0] = 1
        for copy in fetch(blk_expert_ref[0], 0):
            copy.start()

    @pl.when((i == 0) | (blk_expert_ref[i] != blk_expert_ref[jnp.maximum(i - 1, 0)]))
    def _():
        slot = 1 - slot_s[0]
        slot_s[0] = slot
        for copy in fetch(blk_expert_ref[i], slot):
            copy.wait()
        wg_s[...] = wg_buf[slot].astype(BF16)
        wu_s[...] = wu_buf[slot].astype(BF16)
        wd_s[...] = wd_buf[slot].astype(BF16)

        @pl.when(next_expert_ref[i] >= 0)
        def _():
            for copy in fetch(next_expert_ref[i], 1 - slot):
                copy.start()

    @pl.when(i < n_valid_ref[0])
    def _():
        d = wg_s.shape[0]
        x = x_ref[:, pl.ds(0, d)]
        extra = x_ref[:, pl.ds(d, LANES)].astype(F32)
        gate = extra[:, 0:1] + extra[:, 1:2]
        hid = _silu(jnp.dot(x, wg_s[...], preferred_element_type=F32)) * jnp.dot(
            x, wu_s[...], preferred_element_type=F32)
        y_ref[...] = _dot(hid * gate, wd_s[...]).astype(y_ref.dtype)

    @pl.when(i >= n_valid_ref[0])
    def _():
        y_ref[...] = jnp.zeros(y_ref.shape, y_ref.dtype)


def _experts(blk_expert, next_expert, n_valid, xs, wg, wu, wd, layer, n_blocks):
    d = wg.shape[-2]
    f = wg.shape[-1]
    rows = MOE_BLOCK_ROWS
    n_rows = n_blocks * rows
    hbm = pl.BlockSpec(memory_space=pl.ANY)
    grid_spec = pltpu.PrefetchScalarGridSpec(
        num_scalar_prefetch=3,
        grid=(n_blocks,),
        in_specs=[pl.BlockSpec((rows, xs.shape[1]), lambda i, be, ne, nv: (i, 0)), hbm, hbm, hbm],
        out_specs=pl.BlockSpec((rows, d), lambda i, be, ne, nv: (i, 0)),
        scratch_shapes=[pltpu.VMEM((d, f), BF16), pltpu.VMEM((d, f), BF16), pltpu.VMEM((f, d), BF16),
                        pltpu.VMEM((2, d, f), F32), pltpu.VMEM((2, d, f), F32), pltpu.VMEM((2, f, d), F32),
                        pltpu.SMEM((1,), jnp.int32), pltpu.SemaphoreType.DMA((2,))],
    )
    return pl.pallas_call(
        functools.partial(_expert_kernel, layer=layer),
        grid_spec=grid_spec,
        out_shape=jax.ShapeDtypeStruct((n_rows, d), BF16),
        compiler_params=_params("arbitrary"),
        name="moe_experts",
    )(blk_expert, next_expert, n_valid, xs, wg, wu, wd)


def _combine_kernel(cnt_ref, off_ref, region_ref, ys_ref, loc_ref, h_ref, gain_ref, bias_ref, y_ref, yb_ref,
                    buf, sem):
    i = pl.program_id(0)
    n = pl.num_programs(0)
    tm = h_ref.shape[0]
    cap = buf.shape[1]

    def fetch(tile, slot, action):
        for used, copy in _tile_run_copies(cnt_ref, off_ref, region_ref, tile, buf.at[slot], ys_ref,
                                           sem.at[slot], False):
            @pl.when(used)
            def _():
                getattr(copy, action)()

    slot = i % 2

    @pl.when(i == 0)
    def _():
        buf[...] = jnp.zeros(buf.shape, buf.dtype)
        fetch(0, 0, "start")

    @pl.when(i + 1 < n)
    def _():
        fetch(i + 1, 1 - slot, "start")

    fetch(i, slot, "wait")
    loc = loc_ref[...].astype(F32)
    eye = _iota((tm, tm), 0) == _iota((tm, tm), 1)
    loc_col = [jnp.sum(jnp.where(eye, jnp.broadcast_to(loc[k:k + 1, :], (tm, tm)), 0.0), axis=1, keepdims=True)
               for k in range(MOE_TOPK)]
    half = tm // 2
    halves = [slice(k * half, (k + 1) * half) for k in range(2)]
    lane = _iota((half, cap), 1).astype(F32)
    pick = [jnp.where(lane == loc_col[0][rows], 1.0, jnp.where(lane == loc_col[1][rows], 1.0, 0.0)).astype(BF16)
            for rows in halves]
    ffn = [jnp.dot(pick[k], buf[slot], preferred_element_type=F32) for k in range(2)]
    for k, rows in enumerate(halves):
        y = _layer_norm(DEEPNORM_ALPHA * h_ref[rows, :] + ffn[k], gain_ref[...], bias_ref[...])
        y_ref[rows, :] = y
        yb_ref[rows, :] = y.astype(BF16)


def _combine_ln(tables, ys, loc, h, gain, bias):
    t, d = h.shape
    tm = MOE_TILE
    cap = MOE_TOPK * tm + N_EXPERTS * MOE_ALIGN
    grid_spec = pltpu.PrefetchScalarGridSpec(
        num_scalar_prefetch=3,
        grid=(t // tm,),
        in_specs=[pl.BlockSpec(memory_space=pl.ANY),
                  pl.BlockSpec((8, tm), lambda i, *_: (0, i)),
                  pl.BlockSpec((tm, d), lambda i, *_: (i, 0)),
                  pl.BlockSpec((1, d), lambda i, *_: (0, 0)),
                  pl.BlockSpec((1, d), lambda i, *_: (0, 0))],
        out_specs=[pl.BlockSpec((tm, d), lambda i, *_: (i, 0)), pl.BlockSpec((tm, d), lambda i, *_: (i, 0))],
        scratch_shapes=[pltpu.VMEM((2, cap, d), BF16), pltpu.SemaphoreType.DMA((2,))],
    )
    return pl.pallas_call(
        _combine_kernel,
        grid_spec=grid_spec,
        out_shape=[jax.ShapeDtypeStruct((t, d), F32), jax.ShapeDtypeStruct((t, d), BF16)],
        compiler_params=_params("arbitrary"),
        name="moe_combine_ln",
    )(*tables, ys, loc, h, gain, bias)


def _moe_ln(h, logits, rbias, wg, wu, wd, layer, gain, bias):
    t, d = h.shape
    rows = MOE_BLOCK_ROWS
    n_tiles = t // MOE_TILE
    n_blocks = -(-(MOE_TOPK * t + n_tiles * N_EXPERTS * (MOE_ALIGN - 1)) // rows) + N_EXPERTS
    loc, w_t, tab, counts = _route_tokens(logits, rbias)
    tables = tuple(tab[:, :, k].reshape(-1) for k in range(3))
    blocks_per_expert = (counts + rows - 1) // rows
    block_end = jnp.cumsum(blocks_per_expert)
    row_start = (block_end - blocks_per_expert) * rows
    n_valid = block_end[-1:].astype(jnp.int32)
    blk = jnp.arange(n_blocks, dtype=jnp.int32)
    blk_expert = jnp.sum(jnp.minimum(blk, n_valid - 1)[:, None] >= block_end[None, :], axis=1).astype(jnp.int32)
    pad_start = (row_start + counts).astype(jnp.int32)
    pad_len = (blocks_per_expert * rows - counts).astype(jnp.int32)
    xs = _dispatch(tables, pad_start, pad_len, n_valid, h, loc, w_t, n_blocks * rows)
    experts = jnp.arange(N_EXPERTS, dtype=jnp.int32)
    later_used = (experts[None, :] > experts[:, None]) & (blocks_per_expert[None, :] > 0)
    next_used = jnp.min(jnp.where(later_used, experts[None, :], N_EXPERTS), axis=1)
    next_expert = jnp.where(next_used < N_EXPERTS, next_used, -1).astype(jnp.int32)[blk_expert]
    ys = _experts(blk_expert, next_expert, n_valid, xs, wg, wu, wd, layer, n_blocks)
    return _combine_ln(tables, ys, loc, h, gain, bias)


def _rope_tables(seq):
    inv_freq = ROPE_THETA ** (-jnp.arange(0, HEAD_DIM, 2, dtype=F32) / HEAD_DIM)
    ang = jnp.arange(seq, dtype=F32)[:, None] * inv_freq[None, :]
    cos, sin = jnp.cos(ang), jnp.sin(ang)
    return jnp.concatenate([cos, cos], axis=-1), jnp.concatenate([-sin, sin], axis=-1)


def _even_mixer(hb, b, s, w_in, conv_w, a_log, dt_bias, gdn_norm, hgrn_norm, lower_bound):
    gw = GDN_WIDTH
    n_small = 2 * GDN_HEADS
    tail0 = 4 * gw
    w_t = w_in.T
    w_small = jnp.pad(w_t[tail0:tail0 + n_small], ((0, LANES - n_small), (0, 0))).astype(BF16)
    h_a = _matmul(hb, w_t[:tail0].astype(BF16), *EVEN_PROJ_TILE, transposed=True).reshape(b, s, tail0)
    h_b, small = _matmul(hb, w_t[tail0 + n_small:].astype(BF16), *EVEN_PROJ_TILE, transposed=True, side_w=w_small)
    h_b = h_b.reshape(b, s, 4 * HGRN_WIDTH)
    n_chunks = s // GDN_CHUNK
    small = small.reshape(b, s, LANES)
    to_rows = lambda a: a.transpose(0, 2, 1).reshape(b, GDN_HEADS, n_chunks, GDN_CHUNK)
    b_rows, a_rows = to_rows(small[..., :GDN_HEADS]), to_rows(small[..., GDN_HEADS:n_small])
    headvec = lambda v: jnp.broadcast_to(v.astype(F32)[:, None, None], (GDN_HEADS, 1, GDN_CHUNK))
    o_a = _gdn(h_a, conv_w.astype(F32), a_rows, b_rows, headvec(a_log), headvec(dt_bias),
               gdn_norm.astype(F32).reshape(1, HEAD_DIM))
    o_b = _hgrn(h_b, lower_bound.astype(F32).reshape(HGRN_HEADS, 1, HEAD_DIM),
                hgrn_norm.astype(F32).reshape(1, HEAD_DIM))
    return [o_a.reshape(b * s, GDN_WIDTH), o_b.reshape(b * s, HGRN_WIDTH)]


def _odd_mixer(hb, b, s, w_in, cos2, sin2):
    h = _matmul(hb, w_in.astype(BF16), *ODD_PROJ_TILE)
    h3 = h.reshape(b, s, ODD_COLS)
    o_c = _dilated(h3, cos2, sin2)
    o_d = _moba(h3, cos2, sin2)
    return [o_c.reshape(b * s, -1), o_d.reshape(b * s, -1)]


def kernel(x, ev_w_in, ev_conv_w, ev_a_log, ev_dt_bias, ev_gdn_norm, ev_hgrn_norm, hgrn_lb_logits, ev_w_out,
           od_w_in, od_w_out, router_w, router_bias, moe_w_gate, moe_w_up, moe_w_down, ln_gain, ln_bias):
    b, s, d = x.shape
    t = b * s
    cos2, sin2 = _rope_tables(s)
    lower_bounds = jnp.cumsum(jax.nn.softmax(hgrn_lb_logits.astype(F32), axis=0), axis=0)
    rw = jnp.pad(router_w.astype(F32), ((0, 0), (0, LANES - N_EXPERTS)))
    rbias = jnp.pad(router_bias.astype(F32), (0, LANES - N_EXPERTS)).reshape(LANES, 1)
    vec = lambda v: v.astype(F32).reshape(1, d)

    h = x.reshape(t, d)
    hb = h
    for layer in range(DEPTH):
        if layer % 2 == 0:
            e = layer // 2
            parts = _even_mixer(hb, b, s, ev_w_in[e], ev_conv_w[e], ev_a_log[e], ev_dt_bias[e], ev_gdn_norm[e],
                                ev_hgrn_norm[e], lower_bounds[layer])
            w_out = ev_w_out[e].astype(BF16)
        else:
            o = layer // 2
            parts = _odd_mixer(hb, b, s, od_w_in[o], cos2, sin2)
            w_out = od_w_out[o].astype(BF16)
        splits = np.cumsum([p.shape[1] for p in parts])[:-1]
        weights = jnp.split(w_out, splits, axis=0)
        h, logits = _out_ln(parts, weights, h, vec(ln_gain[layer, 0]), vec(ln_bias[layer, 0]), rw, OUT_PROJ_ROWS)
        h, hb = _moe_ln(h, logits, rbias, moe_w_gate, moe_w_up, moe_w_down, layer,
                        vec(ln_gain[layer, 1]), vec(ln_bias[layer, 1]))
    return h.reshape(b, s, d)
```

```python
import functools

import jax
import jax.numpy as jnp
import numpy as np
from jax import lax
from jax.experimental import pallas as pl
from jax.experimental.pallas import tpu as pltpu

F32 = jnp.float32
BF16 = jnp.bfloat16

DEPTH = 2
HEAD_DIM = 128
GDN_HEADS = 8
GDN_CONV = 4
GDN_CHUNK = 64
GDN_WIDTH = GDN_HEADS * HEAD_DIM
HGRN_HEADS = 8
HGRN_CHUNK = 16
HGRN_WIDTH = HGRN_HEADS * HEAD_DIM
DIL_GROUPS = ((128, 1), (512, 4), (2048, 16))
DIL_HEADS_PER_GROUP = 4
DIL_HEADS = len(DIL_GROUPS) * DIL_HEADS_PER_GROUP
MOBA_HEADS = 4
MOBA_BLOCK = 256
MOBA_TOPK = 3
ROPE_THETA = 10000.0
N_EXPERTS = 16
N_EXPERT_GROUPS = 4
EXPERTS_PER_GROUP = N_EXPERTS // N_EXPERT_GROUPS
MOE_TOPK = 2
MOE_BLOCK_ROWS = 512
MOE_TILE = 512
MOE_ALIGN = 16
DEEPNORM_ALPHA = (2.0 * DEPTH) ** 0.25
LN_EPS = 1e-5
RMS_EPS = 1e-6
NEG_INF = -1e30

LANES = 128
VMEM_LIMIT = 56 * 1024 * 1024
EVEN_PROJ_TILE = (1024, 2048)
ODD_PROJ_TILE = (2048, 1024)
OUT_PROJ_ROWS = 256
ATT_BLOCK = MOBA_BLOCK
MOBA_Q_GROUPS = ((7, 0, 6, 1), (5, 2, 4, 3))
DIL_Q_PER_STEP = 4
DIL_BLOCK = 128
GDN_HEADS_PER_STEP = 2
GDN_GROUP = 32
HGRN_GROUP = 4
HGRN_ROWS = 256

ODD_COLS = 3 * DIL_HEADS * HEAD_DIM + 3 * MOBA_HEADS * HEAD_DIM


def _dot(a, b):
    return jnp.dot(a.astype(BF16), b.astype(BF16), preferred_element_type=F32)


def _dot_nt(a, b):
    return lax.dot_general(a.astype(BF16), b.astype(BF16), (((1,), (1,)), ((), ())),
                           preferred_element_type=F32)


def _dot_tn(a, b):
    return lax.dot_general(a.astype(BF16), b.astype(BF16), (((0,), (0,)), ((), ())),
                           preferred_element_type=F32)


def _dot_hi(a, b):
    return jnp.dot(a, b, preferred_element_type=F32, precision=lax.Precision.HIGHEST)


def _dot_nt_hi(a, b):
    return lax.dot_general(a, b, (((1,), (1,)), ((), ())), preferred_element_type=F32,
                           precision=lax.Precision.HIGHEST)


def _dot3(a, b):
    a_hi = a.astype(BF16)
    b_hi = b.astype(BF16)
    a_lo = (a - a_hi.astype(F32)).astype(BF16)
    b_lo = (b - b_hi.astype(F32)).astype(BF16)
    dot = functools.partial(jnp.dot, preferred_element_type=F32)
    return dot(a_hi, b_hi) + (dot(a_hi, b_lo) + dot(a_lo, b_hi))


_dot_inv = _dot


def _silu(x):
    return x * jax.nn.sigmoid(x)


def _iota(shape, dim):
    return lax.broadcasted_iota(jnp.int32, shape, dim)


def _params(*sem):
    return pltpu.CompilerParams(dimension_semantics=sem, vmem_limit_bytes=VMEM_LIMIT)


def _mm_kernel(x_ref, w_ref, *refs, transposed):
    x = x_ref[...].astype(BF16)
    prod = _dot_nt(x, w_ref[...]) if transposed else jnp.dot(x, w_ref[...], preferred_element_type=F32)
    refs[-1 if len(refs) == 1 else -2][...] = prod.astype(F32)
    if len(refs) == 3:
        side_w_ref, _, side_ref = refs

        @pl.when(pl.program_id(1) == 0)
        def _():
            side_ref[...] = _dot_nt(x, side_w_ref[...])


def _matmul(x, w, tm, tn, transposed=False, side_w=None):
    m, k = x.shape
    n = w.shape[0] if transposed else w.shape[1]
    assert m % tm == 0 and n % tn == 0
    w_spec = pl.BlockSpec((tn, k), lambda i, j: (j, 0)) if transposed else pl.BlockSpec((k, tn), lambda i, j: (0, j))
    in_specs = [pl.BlockSpec((tm, k), lambda i, j: (i, 0)), w_spec]
    out_specs = [pl.BlockSpec((tm, tn), lambda i, j: (i, j))]
    out_shape = [jax.ShapeDtypeStruct((m, n), F32)]
    operands = [x, w]
    if side_w is not None:
        in_specs.append(pl.BlockSpec((LANES, k), lambda i, j: (0, 0)))
        out_specs.append(pl.BlockSpec((tm, LANES), lambda i, j: (i, 0)))
        out_shape.append(jax.ShapeDtypeStruct((m, LANES), F32))
        operands.append(side_w)
    outs = pl.pallas_call(
        functools.partial(_mm_kernel, transposed=transposed),
        grid=(m // tm, n // tn),
        in_specs=in_specs,
        out_specs=out_specs,
        out_shape=out_shape,
        compiler_params=_params("parallel", "arbitrary" if side_w is not None else "parallel"),
        name="in_proj",
    )(*operands)
    return outs if side_w is not None else outs[0]


def _gdn_kernel(q_ref, k_ref, v_ref, z_ref, cwq_ref, cwk_ref, cwv_ref, a_ref, b_ref, alog_ref, dt_ref,
                gn_ref, o_ref, pad_s, q_s, k_s, v_s, gcum_s, beta_s, qe_s, ob_s, sm_s, sa_s):
    seq = q_ref.shape[0]
    c = GDN_CHUNK
    n_chunks = seq // c
    rows = 256
    heads = range(GDN_HEADS_PER_STEP)
    lanes = [slice(hh * HEAD_DIM, (hh + 1) * HEAD_DIM) for hh in heads]

    def conv_norm(hh):
        pad_s[pl.ds(0, 8), :] = jnp.zeros((8, HEAD_DIM), F32)
        for x_ref, cw_ref, dst, mode in ((q_ref, cwq_ref, q_s, "q"), (k_ref, cwk_ref, k_s, "k"),
                                         (v_ref, cwv_ref, v_s, "v")):
            pad_s[pl.ds(8, seq), :] = x_ref[:, lanes[hh]]
            cw = cw_ref[:, lanes[hh]]
            for r in range(seq // rows):
                acc = None
                for j in range(GDN_CONV):
                    tap = pad_s[pl.ds(8 + r * rows - (GDN_CONV - 1) + j, rows), :] * cw[j:j + 1, :]
                    acc = tap if acc is None else acc + tap
                y = _silu(acc)
                if mode != "v":
                    y = y * lax.rsqrt(jnp.sum(y * y, axis=-1, keepdims=True) + RMS_EPS)
                if mode == "q":
                    y = y * HEAD_DIM ** -0.5
                dst[pl.ds(r * rows, rows), :] = y

    upper = (_iota((c, c), 0) <= _iota((c, c), 1)).astype(F32)
    for hh in heads:
        g = -jnp.exp(alog_ref[hh]) * jax.nn.softplus(a_ref[hh] + dt_ref[hh])
        gcum_s[hh] = _dot_hi(g, upper)
        beta_s[hh] = jax.nn.sigmoid(b_ref[hh])

    ri = _iota((c, c), 0)
    ci = _iota((c, c), 1)
    eye = ri == ci
    strict = ri > ci
    incl = ri >= ci
    eye_f = eye.astype(F32)
    level1 = ri // 2 == ci // 2
    levels = []
    s = 2
    while s < c:
        levels.append((ri // (2 * s) == ci // (2 * s)) & ((ri // s) % 2 == 1) & ((ci // s) % 2 == 0))
        s *= 2

    dot = functools.partial(jnp.dot, preferred_element_type=F32)

    def to_col(row):
        return jnp.sum(jnp.where(eye, jnp.broadcast_to(row, (c, c)), 0.0), axis=1, keepdims=True)

    def prepare(i, _, hh):
        n0 = i * GDN_GROUP
        grp = range(GDN_GROUP)
        starts = [pl.multiple_of((n0 + j) * c, c) for j in grp]
        qc = [q_s[pl.ds(r0, c), :] for r0 in starts]
        kc = [k_s[pl.ds(r0, c), :] for r0 in starts]
        vc = [v_s[pl.ds(r0, c), :] for r0 in starts]
        g_row = [gcum_s[hh, pl.ds(n0 + j, 1), :] for j in grp]
        g_col = [to_col(g_row[j]) for j in grp]
        b_col = [to_col(beta_s[hh, pl.ds(n0 + j, 1), :]) for j in grp]
        decay = [jnp.exp(jnp.where(incl, g_col[j] - g_row[j], 0.0)) for j in grp]
        n_mat = [b_col[j] * jnp.where(strict, decay[j], 0.0) * _dot_nt(kc[j], kc[j]) for j in grp]
        inv = [eye_f - jnp.where(level1, n_mat[j], 0.0) for j in grp]
        for blk in levels:
            tmp = [_dot_inv(inv[j], jnp.where(blk, n_mat[j], 0.0)) for j in grp]
            inv = [inv[j] - _dot_inv(tmp[j], inv[j]) for j in grp]
        e_col = [jnp.exp(g_col[j]) for j in grp]
        sol = [_dot_inv(inv[j], jnp.concatenate([b_col[j] * vc[j], (b_col[j] * e_col[j]) * kc[j]], axis=1))
               for j in grp]
        qk = [(_dot_nt(qc[j], kc[j]) * jnp.where(incl, decay[j], 0.0)).astype(BF16) for j in grp]
        ub = [sol[j][:, :HEAD_DIM].astype(BF16) for j in grp]
        w = [sol[j][:, HEAD_DIM:].astype(BF16) for j in grp]
        kd = [(kc[j] * jnp.exp(g_row[j][:, c - 1:c] - g_col[j])).astype(BF16) for j in grp]
        q_eff = [(qc[j] * e_col[j] - dot(qk[j], w[j])).astype(BF16) for j in grp]
        o_base = [dot(qk[j], ub[j]) for j in grp]
        s_mat = [_dot_tn(kd[j], w[j]).astype(BF16) for j in grp]
        s_add = [_dot_tn(kd[j], ub[j]) for j in grp]
        for j, r0 in enumerate(starts):
            m0 = pl.multiple_of((n0 + j) * HEAD_DIM, HEAD_DIM)
            qe_s[hh, pl.ds(r0, c), :] = q_eff[j]
            ob_s[hh, pl.ds(r0, c), :] = o_base[j]
            sm_s[hh, pl.ds(m0, HEAD_DIM), :] = s_mat[j]
            sa_s[hh, pl.ds(m0, HEAD_DIM), :] = s_add[j]
        return 0

    for hh in heads:
        conv_norm(hh)
        lax.fori_loop(0, n_chunks // GDN_GROUP, functools.partial(prepare, hh=hh), 0)

    gn = gn_ref[...]

    def chunk(n, states):
        r0 = pl.multiple_of(n * c, c)
        m0 = pl.multiple_of(n * HEAD_DIM, HEAD_DIM)
        g_last = [gcum_s[hh, pl.ds(n, 1), :][:, c - 1:c] for hh in heads]
        lhs = [jnp.concatenate([qe_s[hh, pl.ds(r0, c), :], sm_s[hh, pl.ds(m0, HEAD_DIM), :]], axis=0)
               for hh in heads]
        prod = [dot(lhs[hh], states[hh].astype(BF16)) for hh in heads]
        for hh in heads:
            ob_s[hh, pl.ds(r0, c), :] = prod[hh][:c] + ob_s[hh, pl.ds(r0, c), :]
        return tuple(jnp.exp(g_last[hh]) * states[hh] - prod[hh][c:] + sa_s[hh, pl.ds(m0, HEAD_DIM), :]
                     for hh in heads)

    lax.fori_loop(0, n_chunks, chunk, tuple(jnp.zeros((HEAD_DIM, HEAD_DIM), F32) for _ in heads))

    for hh in heads:
        for r in range(seq // rows):
            sl = pl.ds(r * rows, rows)
            o = ob_s[hh, sl, :]
            o = o * lax.rsqrt(jnp.mean(o * o, axis=-1, keepdims=True) + RMS_EPS) * gn
            o_ref[sl, lanes[hh]] = (o * _silu(z_ref[sl, lanes[hh]])).astype(o_ref.dtype)


def _gdn(h3, conv_w, a_rows, b_rows, alog, dt, gn):
    b, s, _ = h3.shape
    hp = GDN_HEADS_PER_STEP
    nb = GDN_HEADS // hp
    wide = hp * HEAD_DIM
    n_chunks = s // GDN_CHUNK
    col = lambda off: pl.BlockSpec((None, s, wide), lambda bi, hi: (bi, 0, off + hi))
    cw = lambda off: pl.BlockSpec((GDN_CONV, wide), lambda bi, hi: (0, off + hi))
    rowspec = pl.BlockSpec((None, hp, n_chunks, GDN_CHUNK), lambda bi, hi: (bi, hi, 0, 0))
    headvec = pl.BlockSpec((hp, 1, GDN_CHUNK), lambda bi, hi: (hi, 0, 0))
    return pl.pallas_call(
        _gdn_kernel,
        grid=(b, nb),
        in_specs=[col(0), col(nb), col(2 * nb), col(3 * nb), cw(0), cw(nb), cw(2 * nb),
                  rowspec, rowspec, headvec, headvec,
                  pl.BlockSpec((1, HEAD_DIM), lambda bi, hi: (0, 0))],
        out_specs=pl.BlockSpec((None, s, wide), lambda bi, hi: (bi, 0, hi)),
        out_shape=jax.ShapeDtypeStruct((b, s, GDN_WIDTH), BF16),
        scratch_shapes=[pltpu.VMEM((s + 8, HEAD_DIM), F32), pltpu.VMEM((s, HEAD_DIM), F32),
                        pltpu.VMEM((s, HEAD_DIM), F32), pltpu.VMEM((s, HEAD_DIM), F32),
                        pltpu.VMEM((hp, n_chunks, GDN_CHUNK), F32), pltpu.VMEM((hp, n_chunks, GDN_CHUNK), F32),
                        pltpu.VMEM((hp, s, HEAD_DIM), BF16), pltpu.VMEM((hp, s, HEAD_DIM), F32),
                        pltpu.VMEM((hp, n_chunks * HEAD_DIM, HEAD_DIM), BF16),
                        pltpu.VMEM((hp, n_chunks * HEAD_DIM, HEAD_DIM), F32)],
        compiler_params=_params("parallel", "parallel"),
        name="gdn",
    )(h3, h3, h3, h3, conv_w, conv_w, conv_w, a_rows, b_rows, alog, dt, gn)


def _hgrn_kernel(q_ref, f_ref, i_ref, g_ref, lb_ref, hn_ref, o_ref):
    seq = q_ref.shape[0]
    c = HGRN_CHUNK
    rows = HGRN_ROWS
    ri = _iota((rows, rows), 0)
    ci = _iota((rows, rows), 1)
    causal = (ri // c == ci // c) & (ci <= ri)
    row_in_chunk = _iota((rows, HEAD_DIM), 0) % c
    lb = lb_ref[...]
    hn = hn_ref[...]

    chunks = [slice(j * c, (j + 1) * c) for j in range(rows // c)]
    grp = range(HGRN_GROUP)

    def chunk_scan(x):
        step = 1
        while step < c:
            x = x + jnp.where(row_in_chunk >= step, pltpu.roll(x, step, axis=0), 0.0)
            step *= 2
        return x

    def group(n, state_t):
        starts = [pl.multiple_of((n * HGRN_GROUP + j) * rows, rows) for j in grp]
        qc = [q_ref[pl.ds(r0, rows), :] for r0 in starts]
        ic = [i_ref[pl.ds(r0, rows), :].astype(BF16) for r0 in starts]
        f = [lb + (1.0 - lb) * jax.nn.sigmoid(f_ref[pl.ds(r0, rows), :]) for r0 in starts]
        log_f = [jnp.log(f[j]) for j in grp]
        bcum = [chunk_scan(log_f[j]) for j in grp]
        chunk_dec = [jnp.exp(bcum[j]) for j in grp]
        q_dec = [(qc[j] * chunk_dec[j]).astype(BF16) for j in grp]
        k_inv = [((1.0 - f[j]) * jnp.exp(-bcum[j])).astype(BF16) for j in grp]
        p = [jnp.where(causal, _dot_nt(q_dec[j], k_inv[j]), 0.0) for j in grp]
        o_intra = [_dot(p[j], ic[j]) for j in grp]
        updates = [[_dot_tn(ic[j][sl], k_inv[j][sl]) for sl in chunks] for j in grp]
        for j, r0 in enumerate(starts):
            outs = []
            for sl, upd in zip(chunks, updates[j]):
                outs.append(o_intra[j][sl] + _dot_nt(q_dec[j][sl], state_t))
                state_t = (state_t + upd) * chunk_dec[j][sl.stop - 1:sl.stop]
            o = jnp.concatenate(outs, axis=0)
            o = o * lax.rsqrt(jnp.mean(o * o, axis=-1, keepdims=True) + RMS_EPS) * hn
            o_ref[pl.ds(r0, rows), :] = (o * _silu(g_ref[pl.ds(r0, rows), :])).astype(o_ref.dtype)
        return state_t

    lax.fori_loop(0, seq // (rows * HGRN_GROUP), group, jnp.zeros((HEAD_DIM, HEAD_DIM), F32))


def _hgrn(h3, lb, hn):
    b, s, _ = h3.shape
    nh = HGRN_HEADS
    col = lambda off: pl.BlockSpec((None, s, HEAD_DIM), lambda bi, hi: (bi, 0, off + hi))
    return pl.pallas_call(
        _hgrn_kernel,
        grid=(b, nh),
        in_specs=[col(0), col(nh), col(2 * nh), col(3 * nh),
                  pl.BlockSpec((None, 1, HEAD_DIM), lambda bi, hi: (hi, 0, 0)),
                  pl.BlockSpec((1, HEAD_DIM), lambda bi, hi: (0, 0))],
        out_specs=pl.BlockSpec((None, s, HEAD_DIM), lambda bi, hi: (bi, 0, hi)),
        out_shape=jax.ShapeDtypeStruct((b, s, HGRN_WIDTH), BF16),
        compiler_params=_params("parallel", "parallel"),
        name="hgrn2",
    )(h3, h3, h3, h3, lb, hn)


def _rope(x, cos2, sin2):
    return x * cos2 + pltpu.roll(x, HEAD_DIM // 2, axis=1) * sin2


def _flash_step(q_blk, k_blk, v_blk, mask, carry):
    m, l, acc = carry
    s = jnp.where(mask, _dot_nt(q_blk, k_blk), NEG_INF)
    m_new = jnp.maximum(m, jnp.max(s, axis=-1, keepdims=True))
    alpha = jnp.exp(m - m_new)
    p = jnp.exp(s - m_new)
    l = alpha * l + jnp.sum(p, axis=-1, keepdims=True)
    acc = alpha * acc + _dot(p, v_blk)
    return m_new, l, acc


def _flash_init():
    blk = ATT_BLOCK
    return (jnp.full((blk, 1), NEG_INF, F32), jnp.zeros((blk, 1), F32), jnp.zeros((blk, HEAD_DIM), F32))


def _dilated_kernel(*refs):
    n_g = len(DIL_GROUPS)
    q_refs, k_refs, v_refs = refs[0:n_g], refs[n_g:2 * n_g], refs[2 * n_g:3 * n_g]
    cos_ref, sin_ref, o_ref = refs[3 * n_g:3 * n_g + 3]
    q_s, k_s, v_s, og_s, lse_s = refs[3 * n_g + 3:]
    seq = o_ref.shape[0]
    blk = DIL_BLOCK
    piece = 256
    grp = range(n_g)

    for gi, (window, d) in enumerate(DIL_GROUPS):
        assert window // d == blk
        seg = seq // d
        k_s[gi, pl.ds(0, blk), :] = jnp.zeros((blk, HEAD_DIM), BF16)
        v_s[gi, pl.ds(0, blk), :] = jnp.zeros((blk, HEAD_DIM), BF16)
        for r in range(d):
            for c0 in range(0, seg, piece):
                n = min(piece, seg)
                rows = pl.ds(r + c0 * d, n, stride=d) if d > 1 else pl.ds(c0, n)
                cos2 = cos_ref[gi, pl.ds(r * seg + c0, n), :]
                sin2 = sin_ref[gi, pl.ds(r * seg + c0, n), :]
                q_s[gi, pl.ds(r * seg + c0, n), :] = (_rope(q_refs[gi][rows, :], cos2, sin2)
                                                      * HEAD_DIM ** -0.5).astype(BF16)
                k_s[gi, pl.ds(blk + r * seg + c0, n), :] = _rope(k_refs[gi][rows, :], cos2, sin2).astype(BF16)
                v_s[gi, pl.ds(blk + r * seg + c0, n), :] = v_refs[gi][rows, :].astype(BF16)

    ri = _iota((blk, 2 * blk), 0)
    ci = _iota((blk, 2 * blk), 1)
    rel = ri + blk - ci
    in_window = (rel >= 0) & (rel <= blk)
    dot = functools.partial(jnp.dot, preferred_element_type=F32)

    def q_block(m, _):
        segs = [seq // d for _, d in DIL_GROUPS]
        work = [(g, pl.multiple_of((m * DIL_Q_PER_STEP + u) * blk, blk)) for u in range(DIL_Q_PER_STEP) for g in grp]
        ids = range(len(work))
        has_prev = [jnp.where(j0 % segs[g] != 0, blk, 0) for g, j0 in work]
        mask = [ci + has_prev[i] >= blk for i in ids]
        q = [q_s[g, pl.ds(j0, blk), :] for g, j0 in work]
        kw = [k_s[g, pl.ds(j0, 2 * blk), :] for g, j0 in work]
        vw = [v_s[g, pl.ds(j0, 2 * blk), :] for g, j0 in work]
        s = [jnp.where(in_window, jnp.where(mask[i], _dot_nt(q[i], kw[i]), NEG_INF), NEG_INF) for i in ids]
        top = [jnp.max(s[i], axis=-1, keepdims=True) for i in ids]
        p = [jnp.exp(s[i] - top[i]) for i in ids]
        den = [jnp.sum(p[i], axis=-1, keepdims=True) for i in ids]
        o = [dot(p[i].astype(BF16), vw[i]) / den[i] for i in ids]
        lse = [top[i] + jnp.log(den[i]) for i in ids]
        for i, (g, j0) in enumerate(work):
            seg, d = segs[g], DIL_GROUPS[g][1]
            dst = pl.ds((j0 % seg) * d + j0 // seg, blk, stride=d) if d > 1 else pl.ds(j0, blk)
            og_s[g, dst, :] = o[i]
            lse_s[g, dst, :] = jnp.broadcast_to(lse[i], (blk, HEAD_DIM))
        return 0

    lax.fori_loop(0, seq // (blk * DIL_Q_PER_STEP), q_block, 0)

    for c0 in range(0, seq, piece):
        rows = pl.ds(c0, piece)
        lses = [lse_s[g, rows, :] for g in grp]
        top = functools.reduce(jnp.maximum, lses)
        wts = [jnp.exp(x - top) for x in lses]
        den = functools.reduce(lambda a, b: a + b, wts)
        o = functools.reduce(lambda a, b: a + b, [wts[g] * og_s[g, rows, :] for g in grp]) / den
        o_ref[rows, :] = o.astype(o_ref.dtype)


def _dilated(h3, cos2, sin2):
    b, s, _ = h3.shape
    hpg = DIL_HEADS_PER_GROUP
    n_g = len(DIL_GROUPS)
    col = lambda off: pl.BlockSpec((None, s, HEAD_DIM), lambda bi, hi: (bi, 0, off + hi))
    tab = pl.BlockSpec((n_g, s, HEAD_DIM), lambda bi, hi: (0, 0, 0))
    specs = [col(part * DIL_HEADS + gi * hpg) for part in range(3) for gi in range(n_g)]
    residue_major = lambda t: jnp.stack([t.reshape(s // d, d, HEAD_DIM).transpose(1, 0, 2).reshape(s, HEAD_DIM)
                                         for _, d in DIL_GROUPS])
    cos2, sin2 = residue_major(cos2), residue_major(sin2)
    return pl.pallas_call(
        _dilated_kernel,
        grid=(b, hpg),
        in_specs=specs + [tab, tab],
        out_specs=pl.BlockSpec((None, s, HEAD_DIM), lambda bi, hi: (bi, 0, hi)),
        out_shape=jax.ShapeDtypeStruct((b, s, hpg * HEAD_DIM), BF16),
        scratch_shapes=[pltpu.VMEM((n_g, s, HEAD_DIM), BF16), pltpu.VMEM((n_g, s + DIL_BLOCK, HEAD_DIM), BF16),
                        pltpu.VMEM((n_g, s + DIL_BLOCK, HEAD_DIM), BF16), pltpu.VMEM((n_g, s, HEAD_DIM), F32),
                        pltpu.VMEM((n_g, s, HEAD_DIM), F32)],
        compiler_params=_params("parallel", "parallel"),
        name="dilated_attention",
    )(*([h3] * (3 * n_g)), cos2, sin2)


def _moba_kernel(q_ref, k_ref, v_ref, cos_ref, sin_ref, o_ref, qf_s, q_s, k_s, v_s, km_s, sel_s):
    seq = o_ref.shape[0]
    blk = MOBA_BLOCK
    n_blk = seq // blk
    cos2 = cos_ref[...]
    sin2 = sin_ref[...]
    q = _rope(q_ref[...], cos2, sin2)
    qf_s[...] = q
    q_s[...] = (q * HEAD_DIM ** -0.5).astype(BF16)
    km_s[...] = jnp.zeros(km_s.shape, F32)
    for nb in range(n_blk):
        kb = _rope(k_ref[pl.ds(nb * blk, blk), :], cos2[nb * blk:(nb + 1) * blk], sin2[nb * blk:(nb + 1) * blk])
        k_s[pl.ds(nb * blk, blk), :] = kb.astype(BF16)
        km_s[pl.ds(nb, 1), :] = jnp.mean(kb, axis=0, keepdims=True)
    v_s[...] = v_ref[...].astype(BF16)

    lane = _iota((blk, LANES), 1).astype(F32)
    causal = _iota((blk, blk), 0) >= _iota((blk, blk), 1)
    rows = lambda nb: pl.ds(nb * blk, blk)

    past = range(1, n_blk)
    km = km_s[...]
    gate = {qb: jnp.where(lane < qb, _dot_nt_hi(qf_s[rows(qb), :], km), -jnp.inf) for qb in past}
    sel = {qb: jnp.zeros((blk, LANES), F32) for qb in past}
    for _k in range(MOBA_TOPK):
        best = {qb: jnp.max(gate[qb], axis=-1, keepdims=True) for qb in past}
        first = {qb: jnp.min(jnp.where(gate[qb] == best[qb], lane, LANES), axis=-1, keepdims=True) for qb in past}
        pick = {qb: (lane == first[qb]) & (best[qb] > -jnp.inf) for qb in past}
        sel = {qb: jnp.where(pick[qb], 1.0, sel[qb]) for qb in past}
        gate = {qb: jnp.where(pick[qb], -jnp.inf, gate[qb]) for qb in past}
    for qb in past:
        sel_s[rows(qb), :] = sel[qb]

    assert sorted(qb for group in MOBA_Q_GROUPS for qb in group) == list(range(n_blk))
    for group in MOBA_Q_GROUPS:
        carry = {qb: _flash_init() for qb in group}
        for j in range(max(group) + 1):
            for qb in group:
                if j < qb:
                    mask = sel_s[rows(qb), :][:, j:j + 1] > 0.0
                elif j == qb:
                    mask = causal
                else:
                    continue
                carry[qb] = _flash_step(q_s[rows(qb), :], k_s[rows(j), :], v_s[rows(j), :], mask, carry[qb])
        for qb in group:
            m, l, acc = carry[qb]
            o_ref[rows(qb), :] = (acc / l).astype(o_ref.dtype)


def _moba(h3, cos2, sin2):
    b, s, _ = h3.shape
    base = 3 * DIL_HEADS
    col = lambda off: pl.BlockSpec((None, s, HEAD_DIM), lambda bi, hi: (bi, 0, base + off + hi))
    tab = pl.BlockSpec((s, HEAD_DIM), lambda bi, hi: (0, 0))
    return pl.pallas_call(
        _moba_kernel,
        grid=(b, MOBA_HEADS),
        in_specs=[col(0), col(MOBA_HEADS), col(2 * MOBA_HEADS), tab, tab],
        out_specs=pl.BlockSpec((None, s, HEAD_DIM), lambda bi, hi: (bi, 0, hi)),
        out_shape=jax.ShapeDtypeStruct((b, s, MOBA_HEADS * HEAD_DIM), BF16),
        scratch_shapes=[pltpu.VMEM((s, HEAD_DIM), F32), pltpu.VMEM((s, HEAD_DIM), BF16),
                        pltpu.VMEM((s, HEAD_DIM), BF16), pltpu.VMEM((s, HEAD_DIM), BF16),
                        pltpu.VMEM((LANES, HEAD_DIM), F32), pltpu.VMEM((s, LANES), F32)],
        compiler_params=_params("parallel", "parallel"),
        name="moba_attention",
    )(h3, h3, h3, cos2, sin2)


def _layer_norm(x, gain, bias):
    mu = jnp.mean(x, axis=-1, keepdims=True)
    xc = x - mu
    var = jnp.mean(xc * xc, axis=-1, keepdims=True)
    return xc * lax.rsqrt(var + LN_EPS) * gain + bias


def _out_ln_kernel(*refs, n_parts):
    o_refs = refs[0:n_parts]
    w_refs = refs[n_parts:2 * n_parts]
    h_ref, gain_ref, bias_ref, rw_ref, y_ref, logit_ref = refs[2 * n_parts:]
    half = h_ref.shape[0] // 2
    halves = [pl.ds(k * half, half) for k in range(2)]
    mix = []
    for rows in halves:
        parts = [jnp.dot(o_r[rows, :], w_r[...], preferred_element_type=F32) for o_r, w_r in zip(o_refs, w_refs)]
        mix.append(functools.reduce(lambda a, b: a + b, parts))
    y = [_layer_norm(DEEPNORM_ALPHA * h_ref[rows, :] + mix[k], gain_ref[...], bias_ref[...])
         for k, rows in enumerate(halves)]
    logits = [_dot3(y[k], rw_ref[...]) for k in range(2)]
    for k, rows in enumerate(halves):
        y_ref[rows, :] = y[k]
        logit_ref[rows, :] = logits[k]


def _out_ln(parts, weights, h, gain, bias, router_w, tm):
    t, d = h.shape
    n_parts = len(parts)
    row = lambda width: pl.BlockSpec((tm, width), lambda i: (i, 0))
    full = lambda a: pl.BlockSpec(a.shape, lambda i: (0, 0))
    return pl.pallas_call(
        functools.partial(_out_ln_kernel, n_parts=n_parts),
        grid=(t // tm,),
        in_specs=[row(p.shape[1]) for p in parts] + [full(w) for w in weights]
                 + [row(d), full(gain), full(bias), full(router_w)],
        out_specs=[row(d), row(LANES)],
        out_shape=[jax.ShapeDtypeStruct((t, d), F32), jax.ShapeDtypeStruct((t, LANES), F32)],
        compiler_params=_params("parallel"),
        name="out_proj_ln",
    )(*parts, *weights, h, gain, bias, router_w)


def _route(logits_t, rbias_col):
    row = _iota(logits_t.shape, 0)
    scores = jax.nn.sigmoid(logits_t)
    biased = scores + rbias_col

    def first_argmax(vals):
        best = jnp.max(vals, axis=0, keepdims=True)
        return best, jnp.min(jnp.where(vals == best, row, N_EXPERTS), axis=0, keepdims=True)

    best_score = None
    best_group = None
    for g in range(N_EXPERT_GROUPS):
        vals = jnp.where(row // EXPERTS_PER_GROUP == g, biased, -jnp.inf)
        top1, idx1 = first_argmax(vals)
        top2, _ = first_argmax(jnp.where(row == idx1, -jnp.inf, vals))
        score = top1 + top2
        if g == 0:
            best_score, best_group = score, jnp.zeros_like(idx1)
        else:
            better = score > best_score
            best_group = jnp.where(better, g, best_group)
            best_score = jnp.where(better, score, best_score)
    masked = jnp.where(row // EXPERTS_PER_GROUP == best_group, biased, NEG_INF)
    _, i1 = first_argmax(masked)
    _, i2 = first_argmax(jnp.where(row == i1, -jnp.inf, masked))
    s1 = jnp.sum(jnp.where(row == i1, scores, 0.0), axis=0, keepdims=True)
    s2 = jnp.sum(jnp.where(row == i2, scores, 0.0), axis=0, keepdims=True)
    tot = s1 + s2
    return i1, i2, s1 / tot, s2 / tot


def _route_kernel(logit_ref, rbias_ref, loc_ref, w_ref, tab_ref, cnt_ref, carry_s, total_s):
    sweep = pl.program_id(0)
    i = pl.program_id(1)
    tm = logit_ref.shape[0]
    block = float(MOE_BLOCK_ROWS)
    align = float(MOE_ALIGN)

    @pl.when((i == 0) & (sweep == 1))
    def _():
        total_s[...] = carry_s[...]

    @pl.when(i == 0)
    def _():
        carry_s[...] = jnp.zeros(carry_s.shape, F32)

    logits_t = logit_ref[...].T[:N_EXPERTS]
    i1, i2, w1, w2 = _route(logits_t, rbias_ref[...][:N_EXPERTS])
    row = _iota((N_EXPERTS, tm), 0)
    chosen = jnp.where((row == i1) | (row == i2), 1.0, 0.0)
    count = jnp.sum(chosen, axis=1, keepdims=True)
    padded = jnp.floor((count + (align - 1.0)) * (1.0 / align)) * align

    e_r = _iota((N_EXPERTS, N_EXPERTS), 0)
    e_c = _iota((N_EXPERTS, N_EXPERTS), 1)

    def expert_prefix(col, inclusive):
        tri = ((e_c <= e_r) if inclusive else (e_c < e_r)).astype(BF16)
        wide = jnp.broadcast_to(col, (N_EXPERTS, LANES)).astype(BF16)
        return jnp.dot(tri, wide, preferred_element_type=F32)[:, 0:1]

    @pl.when(sweep == 0)
    def _():
        loc_ref[...] = jnp.zeros(loc_ref.shape, loc_ref.dtype)
        w_ref[...] = jnp.zeros(w_ref.shape, w_ref.dtype)
        tab_ref[...] = jnp.zeros(tab_ref.shape, tab_ref.dtype)
        cnt_ref[...] = jnp.zeros(cnt_ref.shape, cnt_ref.dtype)

    @pl.when(sweep == 1)
    def _():
        total = total_s[...]
        blocks = jnp.floor((total + (block - 1.0)) * (1.0 / block))
        region = (expert_prefix(blocks, True) - blocks) * block + carry_s[...]
        run_start = expert_prefix(padded, False)
        earlier = (_iota((tm, tm), 0) < _iota((tm, tm), 1)).astype(BF16)
        slot = run_start + jnp.dot(chosen.astype(BF16), earlier, preferred_element_type=F32)
        l1 = jnp.sum(jnp.where(row == i1, slot, 0.0), axis=0, keepdims=True)
        l2 = jnp.sum(jnp.where(row == i2, slot, 0.0), axis=0, keepdims=True)
        out_row = _iota((8, tm), 0)
        loc_ref[...] = jnp.where(out_row == 0, l1, jnp.where(out_row == 1, l2, 0.0)).astype(jnp.int32)
        w_ref[...] = jnp.where(out_row == 0, w1, jnp.where(out_row == 1, w2, 0.0))
        lane = _iota((N_EXPERTS, LANES), 1)
        tab = jnp.where(lane == 0, padded, jnp.where(lane == 1, run_start, jnp.where(lane == 2, region, 0.0)))
        tab_ref[...] = tab.astype(jnp.int32)
        cnt_ref[...] = jnp.broadcast_to(total, cnt_ref.shape).astype(jnp.int32)

    carry_s[...] = carry_s[...] + padded


def _route_tokens(logits, rbias_col):
    t = logits.shape[0]
    tm = MOE_TILE
    n_tiles = t // tm
    tok = pl.BlockSpec((None, 8, tm), lambda s, i: (s, 0, i))
    loc, w, tab, totals = pl.pallas_call(
        _route_kernel,
        grid=(2, n_tiles),
        in_specs=[pl.BlockSpec((tm, LANES), lambda s, i: (i, 0)), pl.BlockSpec((LANES, 1), lambda s, i: (0, 0))],
        out_specs=[tok, tok, pl.BlockSpec((None, None, N_EXPERTS, LANES), lambda s, i: (s, i, 0, 0)),
                   pl.BlockSpec((N_EXPERTS, LANES), lambda s, i: (0, 0))],
        out_shape=[jax.ShapeDtypeStruct((2, 8, t), jnp.int32), jax.ShapeDtypeStruct((2, 8, t), F32),
                   jax.ShapeDtypeStruct((2, n_tiles, N_EXPERTS, LANES), jnp.int32),
                   jax.ShapeDtypeStruct((N_EXPERTS, LANES), jnp.int32)],
        scratch_shapes=[pltpu.VMEM((N_EXPERTS, 1), F32), pltpu.VMEM((N_EXPERTS, 1), F32)],
        compiler_params=_params("arbitrary", "arbitrary"),
        name="moe_route",
    )(logits, rbias_col)
    return loc[1], w[1], tab[1], totals[:, 0]


def _run_pieces(length, sizes):
    return [(size, length & ~(2 * size - 1), (length & size) != 0) for size in sizes]


def _tile_run_copies(cnt_ref, off_ref, region_ref, tile, compact, sorted_rows, sem, to_sorted):
    copies = []
    for e in range(N_EXPERTS):
        k = tile * N_EXPERTS + e
        length = pl.multiple_of(cnt_ref[k], MOE_ALIGN)
        small = compact.at[pl.ds(pl.multiple_of(off_ref[k], MOE_ALIGN), length)]
        big = sorted_rows.at[pl.ds(pl.multiple_of(region_ref[k], MOE_ALIGN), length)]
        copies.append((length > 0, pltpu.make_async_copy(small, big, sem) if to_sorted
                       else pltpu.make_async_copy(big, small, sem)))
    return copies


def _dispatch_kernel(cnt_ref, off_ref, region_ref, pad_start_ref, pad_len_ref, n_valid_ref,
                     x_ref, loc_ref, w_ref, xs_ref, buf_s, zero_s, sem, zero_sem):
    i = pl.program_id(0)
    d = x_ref.shape[1]
    cap = buf_s.shape[1]
    tm = x_ref.shape[0]
    n_blocks = xs_ref.shape[0] // MOE_BLOCK_ROWS

    @pl.when(i == 0)
    def _():
        zero_s[...] = jnp.zeros(zero_s.shape, zero_s.dtype)
        sizes = [MOE_BLOCK_ROWS >> s for s in range((MOE_BLOCK_ROWS // MOE_ALIGN).bit_length())]
        copies = []
        for e in range(N_EXPERTS):
            for size, offset, used in _run_pieces(pad_len_ref[e], sizes):
                dst = xs_ref.at[pl.ds(pl.multiple_of(pad_start_ref[e] + offset, MOE_ALIGN), size)]
                copies.append((used, pltpu.make_async_copy(zero_s.at[pl.ds(0, size)], dst, zero_sem)))
        for j in range(n_blocks - N_EXPERTS, n_blocks):
            dst = xs_ref.at[pl.ds(j * MOE_BLOCK_ROWS, MOE_BLOCK_ROWS)]
            copies.append((j >= n_valid_ref[0], pltpu.make_async_copy(zero_s, dst, zero_sem)))
        for action in ("start", "wait"):
            for used, copy in copies:
                @pl.when(used)
                def _():
                    getattr(copy, action)()

    loc = loc_ref[...]
    w = w_ref[...]
    slot = i % 2
    x = x_ref[...].astype(BF16)
    half = cap // 2
    parts = [pl.ds(k * half, half) for k in range(2)]
    row = [_iota((half, tm), 0) + k * half for k in range(2)]
    hit1 = [row[k] == loc[0:1, :] for k in range(2)]
    hit2 = [row[k] == loc[1:2, :] for k in range(2)]
    perm = [jnp.where(hit1[k], 1.0, jnp.where(hit2[k], 1.0, 0.0)).astype(BF16) for k in range(2)]
    sorted_x = [jnp.dot(perm[k], x, preferred_element_type=F32).astype(BF16) for k in range(2)]
    gate = [jnp.sum(jnp.where(hit1[k], w[0:1, :], jnp.where(hit2[k], w[1:2, :], 0.0)), axis=1, keepdims=True)
            for k in range(2)]
    lane = _iota((half, LANES), 1)
    for k in range(2):
        g_hi = gate[k].astype(BF16)
        g_lo = (gate[k] - g_hi.astype(F32)).astype(BF16)
        buf_s[slot, parts[k], pl.ds(0, d)] = sorted_x[k]
        buf_s[slot, parts[k], pl.ds(d, LANES)] = jnp.where(lane == 0, g_hi.astype(F32),
                                                          jnp.where(lane == 1, g_lo.astype(F32), 0.0)).astype(BF16)

    def run_copies(tile, which, action):
        for used, copy in _tile_run_copies(cnt_ref, off_ref, region_ref, tile, buf_s.at[which], xs_ref,
                                           sem.at[which], True):
            @pl.when(used)
            def _():
                getattr(copy, action)()

    run_copies(i, slot, "start")

    @pl.when(i > 0)
    def _():
        run_copies(i - 1, 1 - slot, "wait")

    @pl.when(i == pl.num_programs(0) - 1)
    def _():
        run_copies(i, slot, "wait")


def _dispatch(tables, pad_start, pad_len, n_valid, x, loc, w, n_rows):
    t, d = x.shape
    tm = MOE_TILE
    cap = MOE_TOPK * tm + N_EXPERTS * MOE_ALIGN
    tok = pl.BlockSpec((8, tm), lambda i, *_: (0, i))
    grid_spec = pltpu.PrefetchScalarGridSpec(
        num_scalar_prefetch=6,
        grid=(t // tm,),
        in_specs=[pl.BlockSpec((tm, d), lambda i, *_: (i, 0)), tok, tok],
        out_specs=pl.BlockSpec(memory_space=pl.ANY),
        scratch_shapes=[pltpu.VMEM((2, cap, d + LANES), BF16), pltpu.VMEM((MOE_BLOCK_ROWS, d + LANES), BF16),
                        pltpu.SemaphoreType.DMA((2,)), pltpu.SemaphoreType.DMA(())],
    )
    return pl.pallas_call(
        _dispatch_kernel,
        grid_spec=grid_spec,
        out_shape=jax.ShapeDtypeStruct((n_rows, d + LANES), BF16),
        compiler_params=_params("arbitrary"),
        name="moe_dispatch",
    )(*tables, pad_start, pad_len, n_valid, x, loc, w)


def _expert_kernel(blk_expert_ref, next_expert_ref, n_valid_ref, x_ref, wg_ref, wu_ref, wd_ref, y_ref,
                   wg_s, wu_s, wd_s, wg_buf, wu_buf, wd_buf, slot_s, sem, *, layer):
    i = pl.program_id(0)

    def fetch(expert, slot):
        return [pltpu.make_async_copy(src.at[layer, expert], dst.at[slot], sem.at[slot])
                for src, dst in ((wg_ref, wg_buf), (wu_ref, wu_buf), (wd_ref, wd_buf))]

    @pl.when(i == 0)
    def _():
        slot_s[0] = 1
        for copy in fetch(blk_expert_ref[0], 0):
            copy.start()

    @pl.when((i == 0) | (blk_expert_ref[i] != blk_expert_ref[jnp.maximum(i - 1, 0)]))
    def _():
        slot = 1 - slot_s[0]
        slot_s[0] = slot
        for copy in fetch(blk_expert_ref[i], slot):
            copy.wait()
        wg_s[...] = wg_buf[slot].astype(BF16)
        wu_s[...] = wu_buf[slot].astype(BF16)
        wd_s[...] = wd_buf[slot].astype(BF16)

        @pl.when(next_expert_ref[i] >= 0)
        def _():
            for copy in fetch(next_expert_ref[i], 1 - slot):
                copy.start()

    @pl.when(i < n_valid_ref[0])
    def _():
        d = wg_s.shape[0]
        x = x_ref[:, pl.ds(0, d)]
        extra = x_ref[:, pl.ds(d, LANES)].astype(F32)
        gate = extra[:, 0:1] + extra[:, 1:2]
        hid = _silu(jnp.dot(x, wg_s[...], preferred_element_type=F32)) * jnp.dot(
            x, wu_s[...], preferred_element_type=F32)
        y_ref[...] = _dot(hid * gate, wd_s[...]).astype(y_ref.dtype)

    @pl.when(i >= n_valid_ref[0])
    def _():
        y_ref[...] = jnp.zeros(y_ref.shape, y_ref.dtype)


def _experts(blk_expert, next_expert, n_valid, xs, wg, wu, wd, layer, n_blocks):
    d = wg.shape[-2]
    f = wg.shape[-1]
    rows = MOE_BLOCK_ROWS
    n_rows = n_blocks * rows
    hbm = pl.BlockSpec(memory_space=pl.ANY)
    grid_spec = pltpu.PrefetchScalarGridSpec(
        num_scalar_prefetch=3,
        grid=(n_blocks,),
        in_specs=[pl.BlockSpec((rows, xs.shape[1]), lambda i, be, ne, nv: (jnp.minimum(i, nv[0] - 1), 0)),
                  hbm, hbm, hbm],
        out_specs=pl.BlockSpec((rows, d), lambda i, be, ne, nv: (i, 0)),
        scratch_shapes=[pltpu.VMEM((d, f), BF16), pltpu.VMEM((d, f), BF16), pltpu.VMEM((f, d), BF16),
                        pltpu.VMEM((2, d, f), F32), pltpu.VMEM((2, d, f), F32), pltpu.VMEM((2, f, d), F32),
                        pltpu.SMEM((1,), jnp.int32), pltpu.SemaphoreType.DMA((2,))],
    )
    return pl.pallas_call(
        functools.partial(_expert_kernel, layer=layer),
        grid_spec=grid_spec,
        out_shape=jax.ShapeDtypeStruct((n_rows, d), BF16),
        compiler_params=_params("arbitrary"),
        name="moe_experts",
    )(blk_expert, next_expert, n_valid, xs, wg, wu, wd)


def _combine_kernel(cnt_ref, off_ref, region_ref, ys_ref, loc_ref, h_ref, gain_ref, bias_ref, y_ref, yb_ref,
                    buf, sem):
    i = pl.program_id(0)
    n = pl.num_programs(0)
    tm = h_ref.shape[0]
    cap = buf.shape[1]

    def fetch(tile, slot, action):
        for used, copy in _tile_run_copies(cnt_ref, off_ref, region_ref, tile, buf.at[slot], ys_ref,
                                           sem.at[slot], False):
            @pl.when(used)
            def _():
                getattr(copy, action)()

    slot = i % 2

    @pl.when(i == 0)
    def _():
        buf[...] = jnp.zeros(buf.shape, buf.dtype)
        fetch(0, 0, "start")

    @pl.when(i + 1 < n)
    def _():
        fetch(i + 1, 1 - slot, "start")

    fetch(i, slot, "wait")
    loc = loc_ref[...].astype(F32)
    eye = _iota((tm, tm), 0) == _iota((tm, tm), 1)
    loc_col = [jnp.sum(jnp.where(eye, jnp.broadcast_to(loc[k:k + 1, :], (tm, tm)), 0.0), axis=1, keepdims=True)
               for k in range(MOE_TOPK)]
    half = tm // 2
    halves = [slice(k * half, (k + 1) * half) for k in range(2)]
    lane = _iota((half, cap), 1).astype(F32)
    pick = [jnp.where(lane == loc_col[0][rows], 1.0, jnp.where(lane == loc_col[1][rows], 1.0, 0.0)).astype(BF16)
            for rows in halves]
    ffn = [jnp.dot(pick[k], buf[slot], preferred_element_type=F32) for k in range(2)]
    for k, rows in enumerate(halves):
        y = _layer_norm(DEEPNORM_ALPHA * h_ref[rows, :] + ffn[k], gain_ref[...], bias_ref[...])
        y_ref[rows, :] = y
        yb_ref[rows, :] = y.astype(BF16)


def _combine_ln(tables, ys, loc, h, gain, bias):
    t, d = h.shape
    tm = MOE_TILE
    cap = MOE_TOPK * tm + N_EXPERTS * MOE_ALIGN
    grid_spec = pltpu.PrefetchScalarGridSpec(
        num_scalar_prefetch=3,
        grid=(t // tm,),
        in_specs=[pl.BlockSpec(memory_space=pl.ANY),
                  pl.BlockSpec((8, tm), lambda i, *_: (0, i)),
                  pl.BlockSpec((tm, d), lambda i, *_: (i, 0)),
                  pl.BlockSpec((1, d), lambda i, *_: (0, 0)),
                  pl.BlockSpec((1, d), lambda i, *_: (0, 0))],
        out_specs=[pl.BlockSpec((tm, d), lambda i, *_: (i, 0)), pl.BlockSpec((tm, d), lambda i, *_: (i, 0))],
        scratch_shapes=[pltpu.VMEM((2, cap, d), BF16), pltpu.SemaphoreType.DMA((2,))],
    )
    return pl.pallas_call(
        _combine_kernel,
        grid_spec=grid_spec,
        out_shape=[jax.ShapeDtypeStruct((t, d), F32), jax.ShapeDtypeStruct((t, d), BF16)],
        compiler_params=_params("arbitrary"),
        name="moe_combine_ln",
    )(*tables, ys, loc, h, gain, bias)


def _moe_ln(h, logits, rbias, wg, wu, wd, layer, gain, bias):
    t, d = h.shape
    rows = MOE_BLOCK_ROWS
    n_tiles = t // MOE_TILE
    n_blocks = -(-(MOE_TOPK * t + n_tiles * N_EXPERTS * (MOE_ALIGN - 1)) // rows) + N_EXPERTS
    loc, w_t, tab, counts = _route_tokens(logits, rbias)
    tables = tuple(tab[:, :, k].reshape(-1) for k in range(3))
    blocks_per_expert = (counts + rows - 1) // rows
    block_end = jnp.cumsum(blocks_per_expert)
    row_start = (block_end - blocks_per_expert) * rows
    n_valid = block_end[-1:].astype(jnp.int32)
    blk = jnp.arange(n_blocks, dtype=jnp.int32)
    blk_expert = jnp.sum(jnp.minimum(blk, n_valid - 1)[:, None] >= block_end[None, :], axis=1).astype(jnp.int32)
    pad_start = (row_start + counts).astype(jnp.int32)
    pad_len = (blocks_per_expert * rows - counts).astype(jnp.int32)
    xs = _dispatch(tables, pad_start, pad_len, n_valid, h, loc, w_t, n_blocks * rows)
    experts = jnp.arange(N_EXPERTS, dtype=jnp.int32)
    later_used = (experts[None, :] > experts[:, None]) & (blocks_per_expert[None, :] > 0)
    next_used = jnp.min(jnp.where(later_used, experts[None, :], N_EXPERTS), axis=1)
    next_expert = jnp.where(next_used < N_EXPERTS, next_used, -1).astype(jnp.int32)[blk_expert]
    ys = _experts(blk_expert, next_expert, n_valid, xs, wg, wu, wd, layer, n_blocks)
    return _combine_ln(tables, ys, loc, h, gain, bias)


def _rope_tables(seq):
    inv_freq = ROPE_THETA ** (-jnp.arange(0, HEAD_DIM, 2, dtype=F32) / HEAD_DIM)
    ang = jnp.arange(seq, dtype=F32)[:, None] * inv_freq[None, :]
    cos, sin = jnp.cos(ang), jnp.sin(ang)
    return jnp.concatenate([cos, cos], axis=-1), jnp.concatenate([-sin, sin], axis=-1)


def _even_mixer(hb, b, s, w_in, conv_w, a_log, dt_bias, gdn_norm, hgrn_norm, lower_bound):
    gw = GDN_WIDTH
    n_small = 2 * GDN_HEADS
    tail0 = 4 * gw
    w_t = w_in.T
    w_small = jnp.pad(w_t[tail0:tail0 + n_small], ((0, LANES - n_small), (0, 0))).astype(BF16)
    h_a = _matmul(hb, w_t[:tail0].astype(BF16), *EVEN_PROJ_TILE, transposed=True).reshape(b, s, tail0)
    h_b, small = _matmul(hb, w_t[tail0 + n_small:].astype(BF16), *EVEN_PROJ_TILE, transposed=True, side_w=w_small)
    h_b = h_b.reshape(b, s, 4 * HGRN_WIDTH)
    n_chunks = s // GDN_CHUNK
    small = small.reshape(b, s, LANES)
    to_rows = lambda a: a.transpose(0, 2, 1).reshape(b, GDN_HEADS, n_chunks, GDN_CHUNK)
    b_rows, a_rows = to_rows(small[..., :GDN_HEADS]), to_rows(small[..., GDN_HEADS:n_small])
    headvec = lambda v: jnp.broadcast_to(v.astype(F32)[:, None, None], (GDN_HEADS, 1, GDN_CHUNK))
    o_a = _gdn(h_a, conv_w.astype(F32), a_rows, b_rows, headvec(a_log), headvec(dt_bias),
               gdn_norm.astype(F32).reshape(1, HEAD_DIM))
    o_b = _hgrn(h_b, lower_bound.astype(F32).reshape(HGRN_HEADS, 1, HEAD_DIM),
                hgrn_norm.astype(F32).reshape(1, HEAD_DIM))
    return [o_a.reshape(b * s, GDN_WIDTH), o_b.reshape(b * s, HGRN_WIDTH)]


def _odd_mixer(hb, b, s, w_in, cos2, sin2):
    h = _matmul(hb, w_in.astype(BF16), *ODD_PROJ_TILE)
    h3 = h.reshape(b, s, ODD_COLS)
    o_c = _dilated(h3, cos2, sin2)
    o_d = _moba(h3, cos2, sin2)
    return [o_c.reshape(b * s, -1), o_d.reshape(b * s, -1)]


def kernel(x, ev_w_in, ev_conv_w, ev_a_log, ev_dt_bias, ev_gdn_norm, ev_hgrn_norm, hgrn_lb_logits, ev_w_out,
           od_w_in, od_w_out, router_w, router_bias, moe_w_gate, moe_w_up, moe_w_down, ln_gain, ln_bias):
    b, s, d = x.shape
    t = b * s
    cos2, sin2 = _rope_tables(s)
    lower_bounds = jnp.cumsum(jax.nn.softmax(hgrn_lb_logits.astype(F32), axis=0), axis=0)
    rw = jnp.pad(router_w.astype(F32), ((0, 0), (0, LANES - N_EXPERTS)))
    rbias = jnp.pad(router_bias.astype(F32), (0, LANES - N_EXPERTS)).reshape(LANES, 1)
    vec = lambda v: v.astype(F32).reshape(1, d)

    h = x.reshape(t, d)
    hb = h
    for layer in range(DEPTH):
        if layer % 2 == 0:
            e = layer // 2
            parts = _even_mixer(hb, b, s, ev_w_in[e], ev_conv_w[e], ev_a_log[e], ev_dt_bias[e], ev_gdn_norm[e],
                                ev_hgrn_norm[e], lower_bounds[layer])
            w_out = ev_w_out[e].astype(BF16)
        else:
            o = layer // 2
            parts = _odd_mixer(hb, b, s, od_w_in[o], cos2, sin2)
            w_out = od_w_out[o].astype(BF16)
        splits = np.cumsum([p.shape[1] for p in parts])[:-1]
        weights = jnp.split(w_out, splits, axis=0)
        h, logits = _out_ln(parts, weights, h, vec(ln_gain[layer, 0]), vec(ln_bias[layer, 0]), rw, OUT_PROJ_ROWS)
        h, hb = _moe_ln(h, logits, rbias, moe_w_gate, moe_w_up, moe_w_down, layer,
                        vec(ln_gain[layer, 1]), vec(ln_bias[layer, 1]))
    return h.reshape(b, s, d)
```

```python
import functools

import jax
import jax.numpy as jnp
import numpy as np
from jax import lax
from jax.experimental import pallas as pl
from jax.experimental.pallas import tpu as pltpu

F32 = jnp.float32
BF16 = jnp.bfloat16

DEPTH = 2
HEAD_DIM = 128
GDN_HEADS = 8
GDN_CONV = 4
GDN_CHUNK = 64
GDN_WIDTH = GDN_HEADS * HEAD_DIM
HGRN_HEADS = 8
HGRN_CHUNK = 16
HGRN_WIDTH = HGRN_HEADS * HEAD_DIM
DIL_GROUPS = ((128, 1), (512, 4), (2048, 16))
DIL_HEADS_PER_GROUP = 4
DIL_HEADS = len(DIL_GROUPS) * DIL_HEADS_PER_GROUP
MOBA_HEADS = 4
MOBA_BLOCK = 256
MOBA_TOPK = 3
ROPE_THETA = 10000.0
N_EXPERTS = 16
N_EXPERT_GROUPS = 4
EXPERTS_PER_GROUP = N_EXPERTS // N_EXPERT_GROUPS
MOE_TOPK = 2
MOE_BLOCK_ROWS = 512
MOE_TILE = 512
MOE_ALIGN = 16
DEEPNORM_ALPHA = (2.0 * DEPTH) ** 0.25
LN_EPS = 1e-5
RMS_EPS = 1e-6
NEG_INF = -1e30

LANES = 128
VMEM_LIMIT = 56 * 1024 * 1024
EVEN_PROJ_TILE = (1024, 2048)
ODD_PROJ_TILE = (2048, 1024)
OUT_PROJ_ROWS = 512
ATT_BLOCK = MOBA_BLOCK
MOBA_Q_GROUPS = ((7, 0, 6, 1), (5, 2, 4, 3))
DIL_Q_PER_STEP = 4
DIL_BLOCK = 128
GDN_HEADS_PER_STEP = 2
GDN_GROUP = 32
HGRN_GROUP = 4
HGRN_ROWS = 256

ODD_COLS = 3 * DIL_HEADS * HEAD_DIM + 3 * MOBA_HEADS * HEAD_DIM


def _dot(a, b):
    return jnp.dot(a.astype(BF16), b.astype(BF16), preferred_element_type=F32)


def _dot_nt(a, b):
    return lax.dot_general(a.astype(BF16), b.astype(BF16), (((1,), (1,)), ((), ())),
                           preferred_element_type=F32)


def _dot_tn(a, b):
    return lax.dot_general(a.astype(BF16), b.astype(BF16), (((0,), (0,)), ((), ())),
                           preferred_element_type=F32)


def _dot_hi(a, b):
    return jnp.dot(a, b, preferred_element_type=F32, precision=lax.Precision.HIGHEST)


def _dot_nt_hi(a, b):
    return lax.dot_general(a, b, (((1,), (1,)), ((), ())), preferred_element_type=F32,
                           precision=lax.Precision.HIGHEST)


def _dot3(a, b):
    a_hi = a.astype(BF16)
    b_hi = b.astype(BF16)
    a_lo = (a - a_hi.astype(F32)).astype(BF16)
    b_lo = (b - b_hi.astype(F32)).astype(BF16)
    dot = functools.partial(jnp.dot, preferred_element_type=F32)
    return dot(a_hi, b_hi) + (dot(a_hi, b_lo) + dot(a_lo, b_hi))


_dot_inv = _dot


def _silu(x):
    return x * jax.nn.sigmoid(x)


def _iota(shape, dim):
    return lax.broadcasted_iota(jnp.int32, shape, dim)


def _params(*sem):
    return pltpu.CompilerParams(dimension_semantics=sem, vmem_limit_bytes=VMEM_LIMIT)


def _mm_kernel(x_ref, w_ref, *refs, transposed):
    x = x_ref[...].astype(BF16)
    prod = _dot_nt(x, w_ref[...]) if transposed else jnp.dot(x, w_ref[...], preferred_element_type=F32)
    refs[-1 if len(refs) == 1 else -2][...] = prod.astype(F32)
    if len(refs) == 3:
        side_w_ref, _, side_ref = refs

        @pl.when(pl.program_id(1) == 0)
        def _():
            side_ref[...] = _dot_nt(x, side_w_ref[...])


def _matmul(x, w, tm, tn, transposed=False, side_w=None):
    m, k = x.shape
    n = w.shape[0] if transposed else w.shape[1]
    assert m % tm == 0 and n % tn == 0
    w_spec = pl.BlockSpec((tn, k), lambda i, j: (j, 0)) if transposed else pl.BlockSpec((k, tn), lambda i, j: (0, j))
    in_specs = [pl.BlockSpec((tm, k), lambda i, j: (i, 0)), w_spec]
    out_specs = [pl.BlockSpec((tm, tn), lambda i, j: (i, j))]
    out_shape = [jax.ShapeDtypeStruct((m, n), F32)]
    operands = [x, w]
    if side_w is not None:
        in_specs.append(pl.BlockSpec((LANES, k), lambda i, j: (0, 0)))
        out_specs.append(pl.BlockSpec((tm, LANES), lambda i, j: (i, 0)))
        out_shape.append(jax.ShapeDtypeStruct((m, LANES), F32))
        operands.append(side_w)
    outs = pl.pallas_call(
        functools.partial(_mm_kernel, transposed=transposed),
        grid=(m // tm, n // tn),
        in_specs=in_specs,
        out_specs=out_specs,
        out_shape=out_shape,
        compiler_params=_params("parallel", "arbitrary" if side_w is not None else "parallel"),
        name="in_proj",
    )(*operands)
    return outs if side_w is not None else outs[0]


def _gdn_kernel(q_ref, k_ref, v_ref, z_ref, cwq_ref, cwk_ref, cwv_ref, a_ref, b_ref, alog_ref, dt_ref,
                gn_ref, o_ref, pad_s, q_s, k_s, v_s, gcum_s, beta_s, qe_s, ob_s, sm_s, sa_s):
    seq = q_ref.shape[0]
    c = GDN_CHUNK
    n_chunks = seq // c
    rows = 256
    heads = range(GDN_HEADS_PER_STEP)
    lanes = [slice(hh * HEAD_DIM, (hh + 1) * HEAD_DIM) for hh in heads]

    def conv_norm(hh):
        pad_s[pl.ds(0, 8), :] = jnp.zeros((8, HEAD_DIM), F32)
        for x_ref, cw_ref, dst, mode in ((q_ref, cwq_ref, q_s, "q"), (k_ref, cwk_ref, k_s, "k"),
                                         (v_ref, cwv_ref, v_s, "v")):
            pad_s[pl.ds(8, seq), :] = x_ref[:, lanes[hh]]
            cw = cw_ref[:, lanes[hh]]
            for r in range(seq // rows):
                acc = None
                for j in range(GDN_CONV):
                    tap = pad_s[pl.ds(8 + r * rows - (GDN_CONV - 1) + j, rows), :] * cw[j:j + 1, :]
                    acc = tap if acc is None else acc + tap
                y = _silu(acc)
                if mode != "v":
                    y = y * lax.rsqrt(jnp.sum(y * y, axis=-1, keepdims=True) + RMS_EPS)
                if mode == "q":
                    y = y * HEAD_DIM ** -0.5
                dst[pl.ds(r * rows, rows), :] = y

    upper = (_iota((c, c), 0) <= _iota((c, c), 1)).astype(F32)
    for hh in heads:
        g = -jnp.exp(alog_ref[hh]) * jax.nn.softplus(a_ref[hh] + dt_ref[hh])
        gcum_s[hh] = _dot_hi(g, upper)
        beta_s[hh] = jax.nn.sigmoid(b_ref[hh])

    ri = _iota((c, c), 0)
    ci = _iota((c, c), 1)
    eye = ri == ci
    strict = ri > ci
    incl = ri >= ci
    eye_f = eye.astype(F32)
    level1 = ri // 2 == ci // 2
    levels = []
    s = 2
    while s < c:
        levels.append((ri // (2 * s) == ci // (2 * s)) & ((ri // s) % 2 == 1) & ((ci // s) % 2 == 0))
        s *= 2

    dot = functools.partial(jnp.dot, preferred_element_type=F32)

    def to_col(row):
        return jnp.sum(jnp.where(eye, jnp.broadcast_to(row, (c, c)), 0.0), axis=1, keepdims=True)

    def prepare(i, _, hh):
        n0 = i * GDN_GROUP
        grp = range(GDN_GROUP)
        starts = [pl.multiple_of((n0 + j) * c, c) for j in grp]
        qc = [q_s[pl.ds(r0, c), :] for r0 in starts]
        kc = [k_s[pl.ds(r0, c), :] for r0 in starts]
        vc = [v_s[pl.ds(r0, c), :] for r0 in starts]
        g_row = [gcum_s[hh, pl.ds(n0 + j, 1), :] for j in grp]
        g_col = [to_col(g_row[j]) for j in grp]
        b_col = [to_col(beta_s[hh, pl.ds(n0 + j, 1), :]) for j in grp]
        decay = [jnp.exp(jnp.where(incl, g_col[j] - g_row[j], 0.0)) for j in grp]
        n_mat = [b_col[j] * jnp.where(strict, decay[j], 0.0) * _dot_nt(kc[j], kc[j]) for j in grp]
        inv = [eye_f - jnp.where(level1, n_mat[j], 0.0) for j in grp]
        for blk in levels:
            tmp = [_dot_inv(inv[j], jnp.where(blk, n_mat[j], 0.0)) for j in grp]
            inv = [inv[j] - _dot_inv(tmp[j], inv[j]) for j in grp]
        e_col = [jnp.exp(g_col[j]) for j in grp]
        sol = [_dot_inv(inv[j], jnp.concatenate([b_col[j] * vc[j], (b_col[j] * e_col[j]) * kc[j]], axis=1))
               for j in grp]
        qk = [(_dot_nt(qc[j], kc[j]) * jnp.where(incl, decay[j], 0.0)).astype(BF16) for j in grp]
        ub = [sol[j][:, :HEAD_DIM].astype(BF16) for j in grp]
        w = [sol[j][:, HEAD_DIM:].astype(BF16) for j in grp]
        kd = [(kc[j] * jnp.exp(g_row[j][:, c - 1:c] - g_col[j])).astype(BF16) for j in grp]
        q_eff = [(qc[j] * e_col[j] - dot(qk[j], w[j])).astype(BF16) for j in grp]
        o_base = [dot(qk[j], ub[j]) for j in grp]
        s_mat = [_dot_tn(kd[j], w[j]).astype(BF16) for j in grp]
        s_add = [_dot_tn(kd[j], ub[j]) for j in grp]
        for j, r0 in enumerate(starts):
            m0 = pl.multiple_of((n0 + j) * HEAD_DIM, HEAD_DIM)
            qe_s[hh, pl.ds(r0, c), :] = q_eff[j]
            ob_s[hh, pl.ds(r0, c), :] = o_base[j]
            sm_s[hh, pl.ds(m0, HEAD_DIM), :] = s_mat[j]
            sa_s[hh, pl.ds(m0, HEAD_DIM), :] = s_add[j]
        return 0

    for hh in heads:
        conv_norm(hh)
        lax.fori_loop(0, n_chunks // GDN_GROUP, functools.partial(prepare, hh=hh), 0)

    gn = gn_ref[...]

    def chunk(n, states):
        r0 = pl.multiple_of(n * c, c)
        m0 = pl.multiple_of(n * HEAD_DIM, HEAD_DIM)
        g_last = [gcum_s[hh, pl.ds(n, 1), :][:, c - 1:c] for hh in heads]
        lhs = [jnp.concatenate([qe_s[hh, pl.ds(r0, c), :], sm_s[hh, pl.ds(m0, HEAD_DIM), :]], axis=0)
               for hh in heads]
        prod = [dot(lhs[hh], states[hh].astype(BF16)) for hh in heads]
        for hh in heads:
            ob_s[hh, pl.ds(r0, c), :] = prod[hh][:c] + ob_s[hh, pl.ds(r0, c), :]
        return tuple(jnp.exp(g_last[hh]) * states[hh] - prod[hh][c:] + sa_s[hh, pl.ds(m0, HEAD_DIM), :]
                     for hh in heads)

    lax.fori_loop(0, n_chunks, chunk, tuple(jnp.zeros((HEAD_DIM, HEAD_DIM), F32) for _ in heads))

    for hh in heads:
        for r in range(seq // rows):
            sl = pl.ds(r * rows, rows)
            o = ob_s[hh, sl, :]
            o = o * lax.rsqrt(jnp.mean(o * o, axis=-1, keepdims=True) + RMS_EPS) * gn
            o_ref[sl, lanes[hh]] = (o * _silu(z_ref[sl, lanes[hh]])).astype(o_ref.dtype)


def _gdn(h3, conv_w, a_rows, b_rows, alog, dt, gn):
    b, s, _ = h3.shape
    hp = GDN_HEADS_PER_STEP
    nb = GDN_HEADS // hp
    wide = hp * HEAD_DIM
    n_chunks = s // GDN_CHUNK
    col = lambda off: pl.BlockSpec((None, s, wide), lambda bi, hi: (bi, 0, off + hi))
    cw = lambda off: pl.BlockSpec((GDN_CONV, wide), lambda bi, hi: (0, off + hi))
    rowspec = pl.BlockSpec((None, hp, n_chunks, GDN_CHUNK), lambda bi, hi: (bi, hi, 0, 0))
    headvec = pl.BlockSpec((hp, 1, GDN_CHUNK), lambda bi, hi: (hi, 0, 0))
    return pl.pallas_call(
        _gdn_kernel,
        grid=(b, nb),
        in_specs=[col(0), col(nb), col(2 * nb), col(3 * nb), cw(0), cw(nb), cw(2 * nb),
                  rowspec, rowspec, headvec, headvec,
                  pl.BlockSpec((1, HEAD_DIM), lambda bi, hi: (0, 0))],
        out_specs=pl.BlockSpec((None, s, wide), lambda bi, hi: (bi, 0, hi)),
        out_shape=jax.ShapeDtypeStruct((b, s, GDN_WIDTH), BF16),
        scratch_shapes=[pltpu.VMEM((s + 8, HEAD_DIM), F32), pltpu.VMEM((s, HEAD_DIM), F32),
                        pltpu.VMEM((s, HEAD_DIM), F32), pltpu.VMEM((s, HEAD_DIM), F32),
                        pltpu.VMEM((hp, n_chunks, GDN_CHUNK), F32), pltpu.VMEM((hp, n_chunks, GDN_CHUNK), F32),
                        pltpu.VMEM((hp, s, HEAD_DIM), BF16), pltpu.VMEM((hp, s, HEAD_DIM), F32),
                        pltpu.VMEM((hp, n_chunks * HEAD_DIM, HEAD_DIM), BF16),
                        pltpu.VMEM((hp, n_chunks * HEAD_DIM, HEAD_DIM), F32)],
        compiler_params=_params("parallel", "parallel"),
        name="gdn",
    )(h3, h3, h3, h3, conv_w, conv_w, conv_w, a_rows, b_rows, alog, dt, gn)


def _hgrn_kernel(q_ref, f_ref, i_ref, g_ref, lb_ref, hn_ref, o_ref):
    seq = q_ref.shape[0]
    c = HGRN_CHUNK
    rows = HGRN_ROWS
    ri = _iota((rows, rows), 0)
    ci = _iota((rows, rows), 1)
    causal = (ri // c == ci // c) & (ci <= ri)
    row_in_chunk = _iota((rows, HEAD_DIM), 0) % c
    lb = lb_ref[...]
    hn = hn_ref[...]

    chunks = [slice(j * c, (j + 1) * c) for j in range(rows // c)]
    grp = range(HGRN_GROUP)

    def chunk_scan(x):
        step = 1
        while step < c:
            x = x + jnp.where(row_in_chunk >= step, pltpu.roll(x, step, axis=0), 0.0)
            step *= 2
        return x

    def group(n, state_t):
        starts = [pl.multiple_of((n * HGRN_GROUP + j) * rows, rows) for j in grp]
        qc = [q_ref[pl.ds(r0, rows), :] for r0 in starts]
        ic = [i_ref[pl.ds(r0, rows), :].astype(BF16) for r0 in starts]
        f = [lb + (1.0 - lb) * jax.nn.sigmoid(f_ref[pl.ds(r0, rows), :]) for r0 in starts]
        log_f = [jnp.log(f[j]) for j in grp]
        bcum = [chunk_scan(log_f[j]) for j in grp]
        chunk_dec = [jnp.exp(bcum[j]) for j in grp]
        q_dec = [(qc[j] * chunk_dec[j]).astype(BF16) for j in grp]
        k_inv = [((1.0 - f[j]) * jnp.exp(-bcum[j])).astype(BF16) for j in grp]
        p = [jnp.where(causal, _dot_nt(q_dec[j], k_inv[j]), 0.0) for j in grp]
        o_intra = [_dot(p[j], ic[j]) for j in grp]
        updates = [[_dot_tn(ic[j][sl], k_inv[j][sl]) for sl in chunks] for j in grp]
        for j, r0 in enumerate(starts):
            outs = []
            for sl, upd in zip(chunks, updates[j]):
                outs.append(o_intra[j][sl] + _dot_nt(q_dec[j][sl], state_t))
                state_t = (state_t + upd) * chunk_dec[j][sl.stop - 1:sl.stop]
            o = jnp.concatenate(outs, axis=0)
            o = o * lax.rsqrt(jnp.mean(o * o, axis=-1, keepdims=True) + RMS_EPS) * hn
            o_ref[pl.ds(r0, rows), :] = (o * _silu(g_ref[pl.ds(r0, rows), :])).astype(o_ref.dtype)
        return state_t

    lax.fori_loop(0, seq // (rows * HGRN_GROUP), group, jnp.zeros((HEAD_DIM, HEAD_DIM), F32))


def _hgrn(h3, lb, hn):
    b, s, _ = h3.shape
    nh = HGRN_HEADS
    col = lambda off: pl.BlockSpec((None, s, HEAD_DIM), lambda bi, hi: (bi, 0, off + hi))
    return pl.pallas_call(
        _hgrn_kernel,
        grid=(b, nh),
        in_specs=[col(0), col(nh), col(2 * nh), col(3 * nh),
                  pl.BlockSpec((None, 1, HEAD_DIM), lambda bi, hi: (hi, 0, 0)),
                  pl.BlockSpec((1, HEAD_DIM), lambda bi, hi: (0, 0))],
        out_specs=pl.BlockSpec((None, s, HEAD_DIM), lambda bi, hi: (bi, 0, hi)),
        out_shape=jax.ShapeDtypeStruct((b, s, HGRN_WIDTH), BF16),
        compiler_params=_params("parallel", "parallel"),
        name="hgrn2",
    )(h3, h3, h3, h3, lb, hn)


def _rope(x, cos2, sin2):
    return x * cos2 + pltpu.roll(x, HEAD_DIM // 2, axis=1) * sin2


def _flash_step(q_blk, k_blk, v_blk, mask, carry):
    m, l, acc = carry
    s = jnp.where(mask, _dot_nt(q_blk, k_blk), NEG_INF)
    m_new = jnp.maximum(m, jnp.max(s, axis=-1, keepdims=True))
    alpha = jnp.exp(m - m_new)
    p = jnp.exp(s - m_new)
    l = alpha * l + jnp.sum(p, axis=-1, keepdims=True)
    acc = alpha * acc + _dot(p, v_blk)
    return m_new, l, acc


def _flash_init():
    blk = ATT_BLOCK
    return (jnp.full((blk, 1), NEG_INF, F32), jnp.zeros((blk, 1), F32), jnp.zeros((blk, HEAD_DIM), F32))


def _dilated_kernel(*refs):
    n_g = len(DIL_GROUPS)
    q_refs, k_refs, v_refs = refs[0:n_g], refs[n_g:2 * n_g], refs[2 * n_g:3 * n_g]
    cos_ref, sin_ref, o_ref = refs[3 * n_g:3 * n_g + 3]
    q_s, k_s, v_s, og_s, lse_s = refs[3 * n_g + 3:]
    seq = o_ref.shape[0]
    blk = DIL_BLOCK
    piece = 256
    grp = range(n_g)

    for gi, (window, d) in enumerate(DIL_GROUPS):
        assert window // d == blk
        seg = seq // d
        k_s[gi, pl.ds(0, blk), :] = jnp.zeros((blk, HEAD_DIM), BF16)
        v_s[gi, pl.ds(0, blk), :] = jnp.zeros((blk, HEAD_DIM), BF16)
        for r in range(d):
            for c0 in range(0, seg, piece):
                n = min(piece, seg)
                rows = pl.ds(r + c0 * d, n, stride=d) if d > 1 else pl.ds(c0, n)
                cos2 = cos_ref[gi, pl.ds(r * seg + c0, n), :]
                sin2 = sin_ref[gi, pl.ds(r * seg + c0, n), :]
                q_s[gi, pl.ds(r * seg + c0, n), :] = (_rope(q_refs[gi][rows, :], cos2, sin2)
                                                      * HEAD_DIM ** -0.5).astype(BF16)
                k_s[gi, pl.ds(blk + r * seg + c0, n), :] = _rope(k_refs[gi][rows, :], cos2, sin2).astype(BF16)
                v_s[gi, pl.ds(blk + r * seg + c0, n), :] = v_refs[gi][rows, :].astype(BF16)

    ri = _iota((blk, 2 * blk), 0)
    ci = _iota((blk, 2 * blk), 1)
    rel = ri + blk - ci
    in_window = (rel >= 0) & (rel <= blk)
    dot = functools.partial(jnp.dot, preferred_element_type=F32)

    def q_block(m, _):
        segs = [seq // d for _, d in DIL_GROUPS]
        work = [(g, pl.multiple_of((m * DIL_Q_PER_STEP + u) * blk, blk)) for u in range(DIL_Q_PER_STEP) for g in grp]
        ids = range(len(work))
        has_prev = [jnp.where(j0 % segs[g] != 0, blk, 0) for g, j0 in work]
        mask = [ci + has_prev[i] >= blk for i in ids]
        q = [q_s[g, pl.ds(j0, blk), :] for g, j0 in work]
        kw = [k_s[g, pl.ds(j0, 2 * blk), :] for g, j0 in work]
        vw = [v_s[g, pl.ds(j0, 2 * blk), :] for g, j0 in work]
        s = [jnp.where(in_window, jnp.where(mask[i], _dot_nt(q[i], kw[i]), NEG_INF), NEG_INF) for i in ids]
        top = [jnp.max(s[i], axis=-1, keepdims=True) for i in ids]
        p = [jnp.exp(s[i] - top[i]) for i in ids]
        den = [jnp.sum(p[i], axis=-1, keepdims=True) for i in ids]
        o = [dot(p[i].astype(BF16), vw[i]) / den[i] for i in ids]
        lse = [top[i] + jnp.log(den[i]) for i in ids]
        for i, (g, j0) in enumerate(work):
            seg, d = segs[g], DIL_GROUPS[g][1]
            dst = pl.ds((j0 % seg) * d + j0 // seg, blk, stride=d) if d > 1 else pl.ds(j0, blk)
            og_s[g, dst, :] = o[i]
            lse_s[g, dst, :] = jnp.broadcast_to(lse[i], (blk, HEAD_DIM))
        return 0

    lax.fori_loop(0, seq // (blk * DIL_Q_PER_STEP), q_block, 0)

    for c0 in range(0, seq, piece):
        rows = pl.ds(c0, piece)
        lses = [lse_s[g, rows, :] for g in grp]
        top = functools.reduce(jnp.maximum, lses)
        wts = [jnp.exp(x - top) for x in lses]
        den = functools.reduce(lambda a, b: a + b, wts)
        o = functools.reduce(lambda a, b: a + b, [wts[g] * og_s[g, rows, :] for g in grp]) / den
        o_ref[rows, :] = o.astype(o_ref.dtype)


def _dilated(h3, cos2, sin2):
    b, s, _ = h3.shape
    hpg = DIL_HEADS_PER_GROUP
    n_g = len(DIL_GROUPS)
    col = lambda off: pl.BlockSpec((None, s, HEAD_DIM), lambda bi, hi: (bi, 0, off + hi))
    tab = pl.BlockSpec((n_g, s, HEAD_DIM), lambda bi, hi: (0, 0, 0))
    specs = [col(part * DIL_HEADS + gi * hpg) for part in range(3) for gi in range(n_g)]
    residue_major = lambda t: jnp.stack([t.reshape(s // d, d, HEAD_DIM).transpose(1, 0, 2).reshape(s, HEAD_DIM)
                                         for _, d in DIL_GROUPS])
    cos2, sin2 = residue_major(cos2), residue_major(sin2)
    return pl.pallas_call(
        _dilated_kernel,
        grid=(b, hpg),
        in_specs=specs + [tab, tab],
        out_specs=pl.BlockSpec((None, s, HEAD_DIM), lambda bi, hi: (bi, 0, hi)),
        out_shape=jax.ShapeDtypeStruct((b, s, hpg * HEAD_DIM), BF16),
        scratch_shapes=[pltpu.VMEM((n_g, s, HEAD_DIM), BF16), pltpu.VMEM((n_g, s + DIL_BLOCK, HEAD_DIM), BF16),
                        pltpu.VMEM((n_g, s + DIL_BLOCK, HEAD_DIM), BF16), pltpu.VMEM((n_g, s, HEAD_DIM), F32),
                        pltpu.VMEM((n_g, s, HEAD_DIM), F32)],
        compiler_params=_params("parallel", "parallel"),
        name="dilated_attention",
    )(*([h3] * (3 * n_g)), cos2, sin2)


def _moba_kernel(q_ref, k_ref, v_ref, cos_ref, sin_ref, o_ref, qf_s, q_s, k_s, v_s, km_s, sel_s):
    seq = o_ref.shape[0]
    blk = MOBA_BLOCK
    n_blk = seq // blk
    cos2 = cos_ref[...]
    sin2 = sin_ref[...]
    q = _rope(q_ref[...], cos2, sin2)
    qf_s[...] = q
    q_s[...] = (q * HEAD_DIM ** -0.5).astype(BF16)
    km_s[...] = jnp.zeros(km_s.shape, F32)
    for nb in range(n_blk):
        kb = _rope(k_ref[pl.ds(nb * blk, blk), :], cos2[nb * blk:(nb + 1) * blk], sin2[nb * blk:(nb + 1) * blk])
        k_s[pl.ds(nb * blk, blk), :] = kb.astype(BF16)
        km_s[pl.ds(nb, 1), :] = jnp.mean(kb, axis=0, keepdims=True)
    v_s[...] = v_ref[...].astype(BF16)

    lane = _iota((blk, LANES), 1).astype(F32)
    causal = _iota((blk, blk), 0) >= _iota((blk, blk), 1)
    rows = lambda nb: pl.ds(nb * blk, blk)

    past = range(1, n_blk)
    km = km_s[...]
    gate = {qb: jnp.where(lane < qb, _dot_nt_hi(qf_s[rows(qb), :], km), -jnp.inf) for qb in past}
    sel = {qb: jnp.zeros((blk, LANES), F32) for qb in past}
    for _k in range(MOBA_TOPK):
        best = {qb: jnp.max(gate[qb], axis=-1, keepdims=True) for qb in past}
        first = {qb: jnp.min(jnp.where(gate[qb] == best[qb], lane, LANES), axis=-1, keepdims=True) for qb in past}
        pick = {qb: (lane == first[qb]) & (best[qb] > -jnp.inf) for qb in past}
        sel = {qb: jnp.where(pick[qb], 1.0, sel[qb]) for qb in past}
        gate = {qb: jnp.where(pick[qb], -jnp.inf, gate[qb]) for qb in past}
    for qb in past:
        sel_s[rows(qb), :] = sel[qb]

    assert sorted(qb for group in MOBA_Q_GROUPS for qb in group) == list(range(n_blk))
    for group in MOBA_Q_GROUPS:
        carry = {qb: _flash_init() for qb in group}
        for j in range(max(group) + 1):
            for qb in group:
                if j < qb:
                    mask = sel_s[rows(qb), :][:, j:j + 1] > 0.0
                elif j == qb:
                    mask = causal
                else:
                    continue
                carry[qb] = _flash_step(q_s[rows(qb), :], k_s[rows(j), :], v_s[rows(j), :], mask, carry[qb])
        for qb in group:
            m, l, acc = carry[qb]
            o_ref[rows(qb), :] = (acc / l).astype(o_ref.dtype)


def _moba(h3, cos2, sin2):
    b, s, _ = h3.shape
    base = 3 * DIL_HEADS
    col = lambda off: pl.BlockSpec((None, s, HEAD_DIM), lambda bi, hi: (bi, 0, base + off + hi))
    tab = pl.BlockSpec((s, HEAD_DIM), lambda bi, hi: (0, 0))
    return pl.pallas_call(
        _moba_kernel,
        grid=(b, MOBA_HEADS),
        in_specs=[col(0), col(MOBA_HEADS), col(2 * MOBA_HEADS), tab, tab],
        out_specs=pl.BlockSpec((None, s, HEAD_DIM), lambda bi, hi: (bi, 0, hi)),
        out_shape=jax.ShapeDtypeStruct((b, s, MOBA_HEADS * HEAD_DIM), BF16),
        scratch_shapes=[pltpu.VMEM((s, HEAD_DIM), F32), pltpu.VMEM((s, HEAD_DIM), BF16),
                        pltpu.VMEM((s, HEAD_DIM), BF16), pltpu.VMEM((s, HEAD_DIM), BF16),
                        pltpu.VMEM((LANES, HEAD_DIM), F32), pltpu.VMEM((s, LANES), F32)],
        compiler_params=_params("parallel", "parallel"),
        name="moba_attention",
    )(h3, h3, h3, cos2, sin2)


def _layer_norm(x, gain, bias):
    mu = jnp.mean(x, axis=-1, keepdims=True)
    xc = x - mu
    var = jnp.mean(xc * xc, axis=-1, keepdims=True)
    return xc * lax.rsqrt(var + LN_EPS) * gain + bias


def _out_ln_kernel(*refs, n_parts):
    o_refs = refs[0:n_parts]
    w_refs = refs[n_parts:2 * n_parts]
    h_ref, gain_ref, bias_ref, rw_ref, y_ref, logit_ref = refs[2 * n_parts:]
    half = h_ref.shape[0] // 2
    halves = [pl.ds(k * half, half) for k in range(2)]
    mix = []
    for rows in halves:
        parts = [jnp.dot(o_r[rows, :], w_r[...], preferred_element_type=F32) for o_r, w_r in zip(o_refs, w_refs)]
        mix.append(functools.reduce(lambda a, b: a + b, parts))
    y = [_layer_norm(DEEPNORM_ALPHA * h_ref[rows, :] + mix[k], gain_ref[...], bias_ref[...])
         for k, rows in enumerate(halves)]
    logits = [_dot3(y[k], rw_ref[...]) for k in range(2)]
    for k, rows in enumerate(halves):
        y_ref[rows, :] = y[k]
        logit_ref[rows, :] = logits[k]


def _out_ln(parts, weights, h, gain, bias, router_w, tm):
    t, d = h.shape
    n_parts = len(parts)
    row = lambda width: pl.BlockSpec((tm, width), lambda i: (i, 0))
    full = lambda a: pl.BlockSpec(a.shape, lambda i: (0, 0))
    return pl.pallas_call(
        functools.partial(_out_ln_kernel, n_parts=n_parts),
        grid=(t // tm,),
        in_specs=[row(p.shape[1]) for p in parts] + [full(w) for w in weights]
                 + [row(d), full(gain), full(bias), full(router_w)],
        out_specs=[row(d), row(LANES)],
        out_shape=[jax.ShapeDtypeStruct((t, d), F32), jax.ShapeDtypeStruct((t, LANES), F32)],
        compiler_params=_params("parallel"),
        name="out_proj_ln",
    )(*parts, *weights, h, gain, bias, router_w)


def _route(logits_t, rbias_col):
    row = _iota(logits_t.shape, 0)
    scores = jax.nn.sigmoid(logits_t)
    biased = scores + rbias_col

    def first_argmax(vals):
        best = jnp.max(vals, axis=0, keepdims=True)
        return best, jnp.min(jnp.where(vals == best, row, N_EXPERTS), axis=0, keepdims=True)

    best_score = None
    best_group = None
    for g in range(N_EXPERT_GROUPS):
        vals = jnp.where(row // EXPERTS_PER_GROUP == g, biased, -jnp.inf)
        top1, idx1 = first_argmax(vals)
        top2, _ = first_argmax(jnp.where(row == idx1, -jnp.inf, vals))
        score = top1 + top2
        if g == 0:
            best_score, best_group = score, jnp.zeros_like(idx1)
        else:
            better = score > best_score
            best_group = jnp.where(better, g, best_group)
            best_score = jnp.where(better, score, best_score)
    masked = jnp.where(row // EXPERTS_PER_GROUP == best_group, biased, NEG_INF)
    _, i1 = first_argmax(masked)
    _, i2 = first_argmax(jnp.where(row == i1, -jnp.inf, masked))
    s1 = jnp.sum(jnp.where(row == i1, scores, 0.0), axis=0, keepdims=True)
    s2 = jnp.sum(jnp.where(row == i2, scores, 0.0), axis=0, keepdims=True)
    tot = s1 + s2
    return i1, i2, s1 / tot, s2 / tot


def _route_kernel(logit_ref, rbias_ref, loc_ref, w_ref, tab_ref, cnt_ref, carry_s, total_s):
    sweep = pl.program_id(0)
    i = pl.program_id(1)
    tm = logit_ref.shape[0]
    block = float(MOE_BLOCK_ROWS)
    align = float(MOE_ALIGN)

    @pl.when((i == 0) & (sweep == 1))
    def _():
        total_s[...] = carry_s[...]

    @pl.when(i == 0)
    def _():
        carry_s[...] = jnp.zeros(carry_s.shape, F32)

    logits_t = logit_ref[...].T[:N_EXPERTS]
    i1, i2, w1, w2 = _route(logits_t, rbias_ref[...][:N_EXPERTS])
    row = _iota((N_EXPERTS, tm), 0)
    chosen = jnp.where((row == i1) | (row == i2), 1.0, 0.0)
    count = jnp.sum(chosen, axis=1, keepdims=True)
    padded = jnp.floor((count + (align - 1.0)) * (1.0 / align)) * align

    e_r = _iota((N_EXPERTS, N_EXPERTS), 0)
    e_c = _iota((N_EXPERTS, N_EXPERTS), 1)

    def expert_prefix(col, inclusive):
        tri = ((e_c <= e_r) if inclusive else (e_c < e_r)).astype(BF16)
        wide = jnp.broadcast_to(col, (N_EXPERTS, LANES)).astype(BF16)
        return jnp.dot(tri, wide, preferred_element_type=F32)[:, 0:1]

    @pl.when(sweep == 0)
    def _():
        loc_ref[...] = jnp.zeros(loc_ref.shape, loc_ref.dtype)
        w_ref[...] = jnp.zeros(w_ref.shape, w_ref.dtype)
        tab_ref[...] = jnp.zeros(tab_ref.shape, tab_ref.dtype)
        cnt_ref[...] = jnp.zeros(cnt_ref.shape, cnt_ref.dtype)

    @pl.when(sweep == 1)
    def _():
        total = total_s[...]
        blocks = jnp.floor((total + (block - 1.0)) * (1.0 / block))
        region = (expert_prefix(blocks, True) - blocks) * block + carry_s[...]
        run_start = expert_prefix(padded, False)
        earlier = (_iota((tm, tm), 0) < _iota((tm, tm), 1)).astype(BF16)
        slot = run_start + jnp.dot(chosen.astype(BF16), earlier, preferred_element_type=F32)
        l1 = jnp.sum(jnp.where(row == i1, slot, 0.0), axis=0, keepdims=True)
        l2 = jnp.sum(jnp.where(row == i2, slot, 0.0), axis=0, keepdims=True)
        out_row = _iota((8, tm), 0)
        loc_ref[...] = jnp.where(out_row == 0, l1, jnp.where(out_row == 1, l2, 0.0)).astype(jnp.int32)
        w_ref[...] = jnp.where(out_row == 0, w1, jnp.where(out_row == 1, w2, 0.0))
        lane = _iota((N_EXPERTS, LANES), 1)
        tab = jnp.where(lane == 0, padded, jnp.where(lane == 1, run_start, jnp.where(lane == 2, region, 0.0)))
        tab_ref[...] = tab.astype(jnp.int32)
        cnt_ref[...] = jnp.broadcast_to(total, cnt_ref.shape).astype(jnp.int32)

    carry_s[...] = carry_s[...] + padded


def _route_tokens(logits, rbias_col):
    t = logits.shape[0]
    tm = MOE_TILE
    n_tiles = t // tm
    tok = pl.BlockSpec((None, 8, tm), lambda s, i: (s, 0, i))
    loc, w, tab, totals = pl.pallas_call(
        _route_kernel,
        grid=(2, n_tiles),
        in_specs=[pl.BlockSpec((tm, LANES), lambda s, i: (i, 0)), pl.BlockSpec((LANES, 1), lambda s, i: (0, 0))],
        out_specs=[tok, tok, pl.BlockSpec((None, None, N_EXPERTS, LANES), lambda s, i: (s, i, 0, 0)),
                   pl.BlockSpec((N_EXPERTS, LANES), lambda s, i: (0, 0))],
        out_shape=[jax.ShapeDtypeStruct((2, 8, t), jnp.int32), jax.ShapeDtypeStruct((2, 8, t), F32),
                   jax.ShapeDtypeStruct((2, n_tiles, N_EXPERTS, LANES), jnp.int32),
                   jax.ShapeDtypeStruct((N_EXPERTS, LANES), jnp.int32)],
        scratch_shapes=[pltpu.VMEM((N_EXPERTS, 1), F32), pltpu.VMEM((N_EXPERTS, 1), F32)],
        compiler_params=_params("arbitrary", "arbitrary"),
        name="moe_route",
    )(logits, rbias_col)
    return loc[1], w[1], tab[1], totals[:, 0]


def _run_pieces(length, sizes):
    return [(size, length & ~(2 * size - 1), (length & size) != 0) for size in sizes]


def _tile_run_copies(cnt_ref, off_ref, region_ref, tile, compact, sorted_rows, sem, to_sorted):
    copies = []
    for e in range(N_EXPERTS):
        k = tile * N_EXPERTS + e
        length = pl.multiple_of(cnt_ref[k], MOE_ALIGN)
        small = compact.at[pl.ds(pl.multiple_of(off_ref[k], MOE_ALIGN), length)]
        big = sorted_rows.at[pl.ds(pl.multiple_of(region_ref[k], MOE_ALIGN), length)]
        copies.append((length > 0, pltpu.make_async_copy(small, big, sem) if to_sorted
                       else pltpu.make_async_copy(big, small, sem)))
    return copies


def _dispatch_kernel(cnt_ref, off_ref, region_ref, pad_start_ref, pad_len_ref, n_valid_ref,
                     x_ref, loc_ref, w_ref, xs_ref, buf_s, zero_s, sem, zero_sem):
    i = pl.program_id(0)
    d = x_ref.shape[1]
    cap = buf_s.shape[1]
    tm = x_ref.shape[0]
    n_blocks = xs_ref.shape[0] // MOE_BLOCK_ROWS

    @pl.when(i == 0)
    def _():
        zero_s[...] = jnp.zeros(zero_s.shape, zero_s.dtype)
        sizes = [MOE_BLOCK_ROWS >> s for s in range((MOE_BLOCK_ROWS // MOE_ALIGN).bit_length())]
        copies = []
        for e in range(N_EXPERTS):
            for size, offset, used in _run_pieces(pad_len_ref[e], sizes):
                dst = xs_ref.at[pl.ds(pl.multiple_of(pad_start_ref[e] + offset, MOE_ALIGN), size)]
                copies.append((used, pltpu.make_async_copy(zero_s.at[pl.ds(0, size)], dst, zero_sem)))
        for j in range(n_blocks - N_EXPERTS, n_blocks):
            dst = xs_ref.at[pl.ds(j * MOE_BLOCK_ROWS, MOE_BLOCK_ROWS)]
            copies.append((j >= n_valid_ref[0], pltpu.make_async_copy(zero_s, dst, zero_sem)))
        for action in ("start", "wait"):
            for used, copy in copies:
                @pl.when(used)
                def _():
                    getattr(copy, action)()

    loc = loc_ref[...]
    w = w_ref[...]
    slot = i % 2
    x = x_ref[...].astype(BF16)
    half = cap // 2
    parts = [pl.ds(k * half, half) for k in range(2)]
    row = [_iota((half, tm), 0) + k * half for k in range(2)]
    hit1 = [row[k] == loc[0:1, :] for k in range(2)]
    hit2 = [row[k] == loc[1:2, :] for k in range(2)]
    perm = [jnp.where(hit1[k], 1.0, jnp.where(hit2[k], 1.0, 0.0)).astype(BF16) for k in range(2)]
    sorted_x = [jnp.dot(perm[k], x, preferred_element_type=F32).astype(BF16) for k in range(2)]
    gate = [jnp.sum(jnp.where(hit1[k], w[0:1, :], jnp.where(hit2[k], w[1:2, :], 0.0)), axis=1, keepdims=True)
            for k in range(2)]
    lane = _iota((half, LANES), 1)
    for k in range(2):
        g_hi = gate[k].astype(BF16)
        g_lo = (gate[k] - g_hi.astype(F32)).astype(BF16)
        buf_s[slot, parts[k], pl.ds(0, d)] = sorted_x[k]
        buf_s[slot, parts[k], pl.ds(d, LANES)] = jnp.where(lane == 0, g_hi.astype(F32),
                                                          jnp.where(lane == 1, g_lo.astype(F32), 0.0)).astype(BF16)

    def run_copies(tile, which, action):
        for used, copy in _tile_run_copies(cnt_ref, off_ref, region_ref, tile, buf_s.at[which], xs_ref,
                                           sem.at[which], True):
            @pl.when(used)
            def _():
                getattr(copy, action)()

    run_copies(i, slot, "start")

    @pl.when(i > 0)
    def _():
        run_copies(i - 1, 1 - slot, "wait")

    @pl.when(i == pl.num_programs(0) - 1)
    def _():
        run_copies(i, slot, "wait")


def _dispatch(tables, pad_start, pad_len, n_valid, x, loc, w, n_rows):
    t, d = x.shape
    tm = MOE_TILE
    cap = MOE_TOPK * tm + N_EXPERTS * MOE_ALIGN
    tok = pl.BlockSpec((8, tm), lambda i, *_: (0, i))
    grid_spec = pltpu.PrefetchScalarGridSpec(
        num_scalar_prefetch=6,
        grid=(t // tm,),
        in_specs=[pl.BlockSpec((tm, d), lambda i, *_: (i, 0)), tok, tok],
        out_specs=pl.BlockSpec(memory_space=pl.ANY),
        scratch_shapes=[pltpu.VMEM((2, cap, d + LANES), BF16), pltpu.VMEM((MOE_BLOCK_ROWS, d + LANES), BF16),
                        pltpu.SemaphoreType.DMA((2,)), pltpu.SemaphoreType.DMA(())],
    )
    return pl.pallas_call(
        _dispatch_kernel,
        grid_spec=grid_spec,
        out_shape=jax.ShapeDtypeStruct((n_rows, d + LANES), BF16),
        compiler_params=_params("arbitrary"),
        name="moe_dispatch",
    )(*tables, pad_start, pad_len, n_valid, x, loc, w)


def _expert_kernel(blk_expert_ref, next_expert_ref, n_valid_ref, x_ref, wg_ref, wu_ref, wd_ref, y_ref,
                   wg_s, wu_s, wd_s, wg_buf, wu_buf, wd_buf, slot_s, sem, *, layer):
    i = pl.program_id(0)

    def fetch(expert, slot):
        return [pltpu.make_async_copy(src.at[layer, expert], dst.at[slot], sem.at[slot])
                for src, dst in ((wg_ref, wg_buf), (wu_ref, wu_buf), (wd_ref, wd_buf))]

    @pl.when(i == 0)
    def _():
        slot_s[0] = 1
        for copy in fetch(blk_expert_ref[0], 0):
            copy.start()

    @pl.when((i == 0) | (blk_expert_ref[i] != blk_expert_ref[jnp.maximum(i - 1, 0)]))
    def _():
        slot = 1 - slot_s[0]
        slot_s[0] = slot
        for copy in fetch(blk_expert_ref[i], slot):
            copy.wait()
        wg_s[...] = wg_buf[slot].astype(BF16)
        wu_s[...] = wu_buf[slot].astype(BF16)
        wd_s[...] = wd_buf[slot].astype(BF16)

        @pl.when(next_expert_ref[i] >= 0)
        def _():
            for copy in fetch(next_expert_ref[i], 1 - slot):
                copy.start()

    @pl.when(i < n_valid_ref[0])
    def _():
        d = wg_s.shape[0]
        x = x_ref[:, pl.ds(0, d)]
        extra = x_ref[:, pl.ds(d, LANES)].astype(F32)
        gate = extra[:, 0:1] + extra[:, 1:2]
        hid = _silu(jnp.dot(x, wg_s[...], preferred_element_type=F32)) * jnp.dot(
            x, wu_s[...], preferred_element_type=F32)
        y_ref[...] = _dot(hid * gate, wd_s[...]).astype(y_ref.dtype)

    @pl.when(i >= n_valid_ref[0])
    def _():
        y_ref[...] = jnp.zeros(y_ref.shape, y_ref.dtype)


def _experts(blk_expert, next_expert, n_valid, xs, wg, wu, wd, layer, n_blocks):
    d = wg.shape[-2]
    f = wg.shape[-1]
    rows = MOE_BLOCK_ROWS
    n_rows = n_blocks * rows
    hbm = pl.BlockSpec(memory_space=pl.ANY)
    grid_spec = pltpu.PrefetchScalarGridSpec(
        num_scalar_prefetch=3,
        grid=(n_blocks,),
        in_specs=[pl.BlockSpec((rows, xs.shape[1]), lambda i, be, ne, nv: (jnp.minimum(i, nv[0] - 1), 0)),
                  hbm, hbm, hbm],
        out_specs=pl.BlockSpec((rows, d), lambda i, be, ne, nv: (i, 0)),
        scratch_shapes=[pltpu.VMEM((d, f), BF16), pltpu.VMEM((d, f), BF16), pltpu.VMEM((f, d), BF16),
                        pltpu.VMEM((2, d, f), F32), pltpu.VMEM((2, d, f), F32), pltpu.VMEM((2, f, d), F32),
                        pltpu.SMEM((1,), jnp.int32), pltpu.SemaphoreType.DMA((2,))],
    )
    return pl.pallas_call(
        functools.partial(_expert_kernel, layer=layer),
        grid_spec=grid_spec,
        out_shape=jax.ShapeDtypeStruct((n_rows, d), BF16),
        compiler_params=_params("arbitrary"),
        name="moe_experts",
    )(blk_expert, next_expert, n_valid, xs, wg, wu, wd)


def _combine_kernel(cnt_ref, off_ref, region_ref, ys_ref, loc_ref, h_ref, gain_ref, bias_ref, y_ref, yb_ref,
                    buf, sem):
    i = pl.program_id(0)
    n = pl.num_programs(0)
    tm = h_ref.shape[0]
    cap = buf.shape[1]

    def fetch(tile, slot, action):
        for used, copy in _tile_run_copies(cnt_ref, off_ref, region_ref, tile, buf.at[slot], ys_ref,
                                           sem.at[slot], False):
            @pl.when(used)
            def _():
                getattr(copy, action)()

    slot = i % 2

    @pl.when(i == 0)
    def _():
        buf[...] = jnp.zeros(buf.shape, buf.dtype)
        fetch(0, 0, "start")

    @pl.when(i + 1 < n)
    def _():
        fetch(i + 1, 1 - slot, "start")

    fetch(i, slot, "wait")
    loc = loc_ref[...].astype(F32)
    eye = _iota((tm, tm), 0) == _iota((tm, tm), 1)
    loc_col = [jnp.sum(jnp.where(eye, jnp.broadcast_to(loc[k:k + 1, :], (tm, tm)), 0.0), axis=1, keepdims=True)
               for k in range(MOE_TOPK)]
    half = tm // 2
    halves = [slice(k * half, (k + 1) * half) for k in range(2)]
    lane = _iota((half, cap), 1).astype(F32)
    pick = [jnp.where(lane == loc_col[0][rows], 1.0, jnp.where(lane == loc_col[1][rows], 1.0, 0.0)).astype(BF16)
            for rows in halves]
    ffn = [jnp.dot(pick[k], buf[slot], preferred_element_type=F32) for k in range(2)]
    for k, rows in enumerate(halves):
        y = _layer_norm(DEEPNORM_ALPHA * h_ref[rows, :] + ffn[k], gain_ref[...], bias_ref[...])
        y_ref[rows, :] = y
        yb_ref[rows, :] = y.astype(BF16)


def _combine_ln(tables, ys, loc, h, gain, bias):
    t, d = h.shape
    tm = MOE_TILE
    cap = MOE_TOPK * tm + N_EXPERTS * MOE_ALIGN
    grid_spec = pltpu.PrefetchScalarGridSpec(
        num_scalar_prefetch=3,
        grid=(t // tm,),
        in_specs=[pl.BlockSpec(memory_space=pl.ANY),
                  pl.BlockSpec((8, tm), lambda i, *_: (0, i)),
                  pl.BlockSpec((tm, d), lambda i, *_: (i, 0)),
                  pl.BlockSpec((1, d), lambda i, *_: (0, 0)),
                  pl.BlockSpec((1, d), lambda i, *_: (0, 0))],
        out_specs=[pl.BlockSpec((tm, d), lambda i, *_: (i, 0)), pl.BlockSpec((tm, d), lambda i, *_: (i, 0))],
        scratch_shapes=[pltpu.VMEM((2, cap, d), BF16), pltpu.SemaphoreType.DMA((2,))],
    )
    return pl.pallas_call(
        _combine_kernel,
        grid_spec=grid_spec,
        out_shape=[jax.ShapeDtypeStruct((t, d), F32), jax.ShapeDtypeStruct((t, d), BF16)],
        compiler_params=_params("arbitrary"),
        name="moe_combine_ln",
    )(*tables, ys, loc, h, gain, bias)


def _moe_ln(h, logits, rbias, wg, wu, wd, layer, gain, bias):
    t, d = h.shape
    rows = MOE_BLOCK_ROWS
    n_tiles = t // MOE_TILE
    n_blocks = -(-(MOE_TOPK * t + n_tiles * N_EXPERTS * (MOE_ALIGN - 1)) // rows) + N_EXPERTS
    loc, w_t, tab, counts = _route_tokens(logits, rbias)
    tables = tuple(tab[:, :, k].reshape(-1) for k in range(3))
    blocks_per_expert = (counts + rows - 1) // rows
    block_end = jnp.cumsum(blocks_per_expert)
    row_start = (block_end - blocks_per_expert) * rows
    n_valid = block_end[-1:].astype(jnp.int32)
    blk = jnp.arange(n_blocks, dtype=jnp.int32)
    blk_expert = jnp.sum(jnp.minimum(blk, n_valid - 1)[:, None] >= block_end[None, :], axis=1).astype(jnp.int32)
    pad_start = (row_start + counts).astype(jnp.int32)
    pad_len = (blocks_per_expert * rows - counts).astype(jnp.int32)
    xs = _dispatch(tables, pad_start, pad_len, n_valid, h, loc, w_t, n_blocks * rows)
    experts = jnp.arange(N_EXPERTS, dtype=jnp.int32)
    later_used = (experts[None, :] > experts[:, None]) & (blocks_per_expert[None, :] > 0)
    next_used = jnp.min(jnp.where(later_used, experts[None, :], N_EXPERTS), axis=1)
    next_expert = jnp.where(next_used < N_EXPERTS, next_used, -1).astype(jnp.int32)[blk_expert]
    ys = _experts(blk_expert, next_expert, n_valid, xs, wg, wu, wd, layer, n_blocks)
    return _combine_ln(tables, ys, loc, h, gain, bias)


def _rope_tables(seq):
    inv_freq = ROPE_THETA ** (-jnp.arange(0, HEAD_DIM, 2, dtype=F32) / HEAD_DIM)
    ang = jnp.arange(seq, dtype=F32)[:, None] * inv_freq[None, :]
    cos, sin = jnp.cos(ang), jnp.sin(ang)
    return jnp.concatenate([cos, cos], axis=-1), jnp.concatenate([-sin, sin], axis=-1)


def _even_mixer(hb, b, s, w_in, conv_w, a_log, dt_bias, gdn_norm, hgrn_norm, lower_bound):
    gw = GDN_WIDTH
    n_small = 2 * GDN_HEADS
    tail0 = 4 * gw
    w_t = w_in.T
    w_small = jnp.pad(w_t[tail0:tail0 + n_small], ((0, LANES - n_small), (0, 0))).astype(BF16)
    h_a = _matmul(hb, w_t[:tail0].astype(BF16), *EVEN_PROJ_TILE, transposed=True).reshape(b, s, tail0)
    h_b, small = _matmul(hb, w_t[tail0 + n_small:].astype(BF16), *EVEN_PROJ_TILE, transposed=True, side_w=w_small)
    h_b = h_b.reshape(b, s, 4 * HGRN_WIDTH)
    n_chunks = s // GDN_CHUNK
    small = small.reshape(b, s, LANES)
    to_rows = lambda a: a.transpose(0, 2, 1).reshape(b, GDN_HEADS, n_chunks, GDN_CHUNK)
    b_rows, a_rows = to_rows(small[..., :GDN_HEADS]), to_rows(small[..., GDN_HEADS:n_small])
    headvec = lambda v: jnp.broadcast_to(v.astype(F32)[:, None, None], (GDN_HEADS, 1, GDN_CHUNK))
    o_a = _gdn(h_a, conv_w.astype(F32), a_rows, b_rows, headvec(a_log), headvec(dt_bias),
               gdn_norm.astype(F32).reshape(1, HEAD_DIM))
    o_b = _hgrn(h_b, lower_bound.astype(F32).reshape(HGRN_HEADS, 1, HEAD_DIM),
                hgrn_norm.astype(F32).reshape(1, HEAD_DIM))
    return [o_a.reshape(b * s, GDN_WIDTH), o_b.reshape(b * s, HGRN_WIDTH)]


def _odd_mixer(hb, b, s, w_in, cos2, sin2):
    h = _matmul(hb, w_in.astype(BF16), *ODD_PROJ_TILE)
    h3 = h.reshape(b, s, ODD_COLS)
    o_c = _dilated(h3, cos2, sin2)
    o_d = _moba(h3, cos2, sin2)
    return [o_c.reshape(b * s, -1), o_d.reshape(b * s, -1)]


def kernel(x, ev_w_in, ev_conv_w, ev_a_log, ev_dt_bias, ev_gdn_norm, ev_hgrn_norm, hgrn_lb_logits, ev_w_out,
           od_w_in, od_w_out, router_w, router_bias, moe_w_gate, moe_w_up, moe_w_down, ln_gain, ln_bias):
    b, s, d = x.shape
    t = b * s
    cos2, sin2 = _rope_tables(s)
    lower_bounds = jnp.cumsum(jax.nn.softmax(hgrn_lb_logits.astype(F32), axis=0), axis=0)
    rw = jnp.pad(router_w.astype(F32), ((0, 0), (0, LANES - N_EXPERTS)))
    rbias = jnp.pad(router_bias.astype(F32), (0, LANES - N_EXPERTS)).reshape(LANES, 1)
    vec = lambda v: v.astype(F32).reshape(1, d)

    h = x.reshape(t, d)
    hb = h
    for layer in range(DEPTH):
        if layer % 2 == 0:
            e = layer // 2
            parts = _even_mixer(hb, b, s, ev_w_in[e], ev_conv_w[e], ev_a_log[e], ev_dt_bias[e], ev_gdn_norm[e],
                                ev_hgrn_norm[e], lower_bounds[layer])
            w_out = ev_w_out[e].astype(BF16)
        else:
            o = layer // 2
            parts = _odd_mixer(hb, b, s, od_w_in[o], cos2, sin2)
            w_out = od_w_out[o].astype(BF16)
        splits = np.cumsum([p.shape[1] for p in parts])[:-1]
        weights = jnp.split(w_out, splits, axis=0)
        h, logits = _out_ln(parts, weights, h, vec(ln_gain[layer, 0]), vec(ln_bias[layer, 0]), rw, OUT_PROJ_ROWS)
        h, hb = _moe_ln(h, logits, rbias, moe_w_gate, moe_w_up, moe_w_down, layer,
                        vec(ln_gain[layer, 1]), vec(ln_bias[layer, 1]))
    return h.reshape(b, s, d)
```
